```python
import math
import jax
import jax.numpy as jnp
from jax import lax
import numpy as np

D_MODEL = 1024
BATCH = 2
SEQ = 16384
DEPTH = 4

PLE_DIM = 256
N_EVEN = (DEPTH + 1) // 2
N_ODD = DEPTH // 2
RMS_EPS = 1e-6
Q_BLOCK = 128

FOX_HEADS = 8
FOX_HEAD_DIM = 64
MLA_HEADS = 8
MLA_Q_LORA = 384
MLA_KV_LORA = 256
MLA_NOPE = 64
MLA_ROPE = 32
MLA_V = 64
ROPE_THETA = 10000.0
EVEN_SPLITS = (FOX_HEADS * FOX_HEAD_DIM, FOX_HEADS * FOX_HEAD_DIM, FOX_HEADS * FOX_HEAD_DIM,
               FOX_HEADS, MLA_Q_LORA, MLA_KV_LORA, MLA_ROPE)
EVEN_IN = sum(EVEN_SPLITS)
EVEN_MIX = FOX_HEADS * FOX_HEAD_DIM + MLA_HEADS * MLA_V

S5_CH = 512
S5_GROUP_CH = 16
S5_GROUPS = S5_CH // S5_GROUP_CH
S5_STATE = 64
GDN_HEADS = 4
GDN_HEAD_DIM = 128
GDN_W = GDN_HEADS * GDN_HEAD_DIM
GDN_CONV = 4
GDN_CHUNK = 64
ODD_SPLITS = (S5_CH, 3 * GDN_W, GDN_HEADS, GDN_HEADS, GDN_W)
ODD_IN = sum(ODD_SPLITS)
ODD_MIX = S5_CH + GDN_W

MOE_GROUPS = 4
MOE_PER_GROUP = 8
N_EXPERTS = MOE_GROUPS * MOE_PER_GROUP
MOE_TOP_K = 2
MOE_FF = 512
MOE_BLOCK = 128

kernel_name = 'hybrid_fox_mla_s5_gdn_hmoe'


def _split(x, sizes):
    idx = [int(v) for v in np.cumsum(sizes)[:-1]]
    return jnp.split(x, idx, axis=-1)


def _rms_norm(x, g):
    x32 = x.astype(jnp.float32)
    y = x32 * lax.rsqrt(jnp.mean(x32 * x32, axis=-1, keepdims=True) + RMS_EPS)
    return (y * g.astype(jnp.float32)).astype(x.dtype)


def _l2norm(x):
    x32 = x.astype(jnp.float32)
    return x32 * lax.rsqrt(jnp.sum(x32 * x32, axis=-1, keepdims=True) + RMS_EPS)


def _rope(x, positions):
    r = x.shape[-1]
    half = r // 2
    inv = ROPE_THETA ** (-jnp.arange(half, dtype=jnp.float32) * 2.0 / r)
    ang = positions.astype(jnp.float32)[:, :, None] * inv
    cos = jnp.cos(ang)[:, :, None, :]
    sin = jnp.sin(ang)[:, :, None, :]
    x32 = x.astype(jnp.float32)
    x1, x2 = x32[..., :half], x32[..., half:]
    return jnp.concatenate([x1 * cos - x2 * sin, x1 * sin + x2 * cos], axis=-1).astype(x.dtype)


def _causal_block_attention(q, k, v, scale, cum_logf=None):
    b, s, h, dk = q.shape
    dv = v.shape[-1]
    nb = s // Q_BLOCK
    key_pos = jnp.arange(s)
    q_blocks = q.reshape(b, nb, Q_BLOCK, h, dk).swapaxes(0, 1)
    if cum_logf is not None:
        cum_keys = cum_logf.transpose(0, 2, 1)
        f_blocks = cum_logf.reshape(b, nb, Q_BLOCK, h).swapaxes(0, 1)

    def one_block(args):
        i, q_blk = args[0], args[1]
        sc = jnp.einsum('bqhd,bkhd->bhqk', q_blk, k, preferred_element_type=jnp.float32) * scale
        if cum_logf is not None:
            f_q = args[2].transpose(0, 2, 1)
            sc = sc + (f_q[..., :, None] - cum_keys[:, :, None, :])
        q_pos = i * Q_BLOCK + jnp.arange(Q_BLOCK)
        causal = key_pos[None, :] <= q_pos[:, None]
        sc = jnp.where(causal, sc, -jnp.inf)
        pr = jax.nn.softmax(sc, axis=-1).astype(v.dtype)
        return jnp.einsum('bhqk,bkhd->bqhd', pr, v)

    if cum_logf is None:
        xs = (jnp.arange(nb), q_blocks)
    else:
        xs = (jnp.arange(nb), q_blocks, f_blocks)
    out = lax.map(one_block, xs)
    return out.swapaxes(0, 1).reshape(b, s, h, dv)


def _even_mixer(a, positions, w_in, b_f, fox_qn, fox_kn, q_a_norm, w_q_up, kv_a_norm, w_kv_up,
                mla_qn, mla_kn, w_out):
    b, s, _ = a.shape
    fq, fk, fv, f_logit, c_q, c_kv, k_rope = _split(a @ w_in, EVEN_SPLITS)
    fq = _rms_norm(fq.reshape(b, s, FOX_HEADS, FOX_HEAD_DIM), fox_qn)
    fk = _rms_norm(fk.reshape(b, s, FOX_HEADS, FOX_HEAD_DIM), fox_kn)
    fv = fv.reshape(b, s, FOX_HEADS, FOX_HEAD_DIM)
    log_f = jax.nn.log_sigmoid(f_logit.astype(jnp.float32) + b_f)
    cum = jnp.cumsum(log_f, axis=1)
    o_fox = _causal_block_attention(fq, fk, fv, FOX_HEAD_DIM ** -0.5, cum)
    q = (_rms_norm(c_q, q_a_norm) @ w_q_up).reshape(b, s, MLA_HEADS, MLA_NOPE + MLA_ROPE)
    kv = (_rms_norm(c_kv, kv_a_norm) @ w_kv_up).reshape(b, s, MLA_HEADS, MLA_NOPE + MLA_V)
    k_nope, v = kv[..., :MLA_NOPE], kv[..., MLA_NOPE:]
    k = jnp.concatenate([k_nope, jnp.broadcast_to(k_rope[:, :, None, :], (b, s, MLA_HEADS, MLA_ROPE))], axis=-1)
    q = _rms_norm(q, mla_qn)
    k = _rms_norm(k, mla_kn)
    q = jnp.concatenate([q[..., :MLA_NOPE], _rope(q[..., MLA_NOPE:], positions)], axis=-1)
    k = jnp.concatenate([k[..., :MLA_NOPE], _rope(k[..., MLA_NOPE:], positions)], axis=-1)
    o_mla = _causal_block_attention(q, k, v, (MLA_NOPE + MLA_ROPE) ** -0.5)
    o = jnp.concatenate([o_fox.reshape(b, s, -1), o_mla.reshape(b, s, -1)], axis=-1)
    return o @ w_out


def _s5_glu(u, a_re, a_im, b_re, b_im, c_re, c_im, d_skip, log_step, w_glu, b_glu):
    b, s, _ = u.shape
    u32 = u.astype(jnp.float32)
    ug = u32.reshape(b, s, S5_GROUPS, S5_GROUP_CH)
    lam_re = jnp.minimum(a_re.astype(jnp.float32), -1e-4)
    lam_im = a_im.astype(jnp.float32)
    dt = jnp.exp(log_step.astype(jnp.float32))[:, None]
    mag = jnp.exp(lam_re * dt)
    ab_re = mag * jnp.cos(lam_im * dt)
    ab_im = mag * jnp.sin(lam_im * dt)
    den = lam_re * lam_re + lam_im * lam_im
    nr, ni = ab_re - 1.0, ab_im
    gam_re = (nr * lam_re + ni * lam_im) / den
    gam_im = (ni * lam_re - nr * lam_im) / den
    bb_re = gam_re[..., None] * b_re - gam_im[..., None] * b_im
    bb_im = gam_re[..., None] * b_im + gam_im[..., None] * b_re
    bu_re = jnp.einsum('bsgc,gpc->bsgp', ug, bb_re)
    bu_im = jnp.einsum('bsgc,gpc->bsgp', ug, bb_im)
    ar = jnp.broadcast_to(ab_re, bu_re.shape)
    ai = jnp.broadcast_to(ab_im, bu_re.shape)

    def combine(e1, e2):
        a1r, a1i, b1r, b1i = e1
        a2r, a2i, b2r, b2i = e2
        return (a2r * a1r - a2i * a1i, a2r * a1i + a2i * a1r,
                a2r * b1r - a2i * b1i + b2r, a2r * b1i + a2i * b1r + b2i)

    _, _, xr, xi = lax.associative_scan(combine, (ar, ai, bu_re, bu_im), axis=1)
    y = jnp.einsum('bsgp,gcp->bsgc', xr, c_re) - jnp.einsum('bsgp,gcp->bsgc', xi, c_im)
    y = y.reshape(b, s, S5_CH) + d_skip * u32
    hg = jax.nn.gelu(y)
    out = hg * jax.nn.sigmoid(hg @ w_glu + b_glu)
    return out.astype(u.dtype)


def _causal_conv(x, w):
    c = x.shape[-1]
    return lax.conv_general_dilated(x, w[:, None, :].astype(x.dtype), window_strides=(1,),
                                    padding=[(w.shape[0] - 1, 0)],
                                    dimension_numbers=('NWC', 'WIO', 'NWC'),
                                    feature_group_count=c)


def _gated_delta_chunked(q, k, v, g, beta):
    b, s, h, dk = q.shape
    dv = v.shape[-1]
    c = GDN_CHUNK
    nc = s // c
    f32 = jnp.float32

    def to_chunks(t):
        return t.astype(f32).transpose(0, 2, 1, 3).reshape(b, h, nc, c, t.shape[-1])

    qc, kc, vc = to_chunks(q), to_chunks(k), to_chunks(v)
    gc = jnp.cumsum(g.astype(f32).transpose(0, 2, 1).reshape(b, h, nc, c), axis=-1)
    bc = beta.astype(f32).transpose(0, 2, 1).reshape(b, h, nc, c)
    incl = jnp.tril(jnp.ones((c, c), bool))
    strict = jnp.tril(jnp.ones((c, c), bool), -1)
    diff = gc[..., :, None] - gc[..., None, :]
    decay = jnp.where(incl, jnp.exp(jnp.where(incl, diff, 0.0)), 0.0)
    k_beta = kc * bc[..., None]
    a_mat = jnp.where(strict, jnp.einsum('bhnid,bhnjd->bhnij', k_beta, kc) * decay, 0.0)
    eye = jnp.eye(c, dtype=f32)
    t_mat = lax.linalg.triangular_solve(eye + a_mat, jnp.broadcast_to(eye, a_mat.shape),
                                        left_side=True, lower=True)
    u = t_mat @ (vc * bc[..., None])
    w = t_mat @ (k_beta * jnp.exp(gc)[..., None])
    intra = jnp.einsum('bhnid,bhnjd->bhnij', qc, kc) * decay
    q_dec = qc * jnp.exp(gc)[..., None]
    g_last = gc[..., -1]
    k_dec = kc * jnp.exp(g_last[..., None] - gc)[..., None]

    def step(state, xs):
        q_i, a_i, u_i, w_i, k_i, gl = xs
        v_new = u_i - jnp.einsum('bhck,bhkv->bhcv', w_i, state)
        o = jnp.einsum('bhck,bhkv->bhcv', q_i, state) + jnp.einsum('bhij,bhjv->bhiv', a_i, v_new)
        state = state * jnp.exp(gl)[..., None, None] + jnp.einsum('bhck,bhcv->bhkv', k_i, v_new)
        return state, o

    xs = tuple(jnp.moveaxis(t, 2, 0) for t in (q_dec, intra, u, w, k_dec, g_last))
    _, o = lax.scan(step, jnp.zeros((b, h, dk, dv), f32), xs)
    return o.transpose(1, 0, 3, 2, 4).reshape(b, s, h, dv)


def _odd_mixer(a, w_in, a_re, a_im, b_re, b_im, c_re, c_im, d_skip, log_step, w_glu, b_glu,
               conv_w, a_log, dt_bias, o_norm, w_out):
    b, s, _ = a.shape
    u, qkv, g_in, beta_in, z = _split(a @ w_in, ODD_SPLITS)
    y_ssm = _s5_glu(u, a_re, a_im, b_re, b_im, c_re, c_im, d_skip, log_step, w_glu, b_glu)
    qkv = jax.nn.silu(_causal_conv(qkv, conv_w))
    q, k, v = jnp.split(qkv, 3, axis=-1)
    q = _l2norm(q.reshape(b, s, GDN_HEADS, GDN_HEAD_DIM)) * (GDN_HEAD_DIM ** -0.5)
    k = _l2norm(k.reshape(b, s, GDN_HEADS, GDN_HEAD_DIM))
    v = v.reshape(b, s, GDN_HEADS, GDN_HEAD_DIM)
    beta = jax.nn.sigmoid(beta_in.astype(jnp.float32))
    g = -jnp.exp(a_log.astype(jnp.float32)) * jax.nn.softplus(g_in.astype(jnp.float32) + dt_bias)
    o = _gated_delta_chunked(q, k, v, g, beta).astype(a.dtype)
    o = _rms_norm(o, o_norm) * jax.nn.silu(z.reshape(b, s, GDN_HEADS, GDN_HEAD_DIM))
    return jnp.concatenate([y_ssm, o.reshape(b, s, -1)], axis=-1) @ w_out


def _expert_dispatch(xf, experts, weights, w_gate, w_up, w_down):
    n, d = xf.shape
    n_assign = n * MOE_TOP_K
    flat_e = experts.reshape(-1).astype(jnp.int32)
    flat_tok = jnp.arange(n_assign, dtype=jnp.int32) // MOE_TOP_K
    flat_w = weights.reshape(-1)
    order = jnp.argsort(flat_e)
    sorted_e = flat_e[order]
    counts = jnp.bincount(flat_e, length=N_EXPERTS)
    start = jnp.cumsum(counts) - counts
    padded = (counts + MOE_BLOCK - 1) // MOE_BLOCK * MOE_BLOCK
    pad_end = jnp.cumsum(padded)
    pad_start = pad_end - padded
    dest = pad_start[sorted_e] + jnp.arange(n_assign, dtype=jnp.int32) - start[sorted_e]
    n_blocks = -(-n_assign // MOE_BLOCK) + N_EXPERTS
    cap = n_blocks * MOE_BLOCK
    buf_tok = jnp.full((cap,), n, jnp.int32).at[dest].set(flat_tok[order])
    buf_w = jnp.zeros((cap,), xf.dtype).at[dest].set(flat_w[order].astype(xf.dtype))
    blk_e = jnp.minimum(jnp.searchsorted(pad_end, jnp.arange(n_blocks, dtype=jnp.int32) * MOE_BLOCK,
                                         side='right'), N_EXPERTS - 1)
    x_pad = jnp.concatenate([xf, jnp.zeros((1, d), xf.dtype)], axis=0)

    def expert_block(args):
        tok, e = args
        xb = x_pad[tok]
        hb = jax.nn.silu(xb @ w_gate[e]) * (xb @ w_up[e])
        return hb @ w_down[e]

    yb = lax.map(expert_block, (buf_tok.reshape(n_blocks, MOE_BLOCK), blk_e))
    y = jnp.zeros((n + 1, d), yb.dtype).at[buf_tok].add(yb.reshape(cap, d) * buf_w[:, None].astype(yb.dtype))
    return y[:n]


def _hier_moe(m, w_group, b_group, w_expert, b_expert, w_gate, w_up, w_down):
    b, s, d = m.shape
    xf = m.reshape(-1, d)
    n = xf.shape[0]
    g_prob = jax.nn.softmax((xf @ w_group).astype(jnp.float32) + b_group, axis=-1)
    g_w, g_idx = lax.top_k(g_prob, 1)
    e_logits = ((xf @ w_expert).astype(jnp.float32) + b_expert).reshape(n, MOE_GROUPS, MOE_PER_GROUP)
    e_logits = e_logits[jnp.arange(n), g_idx[:, 0]]
    e_prob = jax.nn.softmax(e_logits, axis=-1)
    e_w, e_idx = lax.top_k(e_prob, MOE_TOP_K)
    weights = g_w * e_w / jnp.sum(e_w, axis=-1, keepdims=True)
    experts = g_idx * MOE_PER_GROUP + e_idx
    return _expert_dispatch(xf, experts, weights, w_gate, w_up, w_down).reshape(b, s, d)


def _per_layer_embedding(h, p_i, w_proj, norm_g, w_gate, b_gate):
    gate = jax.nn.sigmoid(_rms_norm(h, norm_g) @ w_gate + b_gate)
    return gate * (p_i @ w_proj)


def setup_inputs(seed: int = 0) -> dict:
    key = jax.random.key(seed)
    ks = iter(jax.random.split(key, 64))
    f32 = jnp.float32

    def nrm(shape, scale):
        return jax.random.normal(next(ks), shape, f32) * scale

    def gain(shape):
        return 1.0 + nrm(shape, 0.02)

    def unif(shape, lo, hi):
        return jax.random.uniform(next(ks), shape, f32, minval=lo, maxval=hi)

    x = nrm((BATCH, SEQ, D_MODEL), 1.0)
    p = nrm((DEPTH, BATCH, SEQ, PLE_DIM), 1.0)
    positions = jnp.broadcast_to(jnp.arange(SEQ, dtype=jnp.int32), (BATCH, SEQ))
    n_idx = jnp.arange(S5_STATE, dtype=f32)
    log_lo, log_hi = math.log(0.001), math.log(0.1)
    dt = jnp.exp(unif((N_ODD, GDN_HEADS), log_lo, log_hi))
    return {
        'x': x,
        'p': p,
        'positions': positions,
        'norm_mix': gain((DEPTH, D_MODEL)),
        'norm_ffn': gain((DEPTH, D_MODEL)),
        'ev_w_in': nrm((N_EVEN, D_MODEL, EVEN_IN), D_MODEL ** -0.5),
        'fox_b_f': unif((N_EVEN, FOX_HEADS), 1.0, 5.0),
        'fox_q_norm': gain((N_EVEN, FOX_HEAD_DIM)),
        'fox_k_norm': gain((N_EVEN, FOX_HEAD_DIM)),
        'mla_q_a_norm': gain((N_EVEN, MLA_Q_LORA)),
        'mla_w_q_up': nrm((N_EVEN, MLA_Q_LORA, MLA_HEADS * (MLA_NOPE + MLA_ROPE)), MLA_Q_LORA ** -0.5),
        'mla_kv_a_norm': gain((N_EVEN, MLA_KV_LORA)),
        'mla_w_kv_up': nrm((N_EVEN, MLA_KV_LORA, MLA_HEADS * (MLA_NOPE + MLA_V)), MLA_KV_LORA ** -0.5),
        'mla_q_norm': gain((N_EVEN, MLA_NOPE + MLA_ROPE)),
        'mla_k_norm': gain((N_EVEN, MLA_NOPE + MLA_ROPE)),
        'ev_w_out': nrm((N_EVEN, EVEN_MIX, D_MODEL), EVEN_MIX ** -0.5),
        'od_w_in': nrm((N_ODD, D_MODEL, ODD_IN), D_MODEL ** -0.5),
        's5_a_re': -0.5 + nrm((N_ODD, S5_GROUPS, S5_STATE), 0.01),
        's5_a_im': math.pi * n_idx + nrm((N_ODD, S5_GROUPS, S5_STATE), 0.01),
        's5_b_re': nrm((N_ODD, S5_GROUPS, S5_STATE, S5_GROUP_CH), (2 * S5_GROUP_CH) ** -0.5),
        's5_b_im': nrm((N_ODD, S5_GROUPS, S5_STATE, S5_GROUP_CH), (2 * S5_GROUP_CH) ** -0.5),
        's5_c_re': nrm((N_ODD, S5_GROUPS, S5_GROUP_CH, S5_STATE), 0.25),
        's5_c_im': nrm((N_ODD, S5_GROUPS, S5_GROUP_CH, S5_STATE), 0.25),
        's5_d': nrm((N_ODD, S5_CH), 0.5),
        's5_log_step': unif((N_ODD, S5_GROUPS), log_lo, log_hi),
        's5_w_glu': nrm((N_ODD, S5_CH, S5_CH), S5_CH ** -0.5),
        's5_b_glu': nrm((N_ODD, S5_CH), 0.01),
        'gdn_conv_w': nrm((N_ODD, GDN_CONV, 3 * GDN_W), 0.5),
        'gdn_a_log': jnp.log(unif((N_ODD, GDN_HEADS), 1.0, 16.0)),
        'gdn_dt_bias': dt + jnp.log(-jnp.expm1(-dt)),
        'gdn_o_norm': gain((N_ODD, GDN_HEAD_DIM)),
        'od_w_out': nrm((N_ODD, ODD_MIX, D_MODEL), ODD_MIX ** -0.5),
        'moe_w_group': nrm((DEPTH, D_MODEL, MOE_GROUPS), D_MODEL ** -0.5),
        'moe_b_group': nrm((DEPTH, MOE_GROUPS), 0.01),
        'moe_w_expert': nrm((DEPTH, D_MODEL, N_EXPERTS), D_MODEL ** -0.5),
        'moe_b_expert': nrm((DEPTH, N_EXPERTS), 0.01),
        'moe_w_gate': nrm((DEPTH, N_EXPERTS, D_MODEL, MOE_FF), D_MODEL ** -0.5),
        'moe_w_up': nrm((DEPTH, N_EXPERTS, D_MODEL, MOE_FF), D_MODEL ** -0.5),
        'moe_w_down': nrm((DEPTH, N_EXPERTS, MOE_FF, D_MODEL), MOE_FF ** -0.5),
        'ple_w_proj': nrm((DEPTH, PLE_DIM, D_MODEL), PLE_DIM ** -0.5),
        'ple_norm': gain((DEPTH, D_MODEL)),
        'ple_w_gate': nrm((DEPTH, D_MODEL, D_MODEL), D_MODEL ** -0.5),
        'ple_b_gate': nrm((DEPTH, D_MODEL), 0.01),
    }


def reference(x, p, positions, norm_mix, norm_ffn, ev_w_in, fox_b_f, fox_q_norm, fox_k_norm,
              mla_q_a_norm, mla_w_q_up, mla_kv_a_norm, mla_w_kv_up, mla_q_norm, mla_k_norm, ev_w_out,
              od_w_in, s5_a_re, s5_a_im, s5_b_re, s5_b_im, s5_c_re, s5_c_im, s5_d, s5_log_step,
              s5_w_glu, s5_b_glu, gdn_conv_w, gdn_a_log, gdn_dt_bias, gdn_o_norm, od_w_out,
              moe_w_group, moe_b_group, moe_w_expert, moe_b_expert, moe_w_gate, moe_w_up, moe_w_down,
              ple_w_proj, ple_norm, ple_w_gate, ple_b_gate):
    h = x
    for i in range(DEPTH):
        a = _rms_norm(h, norm_mix[i])
        j = i // 2
        if i % 2 == 0:
            h = h + _even_mixer(a, positions, ev_w_in[j], fox_b_f[j], fox_q_norm[j], fox_k_norm[j],
                                mla_q_a_norm[j], mla_w_q_up[j], mla_kv_a_norm[j], mla_w_kv_up[j],
                                mla_q_norm[j], mla_k_norm[j], ev_w_out[j])
        else:
            h = h + _odd_mixer(a, od_w_in[j], s5_a_re[j], s5_a_im[j], s5_b_re[j], s5_b_im[j],
                               s5_c_re[j], s5_c_im[j], s5_d[j], s5_log_step[j], s5_w_glu[j], s5_b_glu[j],
                               gdn_conv_w[j], gdn_a_log[j], gdn_dt_bias[j], gdn_o_norm[j], od_w_out[j])
        h = h + _hier_moe(_rms_norm(h, norm_ffn[i]), moe_w_group[i], moe_b_group[i], moe_w_expert[i],
                          moe_b_expert[i], moe_w_gate[i], moe_w_up[i], moe_w_down[i])
        h = h + _per_layer_embedding(h, p[i], ple_w_proj[i], ple_norm[i], ple_w_gate[i], ple_b_gate[i])
    return h
```

```python
import functools
import math

import numpy as np
import jax
import jax.numpy as jnp
from jax import lax
from jax.experimental import pallas as pl
from jax.experimental.pallas import tpu as pltpu

F32 = jnp.float32
BF16 = jnp.bfloat16
HI = lax.Precision.HIGHEST

LANES = 128
RMS_EPS = 1e-6
ROPE_THETA = 10000.0

FOX_HEADS = 8
FOX_HEAD_DIM = 64
MLA_HEADS = 8
MLA_Q_LORA = 384
MLA_KV_LORA = 256
MLA_NOPE = 64
MLA_ROPE = 32
MLA_V = 64
MLA_QK = MLA_NOPE + MLA_ROPE

S5_CH = 512
S5_GROUP_CH = 16
S5_GROUPS = S5_CH // S5_GROUP_CH
S5_STATE = 64
S5_N = S5_GROUPS * S5_STATE

GDN_HEADS = 4
GDN_HEAD_DIM = 128
GDN_W = GDN_HEADS * GDN_HEAD_DIM
GDN_CONV = 4
GDN_CHUNK = 64

MOE_GROUPS = 4
MOE_PER_GROUP = 8
N_EXPERTS = MOE_GROUPS * MOE_PER_GROUP
MOE_ROWS = 128

VMEM_LIMIT = 56 * 1024 * 1024


def _tile(n, pref):
    t = min(n, pref)
    assert n % t == 0, (n, t)
    return t


def _params(sem):
    return pltpu.CompilerParams(dimension_semantics=sem, vmem_limit_bytes=VMEM_LIMIT)


def _full(shape):
    nd = len(shape)
    return pl.BlockSpec(shape, lambda *_: (0,) * nd)


def _rms(x, g):
    return x * lax.rsqrt(jnp.mean(x * x, axis=-1, keepdims=True) + RMS_EPS) * g


def _bdot(a, b):
    return jnp.dot(a.astype(BF16), b.astype(BF16), preferred_element_type=F32)


def _bdot_nt(a, b):
    return lax.dot_general(a.astype(BF16), b.astype(BF16), (((1,), (1,)), ((), ())),
                           preferred_element_type=F32)


def _bdot_tn(a, b):
    return lax.dot_general(a.astype(BF16), b.astype(BF16), (((0,), (0,)), ((), ())),
                           preferred_element_type=F32)


def _hdot(a, b):
    return jnp.dot(a, b, precision=HI, preferred_element_type=F32)


def _split_dot(x, ind):
    hi = x.astype(BF16)
    lo = (x - hi.astype(F32)).astype(BF16)
    return (jnp.dot(hi, ind, preferred_element_type=F32)
            + jnp.dot(lo, ind, preferred_element_type=F32))


def _log_sigmoid(x):
    return jnp.minimum(x, 0.0) - jnp.log(1.0 + jnp.exp(-jnp.abs(x)))


def _softplus(x):
    return jnp.maximum(x, 0.0) + jnp.log(1.0 + jnp.exp(-jnp.abs(x)))


def _silu(x):
    return x * jax.nn.sigmoid(x)


def _head_norm128(x, nheads, denom, gain):
    outs = []
    for hh in range(nheads):
        xh = x[:, LANES * hh:LANES * (hh + 1)]
        ss = jnp.sum(xh * xh, axis=-1, keepdims=True)
        outs.append(xh * lax.rsqrt(ss / denom + RMS_EPS))
    return jnp.concatenate(outs, axis=1) * gain


def _even_pre_kernel(h_ref, pos_ref, nmix_ref, win_ref, ind_ref, fqn_ref, fkn_ref, bf_ref,
                     qan_ref, wq_ref, kvan_ref, wkv_ref, mqn_ref, mkn_ref, freq_ref, s1_ref, s2_ref,
                     tri_ref, fq_o, fk_o, fv_o, cum_o, mq_o, mk_o, mv_o, carry_ref):
    t = pl.program_id(1)

    @pl.when(t == 0)
    def _():
        carry_ref[...] = jnp.zeros_like(carry_ref)

    tm = h_ref.shape[1]
    a = _rms(h_ref[0], nmix_ref[...])
    proj = _bdot(a, win_ref[...])
    nf = FOX_HEADS * FOX_HEAD_DIM
    fq = proj[:, 0:nf]
    fk = proj[:, nf:2 * nf]
    fv = proj[:, 2 * nf:3 * nf]
    o_cq = 3 * nf
    cq = proj[:, o_cq:o_cq + MLA_Q_LORA]
    o_ckv = o_cq + MLA_Q_LORA
    ckv = proj[:, o_ckv:o_ckv + MLA_KV_LORA]
    misc = proj[:, o_ckv + MLA_KV_LORA:]

    ind = ind_ref[...]
    fq_n = fq * lax.rsqrt(_split_dot(fq * fq, ind) / FOX_HEAD_DIM + RMS_EPS) * fqn_ref[...]
    fk_n = fk * lax.rsqrt(_split_dot(fk * fk, ind) / FOX_HEAD_DIM + RMS_EPS) * fkn_ref[...]
    fq_o[0] = (fq_n * (FOX_HEAD_DIM ** -0.5)).astype(BF16)
    fk_o[0] = fk_n.astype(BF16)
    fv_o[0] = fv.astype(BF16)

    lane = lax.broadcasted_iota(jnp.int32, (tm, LANES), 1)
    logf = jnp.where(lane < FOX_HEADS, _log_sigmoid(misc + bf_ref[...]), 0.0)
    cum = _hdot(tri_ref[...], logf) + carry_ref[...]
    carry_ref[...] = cum[tm - 1:tm, :]
    cum_o[0] = cum

    ang = pos_ref[0].astype(F32) * freq_ref[...]
    cos1 = jnp.cos(ang)
    sin1 = jnp.sin(ang)
    cos = jnp.concatenate([cos1] * MLA_HEADS, axis=1)
    sin_a = jnp.concatenate([sin1 * s1_ref[...]] * MLA_HEADS, axis=1)
    sin_b = jnp.concatenate([sin1 * s2_ref[...]] * MLA_HEADS, axis=1)
    width = MLA_HEADS * LANES
    half = MLA_ROPE // 2

    def rope(x):
        return (x * cos + pltpu.roll(x, width - half, 1) * sin_a + pltpu.roll(x, half, 1) * sin_b)

    q = _bdot(_rms(cq, qan_ref[...]), wq_ref[...])
    q = rope(_head_norm128(q, MLA_HEADS, MLA_QK, mqn_ref[...]))
    mq_o[0] = (q * (MLA_QK ** -0.5)).astype(BF16)

    kv = _bdot(_rms(ckv, kvan_ref[...]), wkv_ref[...])
    kr = pltpu.roll(misc, MLA_NOPE - FOX_HEADS, 1)
    kr = jnp.where((lane >= MLA_NOPE) & (lane < MLA_QK), kr, 0.0)
    k = kv[:, :width] + jnp.concatenate([kr] * MLA_HEADS, axis=1)
    k = rope(_head_norm128(k, MLA_HEADS, MLA_QK, mkn_ref[...]))
    mk_o[0] = k.astype(BF16)
    mv_o[0] = kv[:, width:].astype(BF16)


def _even_pre(h, positions, norm_mix, w_in, b_f, fox_qn, fox_kn, q_a_norm, w_q_up, kv_a_norm, w_kv_up,
              mla_qn, mla_kn):
    b, s, d = h.shape
    tm = _tile(s, 256)
    nf = FOX_HEADS * FOX_HEAD_DIM
    sizes = (nf, nf, nf, FOX_HEADS, MLA_Q_LORA, MLA_KV_LORA, MLA_ROPE)
    offs = np.concatenate([[0], np.cumsum(sizes)])
    parts = [w_in[:, offs[i]:offs[i + 1]] for i in range(len(sizes))]
    pad = jnp.zeros((d, LANES - FOX_HEADS - MLA_ROPE), w_in.dtype)
    win = jnp.concatenate([parts[0], parts[1], parts[2], parts[4], parts[5], parts[3], parts[6], pad],
                          axis=1).astype(BF16)
    ncol = win.shape[1]
    gidx = np.arange(nf) // FOX_HEAD_DIM
    ind = jnp.asarray(gidx[:, None] == gidx[None, :], BF16)
    fqn = jnp.tile(fox_qn, FOX_HEADS)[None, :]
    fkn = jnp.tile(fox_kn, FOX_HEADS)[None, :]
    bf = jnp.zeros((1, LANES), F32).at[0, :FOX_HEADS].set(b_f)
    padq = LANES - MLA_QK
    wq = jnp.pad(w_q_up.reshape(MLA_Q_LORA, MLA_HEADS, MLA_QK), ((0, 0), (0, 0), (0, padq)))
    wq = wq.reshape(MLA_Q_LORA, MLA_HEADS * LANES).astype(BF16)
    wkv3 = w_kv_up.reshape(MLA_KV_LORA, MLA_HEADS, MLA_NOPE + MLA_V)
    wk = jnp.pad(wkv3[:, :, :MLA_NOPE], ((0, 0), (0, 0), (0, LANES - MLA_NOPE)))
    wkv = jnp.concatenate([wk.reshape(MLA_KV_LORA, MLA_HEADS * LANES),
                           wkv3[:, :, MLA_NOPE:].reshape(MLA_KV_LORA, MLA_HEADS * MLA_V)],
                          axis=1).astype(BF16)
    mqn = jnp.tile(jnp.pad(mla_qn, (0, padq)), MLA_HEADS)[None, :]
    mkn = jnp.tile(jnp.pad(mla_kn, (0, padq)), MLA_HEADS)[None, :]
    half = MLA_ROPE // 2
    inv = ROPE_THETA ** (-jnp.arange(half, dtype=F32) * 2.0 / MLA_ROPE)
    freq = jnp.zeros((1, LANES), F32).at[0, MLA_NOPE:MLA_NOPE + half].set(inv)
    freq = freq.at[0, MLA_NOPE + half:MLA_QK].set(inv)
    s1 = jnp.zeros((1, LANES), F32).at[0, MLA_NOPE:MLA_NOPE + half].set(-1.0)
    s2 = jnp.zeros((1, LANES), F32).at[0, MLA_NOPE + half:MLA_QK].set(1.0)
    tri = jnp.asarray(np.tril(np.ones((tm, tm), np.float32)))
    pos3 = positions.reshape(b, s, 1)

    row = lambda n: pl.BlockSpec((1, tm, n), lambda bi, ti: (bi, ti, 0))
    consts = [norm_mix[None, :], win, ind, fqn, fkn, bf, q_a_norm[None, :], wq, kv_a_norm[None, :], wkv,
              mqn, mkn, freq, s1, s2, tri]
    out_shape = [jax.ShapeDtypeStruct((b, s, nf), BF16)] * 3 + [
        jax.ShapeDtypeStruct((b, s, LANES), F32),
        jax.ShapeDtypeStruct((b, s, MLA_HEADS * LANES), BF16),
        jax.ShapeDtypeStruct((b, s, MLA_HEADS * LANES), BF16),
        jax.ShapeDtypeStruct((b, s, MLA_HEADS * MLA_V), BF16)]
    return pl.pallas_call(
        _even_pre_kernel,
        grid=(b, s // tm),
        in_specs=[row(d), row(1)] + [_full(c.shape) for c in consts],
        out_specs=[row(nf), row(nf), row(nf), row(LANES), row(MLA_HEADS * LANES), row(MLA_HEADS * LANES),
                   row(MLA_HEADS * MLA_V)],
        out_shape=out_shape,
        scratch_shapes=[pltpu.VMEM((1, LANES), F32)],
        compiler_params=_params(("arbitrary", "arbitrary")),
        name="even_pre",
    )(h, pos3, *consts)


def _attn_kernel(*refs, tq, fox):
    if fox:
        q_ref, k_ref, v_ref, cc_ref, cr_ref, o_ref = refs
    else:
        q_ref, k_ref, v_ref, o_ref = refs
    hp = pl.program_id(1)
    i = pl.program_id(2)
    lane = lax.broadcasted_iota(jnp.int32, (tq, LANES), 1)
    rowi = lax.broadcasted_iota(jnp.int32, (tq, tq), 0)
    coli = lax.broadcasted_iota(jnp.int32, (tq, tq), 1)
    outs = []
    for hh in range(2):
        if fox:
            in_head = (lane >= FOX_HEAD_DIM * hh) & (lane < FOX_HEAD_DIM * (hh + 1))
            qh = jnp.where(in_head, q_ref[0], jnp.zeros((), BF16))
            hidx = 2 * hp + hh
            cq = jnp.sum(jnp.where(lane == hidx, cc_ref[0], 0.0), axis=-1, keepdims=True)
        else:
            qh = q_ref[0, :, LANES * hh:LANES * (hh + 1)]

        def step(j, carry, masked):
            m, l, acc = carry
            koff = pl.multiple_of(j * tq, tq)
            if fox:
                kj = k_ref[0, pl.ds(koff, tq), :]
            else:
                kj = k_ref[0, pl.ds(koff, tq), LANES * hh:LANES * (hh + 1)]
            sc = lax.dot_general(qh, kj, (((1,), (1,)), ((), ())), preferred_element_type=F32)
            if fox:
                sc = sc + (cq - cr_ref[0, pl.ds(hidx, 1), pl.ds(koff, tq)])
            if masked:
                sc = jnp.where(coli <= rowi, sc, -jnp.inf)
            m_new = jnp.maximum(m, jnp.max(sc, axis=-1, keepdims=True))
            alpha = jnp.exp(m - m_new)
            p = jnp.exp(sc - m_new)
            l = alpha * l + jnp.sum(p, axis=-1, keepdims=True)
            acc = alpha * acc + jnp.dot(p.astype(BF16), v_ref[0, pl.ds(koff, tq), :],
                                        preferred_element_type=F32)
            return m_new, l, acc

        init = (jnp.full((tq, 1), -jnp.inf, F32), jnp.zeros((tq, 1), F32), jnp.zeros((tq, LANES), F32))
        carry = lax.fori_loop(0, i, functools.partial(step, masked=False), init)
        _, l, acc = step(i, carry, True)
        outs.append(acc / l)
    o_ref[0] = jnp.where(lane < MLA_V, outs[0], outs[1]).astype(o_ref.dtype)


def _attention(q, k, v, cum_col=None, cum_row=None):
    b, s, _ = v.shape
    fox = cum_col is not None
    qw = LANES if fox else 2 * LANES
    tq = _tile(s, 512)
    npairs = v.shape[2] // LANES
    in_specs = [pl.BlockSpec((1, tq, qw), lambda bi, hp, i: (bi, i, hp)),
                pl.BlockSpec((1, s, qw), lambda bi, hp, i: (bi, 0, hp)),
                pl.BlockSpec((1, s, LANES), lambda bi, hp, i: (bi, 0, hp))]
    args = [q, k, v]
    if fox:
        in_specs += [pl.BlockSpec((1, tq, LANES), lambda bi, hp, i: (bi, i, 0)),
                     pl.BlockSpec((1, FOX_HEADS, s), lambda bi, hp, i: (bi, 0, 0))]
        args += [cum_col, cum_row]
    return pl.pallas_call(
        functools.partial(_attn_kernel, tq=tq, fox=fox),
        grid=(b, npairs, s // tq),
        in_specs=in_specs,
        out_specs=pl.BlockSpec((1, tq, LANES), lambda bi, hp, i: (bi, i, hp)),
        out_shape=jax.ShapeDtypeStruct((b, s, npairs * LANES), BF16),
        compiler_params=_params(("arbitrary", "arbitrary", "arbitrary")),
        name="fox_attention" if fox else "mla_attention",
    )(*args)


def _proj2_kernel(a_ref, b_ref, wa_ref, wb_ref, h_ref, o_ref):
    o_ref[0] = (h_ref[0] + jnp.dot(a_ref[0], wa_ref[...], preferred_element_type=F32)
                + jnp.dot(b_ref[0], wb_ref[...], preferred_element_type=F32))


def _proj2_residual(a, bb, w_out, h):
    b, s, d = h.shape
    na = a.shape[2]
    nb = bb.shape[2]
    tm = _tile(s, 512)
    wa = w_out[:na].astype(BF16)
    wb = w_out[na:].astype(BF16)
    row = lambda n: pl.BlockSpec((1, tm, n), lambda bi, ti: (bi, ti, 0))
    return pl.pallas_call(
        _proj2_kernel,
        grid=(b, s // tm),
        in_specs=[row(na), row(nb), _full(wa.shape), _full(wb.shape), row(d)],
        out_specs=row(d),
        out_shape=jax.ShapeDtypeStruct((b, s, d), F32),
        compiler_params=_params(("arbitrary", "arbitrary")),
        name="out_proj",
    )(a, bb, wa, wb, h)


def _odd_pre_kernel(h_ref, nmix_ref, win_ref, u_o, qkv_o, z_o, gb_o):
    a = _rms(h_ref[0], nmix_ref[...])
    proj = _bdot(a, win_ref[...])
    o1 = S5_CH
    o2 = o1 + 3 * GDN_W
    o3 = o2 + GDN_W
    u_o[0] = proj[:, :o1]
    qkv_o[0] = proj[:, o1:o2]
    z_o[0] = proj[:, o2:o3]
    gb_o[0] = proj[:, o3:]


def _odd_pre(h, norm_mix, w_in):
    b, s, d = h.shape
    tm = _tile(s, 256)
    sizes = (S5_CH, 3 * GDN_W, GDN_HEADS, GDN_HEADS, GDN_W)
    offs = np.concatenate([[0], np.cumsum(sizes)])
    parts = [w_in[:, offs[i]:offs[i + 1]] for i in range(len(sizes))]
    pad = jnp.zeros((d, LANES - 2 * GDN_HEADS), w_in.dtype)
    win = jnp.concatenate([parts[0], parts[1], parts[4], parts[2], parts[3], pad], axis=1).astype(BF16)
    row = lambda n: pl.BlockSpec((1, tm, n), lambda bi, ti: (bi, ti, 0))
    widths = (S5_CH, 3 * GDN_W, GDN_W, LANES)
    return pl.pallas_call(
        _odd_pre_kernel,
        grid=(b, s // tm),
        in_specs=[row(d), _full((1, d)), _full(win.shape)],
        out_specs=[row(n) for n in widths],
        out_shape=[jax.ShapeDtypeStruct((b, s, n), F32) for n in widths],
        compiler_params=_params(("arbitrary", "arbitrary")),
        name="odd_pre",
    )(h, norm_mix[None, :], win)


def _s5_kernel(u_ref, bbd_ref, cbd_ref, ar_ref, ai_ref, d_ref, wglu_ref, bglu_ref, o_ref,
               x_ref, sr_ref, si_ref):
    t = pl.program_id(1)

    @pl.when(t == 0)
    def _():
        sr_ref[...] = jnp.zeros_like(sr_ref)
        si_ref[...] = jnp.zeros_like(si_ref)

    tm = u_ref.shape[1]
    u = u_ref[0]
    x_ref[...] = _bdot(u, bbd_ref[...])
    ar = ar_ref[...]
    ai = ai_ref[...]

    def step(r, carry):
        xr, xi = carry
        br = x_ref[pl.ds(r, 1), 0:S5_N]
        bi = x_ref[pl.ds(r, 1), S5_N:2 * S5_N]
        nr = ar * xr - ai * xi + br
        ni = ar * xi + ai * xr + bi
        x_ref[pl.ds(r, 1), 0:S5_N] = nr
        x_ref[pl.ds(r, 1), S5_N:2 * S5_N] = ni
        return nr, ni

    xr, xi = lax.fori_loop(0, tm, step, (sr_ref[...], si_ref[...]), unroll=8)
    sr_ref[...] = xr
    si_ref[...] = xi
    y = _bdot(x_ref[...], cbd_ref[...]) + d_ref[...] * u
    hg = jax.nn.gelu(y)
    o_ref[0] = (hg * jax.nn.sigmoid(_bdot(hg, wglu_ref[...]) + bglu_ref[...])).astype(o_ref.dtype)


def _s5(u, a_re, a_im, b_re, b_im, c_re, c_im, d_skip, log_step, w_glu, b_glu):
    b, s, _ = u.shape
    tm = _tile(s, 256)
    lam_re = jnp.minimum(a_re, -1e-4)
    lam_im = a_im
    dt = jnp.exp(log_step)[:, None]
    mag = jnp.exp(lam_re * dt)
    ab_re = mag * jnp.cos(lam_im * dt)
    ab_im = mag * jnp.sin(lam_im * dt)
    den = lam_re * lam_re + lam_im * lam_im
    nr, ni = ab_re - 1.0, ab_im
    gam_re = (nr * lam_re + ni * lam_im) / den
    gam_im = (ni * lam_re - nr * lam_im) / den
    bb_re = gam_re[..., None] * b_re - gam_im[..., None] * b_im
    bb_im = gam_re[..., None] * b_im + gam_im[..., None] * b_re
    eye = jnp.eye(S5_GROUPS, dtype=F32)
    bd_in = lambda m: jnp.einsum('gpc,gh->gchp', m, eye).reshape(S5_CH, S5_N)
    bd_out = lambda m: jnp.einsum('gcp,gh->gphc', m, eye).reshape(S5_N, S5_CH)
    bbd = jnp.concatenate([bd_in(bb_re), bd_in(bb_im)], axis=1).astype(BF16)
    cbd = jnp.concatenate([bd_out(c_re), -bd_out(c_im)], axis=0).astype(BF16)
    consts = [bbd, cbd, ab_re.reshape(1, S5_N), ab_im.reshape(1, S5_N), d_skip[None, :],
              w_glu.astype(BF16), b_glu[None, :]]
    row = pl.BlockSpec((1, tm, S5_CH), lambda bi, ti: (bi, ti, 0))
    return pl.pallas_call(
        _s5_kernel,
        grid=(b, s // tm),
        in_specs=[row] + [_full(c.shape) for c in consts],
        out_specs=row,
        out_shape=jax.ShapeDtypeStruct((b, s, S5_CH), BF16),
        scratch_shapes=[pltpu.VMEM((tm, 2 * S5_N), F32), pltpu.VMEM((1, S5_N), F32),
                        pltpu.VMEM((1, S5_N), F32)],
        compiler_params=_params(("arbitrary", "arbitrary")),
        name="s5",
    )(u, *consts)


def _gdn_kernel(x_ref, z_ref, gb_ref, cw_ref, nega_ref, dtb_ref, onorm_ref, tril_ref, triu_ref, o_ref,
                xpad_ref, state_ref):
    t = pl.program_id(1)
    tm = x_ref.shape[1]
    c = GDN_CHUNK
    hd = GDN_HEAD_DIM

    @pl.when(t == 0)
    def _():
        xpad_ref[0:8, :] = jnp.zeros((8, xpad_ref.shape[1]), F32)
        state_ref[...] = jnp.zeros_like(state_ref)

    @pl.when(t > 0)
    def _():
        xpad_ref[0:8, :] = xpad_ref[tm:tm + 8, :]

    xpad_ref[8:tm + 8, :] = x_ref[0]
    conv = cw_ref[0:1, :] * xpad_ref[pl.ds(8 - (GDN_CONV - 1), tm), :]
    for i in range(1, GDN_CONV):
        conv = conv + cw_ref[i:i + 1, :] * xpad_ref[pl.ds(8 - (GDN_CONV - 1) + i, tm), :]
    act = _silu(conv)

    def l2n(x):
        return x * lax.rsqrt(jnp.sum(x * x, axis=-1, keepdims=True) + RMS_EPS)

    gb = gb_ref[0]
    g = nega_ref[...] * _softplus(gb + dtb_ref[...])
    beta = jax.nn.sigmoid(gb)
    gc = _hdot(tril_ref[...], g)
    gct = _hdot(g.T, triu_ref[...])

    ri = lax.broadcasted_iota(jnp.int32, (c, c), 0)
    ci = lax.broadcasted_iota(jnp.int32, (c, c), 1)
    incl = ri >= ci
    strict = ri > ci
    eye = (ri == ci).astype(F32)
    z = z_ref[0]

    for n in range(tm // c):
        r0 = n * c
        for hh in range(GDN_HEADS):
            lo = hh * hd
            q = l2n(act[r0:r0 + c, lo:lo + hd]) * (hd ** -0.5)
            k = l2n(act[r0:r0 + c, GDN_W + lo:GDN_W + lo + hd])
            v = act[r0:r0 + c, 2 * GDN_W + lo:2 * GDN_W + lo + hd]
            bcol = beta[r0:r0 + c, GDN_HEADS + hh:GDN_HEADS + hh + 1]
            gcol = gc[r0:r0 + c, hh:hh + 1]
            grow = gct[hh:hh + 1, r0:r0 + c]
            decay = jnp.where(incl, jnp.exp(jnp.where(incl, gcol - grow, 0.0)), 0.0)
            kb = k * bcol
            a_mat = jnp.where(strict, _bdot_nt(kb, k) * decay, 0.0)
            t_mat = eye
            bs = 1
            while bs < c:
                off = ((ri // (2 * bs)) == (ci // (2 * bs))) & ((ri % (2 * bs)) >= bs) & ((ci % (2 * bs)) < bs)
                t_mat = t_mat - _hdot(_hdot(t_mat, jnp.where(off, a_mat, 0.0)), t_mat)
                bs *= 2
            eg = jnp.exp(gcol)
            u = _bdot(t_mat, v * bcol)
            w = _bdot(t_mat, kb * eg)
            intra = jnp.where(incl, _bdot_nt(q, k) * decay, 0.0)
            glast = gcol[c - 1:c, :]
            state = state_ref[hh]
            v_new = u - _bdot(w, state)
            o = _bdot(q * eg, state) + _bdot(intra, v_new)
            state_ref[hh] = state * jnp.exp(glast) + _bdot_tn(k * jnp.exp(glast - gcol), v_new)
            on = o * lax.rsqrt(jnp.mean(o * o, axis=-1, keepdims=True) + RMS_EPS) * onorm_ref[...]
            o_ref[0, r0:r0 + c, lo:lo + hd] = (on * _silu(z[r0:r0 + c, lo:lo + hd])).astype(o_ref.dtype)


def _gdn(qkv, z, gb, conv_w, a_log, dt_bias, o_norm):
    b, s, cw = qkv.shape
    tm = _tile(s, 256)
    nega = jnp.zeros((1, LANES), F32).at[0, :GDN_HEADS].set(-jnp.exp(a_log))
    dtb = jnp.zeros((1, LANES), F32).at[0, :GDN_HEADS].set(dt_bias)
    cwp = jnp.pad(conv_w, ((0, 8 - GDN_CONV), (0, 0)))
    ridx = np.arange(tm)
    same = (ridx[:, None] // GDN_CHUNK) == (ridx[None, :] // GDN_CHUNK)
    tril = jnp.asarray((same & (ridx[:, None] >= ridx[None, :])).astype(np.float32))
    triu = jnp.asarray((same & (ridx[:, None] <= ridx[None, :])).astype(np.float32))
    consts = [cwp, nega, dtb, o_norm[None, :], tril, triu]
    row = lambda n: pl.BlockSpec((1, tm, n), lambda bi, ti: (bi, ti, 0))
    return pl.pallas_call(
        _gdn_kernel,
        grid=(b, s // tm),
        in_specs=[row(cw), row(GDN_W), row(LANES)] + [_full(c.shape) for c in consts],
        out_specs=row(GDN_W),
        out_shape=jax.ShapeDtypeStruct((b, s, GDN_W), BF16),
        scratch_shapes=[pltpu.VMEM((tm + 8, cw), F32), pltpu.VMEM((GDN_HEADS, GDN_HEAD_DIM, GDN_HEAD_DIM), F32)],
        compiler_params=_params(("arbitrary", "arbitrary")),
        name="gdn",
    )(qkv, z, gb, *consts)


def _router_kernel(h_ref, g_ref, wr_ref, br_ref, tri_ref, m_o, keyt_o, wt_o, cnt_o):
    tm = h_ref.shape[0]
    m = _rms(h_ref[...], g_ref[...])
    m_o[...] = m.astype(BF16)
    logits = _hdot(m, wr_ref[...]) + br_ref[...]
    lane = lax.broadcasted_iota(jnp.int32, (tm, LANES), 1)
    neg = -jnp.inf

    def first_argmax(x):
        mx = jnp.max(x, axis=-1, keepdims=True)
        idx = jnp.min(jnp.where(x == mx, lane, LANES), axis=-1, keepdims=True)
        return mx, idx

    is_g = (lane >= N_EXPERTS) & (lane < N_EXPERTS + MOE_GROUPS)
    gl = jnp.where(is_g, logits, neg)
    gmax, gidx = first_argmax(gl)
    g_w = 1.0 / jnp.sum(jnp.where(is_g, jnp.exp(gl - gmax), 0.0), axis=-1, keepdims=True)
    in_group = (lane // MOE_PER_GROUP) == (gidx - N_EXPERTS)
    el = jnp.where(in_group & (lane < N_EXPERTS), logits, neg)
    m1, i1 = first_argmax(el)
    el2 = jnp.where(lane == i1, neg, el)
    m2, i2 = first_argmax(el2)
    r = jnp.exp(m2 - m1)
    w1 = g_w / (1.0 + r)
    w2 = g_w * r / (1.0 + r)
    chose = (lane == i1) | (lane == i2)
    wmat = jnp.where(lane == i1, w1, jnp.where(lane == i2, w2, 0.0))
    ch = chose.astype(F32)
    rank = jnp.dot(tri_ref[...], ch.astype(BF16), preferred_element_type=F32)
    key = jnp.where(chose, rank, -1.0)
    keyt_o[0] = key.T
    wt_o[0] = wmat.T
    cnt_o[0] = jnp.sum(ch, axis=0, keepdims=True).astype(jnp.int32)


def _moe_router(hf, norm_g, w_group, b_group, w_expert, b_expert, tb):
    n, d = hf.shape
    nblk = n // tb
    wr = jnp.zeros((d, LANES), F32).at[:, :N_EXPERTS].set(w_expert)
    wr = wr.at[:, N_EXPERTS:N_EXPERTS + MOE_GROUPS].set(w_group)
    br = jnp.zeros((1, LANES), F32).at[0, :N_EXPERTS].set(b_expert)
    br = br.at[0, N_EXPERTS:N_EXPERTS + MOE_GROUPS].set(b_group)
    tri = jnp.asarray(np.tril(np.ones((tb, tb), np.float32), -1), BF16)
    blk = pl.BlockSpec((1, LANES, tb), lambda i: (i, 0, 0))
    return pl.pallas_call(
        _router_kernel,
        grid=(nblk,),
        in_specs=[pl.BlockSpec((tb, d), lambda i: (i, 0)), _full((1, d)), _full(wr.shape), _full(br.shape),
                  _full(tri.shape)],
        out_specs=[pl.BlockSpec((tb, d), lambda i: (i, 0)), blk, blk,
                   pl.BlockSpec((1, 1, LANES), lambda i: (i, 0, 0))],
        out_shape=[jax.ShapeDtypeStruct((n, d), BF16), jax.ShapeDtypeStruct((nblk, LANES, tb), F32),
                   jax.ShapeDtypeStruct((nblk, LANES, tb), F32),
                   jax.ShapeDtypeStruct((nblk, 1, LANES), jnp.int32)],
        compiler_params=_params(("arbitrary",)),
        name="moe_router",
    )(hf, norm_g[None, :], wr, br, tri)


def _experts_kernel(cnt_ref, m_ref, keyt_ref, wt_ref, h_ref, wg_ref, wu_ref, wd_ref, o_ref, acc_ref):
    blk = pl.program_id(0)
    e = pl.program_id(1)
    tb = m_ref.shape[0]

    @pl.when(e == 0)
    def _():
        acc_ref[...] = h_ref[...]

    cnt = cnt_ref[blk * LANES + e]
    krow = keyt_ref[0, pl.ds(e, 1), :]
    wrow = wt_ref[0, pl.ds(e, 1), :]
    riota = lax.broadcasted_iota(jnp.int32, (MOE_ROWS, tb), 0).astype(F32)

    def chunk(ci, carry):
        hit = krow == (riota + (ci * MOE_ROWS).astype(F32))
        sel = jnp.where(hit, 1.0, 0.0).astype(BF16)
        xg = jnp.dot(sel, m_ref[...], preferred_element_type=F32).astype(BF16)
        hid = _silu(jnp.dot(xg, wg_ref[0], preferred_element_type=F32)) * jnp.dot(
            xg, wu_ref[0], preferred_element_type=F32)
        y = jnp.dot(hid.astype(BF16), wd_ref[0], preferred_element_type=F32)
        selw = jnp.where(hit, wrow, 0.0)
        acc_ref[...] += _bdot_tn(selw, y)
        return carry

    lax.fori_loop(0, (cnt + MOE_ROWS - 1) // MOE_ROWS, chunk, 0)

    @pl.when(e == N_EXPERTS - 1)
    def _():
        o_ref[...] = acc_ref[...]


def _moe_experts(cnt, m, keyt, wt, hf, w_gate, w_up, w_down, tb):
    n, d = hf.shape
    ff = w_gate.shape[2]
    nblk = n // tb
    grid_spec = pltpu.PrefetchScalarGridSpec(
        num_scalar_prefetch=1,
        grid=(nblk, N_EXPERTS),
        in_specs=[pl.BlockSpec((tb, d), lambda i, e, c: (i, 0)),
                  pl.BlockSpec((1, LANES, tb), lambda i, e, c: (i, 0, 0)),
                  pl.BlockSpec((1, LANES, tb), lambda i, e, c: (i, 0, 0)),
                  pl.BlockSpec((tb, d), lambda i, e, c: (i, 0)),
                  pl.BlockSpec((1, d, ff), lambda i, e, c: (e, 0, 0)),
                  pl.BlockSpec((1, d, ff), lambda i, e, c: (e, 0, 0)),
                  pl.BlockSpec((1, ff, d), lambda i, e, c: (e, 0, 0))],
        out_specs=pl.BlockSpec((tb, d), lambda i, e, c: (i, 0)),
        scratch_shapes=[pltpu.VMEM((tb, d), F32)],
    )
    return pl.pallas_call(
        _experts_kernel,
        grid_spec=grid_spec,
        out_shape=jax.ShapeDtypeStruct((n, d), F32),
        compiler_params=_params(("arbitrary", "arbitrary")),
        name="moe_experts",
    )(cnt, m, keyt, wt, hf, w_gate, w_up, w_down)


def _hier_moe_residual(h, norm_g, w_group, b_group, w_expert, b_expert, w_gate, w_up, w_down):
    b, s, d = h.shape
    hf = h.reshape(b * s, d)
    tb = _tile(b * s, 1024)
    m, keyt, wt, cnt = _moe_router(hf, norm_g, w_group, b_group, w_expert, b_expert, tb)
    out = _moe_experts(cnt.reshape(-1), m, keyt, wt, hf, w_gate.astype(BF16), w_up.astype(BF16),
                       w_down.astype(BF16), tb)
    return out.reshape(b, s, d)


def _ple_kernel(h_ref, p_ref, g_ref, wg_ref, bg_ref, wp_ref, o_ref):
    h = h_ref[0]
    gate = jax.nn.sigmoid(_bdot(_rms(h, g_ref[...]), wg_ref[...]) + bg_ref[...])
    o_ref[0] = h + gate * _bdot(p_ref[0], wp_ref[...])


def _ple_residual(h, p_i, w_proj, norm_g, w_gate, b_gate):
    b, s, d = h.shape
    pd = p_i.shape[2]
    tm = _tile(s, 512)
    row = lambda n: pl.BlockSpec((1, tm, n), lambda bi, ti: (bi, ti, 0))
    return pl.pallas_call(
        _ple_kernel,
        grid=(b, s // tm),
        in_specs=[row(d), row(pd), _full((1, d)), _full((d, d)), _full((1, d)), _full((pd, d))],
        out_specs=row(d),
        out_shape=jax.ShapeDtypeStruct((b, s, d), F32),
        compiler_params=_params(("arbitrary", "arbitrary")),
        name="ple",
    )(h, p_i, norm_g[None, :], w_gate.astype(BF16), b_gate[None, :], w_proj.astype(BF16))


def _even_layer(h, positions, norm_mix, w_in, b_f, fox_qn, fox_kn, q_a_norm, w_q_up, kv_a_norm, w_kv_up,
                mla_qn, mla_kn, w_out):
    fq, fk, fv, cum, mq, mk, mv = _even_pre(h, positions, norm_mix, w_in, b_f, fox_qn, fox_kn, q_a_norm,
                                            w_q_up, kv_a_norm, w_kv_up, mla_qn, mla_kn)
    cum_row = jnp.swapaxes(cum[:, :, :FOX_HEADS], 1, 2)
    o_fox = _attention(fq, fk, fv, cum, cum_row)
    o_mla = _attention(mq, mk, mv)
    return _proj2_residual(o_fox, o_mla, w_out, h)


def _odd_layer(h, norm_mix, w_in, a_re, a_im, b_re, b_im, c_re, c_im, d_skip, log_step, w_glu, b_glu,
               conv_w, a_log, dt_bias, o_norm, w_out):
    u, qkv, z, gb = _odd_pre(h, norm_mix, w_in)
    y_ssm = _s5(u, a_re, a_im, b_re, b_im, c_re, c_im, d_skip, log_step, w_glu, b_glu)
    o_gdn = _gdn(qkv, z, gb, conv_w, a_log, dt_bias, jnp.tile(o_norm, 1))
    return _proj2_residual(y_ssm, o_gdn, w_out, h)


def kernel(x, p, positions, norm_mix, norm_ffn, ev_w_in, fox_b_f, fox_q_norm, fox_k_norm, mla_q_a_norm, mla_w_q_up, mla_kv_a_norm, mla_w_kv_up, mla_q_norm, mla_k_norm, ev_w_out, od_w_in, s5_a_re, s5_a_im, s5_b_re, s5_b_im, s5_c_re, s5_c_im, s5_d, s5_log_step, s5_w_glu, s5_b_glu, gdn_conv_w, gdn_a_log, gdn_dt_bias, gdn_o_norm, od_w_out, moe_w_group, moe_b_group, moe_w_expert, moe_b_expert, moe_w_gate, moe_w_up, moe_w_down, ple_w_proj, ple_norm, ple_w_gate, ple_b_gate):
    h = x
    depth = p.shape[0]
    for i in range(depth):
        j = i // 2
        if i % 2 == 0:
            h = _even_layer(h, positions, norm_mix[i], ev_w_in[j], fox_b_f[j], fox_q_norm[j], fox_k_norm[j],
                            mla_q_a_norm[j], mla_w_q_up[j], mla_kv_a_norm[j], mla_w_kv_up[j], mla_q_norm[j],
                            mla_k_norm[j], ev_w_out[j])
        else:
            h = _odd_layer(h, norm_mix[i], od_w_in[j], s5_a_re[j], s5_a_im[j], s5_b_re[j], s5_b_im[j],
                           s5_c_re[j], s5_c_im[j], s5_d[j], s5_log_step[j], s5_w_glu[j], s5_b_glu[j],
                           gdn_conv_w[j], gdn_a_log[j], gdn_dt_bias[j], gdn_o_norm[j], od_w_out[j])
        h = _hier_moe_residual(h, norm_ffn[i], moe_w_group[i], moe_b_group[i], moe_w_expert[i],
                               moe_b_expert[i], moe_w_gate[i], moe_w_up[i], moe_w_down[i])
        h = _ple_residual(h, p[i], ple_w_proj[i], ple_norm[i], ple_w_gate[i], ple_b_gate[i])
    return h
```

```python
import functools
import math

import numpy as np
import jax
import jax.numpy as jnp
from jax import lax
from jax.experimental import pallas as pl
from jax.experimental.pallas import tpu as pltpu

F32 = jnp.float32
BF16 = jnp.bfloat16
HI = lax.Precision.HIGHEST

LANES = 128
RMS_EPS = 1e-6
ROPE_THETA = 10000.0
LOG2E = math.log2(math.e)

FOX_HEADS = 8
FOX_HEAD_DIM = 64
MLA_HEADS = 8
MLA_Q_LORA = 384
MLA_KV_LORA = 256
MLA_NOPE = 64
MLA_ROPE = 32
MLA_V = 64
MLA_QK = MLA_NOPE + MLA_ROPE

S5_CH = 512
S5_GROUP_CH = 16
S5_GROUPS = S5_CH // S5_GROUP_CH
S5_STATE = 64
S5_N = S5_GROUPS * S5_STATE

GDN_HEADS = 4
GDN_HEAD_DIM = 128
GDN_W = GDN_HEADS * GDN_HEAD_DIM
GDN_CONV = 4
GDN_CHUNK = 64

MOE_GROUPS = 4
MOE_PER_GROUP = 8
N_EXPERTS = MOE_GROUPS * MOE_PER_GROUP
MOE_ROWS = 128
ATTN_TQ = 1024
ATTN_TK = 512

VMEM_LIMIT = 56 * 1024 * 1024


def _tile(n, pref):
    t = min(n, pref)
    assert n % t == 0, (n, t)
    return t


def _params(sem):
    return pltpu.CompilerParams(dimension_semantics=sem, vmem_limit_bytes=VMEM_LIMIT)


def _full(shape):
    nd = len(shape)
    return pl.BlockSpec(shape, lambda *_: (0,) * nd)


def _rms(x, g):
    return x * lax.rsqrt(jnp.mean(x * x, axis=-1, keepdims=True) + RMS_EPS) * g


def _bdot(a, b):
    return jnp.dot(a.astype(BF16), b.astype(BF16), preferred_element_type=F32)


def _bdot_nt(a, b):
    return lax.dot_general(a.astype(BF16), b.astype(BF16), (((1,), (1,)), ((), ())),
                           preferred_element_type=F32)


def _bdot_tn(a, b):
    return lax.dot_general(a.astype(BF16), b.astype(BF16), (((0,), (0,)), ((), ())),
                           preferred_element_type=F32)


def _hdot(a, b):
    return jnp.dot(a, b, precision=HI, preferred_element_type=F32)


def _split_dot(x, ind):
    hi = x.astype(BF16)
    lo = (x - hi.astype(F32)).astype(BF16)
    return (jnp.dot(hi, ind, preferred_element_type=F32)
            + jnp.dot(lo, ind, preferred_element_type=F32))


def _log_sigmoid(x):
    return jnp.minimum(x, 0.0) - jnp.log(1.0 + jnp.exp(-jnp.abs(x)))


def _softplus(x):
    return jnp.maximum(x, 0.0) + jnp.log(1.0 + jnp.exp(-jnp.abs(x)))


def _silu(x):
    return x * jax.nn.sigmoid(x)


def _head_norm128(x, nheads, denom, gain):
    outs = []
    for hh in range(nheads):
        xh = x[:, LANES * hh:LANES * (hh + 1)]
        ss = jnp.sum(xh * xh, axis=-1, keepdims=True)
        outs.append(xh * lax.rsqrt(ss / denom + RMS_EPS))
    return jnp.concatenate(outs, axis=1) * gain


def _even_pre_kernel(h_ref, pos_ref, nmix_ref, win_ref, ind_ref, fqn_ref, fkn_ref, bf_ref,
                     qan_ref, wq_ref, kvan_ref, wkv_ref, mqn_ref, mkn_ref, freq_ref, s1_ref, s2_ref,
                     tri_ref, fq_o, fk_o, fv_o, cum_o, mq_o, mk_o, mv_o, carry_ref):
    t = pl.program_id(1)

    @pl.when(t == 0)
    def _():
        carry_ref[...] = jnp.zeros_like(carry_ref)

    tm = h_ref.shape[1]
    a = _rms(h_ref[0], nmix_ref[...])
    proj = _bdot(a, win_ref[...])
    nf = FOX_HEADS * FOX_HEAD_DIM
    fq = proj[:, 0:nf]
    fk = proj[:, nf:2 * nf]
    fv = proj[:, 2 * nf:3 * nf]
    o_cq = 3 * nf
    cq = proj[:, o_cq:o_cq + MLA_Q_LORA]
    o_ckv = o_cq + MLA_Q_LORA
    ckv = proj[:, o_ckv:o_ckv + MLA_KV_LORA]
    misc = proj[:, o_ckv + MLA_KV_LORA:]

    ind = ind_ref[...]
    fq_n = fq * lax.rsqrt(_split_dot(fq * fq, ind) / FOX_HEAD_DIM + RMS_EPS) * fqn_ref[...]
    fk_n = fk * lax.rsqrt(_split_dot(fk * fk, ind) / FOX_HEAD_DIM + RMS_EPS) * fkn_ref[...]
    fq_o[0] = (fq_n * (FOX_HEAD_DIM ** -0.5 * LOG2E)).astype(BF16)
    fk_o[0] = fk_n.astype(BF16)
    fv_o[0] = fv.astype(BF16)

    lane = lax.broadcasted_iota(jnp.int32, (tm, LANES), 1)
    logf = jnp.where(lane < FOX_HEADS, _log_sigmoid(misc + bf_ref[...]), 0.0)
    cum = _hdot(tri_ref[...], logf) + carry_ref[...]
    carry_ref[...] = cum[tm - 1:tm, :]
    cum_o[0] = (cum * LOG2E).T[:FOX_HEADS, :]

    ang = pos_ref[0].astype(F32) * freq_ref[...]
    cos1 = jnp.cos(ang)
    sin1 = jnp.sin(ang)
    cos = jnp.concatenate([cos1] * MLA_HEADS, axis=1)
    sin_a = jnp.concatenate([sin1 * s1_ref[...]] * MLA_HEADS, axis=1)
    sin_b = jnp.concatenate([sin1 * s2_ref[...]] * MLA_HEADS, axis=1)
    width = MLA_HEADS * LANES
    half = MLA_ROPE // 2

    def rope(x):
        return (x * cos + pltpu.roll(x, width - half, 1) * sin_a + pltpu.roll(x, half, 1) * sin_b)

    q = _bdot(_rms(cq, qan_ref[...]), wq_ref[...])
    q = rope(_head_norm128(q, MLA_HEADS, MLA_QK, mqn_ref[...]))
    mq_o[0] = (q * (MLA_QK ** -0.5 * LOG2E)).astype(BF16)

    kv = _bdot(_rms(ckv, kvan_ref[...]), wkv_ref[...])
    kr = pltpu.roll(misc, MLA_NOPE - FOX_HEADS, 1)
    kr = jnp.where((lane >= MLA_NOPE) & (lane < MLA_QK), kr, 0.0)
    k = kv[:, :width] + jnp.concatenate([kr] * MLA_HEADS, axis=1)
    k = rope(_head_norm128(k, MLA_HEADS, MLA_QK, mkn_ref[...]))
    mk_o[0] = k.astype(BF16)
    mv_o[0] = kv[:, width:].astype(BF16)


def _even_pre(h, positions, norm_mix, w_in, b_f, fox_qn, fox_kn, q_a_norm, w_q_up, kv_a_norm, w_kv_up,
              mla_qn, mla_kn):
    b, s, d = h.shape
    tm = _tile(s, 256)
    nf = FOX_HEADS * FOX_HEAD_DIM
    sizes = (nf, nf, nf, FOX_HEADS, MLA_Q_LORA, MLA_KV_LORA, MLA_ROPE)
    offs = np.concatenate([[0], np.cumsum(sizes)])
    parts = [w_in[:, offs[i]:offs[i + 1]] for i in range(len(sizes))]
    pad = jnp.zeros((d, LANES - FOX_HEADS - MLA_ROPE), w_in.dtype)
    win = jnp.concatenate([parts[0], parts[1], parts[2], parts[4], parts[5], parts[3], parts[6], pad],
                          axis=1).astype(BF16)
    ncol = win.shape[1]
    gidx = np.arange(nf) // FOX_HEAD_DIM
    ind = jnp.asarray(gidx[:, None] == gidx[None, :], BF16)
    fqn = jnp.tile(fox_qn, FOX_HEADS)[None, :]
    fkn = jnp.tile(fox_kn, FOX_HEADS)[None, :]
    bf = jnp.zeros((1, LANES), F32).at[0, :FOX_HEADS].set(b_f)
    padq = LANES - MLA_QK
    wq = jnp.pad(w_q_up.reshape(MLA_Q_LORA, MLA_HEADS, MLA_QK), ((0, 0), (0, 0), (0, padq)))
    wq = wq.reshape(MLA_Q_LORA, MLA_HEADS * LANES).astype(BF16)
    wkv3 = w_kv_up.reshape(MLA_KV_LORA, MLA_HEADS, MLA_NOPE + MLA_V)
    wk = jnp.pad(wkv3[:, :, :MLA_NOPE], ((0, 0), (0, 0), (0, LANES - MLA_NOPE)))
    wkv = jnp.concatenate([wk.reshape(MLA_KV_LORA, MLA_HEADS * LANES),
                           wkv3[:, :, MLA_NOPE:].reshape(MLA_KV_LORA, MLA_HEADS * MLA_V)],
                          axis=1).astype(BF16)
    mqn = jnp.tile(jnp.pad(mla_qn, (0, padq)), MLA_HEADS)[None, :]
    mkn = jnp.tile(jnp.pad(mla_kn, (0, padq)), MLA_HEADS)[None, :]
    half = MLA_ROPE // 2
    inv = ROPE_THETA ** (-jnp.arange(half, dtype=F32) * 2.0 / MLA_ROPE)
    freq = jnp.zeros((1, LANES), F32).at[0, MLA_NOPE:MLA_NOPE + half].set(inv)
    freq = freq.at[0, MLA_NOPE + half:MLA_QK].set(inv)
    s1 = jnp.zeros((1, LANES), F32).at[0, MLA_NOPE:MLA_NOPE + half].set(-1.0)
    s2 = jnp.zeros((1, LANES), F32).at[0, MLA_NOPE + half:MLA_QK].set(1.0)
    tri = jnp.asarray(np.tril(np.ones((tm, tm), np.float32)))
    pos3 = positions.reshape(b, s, 1)

    row = lambda n: pl.BlockSpec((1, tm, n), lambda bi, ti: (bi, ti, 0))
    consts = [norm_mix[None, :], win, ind, fqn, fkn, bf, q_a_norm[None, :], wq, kv_a_norm[None, :], wkv,
              mqn, mkn, freq, s1, s2, tri]
    out_shape = [jax.ShapeDtypeStruct((b, s, nf), BF16)] * 3 + [
        jax.ShapeDtypeStruct((b, FOX_HEADS, s), F32),
        jax.ShapeDtypeStruct((b, s, MLA_HEADS * LANES), BF16),
        jax.ShapeDtypeStruct((b, s, MLA_HEADS * LANES), BF16),
        jax.ShapeDtypeStruct((b, s, MLA_HEADS * MLA_V), BF16)]
    return pl.pallas_call(
        _even_pre_kernel,
        grid=(b, s // tm),
        in_specs=[row(d), row(1)] + [_full(c.shape) for c in consts],
        out_specs=[row(nf), row(nf), row(nf), pl.BlockSpec((1, FOX_HEADS, tm), lambda bi, ti: (bi, 0, ti)),
                   row(MLA_HEADS * LANES), row(MLA_HEADS * LANES), row(MLA_HEADS * MLA_V)],
        out_shape=out_shape,
        scratch_shapes=[pltpu.VMEM((1, LANES), F32)],
        compiler_params=_params(("arbitrary", "arbitrary")),
        name="even_pre",
    )(h, pos3, *consts)


def _attn_kernel(*refs, tq, tk, fox):
    if fox:
        q_ref, k_ref, v_ref, cr_ref, o_ref = refs
    else:
        q_ref, k_ref, v_ref, o_ref = refs
    hp = pl.program_id(1)
    i = pl.program_id(2)
    lane = lax.broadcasted_iota(jnp.int32, (tq, LANES), 1)
    rowi = lax.broadcasted_iota(jnp.int32, (tq, tk), 0)
    coli = lax.broadcasted_iota(jnp.int32, (tq, tk), 1)
    qs = []
    for hh in range(2):
        if fox:
            in_head = (lane >= FOX_HEAD_DIM * hh) & (lane < FOX_HEAD_DIM * (hh + 1))
            qs.append(jnp.where(in_head, q_ref[0], jnp.zeros((), BF16)))
        else:
            qs.append(q_ref[0, :, LANES * hh:LANES * (hh + 1)])

    def scores(hh, koff, diag):
        if fox:
            kj = k_ref[0, pl.ds(koff, tk), :]
        else:
            kj = k_ref[0, pl.ds(koff, tk), LANES * hh:LANES * (hh + 1)]
        sc = lax.dot_general(qs[hh], kj, (((1,), (1,)), ((), ())), preferred_element_type=F32)
        if fox:
            sc = sc - cr_ref[0, pl.ds(2 * hp + hh, 1), pl.ds(koff, tk)]
        if diag is not None:
            sc = jnp.where(coli + diag <= rowi, sc, -jnp.inf)
        return sc

    def step(j, carry, diag=None):
        koff = pl.multiple_of(j * tk, tk)
        vj = v_ref[0, pl.ds(koff, tk), :]
        new = []
        for hh in range(2):
            m, l, acc = carry[hh]
            sc = scores(hh, koff, diag)
            m_new = jnp.maximum(m, jnp.max(sc, axis=-1, keepdims=True))
            alpha = jnp.exp2(m - m_new)
            p = jnp.exp2(sc - m_new)
            l = alpha * l + jnp.sum(p, axis=-1, keepdims=True)
            acc = alpha * acc + jnp.dot(p.astype(BF16), vj, preferred_element_type=F32)
            new.append((m_new, l, acc))
        return tuple(new)

    def body(jj, carry):
        for r in range(ratio):
            carry = step(jj * ratio + r, carry)
        return carry

    one = (jnp.full((tq, 1), -jnp.inf, F32), jnp.zeros((tq, 1), F32), jnp.zeros((tq, LANES), F32))
    ratio = tq // tk
    carry = lax.fori_loop(0, i, body, (one, one))
    for r in range(ratio):
        carry = step(i * ratio + r, carry, diag=r * tk)
    (_, l0, acc0), (_, l1, acc1) = carry
    o_ref[0] = jnp.where(lane < MLA_V, acc0 / l0, acc1 / l1).astype(o_ref.dtype)


def _attention(q, k, v, cum_row=None):
    b, s, _ = v.shape
    fox = cum_row is not None
    qw = LANES if fox else 2 * LANES
    tq = _tile(s, ATTN_TQ)
    tk = _tile(tq, ATTN_TK)
    npairs = v.shape[2] // LANES
    in_specs = [pl.BlockSpec((1, tq, qw), lambda bi, hp, i: (bi, i, hp)),
                pl.BlockSpec((1, s, qw), lambda bi, hp, i: (bi, 0, hp)),
                pl.BlockSpec((1, s, LANES), lambda bi, hp, i: (bi, 0, hp))]
    args = [q, k, v]
    if fox:
        in_specs += [pl.BlockSpec((1, FOX_HEADS, s), lambda bi, hp, i: (bi, 0, 0))]
        args += [cum_row]
    return pl.pallas_call(
        functools.partial(_attn_kernel, tq=tq, tk=tk, fox=fox),
        grid=(b, npairs, s // tq),
        in_specs=in_specs,
        out_specs=pl.BlockSpec((1, tq, LANES), lambda bi, hp, i: (bi, i, hp)),
        out_shape=jax.ShapeDtypeStruct((b, s, npairs * LANES), BF16),
        compiler_params=_params(("arbitrary", "arbitrary", "arbitrary")),
        name="fox_attention" if fox else "mla_attention",
    )(*args)


def _proj2_kernel(a_ref, b_ref, wa_ref, wb_ref, h_ref, o_ref):
    o_ref[0] = (h_ref[0] + jnp.dot(a_ref[0], wa_ref[...], preferred_element_type=F32)
                + jnp.dot(b_ref[0], wb_ref[...], preferred_element_type=F32))


def _proj2_residual(a, bb, w_out, h):
    b, s, d = h.shape
    na = a.shape[2]
    nb = bb.shape[2]
    tm = _tile(s, 512)
    wa = w_out[:na].astype(BF16)
    wb = w_out[na:].astype(BF16)
    row = lambda n: pl.BlockSpec((1, tm, n), lambda bi, ti: (bi, ti, 0))
    return pl.pallas_call(
        _proj2_kernel,
        grid=(b, s // tm),
        in_specs=[row(na), row(nb), _full(wa.shape), _full(wb.shape), row(d)],
        out_specs=row(d),
        out_shape=jax.ShapeDtypeStruct((b, s, d), F32),
        compiler_params=_params(("arbitrary", "arbitrary")),
        name="out_proj",
    )(a, bb, wa, wb, h)


def _odd_pre_kernel(h_ref, nmix_ref, win_ref, u_o, qkv_o, z_o, gb_o):
    a = _rms(h_ref[0], nmix_ref[...])
    proj = _bdot(a, win_ref[...])
    o1 = S5_CH
    o2 = o1 + 3 * GDN_W
    o3 = o2 + GDN_W
    u_o[0] = proj[:, :o1]
    qkv_o[0] = proj[:, o1:o2]
    z_o[0] = proj[:, o2:o3]
    gb_o[0] = proj[:, o3:]


def _odd_pre(h, norm_mix, w_in):
    b, s, d = h.shape
    tm = _tile(s, 256)
    sizes = (S5_CH, 3 * GDN_W, GDN_HEADS, GDN_HEADS, GDN_W)
    offs = np.concatenate([[0], np.cumsum(sizes)])
    parts = [w_in[:, offs[i]:offs[i + 1]] for i in range(len(sizes))]
    pad = jnp.zeros((d, LANES - 2 * GDN_HEADS), w_in.dtype)
    win = jnp.concatenate([parts[0], parts[1], parts[4], parts[2], parts[3], pad], axis=1).astype(BF16)
    row = lambda n: pl.BlockSpec((1, tm, n), lambda bi, ti: (bi, ti, 0))
    widths = (S5_CH, 3 * GDN_W, GDN_W, LANES)
    return pl.pallas_call(
        _odd_pre_kernel,
        grid=(b, s // tm),
        in_specs=[row(d), _full((1, d)), _full(win.shape)],
        out_specs=[row(n) for n in widths],
        out_shape=[jax.ShapeDtypeStruct((b, s, n), F32) for n in widths],
        compiler_params=_params(("arbitrary", "arbitrary")),
        name="odd_pre",
    )(h, norm_mix[None, :], win)


def _s5_kernel(u_ref, bbd_ref, cbd_ref, ar_ref, ai_ref, d_ref, wglu_ref, bglu_ref, o_ref,
               x_ref, sr_ref, si_ref):
    t = pl.program_id(1)

    @pl.when(t == 0)
    def _():
        sr_ref[...] = jnp.zeros_like(sr_ref)
        si_ref[...] = jnp.zeros_like(si_ref)

    tm = u_ref.shape[1]
    u = u_ref[0]
    x_ref[...] = _bdot(u, bbd_ref[...])
    ar = ar_ref[...]
    ai = ai_ref[...]

    def step(r, carry):
        xr, xi = carry
        br = x_ref[pl.ds(r, 1), 0:S5_N]
        bi = x_ref[pl.ds(r, 1), S5_N:2 * S5_N]
        nr = ar * xr - ai * xi + br
        ni = ar * xi + ai * xr + bi
        x_ref[pl.ds(r, 1), 0:S5_N] = nr
        x_ref[pl.ds(r, 1), S5_N:2 * S5_N] = ni
        return nr, ni

    xr, xi = lax.fori_loop(0, tm, step, (sr_ref[...], si_ref[...]), unroll=8)
    sr_ref[...] = xr
    si_ref[...] = xi
    y = _bdot(x_ref[...], cbd_ref[...]) + d_ref[...] * u
    hg = jax.nn.gelu(y)
    o_ref[0] = (hg * jax.nn.sigmoid(_bdot(hg, wglu_ref[...]) + bglu_ref[...])).astype(o_ref.dtype)


def _s5(u, a_re, a_im, b_re, b_im, c_re, c_im, d_skip, log_step, w_glu, b_glu):
    b, s, _ = u.shape
    tm = _tile(s, 256)
    lam_re = jnp.minimum(a_re, -1e-4)
    lam_im = a_im
    dt = jnp.exp(log_step)[:, None]
    mag = jnp.exp(lam_re * dt)
    ab_re = mag * jnp.cos(lam_im * dt)
    ab_im = mag * jnp.sin(lam_im * dt)
    den = lam_re * lam_re + lam_im * lam_im
    nr, ni = ab_re - 1.0, ab_im
    gam_re = (nr * lam_re + ni * lam_im) / den
    gam_im = (ni * lam_re - nr * lam_im) / den
    bb_re = gam_re[..., None] * b_re - gam_im[..., None] * b_im
    bb_im = gam_re[..., None] * b_im + gam_im[..., None] * b_re
    eye = jnp.eye(S5_GROUPS, dtype=F32)
    bd_in = lambda m: jnp.einsum('gpc,gh->gchp', m, eye).reshape(S5_CH, S5_N)
    bd_out = lambda m: jnp.einsum('gcp,gh->gphc', m, eye).reshape(S5_N, S5_CH)
    bbd = jnp.concatenate([bd_in(bb_re), bd_in(bb_im)], axis=1).astype(BF16)
    cbd = jnp.concatenate([bd_out(c_re), -bd_out(c_im)], axis=0).astype(BF16)
    consts = [bbd, cbd, ab_re.reshape(1, S5_N), ab_im.reshape(1, S5_N), d_skip[None, :],
              w_glu.astype(BF16), b_glu[None, :]]
    row = pl.BlockSpec((1, tm, S5_CH), lambda bi, ti: (bi, ti, 0))
    return pl.pallas_call(
        _s5_kernel,
        grid=(b, s // tm),
        in_specs=[row] + [_full(c.shape) for c in consts],
        out_specs=row,
        out_shape=jax.ShapeDtypeStruct((b, s, S5_CH), BF16),
        scratch_shapes=[pltpu.VMEM((tm, 2 * S5_N), F32), pltpu.VMEM((1, S5_N), F32),
                        pltpu.VMEM((1, S5_N), F32)],
        compiler_params=_params(("arbitrary", "arbitrary")),
        name="s5",
    )(u, *consts)


def _gdn_kernel(x_ref, z_ref, gb_ref, cw_ref, nega_ref, dtb_ref, onorm_ref, tril_ref, triu_ref, o_ref,
                xpad_ref, state_ref):
    t = pl.program_id(1)
    tm = x_ref.shape[1]
    c = GDN_CHUNK
    hd = GDN_HEAD_DIM

    @pl.when(t == 0)
    def _():
        xpad_ref[0:8, :] = jnp.zeros((8, xpad_ref.shape[1]), F32)
        state_ref[...] = jnp.zeros_like(state_ref)

    @pl.when(t > 0)
    def _():
        xpad_ref[0:8, :] = xpad_ref[tm:tm + 8, :]

    xpad_ref[8:tm + 8, :] = x_ref[0]
    conv = cw_ref[0:1, :] * xpad_ref[pl.ds(8 - (GDN_CONV - 1), tm), :]
    for i in range(1, GDN_CONV):
        conv = conv + cw_ref[i:i + 1, :] * xpad_ref[pl.ds(8 - (GDN_CONV - 1) + i, tm), :]
    act = _silu(conv)

    def l2n(x):
        return x * lax.rsqrt(jnp.sum(x * x, axis=-1, keepdims=True) + RMS_EPS)

    gb = gb_ref[0]
    g = nega_ref[...] * _softplus(gb + dtb_ref[...])
    beta = jax.nn.sigmoid(gb)
    gc = _hdot(tril_ref[...], g)
    gct = _hdot(g.T, triu_ref[...])

    ri = lax.broadcasted_iota(jnp.int32, (tm, tm), 0)
    ci = lax.broadcasted_iota(jnp.int32, (tm, tm), 1)
    same = (ri // c) == (ci // c)
    incl = same & (ri >= ci)
    strict = same & (ri > ci)
    eye = (ri == ci).astype(F32)
    offs = []
    bs = 1
    while bs < c:
        offs.append(((ri // (2 * bs)) == (ci // (2 * bs))) & ((ri % (2 * bs)) >= bs) & ((ci % (2 * bs)) < bs))
        bs *= 2
    z = z_ref[0]
    nchunks = tm // c

    for hh in range(GDN_HEADS):
        lo = hh * hd
        q = l2n(act[:, lo:lo + hd]) * (hd ** -0.5)
        k = l2n(act[:, GDN_W + lo:GDN_W + lo + hd])
        v = act[:, 2 * GDN_W + lo:2 * GDN_W + lo + hd]
        bcol = beta[:, GDN_HEADS + hh:GDN_HEADS + hh + 1]
        gcol = gc[:, hh:hh + 1]
        grow = gct[hh:hh + 1, :]
        decay = jnp.where(incl, jnp.exp(jnp.where(incl, gcol - grow, 0.0)), 0.0)
        kb = k * bcol
        a_mat = jnp.where(strict, _bdot_nt(kb, k) * decay, 0.0)
        t_mat = eye
        for off in offs:
            t_mat = t_mat - _bdot(_bdot(t_mat, jnp.where(off, a_mat, 0.0)), t_mat)
        th = t_mat.astype(BF16)
        tl = (t_mat - th.astype(F32)).astype(BF16)
        ah = a_mat.astype(BF16)
        al = (a_mat - ah.astype(F32)).astype(BF16)
        a_t = (jnp.dot(ah, th, preferred_element_type=F32) + jnp.dot(ah, tl, preferred_element_type=F32)
               + jnp.dot(al, th, preferred_element_type=F32))
        t_mat = t_mat + jnp.dot(th, (eye - t_mat - a_t).astype(BF16), preferred_element_type=F32)
        eg = jnp.exp(gcol)
        u = _bdot(t_mat, v * bcol)
        w = _bdot(t_mat, kb * eg)
        intra = jnp.where(incl, _bdot_nt(q, k) * decay, 0.0).astype(BF16)
        qd = q * eg
        state = state_ref[hh]
        for n in range(nchunks):
            r0 = n * c
            gcn = gcol[r0:r0 + c, :]
            glast = gcol[r0 + c - 1:r0 + c, :]
            v_new = u[r0:r0 + c, :] - _bdot(w[r0:r0 + c, :], state)
            v_rep = jnp.concatenate([v_new.astype(BF16)] * nchunks, axis=0)
            o = _bdot(qd[r0:r0 + c, :], state) + jnp.dot(intra[r0:r0 + c, :], v_rep,
                                                          preferred_element_type=F32)
            state = state * jnp.exp(glast) + _bdot_tn(k[r0:r0 + c, :] * jnp.exp(glast - gcn), v_new)
            on = o * lax.rsqrt(jnp.mean(o * o, axis=-1, keepdims=True) + RMS_EPS) * onorm_ref[...]
            o_ref[0, r0:r0 + c, lo:lo + hd] = (on * _silu(z[r0:r0 + c, lo:lo + hd])).astype(o_ref.dtype)
        state_ref[hh] = state


def _gdn(qkv, z, gb, conv_w, a_log, dt_bias, o_norm):
    b, s, cw = qkv.shape
    tm = _tile(s, 256)
    nega = jnp.zeros((1, LANES), F32).at[0, :GDN_HEADS].set(-jnp.exp(a_log))
    dtb = jnp.zeros((1, LANES), F32).at[0, :GDN_HEADS].set(dt_bias)
    cwp = jnp.pad(conv_w, ((0, 8 - GDN_CONV), (0, 0)))
    ridx = np.arange(tm)
    same = (ridx[:, None] // GDN_CHUNK) == (ridx[None, :] // GDN_CHUNK)
    tril = jnp.asarray((same & (ridx[:, None] >= ridx[None, :])).astype(np.float32))
    triu = jnp.asarray((same & (ridx[:, None] <= ridx[None, :])).astype(np.float32))
    consts = [cwp, nega, dtb, o_norm[None, :], tril, triu]
    row = lambda n: pl.BlockSpec((1, tm, n), lambda bi, ti: (bi, ti, 0))
    return pl.pallas_call(
        _gdn_kernel,
        grid=(b, s // tm),
        in_specs=[row(cw), row(GDN_W), row(LANES)] + [_full(c.shape) for c in consts],
        out_specs=row(GDN_W),
        out_shape=jax.ShapeDtypeStruct((b, s, GDN_W), BF16),
        scratch_shapes=[pltpu.VMEM((tm + 8, cw), F32), pltpu.VMEM((GDN_HEADS, GDN_HEAD_DIM, GDN_HEAD_DIM), F32)],
        compiler_params=_params(("arbitrary", "arbitrary")),
        name="gdn",
    )(qkv, z, gb, *consts)


def _router_kernel(h_ref, g_ref, wr_ref, br_ref, tri_ref, m_o, keyt_o, wt_o, cnt_o):
    tm = h_ref.shape[0]
    m = _rms(h_ref[...], g_ref[...])
    m_o[...] = m.astype(BF16)
    logits = _hdot(m, wr_ref[...]) + br_ref[...]
    lane = lax.broadcasted_iota(jnp.int32, (tm, LANES), 1)
    neg = -jnp.inf

    def first_argmax(x):
        mx = jnp.max(x, axis=-1, keepdims=True)
        idx = jnp.min(jnp.where(x == mx, lane, LANES), axis=-1, keepdims=True)
        return mx, idx

    is_g = (lane >= N_EXPERTS) & (lane < N_EXPERTS + MOE_GROUPS)
    gl = jnp.where(is_g, logits, neg)
    gmax, gidx = first_argmax(gl)
    g_w = 1.0 / jnp.sum(jnp.where(is_g, jnp.exp(gl - gmax), 0.0), axis=-1, keepdims=True)
    in_group = (lane // MOE_PER_GROUP) == (gidx - N_EXPERTS)
    el = jnp.where(in_group & (lane < N_EXPERTS), logits, neg)
    m1, i1 = first_argmax(el)
    el2 = jnp.where(lane == i1, neg, el)
    m2, i2 = first_argmax(el2)
    r = jnp.exp(m2 - m1)
    w1 = g_w / (1.0 + r)
    w2 = g_w * r / (1.0 + r)
    chose = (lane == i1) | (lane == i2)
    wmat = jnp.where(lane == i1, w1, jnp.where(lane == i2, w2, 0.0))
    ch = chose.astype(F32)
    rank = jnp.dot(tri_ref[...], ch.astype(BF16), preferred_element_type=F32)
    key = jnp.where(chose, rank, -1.0)
    keyt_o[0] = key.T
    wt_o[0] = wmat.T
    cnt_o[0] = jnp.sum(ch, axis=0, keepdims=True).astype(jnp.int32)


def _moe_router(hf, norm_g, w_group, b_group, w_expert, b_expert, tb):
    n, d = hf.shape
    nblk = n // tb
    wr = jnp.zeros((d, LANES), F32).at[:, :N_EXPERTS].set(w_expert)
    wr = wr.at[:, N_EXPERTS:N_EXPERTS + MOE_GROUPS].set(w_group)
    br = jnp.zeros((1, LANES), F32).at[0, :N_EXPERTS].set(b_expert)
    br = br.at[0, N_EXPERTS:N_EXPERTS + MOE_GROUPS].set(b_group)
    tri = jnp.asarray(np.tril(np.ones((tb, tb), np.float32), -1), BF16)
    blk = pl.BlockSpec((1, LANES, tb), lambda i: (i, 0, 0))
    return pl.pallas_call(
        _router_kernel,
        grid=(nblk,),
        in_specs=[pl.BlockSpec((tb, d), lambda i: (i, 0)), _full((1, d)), _full(wr.shape), _full(br.shape),
                  _full(tri.shape)],
        out_specs=[pl.BlockSpec((tb, d), lambda i: (i, 0)), blk, blk,
                   pl.BlockSpec((1, 1, LANES), lambda i: (i, 0, 0))],
        out_shape=[jax.ShapeDtypeStruct((n, d), BF16), jax.ShapeDtypeStruct((nblk, LANES, tb), F32),
                   jax.ShapeDtypeStruct((nblk, LANES, tb), F32),
                   jax.ShapeDtypeStruct((nblk, 1, LANES), jnp.int32)],
        compiler_params=_params(("arbitrary",)),
        name="moe_router",
    )(hf, norm_g[None, :], wr, br, tri)


def _experts_kernel(cnt_ref, m_ref, keyt_ref, wt_ref, h_ref, wg_ref, wu_ref, wd_ref, o_ref, acc_ref):
    blk = pl.program_id(0)
    e = pl.program_id(1)
    tb = m_ref.shape[0]

    @pl.when(e == 0)
    def _():
        acc_ref[...] = h_ref[...]

    cnt = cnt_ref[blk * LANES + e]
    krow = keyt_ref[0, pl.ds(e, 1), :]
    wrow = wt_ref[0, pl.ds(e, 1), :]
    riota = lax.broadcasted_iota(jnp.int32, (MOE_ROWS, tb), 0).astype(F32)

    def chunk(ci, carry):
        hit = krow == (riota + (ci * MOE_ROWS).astype(F32))
        sel = jnp.where(hit, 1.0, 0.0).astype(BF16)
        xg = jnp.dot(sel, m_ref[...], preferred_element_type=F32).astype(BF16)
        hid = _silu(jnp.dot(xg, wg_ref[0], preferred_element_type=F32)) * jnp.dot(
            xg, wu_ref[0], preferred_element_type=F32)
        y = jnp.dot(hid.astype(BF16), wd_ref[0], preferred_element_type=F32)
        selw = jnp.where(hit, wrow, 0.0)
        acc_ref[...] += _bdot_tn(selw, y)
        return carry

    lax.fori_loop(0, (cnt + MOE_ROWS - 1) // MOE_ROWS, chunk, 0)

    @pl.when(e == N_EXPERTS - 1)
    def _():
        o_ref[...] = acc_ref[...]


def _moe_experts(cnt, m, keyt, wt, hf, w_gate, w_up, w_down, tb):
    n, d = hf.shape
    ff = w_gate.shape[2]
    nblk = n // tb
    grid_spec = pltpu.PrefetchScalarGridSpec(
        num_scalar_prefetch=1,
        grid=(nblk, N_EXPERTS),
        in_specs=[pl.BlockSpec((tb, d), lambda i, e, c: (i, 0)),
                  pl.BlockSpec((1, LANES, tb), lambda i, e, c: (i, 0, 0)),
                  pl.BlockSpec((1, LANES, tb), lambda i, e, c: (i, 0, 0)),
                  pl.BlockSpec((tb, d), lambda i, e, c: (i, 0)),
                  pl.BlockSpec((1, d, ff), lambda i, e, c: (e, 0, 0)),
                  pl.BlockSpec((1, d, ff), lambda i, e, c: (e, 0, 0)),
                  pl.BlockSpec((1, ff, d), lambda i, e, c: (e, 0, 0))],
        out_specs=pl.BlockSpec((tb, d), lambda i, e, c: (i, 0)),
        scratch_shapes=[pltpu.VMEM((tb, d), F32)],
    )
    return pl.pallas_call(
        _experts_kernel,
        grid_spec=grid_spec,
        out_shape=jax.ShapeDtypeStruct((n, d), F32),
        compiler_params=_params(("arbitrary", "arbitrary")),
        name="moe_experts",
    )(cnt, m, keyt, wt, hf, w_gate, w_up, w_down)


def _hier_moe_residual(h, norm_g, w_group, b_group, w_expert, b_expert, w_gate, w_up, w_down):
    b, s, d = h.shape
    hf = h.reshape(b * s, d)
    tb = _tile(b * s, 1024)
    m, keyt, wt, cnt = _moe_router(hf, norm_g, w_group, b_group, w_expert, b_expert, tb)
    out = _moe_experts(cnt.reshape(-1), m, keyt, wt, hf, w_gate.astype(BF16), w_up.astype(BF16),
                       w_down.astype(BF16), tb)
    return out.reshape(b, s, d)


def _ple_kernel(h_ref, p_ref, g_ref, wg_ref, bg_ref, wp_ref, o_ref):
    h = h_ref[0]
    gate = jax.nn.sigmoid(_bdot(_rms(h, g_ref[...]), wg_ref[...]) + bg_ref[...])
    o_ref[0] = h + gate * _bdot(p_ref[0], wp_ref[...])


def _ple_residual(h, p_i, w_proj, norm_g, w_gate, b_gate):
    b, s, d = h.shape
    pd = p_i.shape[2]
    tm = _tile(s, 512)
    row = lambda n: pl.BlockSpec((1, tm, n), lambda bi, ti: (bi, ti, 0))
    return pl.pallas_call(
        _ple_kernel,
        grid=(b, s // tm),
        in_specs=[row(d), row(pd), _full((1, d)), _full((d, d)), _full((1, d)), _full((pd, d))],
        out_specs=row(d),
        out_shape=jax.ShapeDtypeStruct((b, s, d), F32),
        compiler_params=_params(("arbitrary", "arbitrary")),
        name="ple",
    )(h, p_i, norm_g[None, :], w_gate.astype(BF16), b_gate[None, :], w_proj.astype(BF16))


def _even_layer(h, positions, norm_mix, w_in, b_f, fox_qn, fox_kn, q_a_norm, w_q_up, kv_a_norm, w_kv_up,
                mla_qn, mla_kn, w_out):
    fq, fk, fv, cum, mq, mk, mv = _even_pre(h, positions, norm_mix, w_in, b_f, fox_qn, fox_kn, q_a_norm,
                                            w_q_up, kv_a_norm, w_kv_up, mla_qn, mla_kn)
    o_fox = _attention(fq, fk, fv, cum)
    o_mla = _attention(mq, mk, mv)
    return _proj2_residual(o_fox, o_mla, w_out, h)


def _odd_layer(h, norm_mix, w_in, a_re, a_im, b_re, b_im, c_re, c_im, d_skip, log_step, w_glu, b_glu,
               conv_w, a_log, dt_bias, o_norm, w_out):
    u, qkv, z, gb = _odd_pre(h, norm_mix, w_in)
    y_ssm = _s5(u, a_re, a_im, b_re, b_im, c_re, c_im, d_skip, log_step, w_glu, b_glu)
    o_gdn = _gdn(qkv, z, gb, conv_w, a_log, dt_bias, jnp.tile(o_norm, 1))
    return _proj2_residual(y_ssm, o_gdn, w_out, h)


def kernel(x, p, positions, norm_mix, norm_ffn, ev_w_in, fox_b_f, fox_q_norm, fox_k_norm, mla_q_a_norm, mla_w_q_up, mla_kv_a_norm, mla_w_kv_up, mla_q_norm, mla_k_norm, ev_w_out, od_w_in, s5_a_re, s5_a_im, s5_b_re, s5_b_im, s5_c_re, s5_c_im, s5_d, s5_log_step, s5_w_glu, s5_b_glu, gdn_conv_w, gdn_a_log, gdn_dt_bias, gdn_o_norm, od_w_out, moe_w_group, moe_b_group, moe_w_expert, moe_b_expert, moe_w_gate, moe_w_up, moe_w_down, ple_w_proj, ple_norm, ple_w_gate, ple_b_gate):
    h = x
    depth = p.shape[0]
    for i in range(depth):
        j = i // 2
        if i % 2 == 0:
            h = _even_layer(h, positions, norm_mix[i], ev_w_in[j], fox_b_f[j], fox_q_norm[j], fox_k_norm[j],
                            mla_q_a_norm[j], mla_w_q_up[j], mla_kv_a_norm[j], mla_w_kv_up[j], mla_q_norm[j],
                            mla_k_norm[j], ev_w_out[j])
        else:
            h = _odd_layer(h, norm_mix[i], od_w_in[j], s5_a_re[j], s5_a_im[j], s5_b_re[j], s5_b_im[j],
                           s5_c_re[j], s5_c_im[j], s5_d[j], s5_log_step[j], s5_w_glu[j], s5_b_glu[j],
                           gdn_conv_w[j], gdn_a_log[j], gdn_dt_bias[j], gdn_o_norm[j], od_w_out[j])
        h = _hier_moe_residual(h, norm_ffn[i], moe_w_group[i], moe_b_group[i], moe_w_expert[i],
                               moe_b_expert[i], moe_w_gate[i], moe_w_up[i], moe_w_down[i])
        h = _ple_residual(h, p[i], ple_w_proj[i], ple_norm[i], ple_w_gate[i], ple_b_gate[i])
    return h
```

```python
import functools
import math

import numpy as np
import jax
import jax.numpy as jnp
from jax import lax
from jax.experimental import pallas as pl
from jax.experimental.pallas import tpu as pltpu

F32 = jnp.float32
BF16 = jnp.bfloat16
HI = lax.Precision.HIGHEST

LANES = 128
RMS_EPS = 1e-6
ROPE_THETA = 10000.0
LOG2E = math.log2(math.e)

FOX_HEADS = 8
FOX_HEAD_DIM = 64
MLA_HEADS = 8
MLA_Q_LORA = 384
MLA_KV_LORA = 256
MLA_NOPE = 64
MLA_ROPE = 32
MLA_V = 64
MLA_QK = MLA_NOPE + MLA_ROPE

S5_CH = 512
S5_GROUP_CH = 16
S5_GROUPS = S5_CH // S5_GROUP_CH
S5_STATE = 64
S5_N = S5_GROUPS * S5_STATE

GDN_HEADS = 4
GDN_HEAD_DIM = 128
GDN_W = GDN_HEADS * GDN_HEAD_DIM
GDN_CONV = 4
GDN_CHUNK = 64

MOE_GROUPS = 4
MOE_PER_GROUP = 8
N_EXPERTS = MOE_GROUPS * MOE_PER_GROUP
MOE_TB = 512
MOE_CAP = 64
MOE_ROWS = 128
ATTN_TQ = 1024
ATTN_TK = 512

VMEM_LIMIT = 56 * 1024 * 1024


def _tile(n, pref):
    t = min(n, pref)
    assert n % t == 0, (n, t)
    return t


def _params(sem):
    return pltpu.CompilerParams(dimension_semantics=sem, vmem_limit_bytes=VMEM_LIMIT)


def _full(shape):
    nd = len(shape)
    return pl.BlockSpec(shape, lambda *_: (0,) * nd)


def _rms(x, g):
    return x * lax.rsqrt(jnp.mean(x * x, axis=-1, keepdims=True) + RMS_EPS) * g


def _bdot(a, b):
    return jnp.dot(a.astype(BF16), b.astype(BF16), preferred_element_type=F32)


def _bdot_nt(a, b):
    return lax.dot_general(a.astype(BF16), b.astype(BF16), (((1,), (1,)), ((), ())),
                           preferred_element_type=F32)


def _bdot_tn(a, b):
    return lax.dot_general(a.astype(BF16), b.astype(BF16), (((0,), (0,)), ((), ())),
                           preferred_element_type=F32)


def _hdot(a, b):
    return jnp.dot(a, b, precision=HI, preferred_element_type=F32)


def _split_dot(x, ind):
    hi = x.astype(BF16)
    lo = (x - hi.astype(F32)).astype(BF16)
    return (jnp.dot(hi, ind, preferred_element_type=F32)
            + jnp.dot(lo, ind, preferred_element_type=F32))


def _log_sigmoid(x):
    return jnp.minimum(x, 0.0) - jnp.log(1.0 + jnp.exp(-jnp.abs(x)))


def _softplus(x):
    return jnp.maximum(x, 0.0) + jnp.log(1.0 + jnp.exp(-jnp.abs(x)))


def _silu(x):
    return x * jax.nn.sigmoid(x)


def _head_norm128(x, nheads, denom, gain):
    outs = []
    for hh in range(nheads):
        xh = x[:, LANES * hh:LANES * (hh + 1)]
        ss = jnp.sum(xh * xh, axis=-1, keepdims=True)
        outs.append(xh * lax.rsqrt(ss / denom + RMS_EPS))
    return jnp.concatenate(outs, axis=1) * gain


def _even_pre_kernel(h_ref, pos_ref, nmix_ref, win_ref, ind_ref, fqn_ref, fkn_ref, bf_ref,
                     qan_ref, wq_ref, kvan_ref, wkv_ref, mqn_ref, mkn_ref, freq_ref, s1_ref, s2_ref,
                     tri_ref, fq_o, fk_o, fv_o, cum_o, mq_o, mk_o, mv_o, carry_ref):
    t = pl.program_id(1)

    @pl.when(t == 0)
    def _():
        carry_ref[...] = jnp.zeros_like(carry_ref)

    tm = h_ref.shape[1]
    a = _rms(h_ref[0], nmix_ref[...])
    proj = _bdot(a, win_ref[...])
    nf = FOX_HEADS * FOX_HEAD_DIM
    fq = proj[:, 0:nf]
    fk = proj[:, nf:2 * nf]
    fv = proj[:, 2 * nf:3 * nf]
    o_cq = 3 * nf
    cq = proj[:, o_cq:o_cq + MLA_Q_LORA]
    o_ckv = o_cq + MLA_Q_LORA
    ckv = proj[:, o_ckv:o_ckv + MLA_KV_LORA]
    misc = proj[:, o_ckv + MLA_KV_LORA:]

    ind = ind_ref[...]
    fq_n = fq * lax.rsqrt(_split_dot(fq * fq, ind) / FOX_HEAD_DIM + RMS_EPS) * fqn_ref[...]
    fk_n = fk * lax.rsqrt(_split_dot(fk * fk, ind) / FOX_HEAD_DIM + RMS_EPS) * fkn_ref[...]
    fq_o[0] = (fq_n * (FOX_HEAD_DIM ** -0.5 * LOG2E)).astype(BF16)
    fk_o[0] = fk_n.astype(BF16)
    fv_o[0] = fv.astype(BF16)

    lane = lax.broadcasted_iota(jnp.int32, (tm, LANES), 1)
    logf = jnp.where(lane < FOX_HEADS, _log_sigmoid(misc + bf_ref[...]), 0.0)
    cum = _hdot(tri_ref[...], logf) + carry_ref[...]
    carry_ref[...] = cum[tm - 1:tm, :]
    cum_o[0] = (cum * LOG2E).T[:FOX_HEADS, :]

    ang = pos_ref[0].astype(F32) * freq_ref[...]
    cos1 = jnp.cos(ang)
    sin1 = jnp.sin(ang)
    cos = jnp.concatenate([cos1] * MLA_HEADS, axis=1)
    sin_a = jnp.concatenate([sin1 * s1_ref[...]] * MLA_HEADS, axis=1)
    sin_b = jnp.concatenate([sin1 * s2_ref[...]] * MLA_HEADS, axis=1)
    width = MLA_HEADS * LANES
    half = MLA_ROPE // 2

    def rope(x):
        return (x * cos + pltpu.roll(x, width - half, 1) * sin_a + pltpu.roll(x, half, 1) * sin_b)

    q = _bdot(_rms(cq, qan_ref[...]), wq_ref[...])
    q = rope(_head_norm128(q, MLA_HEADS, MLA_QK, mqn_ref[...]))
    mq_o[0] = (q * (MLA_QK ** -0.5 * LOG2E)).astype(BF16)

    kv = _bdot(_rms(ckv, kvan_ref[...]), wkv_ref[...])
    kr = pltpu.roll(misc, MLA_NOPE - FOX_HEADS, 1)
    kr = jnp.where((lane >= MLA_NOPE) & (lane < MLA_QK), kr, 0.0)
    k = kv[:, :width] + jnp.concatenate([kr] * MLA_HEADS, axis=1)
    k = rope(_head_norm128(k, MLA_HEADS, MLA_QK, mkn_ref[...]))
    mk_o[0] = k.astype(BF16)
    mv_o[0] = kv[:, width:].astype(BF16)


def _even_pre(h, positions, norm_mix, w_in, b_f, fox_qn, fox_kn, q_a_norm, w_q_up, kv_a_norm, w_kv_up,
              mla_qn, mla_kn):
    b, s, d = h.shape
    tm = _tile(s, 256)
    nf = FOX_HEADS * FOX_HEAD_DIM
    sizes = (nf, nf, nf, FOX_HEADS, MLA_Q_LORA, MLA_KV_LORA, MLA_ROPE)
    offs = np.concatenate([[0], np.cumsum(sizes)])
    parts = [w_in[:, offs[i]:offs[i + 1]] for i in range(len(sizes))]
    pad = jnp.zeros((d, LANES - FOX_HEADS - MLA_ROPE), w_in.dtype)
    win = jnp.concatenate([parts[0], parts[1], parts[2], parts[4], parts[5], parts[3], parts[6], pad],
                          axis=1).astype(BF16)
    ncol = win.shape[1]
    gidx = np.arange(nf) // FOX_HEAD_DIM
    ind = jnp.asarray(gidx[:, None] == gidx[None, :], BF16)
    fqn = jnp.tile(fox_qn, FOX_HEADS)[None, :]
    fkn = jnp.tile(fox_kn, FOX_HEADS)[None, :]
    bf = jnp.zeros((1, LANES), F32).at[0, :FOX_HEADS].set(b_f)
    padq = LANES - MLA_QK
    wq = jnp.pad(w_q_up.reshape(MLA_Q_LORA, MLA_HEADS, MLA_QK), ((0, 0), (0, 0), (0, padq)))
    wq = wq.reshape(MLA_Q_LORA, MLA_HEADS * LANES).astype(BF16)
    wkv3 = w_kv_up.reshape(MLA_KV_LORA, MLA_HEADS, MLA_NOPE + MLA_V)
    wk = jnp.pad(wkv3[:, :, :MLA_NOPE], ((0, 0), (0, 0), (0, LANES - MLA_NOPE)))
    wkv = jnp.concatenate([wk.reshape(MLA_KV_LORA, MLA_HEADS * LANES),
                           wkv3[:, :, MLA_NOPE:].reshape(MLA_KV_LORA, MLA_HEADS * MLA_V)],
                          axis=1).astype(BF16)
    mqn = jnp.tile(jnp.pad(mla_qn, (0, padq)), MLA_HEADS)[None, :]
    mkn = jnp.tile(jnp.pad(mla_kn, (0, padq)), MLA_HEADS)[None, :]
    half = MLA_ROPE // 2
    inv = ROPE_THETA ** (-jnp.arange(half, dtype=F32) * 2.0 / MLA_ROPE)
    freq = jnp.zeros((1, LANES), F32).at[0, MLA_NOPE:MLA_NOPE + half].set(inv)
    freq = freq.at[0, MLA_NOPE + half:MLA_QK].set(inv)
    s1 = jnp.zeros((1, LANES), F32).at[0, MLA_NOPE:MLA_NOPE + half].set(-1.0)
    s2 = jnp.zeros((1, LANES), F32).at[0, MLA_NOPE + half:MLA_QK].set(1.0)
    tri = jnp.asarray(np.tril(np.ones((tm, tm), np.float32)))
    pos3 = positions.reshape(b, s, 1)

    row = lambda n: pl.BlockSpec((1, tm, n), lambda bi, ti: (bi, ti, 0))
    consts = [norm_mix[None, :], win, ind, fqn, fkn, bf, q_a_norm[None, :], wq, kv_a_norm[None, :], wkv,
              mqn, mkn, freq, s1, s2, tri]
    out_shape = [jax.ShapeDtypeStruct((b, s, nf), BF16)] * 3 + [
        jax.ShapeDtypeStruct((b, FOX_HEADS, s), F32),
        jax.ShapeDtypeStruct((b, s, MLA_HEADS * LANES), BF16),
        jax.ShapeDtypeStruct((b, s, MLA_HEADS * LANES), BF16),
        jax.ShapeDtypeStruct((b, s, MLA_HEADS * MLA_V), BF16)]
    return pl.pallas_call(
        _even_pre_kernel,
        grid=(b, s // tm),
        in_specs=[row(d), row(1)] + [_full(c.shape) for c in consts],
        out_specs=[row(nf), row(nf), row(nf), pl.BlockSpec((1, FOX_HEADS, tm), lambda bi, ti: (bi, 0, ti)),
                   row(MLA_HEADS * LANES), row(MLA_HEADS * LANES), row(MLA_HEADS * MLA_V)],
        out_shape=out_shape,
        scratch_shapes=[pltpu.VMEM((1, LANES), F32)],
        compiler_params=_params(("arbitrary", "arbitrary")),
        name="even_pre",
    )(h, pos3, *consts)


def _attn_kernel(*refs, tq, tk, fox):
    if fox:
        q_ref, k_ref, v_ref, cr_ref, o_ref = refs
    else:
        q_ref, k_ref, v_ref, o_ref = refs
    hp = pl.program_id(1)
    i = pl.program_id(2)
    lane = lax.broadcasted_iota(jnp.int32, (tq, LANES), 1)
    rowi = lax.broadcasted_iota(jnp.int32, (tq, tk), 0)
    coli = lax.broadcasted_iota(jnp.int32, (tq, tk), 1)
    qs = []
    for hh in range(2):
        if fox:
            in_head = (lane >= FOX_HEAD_DIM * hh) & (lane < FOX_HEAD_DIM * (hh + 1))
            qs.append(jnp.where(in_head, q_ref[0], jnp.zeros((), BF16)))
        else:
            qs.append(q_ref[0, :, LANES * hh:LANES * (hh + 1)])

    def scores(hh, koff, diag):
        if fox:
            kj = k_ref[0, pl.ds(koff, tk), :]
        else:
            kj = k_ref[0, pl.ds(koff, tk), LANES * hh:LANES * (hh + 1)]
        sc = lax.dot_general(qs[hh], kj, (((1,), (1,)), ((), ())), preferred_element_type=F32)
        if fox:
            sc = sc - cr_ref[0, pl.ds(2 * hp + hh, 1), pl.ds(koff, tk)]
        if diag is not None:
            sc = jnp.where(coli + diag <= rowi, sc, -jnp.inf)
        return sc

    def step(j, carry, diag=None):
        koff = pl.multiple_of(j * tk, tk)
        vj = v_ref[0, pl.ds(koff, tk), :]
        new = []
        for hh in range(2):
            m, l, acc = carry[hh]
            sc = scores(hh, koff, diag)
            m_new = jnp.maximum(m, jnp.max(sc, axis=-1, keepdims=True))
            alpha = jnp.exp2(m - m_new)
            p = jnp.exp2(sc - m_new)
            l = alpha * l + jnp.sum(p, axis=-1, keepdims=True)
            acc = alpha * acc + jnp.dot(p.astype(BF16), vj, preferred_element_type=F32)
            new.append((m_new, l, acc))
        return tuple(new)

    def body(jj, carry):
        for r in range(ratio):
            carry = step(jj * ratio + r, carry)
        return carry

    one = (jnp.full((tq, 1), -jnp.inf, F32), jnp.zeros((tq, 1), F32), jnp.zeros((tq, LANES), F32))
    ratio = tq // tk
    carry = lax.fori_loop(0, i, body, (one, one))
    for r in range(ratio):
        carry = step(i * ratio + r, carry, diag=r * tk)
    (_, l0, acc0), (_, l1, acc1) = carry
    o_ref[0] = jnp.where(lane < MLA_V, acc0 / l0, acc1 / l1).astype(o_ref.dtype)


def _attention(q, k, v, cum_row=None):
    b, s, _ = v.shape
    fox = cum_row is not None
    qw = LANES if fox else 2 * LANES
    tq = _tile(s, ATTN_TQ)
    tk = _tile(tq, ATTN_TK)
    npairs = v.shape[2] // LANES
    in_specs = [pl.BlockSpec((1, tq, qw), lambda bi, hp, i: (bi, i, hp)),
                pl.BlockSpec((1, s, qw), lambda bi, hp, i: (bi, 0, hp)),
                pl.BlockSpec((1, s, LANES), lambda bi, hp, i: (bi, 0, hp))]
    args = [q, k, v]
    if fox:
        in_specs += [pl.BlockSpec((1, FOX_HEADS, s), lambda bi, hp, i: (bi, 0, 0))]
        args += [cum_row]
    return pl.pallas_call(
        functools.partial(_attn_kernel, tq=tq, tk=tk, fox=fox),
        grid=(b, npairs, s // tq),
        in_specs=in_specs,
        out_specs=pl.BlockSpec((1, tq, LANES), lambda bi, hp, i: (bi, i, hp)),
        out_shape=jax.ShapeDtypeStruct((b, s, npairs * LANES), BF16),
        compiler_params=_params(("arbitrary", "arbitrary", "arbitrary")),
        name="fox_attention" if fox else "mla_attention",
    )(*args)


def _proj2_kernel(a_ref, b_ref, wa_ref, wb_ref, h_ref, o_ref):
    o_ref[0] = (h_ref[0] + jnp.dot(a_ref[0], wa_ref[...], preferred_element_type=F32)
                + jnp.dot(b_ref[0], wb_ref[...], preferred_element_type=F32))


def _proj2_residual(a, bb, w_out, h):
    b, s, d = h.shape
    na = a.shape[2]
    nb = bb.shape[2]
    tm = _tile(s, 512)
    wa = w_out[:na].astype(BF16)
    wb = w_out[na:].astype(BF16)
    row = lambda n: pl.BlockSpec((1, tm, n), lambda bi, ti: (bi, ti, 0))
    return pl.pallas_call(
        _proj2_kernel,
        grid=(b, s // tm),
        in_specs=[row(na), row(nb), _full(wa.shape), _full(wb.shape), row(d)],
        out_specs=row(d),
        out_shape=jax.ShapeDtypeStruct((b, s, d), F32),
        compiler_params=_params(("arbitrary", "arbitrary")),
        name="out_proj",
    )(a, bb, wa, wb, h)


def _odd_pre_kernel(h_ref, nmix_ref, win_ref, u_o, qkv_o, z_o, gb_o):
    a = _rms(h_ref[0], nmix_ref[...])
    proj = _bdot(a, win_ref[...])
    o1 = S5_CH
    o2 = o1 + 3 * GDN_W
    o3 = o2 + GDN_W
    u_o[0] = proj[:, :o1]
    qkv_o[0] = proj[:, o1:o2]
    z_o[0] = proj[:, o2:o3]
    gb_o[0] = proj[:, o3:]


def _odd_pre(h, norm_mix, w_in):
    b, s, d = h.shape
    tm = _tile(s, 256)
    sizes = (S5_CH, 3 * GDN_W, GDN_HEADS, GDN_HEADS, GDN_W)
    offs = np.concatenate([[0], np.cumsum(sizes)])
    parts = [w_in[:, offs[i]:offs[i + 1]] for i in range(len(sizes))]
    pad = jnp.zeros((d, LANES - 2 * GDN_HEADS), w_in.dtype)
    win = jnp.concatenate([parts[0], parts[1], parts[4], parts[2], parts[3], pad], axis=1).astype(BF16)
    row = lambda n: pl.BlockSpec((1, tm, n), lambda bi, ti: (bi, ti, 0))
    widths = (S5_CH, 3 * GDN_W, GDN_W, LANES)
    return pl.pallas_call(
        _odd_pre_kernel,
        grid=(b, s // tm),
        in_specs=[row(d), _full((1, d)), _full(win.shape)],
        out_specs=[row(n) for n in widths],
        out_shape=[jax.ShapeDtypeStruct((b, s, n), F32) for n in widths],
        compiler_params=_params(("arbitrary", "arbitrary")),
        name="odd_pre",
    )(h, norm_mix[None, :], win)


def _s5_kernel(u_ref, bbd_ref, cbd_ref, ar_ref, ai_ref, d_ref, wglu_ref, bglu_ref, o_ref,
               x_ref, sr_ref, si_ref):
    t = pl.program_id(1)

    @pl.when(t == 0)
    def _():
        sr_ref[...] = jnp.zeros_like(sr_ref)
        si_ref[...] = jnp.zeros_like(si_ref)

    tm = u_ref.shape[1]
    u = u_ref[0]
    x_ref[...] = _bdot(u, bbd_ref[...])
    ar = ar_ref[...]
    ai = ai_ref[...]

    def step(r, carry):
        xr, xi = carry
        br = x_ref[pl.ds(r, 1), 0:S5_N]
        bi = x_ref[pl.ds(r, 1), S5_N:2 * S5_N]
        nr = ar * xr - ai * xi + br
        ni = ar * xi + ai * xr + bi
        x_ref[pl.ds(r, 1), 0:S5_N] = nr
        x_ref[pl.ds(r, 1), S5_N:2 * S5_N] = ni
        return nr, ni

    xr, xi = lax.fori_loop(0, tm, step, (sr_ref[...], si_ref[...]), unroll=8)
    sr_ref[...] = xr
    si_ref[...] = xi
    y = _bdot(x_ref[...], cbd_ref[...]) + d_ref[...] * u
    hg = jax.nn.gelu(y)
    o_ref[0] = (hg * jax.nn.sigmoid(_bdot(hg, wglu_ref[...]) + bglu_ref[...])).astype(o_ref.dtype)


def _s5(u, a_re, a_im, b_re, b_im, c_re, c_im, d_skip, log_step, w_glu, b_glu):
    b, s, _ = u.shape
    tm = _tile(s, 256)
    lam_re = jnp.minimum(a_re, -1e-4)
    lam_im = a_im
    dt = jnp.exp(log_step)[:, None]
    mag = jnp.exp(lam_re * dt)
    ab_re = mag * jnp.cos(lam_im * dt)
    ab_im = mag * jnp.sin(lam_im * dt)
    den = lam_re * lam_re + lam_im * lam_im
    nr, ni = ab_re - 1.0, ab_im
    gam_re = (nr * lam_re + ni * lam_im) / den
    gam_im = (ni * lam_re - nr * lam_im) / den
    bb_re = gam_re[..., None] * b_re - gam_im[..., None] * b_im
    bb_im = gam_re[..., None] * b_im + gam_im[..., None] * b_re
    eye = jnp.eye(S5_GROUPS, dtype=F32)
    bd_in = lambda m: jnp.einsum('gpc,gh->gchp', m, eye).reshape(S5_CH, S5_N)
    bd_out = lambda m: jnp.einsum('gcp,gh->gphc', m, eye).reshape(S5_N, S5_CH)
    bbd = jnp.concatenate([bd_in(bb_re), bd_in(bb_im)], axis=1).astype(BF16)
    cbd = jnp.concatenate([bd_out(c_re), -bd_out(c_im)], axis=0).astype(BF16)
    consts = [bbd, cbd, ab_re.reshape(1, S5_N), ab_im.reshape(1, S5_N), d_skip[None, :],
              w_glu.astype(BF16), b_glu[None, :]]
    row = pl.BlockSpec((1, tm, S5_CH), lambda bi, ti: (bi, ti, 0))
    return pl.pallas_call(
        _s5_kernel,
        grid=(b, s // tm),
        in_specs=[row] + [_full(c.shape) for c in consts],
        out_specs=row,
        out_shape=jax.ShapeDtypeStruct((b, s, S5_CH), BF16),
        scratch_shapes=[pltpu.VMEM((tm, 2 * S5_N), F32), pltpu.VMEM((1, S5_N), F32),
                        pltpu.VMEM((1, S5_N), F32)],
        compiler_params=_params(("arbitrary", "arbitrary")),
        name="s5",
    )(u, *consts)


def _gdn_kernel(x_ref, z_ref, gb_ref, cw_ref, nega_ref, dtb_ref, onorm_ref, tril_ref, triu_ref, o_ref,
                xpad_ref, state_ref):
    t = pl.program_id(1)
    tm = x_ref.shape[1]
    c = GDN_CHUNK
    hd = GDN_HEAD_DIM

    @pl.when(t == 0)
    def _():
        xpad_ref[0:8, :] = jnp.zeros((8, xpad_ref.shape[1]), F32)
        state_ref[...] = jnp.zeros_like(state_ref)

    @pl.when(t > 0)
    def _():
        xpad_ref[0:8, :] = xpad_ref[tm:tm + 8, :]

    xpad_ref[8:tm + 8, :] = x_ref[0]
    conv = cw_ref[0:1, :] * xpad_ref[pl.ds(8 - (GDN_CONV - 1), tm), :]
    for i in range(1, GDN_CONV):
        conv = conv + cw_ref[i:i + 1, :] * xpad_ref[pl.ds(8 - (GDN_CONV - 1) + i, tm), :]
    act = _silu(conv)

    def l2n(x):
        return x * lax.rsqrt(jnp.sum(x * x, axis=-1, keepdims=True) + RMS_EPS)

    gb = gb_ref[0]
    g = nega_ref[...] * _softplus(gb + dtb_ref[...])
    beta = jax.nn.sigmoid(gb)
    gc = _hdot(tril_ref[...], g)
    gct = _hdot(g.T, triu_ref[...])

    ri = lax.broadcasted_iota(jnp.int32, (tm, tm), 0)
    ci = lax.broadcasted_iota(jnp.int32, (tm, tm), 1)
    same = (ri // c) == (ci // c)
    incl = same & (ri >= ci)
    strict = same & (ri > ci)
    eye = (ri == ci).astype(F32)
    offs = []
    bs = 1
    while bs < c:
        offs.append(((ri // (2 * bs)) == (ci // (2 * bs))) & ((ri % (2 * bs)) >= bs) & ((ci % (2 * bs)) < bs))
        bs *= 2
    z = z_ref[0]
    nchunks = tm // c

    for hh in range(GDN_HEADS):
        lo = hh * hd
        q = l2n(act[:, lo:lo + hd]) * (hd ** -0.5)
        k = l2n(act[:, GDN_W + lo:GDN_W + lo + hd])
        v = act[:, 2 * GDN_W + lo:2 * GDN_W + lo + hd]
        bcol = beta[:, GDN_HEADS + hh:GDN_HEADS + hh + 1]
        gcol = gc[:, hh:hh + 1]
        grow = gct[hh:hh + 1, :]
        decay = jnp.where(incl, jnp.exp(jnp.where(incl, gcol - grow, 0.0)), 0.0)
        kb = k * bcol
        a_mat = jnp.where(strict, _bdot_nt(kb, k) * decay, 0.0)
        t_mat = eye
        for off in offs:
            t_mat = t_mat - _bdot(_bdot(t_mat, jnp.where(off, a_mat, 0.0)), t_mat)
        th = t_mat.astype(BF16)
        tl = (t_mat - th.astype(F32)).astype(BF16)
        ah = a_mat.astype(BF16)
        al = (a_mat - ah.astype(F32)).astype(BF16)
        a_t = (jnp.dot(ah, th, preferred_element_type=F32) + jnp.dot(ah, tl, preferred_element_type=F32)
               + jnp.dot(al, th, preferred_element_type=F32))
        t_mat = t_mat + jnp.dot(th, (eye - t_mat - a_t).astype(BF16), preferred_element_type=F32)
        eg = jnp.exp(gcol)
        u = _bdot(t_mat, v * bcol)
        w = _bdot(t_mat, kb * eg)
        intra = jnp.where(incl, _bdot_nt(q, k) * decay, 0.0).astype(BF16)
        qd = q * eg
        state = state_ref[hh]
        for n in range(nchunks):
            r0 = n * c
            gcn = gcol[r0:r0 + c, :]
            glast = gcol[r0 + c - 1:r0 + c, :]
            v_new = u[r0:r0 + c, :] - _bdot(w[r0:r0 + c, :], state)
            v_rep = jnp.concatenate([v_new.astype(BF16)] * nchunks, axis=0)
            o = _bdot(qd[r0:r0 + c, :], state) + jnp.dot(intra[r0:r0 + c, :], v_rep,
                                                          preferred_element_type=F32)
            state = state * jnp.exp(glast) + _bdot_tn(k[r0:r0 + c, :] * jnp.exp(glast - gcn), v_new)
            on = o * lax.rsqrt(jnp.mean(o * o, axis=-1, keepdims=True) + RMS_EPS) * onorm_ref[...]
            o_ref[0, r0:r0 + c, lo:lo + hd] = (on * _silu(z[r0:r0 + c, lo:lo + hd])).astype(o_ref.dtype)
        state_ref[hh] = state


def _gdn(qkv, z, gb, conv_w, a_log, dt_bias, o_norm):
    b, s, cw = qkv.shape
    tm = _tile(s, 256)
    nega = jnp.zeros((1, LANES), F32).at[0, :GDN_HEADS].set(-jnp.exp(a_log))
    dtb = jnp.zeros((1, LANES), F32).at[0, :GDN_HEADS].set(dt_bias)
    cwp = jnp.pad(conv_w, ((0, 8 - GDN_CONV), (0, 0)))
    ridx = np.arange(tm)
    same = (ridx[:, None] // GDN_CHUNK) == (ridx[None, :] // GDN_CHUNK)
    tril = jnp.asarray((same & (ridx[:, None] >= ridx[None, :])).astype(np.float32))
    triu = jnp.asarray((same & (ridx[:, None] <= ridx[None, :])).astype(np.float32))
    consts = [cwp, nega, dtb, o_norm[None, :], tril, triu]
    row = lambda n: pl.BlockSpec((1, tm, n), lambda bi, ti: (bi, ti, 0))
    return pl.pallas_call(
        _gdn_kernel,
        grid=(b, s // tm),
        in_specs=[row(cw), row(GDN_W), row(LANES)] + [_full(c.shape) for c in consts],
        out_specs=row(GDN_W),
        out_shape=jax.ShapeDtypeStruct((b, s, GDN_W), BF16),
        scratch_shapes=[pltpu.VMEM((tm + 8, cw), F32), pltpu.VMEM((GDN_HEADS, GDN_HEAD_DIM, GDN_HEAD_DIM), F32)],
        compiler_params=_params(("arbitrary", "arbitrary")),
        name="gdn",
    )(qkv, z, gb, *consts)


def _router_kernel(h_ref, g_ref, wr_ref, br_ref, tri_ref, xs_o, keyt_o, wt_o, cnt_o, sel_ref):
    tm = h_ref.shape[0]
    m = _rms(h_ref[...], g_ref[...])
    logits = _hdot(m, wr_ref[...]) + br_ref[...]
    lane = lax.broadcasted_iota(jnp.int32, (tm, LANES), 1)
    neg = -jnp.inf

    def first_argmax(x):
        mx = jnp.max(x, axis=-1, keepdims=True)
        idx = jnp.min(jnp.where(x == mx, lane, LANES), axis=-1, keepdims=True)
        return mx, idx

    is_g = (lane >= N_EXPERTS) & (lane < N_EXPERTS + MOE_GROUPS)
    gl = jnp.where(is_g, logits, neg)
    gmax, gidx = first_argmax(gl)
    g_w = 1.0 / jnp.sum(jnp.where(is_g, jnp.exp(gl - gmax), 0.0), axis=-1, keepdims=True)
    in_group = (lane // MOE_PER_GROUP) == (gidx - N_EXPERTS)
    el = jnp.where(in_group & (lane < N_EXPERTS), logits, neg)
    m1, i1 = first_argmax(el)
    el2 = jnp.where(lane == i1, neg, el)
    m2, i2 = first_argmax(el2)
    r = jnp.exp(m2 - m1)
    w1 = g_w / (1.0 + r)
    w2 = g_w * r / (1.0 + r)
    chose = (lane == i1) | (lane == i2)
    wmat = jnp.where(lane == i1, w1, jnp.where(lane == i2, w2, 0.0))
    ch = chose.astype(F32)
    rank = jnp.dot(tri_ref[...], ch.astype(BF16), preferred_element_type=F32)
    keyt = jnp.where(chose, rank, -1.0).T
    keyt_o[0] = keyt
    wt_o[0] = wmat.T
    cnt_o[0] = jnp.sum(ch, axis=0, keepdims=True).astype(jnp.int32)
    riota = lax.broadcasted_iota(jnp.int32, (MOE_CAP, tm), 0).astype(F32)
    for e in range(N_EXPERTS):
        sel_ref[e * MOE_CAP:(e + 1) * MOE_CAP, :] = jnp.where(keyt[e:e + 1, :] == riota, 1.0, 0.0).astype(BF16)
    xg = jnp.dot(sel_ref[...], m.astype(BF16), preferred_element_type=F32)
    xs_o[...] = xg.astype(BF16).reshape(xs_o.shape)


def _moe_router(hf, norm_g, w_group, b_group, w_expert, b_expert, tb):
    n, d = hf.shape
    nblk = n // tb
    wr = jnp.zeros((d, LANES), F32).at[:, :N_EXPERTS].set(w_expert)
    wr = wr.at[:, N_EXPERTS:N_EXPERTS + MOE_GROUPS].set(w_group)
    br = jnp.zeros((1, LANES), F32).at[0, :N_EXPERTS].set(b_expert)
    br = br.at[0, N_EXPERTS:N_EXPERTS + MOE_GROUPS].set(b_group)
    tri = jnp.asarray(np.tril(np.ones((tb, tb), np.float32), -1), BF16)
    blk = pl.BlockSpec((1, LANES, tb), lambda i: (i, 0, 0))
    return pl.pallas_call(
        _router_kernel,
        grid=(nblk,),
        in_specs=[pl.BlockSpec((tb, d), lambda i: (i, 0)), _full((1, d)), _full(wr.shape), _full(br.shape),
                  _full(tri.shape)],
        out_specs=[pl.BlockSpec((N_EXPERTS, MOE_CAP, d), lambda i: (0, i, 0)), blk, blk,
                   pl.BlockSpec((1, 1, LANES), lambda i: (i, 0, 0))],
        out_shape=[jax.ShapeDtypeStruct((N_EXPERTS, nblk * MOE_CAP, d), BF16),
                   jax.ShapeDtypeStruct((nblk, LANES, tb), F32),
                   jax.ShapeDtypeStruct((nblk, LANES, tb), F32),
                   jax.ShapeDtypeStruct((nblk, 1, LANES), jnp.int32)],
        scratch_shapes=[pltpu.VMEM((N_EXPERTS * MOE_CAP, tb), BF16)],
        compiler_params=_params(("arbitrary",)),
        name="moe_router",
    )(hf, norm_g[None, :], wr, br, tri)


def _expert_mlp_kernel(x_ref, wg_ref, wu_ref, wd_ref, y_ref, wg_sc, wu_sc, wd_sc):
    @pl.when(pl.program_id(1) == 0)
    def _():
        wg_sc[...] = wg_ref[0].astype(BF16)
        wu_sc[...] = wu_ref[0].astype(BF16)
        wd_sc[...] = wd_ref[0].astype(BF16)

    x = x_ref[0]
    hid = _silu(jnp.dot(x, wg_sc[...], preferred_element_type=F32)) * jnp.dot(
        x, wu_sc[...], preferred_element_type=F32)
    y_ref[0] = jnp.dot(hid.astype(BF16), wd_sc[...], preferred_element_type=F32).astype(BF16)


def _expert_mlp(xs, w_gate, w_up, w_down):
    ne, rows, d = xs.shape
    ff = w_gate.shape[2]
    tr = _tile(rows, 1024)
    return pl.pallas_call(
        _expert_mlp_kernel,
        grid=(ne, rows // tr),
        in_specs=[pl.BlockSpec((1, tr, d), lambda e, i: (e, i, 0)),
                  pl.BlockSpec((1, d, ff), lambda e, i: (e, 0, 0)),
                  pl.BlockSpec((1, d, ff), lambda e, i: (e, 0, 0)),
                  pl.BlockSpec((1, ff, d), lambda e, i: (e, 0, 0))],
        out_specs=pl.BlockSpec((1, tr, d), lambda e, i: (e, i, 0)),
        out_shape=jax.ShapeDtypeStruct((ne, rows, d), BF16),
        scratch_shapes=[pltpu.VMEM((d, ff), BF16), pltpu.VMEM((d, ff), BF16), pltpu.VMEM((ff, d), BF16)],
        compiler_params=_params(("arbitrary", "arbitrary")),
        name="moe_expert_mlp",
    )(xs, w_gate, w_up, w_down)


def _combine_kernel(y_ref, keyt_ref, wt_ref, h_ref, o_ref, sel_ref):
    tb = h_ref.shape[0]
    riota = lax.broadcasted_iota(jnp.int32, (MOE_CAP, tb), 0).astype(F32)
    for e in range(N_EXPERTS):
        hit = keyt_ref[0, e:e + 1, :] == riota
        sel_ref[e * MOE_CAP:(e + 1) * MOE_CAP, :] = jnp.where(hit, wt_ref[0, e:e + 1, :], 0.0).astype(BF16)
    y = y_ref[...].reshape(N_EXPERTS * MOE_CAP, y_ref.shape[2])
    o_ref[...] = h_ref[...] + lax.dot_general(sel_ref[...], y, (((0,), (0,)), ((), ())),
                                              preferred_element_type=F32)


def _moe_combine(ys, keyt, wt, hf, tb):
    n, d = hf.shape
    nblk = n // tb
    blk = pl.BlockSpec((1, LANES, tb), lambda i: (i, 0, 0))
    return pl.pallas_call(
        _combine_kernel,
        grid=(nblk,),
        in_specs=[pl.BlockSpec((N_EXPERTS, MOE_CAP, d), lambda i: (0, i, 0)), blk, blk,
                  pl.BlockSpec((tb, d), lambda i: (i, 0))],
        out_specs=pl.BlockSpec((tb, d), lambda i: (i, 0)),
        out_shape=jax.ShapeDtypeStruct((n, d), F32),
        scratch_shapes=[pltpu.VMEM((N_EXPERTS * MOE_CAP, tb), BF16)],
        compiler_params=_params(("arbitrary",)),
        name="moe_combine",
    )(ys, keyt, wt, hf)


def _overflow_kernel(cnt_ref, h_ref, g_ref, keyt_ref, wt_ref, base_ref, wg_ref, wu_ref, wd_ref, o_ref,
                     acc_ref, m_ref):
    blk = pl.program_id(0)
    e = pl.program_id(1)
    tb = h_ref.shape[0]

    @pl.when(e == 0)
    def _():
        acc_ref[...] = base_ref[...]
        m_ref[...] = _rms(h_ref[...], g_ref[...]).astype(BF16)

    extra = jnp.maximum(cnt_ref[blk * LANES + e] - MOE_CAP, 0)
    krow = keyt_ref[0, pl.ds(e, 1), :]
    wrow = wt_ref[0, pl.ds(e, 1), :]
    riota = lax.broadcasted_iota(jnp.int32, (MOE_ROWS, tb), 0).astype(F32)

    def chunk(ci, carry):
        hit = krow == (riota + (MOE_CAP + ci * MOE_ROWS).astype(F32))
        sel = jnp.where(hit, 1.0, 0.0).astype(BF16)
        xg = jnp.dot(sel, m_ref[...], preferred_element_type=F32).astype(BF16)
        hid = _silu(_bdot(xg, wg_ref[0])) * _bdot(xg, wu_ref[0])
        y = _bdot(hid, wd_ref[0]).astype(BF16)
        acc_ref[...] += _bdot_tn(jnp.where(hit, wrow, 0.0), y)
        return carry

    lax.fori_loop(0, (extra + MOE_ROWS - 1) // MOE_ROWS, chunk, 0)

    @pl.when(e == N_EXPERTS - 1)
    def _():
        o_ref[...] = acc_ref[...]


def _moe_overflow(cnt, hf, norm_g, keyt, wt, base, w_gate, w_up, w_down, tb):
    n, d = hf.shape
    ff = w_gate.shape[2]
    nblk = n // tb
    tok = pl.BlockSpec((tb, d), lambda i, e, c: (i, 0))
    blk = pl.BlockSpec((1, LANES, tb), lambda i, e, c: (i, 0, 0))
    grid_spec = pltpu.PrefetchScalarGridSpec(
        num_scalar_prefetch=1,
        grid=(nblk, N_EXPERTS),
        in_specs=[tok, pl.BlockSpec((1, d), lambda i, e, c: (0, 0)), blk, blk, tok,
                  pl.BlockSpec((1, d, ff), lambda i, e, c: (e, 0, 0)),
                  pl.BlockSpec((1, d, ff), lambda i, e, c: (e, 0, 0)),
                  pl.BlockSpec((1, ff, d), lambda i, e, c: (e, 0, 0))],
        out_specs=tok,
        scratch_shapes=[pltpu.VMEM((tb, d), F32), pltpu.VMEM((tb, d), BF16)],
    )
    return pl.pallas_call(
        _overflow_kernel,
        grid_spec=grid_spec,
        out_shape=jax.ShapeDtypeStruct((n, d), F32),
        compiler_params=_params(("arbitrary", "arbitrary")),
        name="moe_overflow",
    )(cnt, hf, norm_g[None, :], keyt, wt, base, w_gate, w_up, w_down)


def _hier_moe_residual(h, norm_g, w_group, b_group, w_expert, b_expert, w_gate, w_up, w_down):
    b, s, d = h.shape
    hf = h.reshape(b * s, d)
    tb = _tile(b * s, MOE_TB)
    xs, keyt, wt, cnt = _moe_router(hf, norm_g, w_group, b_group, w_expert, b_expert, tb)
    ys = _expert_mlp(xs, w_gate, w_up, w_down)
    out = _moe_combine(ys, keyt, wt, hf, tb)
    out = lax.cond(jnp.max(cnt[:, 0, :N_EXPERTS]) > MOE_CAP,
                   lambda o: _moe_overflow(cnt.reshape(-1), hf, norm_g, keyt, wt, o, w_gate, w_up, w_down, tb),
                   lambda o: o, out)
    return out.reshape(b, s, d)


def _ple_kernel(h_ref, p_ref, g_ref, wg_ref, bg_ref, wp_ref, o_ref):
    h = h_ref[0]
    gate = jax.nn.sigmoid(_bdot(_rms(h, g_ref[...]), wg_ref[...]) + bg_ref[...])
    o_ref[0] = h + gate * _bdot(p_ref[0], wp_ref[...])


def _ple_residual(h, p_i, w_proj, norm_g, w_gate, b_gate):
    b, s, d = h.shape
    pd = p_i.shape[2]
    tm = _tile(s, 512)
    row = lambda n: pl.BlockSpec((1, tm, n), lambda bi, ti: (bi, ti, 0))
    return pl.pallas_call(
        _ple_kernel,
        grid=(b, s // tm),
        in_specs=[row(d), row(pd), _full((1, d)), _full((d, d)), _full((1, d)), _full((pd, d))],
        out_specs=row(d),
        out_shape=jax.ShapeDtypeStruct((b, s, d), F32),
        compiler_params=_params(("arbitrary", "arbitrary")),
        name="ple",
    )(h, p_i, norm_g[None, :], w_gate.astype(BF16), b_gate[None, :], w_proj.astype(BF16))


def _even_layer(h, positions, norm_mix, w_in, b_f, fox_qn, fox_kn, q_a_norm, w_q_up, kv_a_norm, w_kv_up,
                mla_qn, mla_kn, w_out):
    fq, fk, fv, cum, mq, mk, mv = _even_pre(h, positions, norm_mix, w_in, b_f, fox_qn, fox_kn, q_a_norm,
                                            w_q_up, kv_a_norm, w_kv_up, mla_qn, mla_kn)
    o_fox = _attention(fq, fk, fv, cum)
    o_mla = _attention(mq, mk, mv)
    return _proj2_residual(o_fox, o_mla, w_out, h)


def _odd_layer(h, norm_mix, w_in, a_re, a_im, b_re, b_im, c_re, c_im, d_skip, log_step, w_glu, b_glu,
               conv_w, a_log, dt_bias, o_norm, w_out):
    u, qkv, z, gb = _odd_pre(h, norm_mix, w_in)
    y_ssm = _s5(u, a_re, a_im, b_re, b_im, c_re, c_im, d_skip, log_step, w_glu, b_glu)
    o_gdn = _gdn(qkv, z, gb, conv_w, a_log, dt_bias, jnp.tile(o_norm, 1))
    return _proj2_residual(y_ssm, o_gdn, w_out, h)


def kernel(x, p, positions, norm_mix, norm_ffn, ev_w_in, fox_b_f, fox_q_norm, fox_k_norm, mla_q_a_norm, mla_w_q_up, mla_kv_a_norm, mla_w_kv_up, mla_q_norm, mla_k_norm, ev_w_out, od_w_in, s5_a_re, s5_a_im, s5_b_re, s5_b_im, s5_c_re, s5_c_im, s5_d, s5_log_step, s5_w_glu, s5_b_glu, gdn_conv_w, gdn_a_log, gdn_dt_bias, gdn_o_norm, od_w_out, moe_w_group, moe_b_group, moe_w_expert, moe_b_expert, moe_w_gate, moe_w_up, moe_w_down, ple_w_proj, ple_norm, ple_w_gate, ple_b_gate):
    h = x
    depth = p.shape[0]
    for i in range(depth):
        j = i // 2
        if i % 2 == 0:
            h = _even_layer(h, positions, norm_mix[i], ev_w_in[j], fox_b_f[j], fox_q_norm[j], fox_k_norm[j],
                            mla_q_a_norm[j], mla_w_q_up[j], mla_kv_a_norm[j], mla_w_kv_up[j], mla_q_norm[j],
                            mla_k_norm[j], ev_w_out[j])
        else:
            h = _odd_layer(h, norm_mix[i], od_w_in[j], s5_a_re[j], s5_a_im[j], s5_b_re[j], s5_b_im[j],
                           s5_c_re[j], s5_c_im[j], s5_d[j], s5_log_step[j], s5_w_glu[j], s5_b_glu[j],
                           gdn_conv_w[j], gdn_a_log[j], gdn_dt_bias[j], gdn_o_norm[j], od_w_out[j])
        h = _hier_moe_residual(h, norm_ffn[i], moe_w_group[i], moe_b_group[i], moe_w_expert[i],
                               moe_b_expert[i], moe_w_gate[i], moe_w_up[i], moe_w_down[i])
        h = _ple_residual(h, p[i], ple_w_proj[i], ple_norm[i], ple_w_gate[i], ple_b_gate[i])
    return h
```

```python
import functools
import math

import numpy as np
import jax
import jax.numpy as jnp
from jax import lax
from jax.experimental import pallas as pl
from jax.experimental.pallas import tpu as pltpu

F32 = jnp.float32
BF16 = jnp.bfloat16
HI = lax.Precision.HIGHEST

LANES = 128
RMS_EPS = 1e-6
ROPE_THETA = 10000.0
LOG2E = math.log2(math.e)

FOX_HEADS = 8
FOX_HEAD_DIM = 64
MLA_HEADS = 8
MLA_Q_LORA = 384
MLA_KV_LORA = 256
MLA_NOPE = 64
MLA_ROPE = 32
MLA_V = 64
MLA_QK = MLA_NOPE + MLA_ROPE

S5_CH = 512
S5_GROUP_CH = 16
S5_GROUPS = S5_CH // S5_GROUP_CH
S5_STATE = 64
S5_N = S5_GROUPS * S5_STATE

GDN_HEADS = 4
GDN_HEAD_DIM = 128
GDN_W = GDN_HEADS * GDN_HEAD_DIM
GDN_CONV = 4
GDN_CHUNK = 64

MOE_GROUPS = 4
MOE_PER_GROUP = 8
N_EXPERTS = MOE_GROUPS * MOE_PER_GROUP
MOE_TB = 512
MOE_CAP = 64
MOE_ROWS = 128
ATTN_TQ = 1024
ATTN_TK = 512

VMEM_LIMIT = 56 * 1024 * 1024


def _tile(n, pref):
    t = min(n, pref)
    assert n % t == 0, (n, t)
    return t


def _params(sem):
    return pltpu.CompilerParams(dimension_semantics=sem, vmem_limit_bytes=VMEM_LIMIT)


def _full(shape):
    nd = len(shape)
    return pl.BlockSpec(shape, lambda *_: (0,) * nd)


def _rms(x, g):
    return x * lax.rsqrt(jnp.mean(x * x, axis=-1, keepdims=True) + RMS_EPS) * g


def _bdot(a, b):
    return jnp.dot(a.astype(BF16), b.astype(BF16), preferred_element_type=F32)


def _bdot_nt(a, b):
    return lax.dot_general(a.astype(BF16), b.astype(BF16), (((1,), (1,)), ((), ())),
                           preferred_element_type=F32)


def _bdot_tn(a, b):
    return lax.dot_general(a.astype(BF16), b.astype(BF16), (((0,), (0,)), ((), ())),
                           preferred_element_type=F32)


def _hdot(a, b):
    return jnp.dot(a, b, precision=HI, preferred_element_type=F32)


def _split_dot(x, ind):
    hi = x.astype(BF16)
    lo = (x - hi.astype(F32)).astype(BF16)
    return (jnp.dot(hi, ind, preferred_element_type=F32)
            + jnp.dot(lo, ind, preferred_element_type=F32))


def _log_sigmoid(x):
    return jnp.minimum(x, 0.0) - jnp.log(1.0 + jnp.exp(-jnp.abs(x)))


def _softplus(x):
    return jnp.maximum(x, 0.0) + jnp.log(1.0 + jnp.exp(-jnp.abs(x)))


def _silu(x):
    return x * jax.nn.sigmoid(x)


def _head_norm128(x, nheads, denom, gain):
    outs = []
    for hh in range(nheads):
        xh = x[:, LANES * hh:LANES * (hh + 1)]
        ss = jnp.sum(xh * xh, axis=-1, keepdims=True)
        outs.append(xh * lax.rsqrt(ss / denom + RMS_EPS))
    return jnp.concatenate(outs, axis=1) * gain


def _even_pre_kernel(h_ref, pos_ref, nmix_ref, win_ref, ind_ref, fqn_ref, fkn_ref, bf_ref,
                     qan_ref, wq_ref, kvan_ref, wkv_ref, mqn_ref, mkn_ref, freq_ref, s1_ref, s2_ref,
                     tri_ref, fq_o, fk_o, fv_o, cum_o, mq_o, mk_o, mv_o, carry_ref):
    t = pl.program_id(1)

    @pl.when(t == 0)
    def _():
        carry_ref[...] = jnp.zeros_like(carry_ref)

    tm = h_ref.shape[1]
    a = _rms(h_ref[0], nmix_ref[...])
    proj = _bdot(a, win_ref[...])
    nf = FOX_HEADS * FOX_HEAD_DIM
    fq = proj[:, 0:nf]
    fk = proj[:, nf:2 * nf]
    fv = proj[:, 2 * nf:3 * nf]
    o_cq = 3 * nf
    cq = proj[:, o_cq:o_cq + MLA_Q_LORA]
    o_ckv = o_cq + MLA_Q_LORA
    ckv = proj[:, o_ckv:o_ckv + MLA_KV_LORA]
    misc = proj[:, o_ckv + MLA_KV_LORA:]

    ind = ind_ref[...]
    fq_n = fq * lax.rsqrt(_split_dot(fq * fq, ind) / FOX_HEAD_DIM + RMS_EPS) * fqn_ref[...]
    fk_n = fk * lax.rsqrt(_split_dot(fk * fk, ind) / FOX_HEAD_DIM + RMS_EPS) * fkn_ref[...]
    fq_o[0] = (fq_n * (FOX_HEAD_DIM ** -0.5 * LOG2E)).astype(BF16)
    fk_o[0] = fk_n.astype(BF16)
    fv_o[0] = fv.astype(BF16)

    lane = lax.broadcasted_iota(jnp.int32, (tm, LANES), 1)
    logf = jnp.where(lane < FOX_HEADS, _log_sigmoid(misc + bf_ref[...]), 0.0)
    cum = _hdot(tri_ref[...], logf) + carry_ref[...]
    carry_ref[...] = cum[tm - 1:tm, :]
    cum_o[0] = (cum * LOG2E).T[:FOX_HEADS, :]

    ang = pos_ref[0].astype(F32) * freq_ref[...]
    cos1 = jnp.cos(ang)
    sin1 = jnp.sin(ang)
    cos = jnp.concatenate([cos1] * MLA_HEADS, axis=1)
    sin_a = jnp.concatenate([sin1 * s1_ref[...]] * MLA_HEADS, axis=1)
    sin_b = jnp.concatenate([sin1 * s2_ref[...]] * MLA_HEADS, axis=1)
    width = MLA_HEADS * LANES
    half = MLA_ROPE // 2

    def rope(x):
        return (x * cos + pltpu.roll(x, width - half, 1) * sin_a + pltpu.roll(x, half, 1) * sin_b)

    q = _bdot(_rms(cq, qan_ref[...]), wq_ref[...])
    q = rope(_head_norm128(q, MLA_HEADS, MLA_QK, mqn_ref[...]))
    mq_o[0] = (q * (MLA_QK ** -0.5 * LOG2E)).astype(BF16)

    kv = _bdot(_rms(ckv, kvan_ref[...]), wkv_ref[...])
    kr = pltpu.roll(misc, MLA_NOPE - FOX_HEADS, 1)
    kr = jnp.where((lane >= MLA_NOPE) & (lane < MLA_QK), kr, 0.0)
    k = kv[:, :width] + jnp.concatenate([kr] * MLA_HEADS, axis=1)
    k = rope(_head_norm128(k, MLA_HEADS, MLA_QK, mkn_ref[...]))
    mk_o[0] = k.astype(BF16)
    mv_o[0] = kv[:, width:].astype(BF16)


def _even_pre(h, positions, norm_mix, w_in, b_f, fox_qn, fox_kn, q_a_norm, w_q_up, kv_a_norm, w_kv_up,
              mla_qn, mla_kn):
    b, s, d = h.shape
    tm = _tile(s, 256)
    nf = FOX_HEADS * FOX_HEAD_DIM
    sizes = (nf, nf, nf, FOX_HEADS, MLA_Q_LORA, MLA_KV_LORA, MLA_ROPE)
    offs = np.concatenate([[0], np.cumsum(sizes)])
    parts = [w_in[:, offs[i]:offs[i + 1]] for i in range(len(sizes))]
    pad = jnp.zeros((d, LANES - FOX_HEADS - MLA_ROPE), w_in.dtype)
    win = jnp.concatenate([parts[0], parts[1], parts[2], parts[4], parts[5], parts[3], parts[6], pad],
                          axis=1).astype(BF16)
    ncol = win.shape[1]
    gidx = np.arange(nf) // FOX_HEAD_DIM
    ind = jnp.asarray(gidx[:, None] == gidx[None, :], BF16)
    fqn = jnp.tile(fox_qn, FOX_HEADS)[None, :]
    fkn = jnp.tile(fox_kn, FOX_HEADS)[None, :]
    bf = jnp.zeros((1, LANES), F32).at[0, :FOX_HEADS].set(b_f)
    padq = LANES - MLA_QK
    wq = jnp.pad(w_q_up.reshape(MLA_Q_LORA, MLA_HEADS, MLA_QK), ((0, 0), (0, 0), (0, padq)))
    wq = wq.reshape(MLA_Q_LORA, MLA_HEADS * LANES).astype(BF16)
    wkv3 = w_kv_up.reshape(MLA_KV_LORA, MLA_HEADS, MLA_NOPE + MLA_V)
    wk = jnp.pad(wkv3[:, :, :MLA_NOPE], ((0, 0), (0, 0), (0, LANES - MLA_NOPE)))
    wkv = jnp.concatenate([wk.reshape(MLA_KV_LORA, MLA_HEADS * LANES),
                           wkv3[:, :, MLA_NOPE:].reshape(MLA_KV_LORA, MLA_HEADS * MLA_V)],
                          axis=1).astype(BF16)
    mqn = jnp.tile(jnp.pad(mla_qn, (0, padq)), MLA_HEADS)[None, :]
    mkn = jnp.tile(jnp.pad(mla_kn, (0, padq)), MLA_HEADS)[None, :]
    half = MLA_ROPE // 2
    inv = ROPE_THETA ** (-jnp.arange(half, dtype=F32) * 2.0 / MLA_ROPE)
    freq = jnp.zeros((1, LANES), F32).at[0, MLA_NOPE:MLA_NOPE + half].set(inv)
    freq = freq.at[0, MLA_NOPE + half:MLA_QK].set(inv)
    s1 = jnp.zeros((1, LANES), F32).at[0, MLA_NOPE:MLA_NOPE + half].set(-1.0)
    s2 = jnp.zeros((1, LANES), F32).at[0, MLA_NOPE + half:MLA_QK].set(1.0)
    tri = jnp.asarray(np.tril(np.ones((tm, tm), np.float32)))
    pos3 = positions.reshape(b, s, 1)

    row = lambda n: pl.BlockSpec((1, tm, n), lambda bi, ti: (bi, ti, 0))
    consts = [norm_mix[None, :], win, ind, fqn, fkn, bf, q_a_norm[None, :], wq, kv_a_norm[None, :], wkv,
              mqn, mkn, freq, s1, s2, tri]
    out_shape = [jax.ShapeDtypeStruct((b, s, nf), BF16)] * 3 + [
        jax.ShapeDtypeStruct((b, FOX_HEADS, s), F32),
        jax.ShapeDtypeStruct((b, s, MLA_HEADS * LANES), BF16),
        jax.ShapeDtypeStruct((b, s, MLA_HEADS * LANES), BF16),
        jax.ShapeDtypeStruct((b, s, MLA_HEADS * MLA_V), BF16)]
    return pl.pallas_call(
        _even_pre_kernel,
        grid=(b, s // tm),
        in_specs=[row(d), row(1)] + [_full(c.shape) for c in consts],
        out_specs=[row(nf), row(nf), row(nf), pl.BlockSpec((1, FOX_HEADS, tm), lambda bi, ti: (bi, 0, ti)),
                   row(MLA_HEADS * LANES), row(MLA_HEADS * LANES), row(MLA_HEADS * MLA_V)],
        out_shape=out_shape,
        scratch_shapes=[pltpu.VMEM((1, LANES), F32)],
        compiler_params=_params(("arbitrary", "arbitrary")),
        name="even_pre",
    )(h, pos3, *consts)


def _attn_kernel(*refs, tq, tk, fox):
    if fox:
        q_ref, k_ref, v_ref, cr_ref, o_ref = refs
    else:
        q_ref, k_ref, v_ref, o_ref = refs
    hp = pl.program_id(1)
    i = pl.program_id(2)
    lane = lax.broadcasted_iota(jnp.int32, (tq, LANES), 1)
    rowi = lax.broadcasted_iota(jnp.int32, (tq, tk), 0)
    coli = lax.broadcasted_iota(jnp.int32, (tq, tk), 1)
    qs = []
    for hh in range(2):
        if fox:
            in_head = (lane >= FOX_HEAD_DIM * hh) & (lane < FOX_HEAD_DIM * (hh + 1))
            qs.append(jnp.where(in_head, q_ref[0], jnp.zeros((), BF16)))
        else:
            qs.append(q_ref[0, :, LANES * hh:LANES * (hh + 1)])

    def scores(hh, koff, diag):
        if fox:
            kj = k_ref[0, pl.ds(koff, tk), :]
        else:
            kj = k_ref[0, pl.ds(koff, tk), LANES * hh:LANES * (hh + 1)]
        sc = lax.dot_general(qs[hh], kj, (((1,), (1,)), ((), ())), preferred_element_type=F32)
        if fox:
            sc = sc - cr_ref[0, pl.ds(2 * hp + hh, 1), pl.ds(koff, tk)]
        if diag is not None:
            sc = jnp.where(coli + diag <= rowi, sc, -jnp.inf)
        return sc

    def step(j, carry, diag=None):
        koff = pl.multiple_of(j * tk, tk)
        vj = v_ref[0, pl.ds(koff, tk), :]
        new = []
        for hh in range(2):
            m, l, acc = carry[hh]
            sc = scores(hh, koff, diag)
            m_new = jnp.maximum(m, jnp.max(sc, axis=-1, keepdims=True))
            alpha = jnp.exp2(m - m_new)
            p = jnp.exp2(sc - m_new)
            l = alpha * l + jnp.sum(p, axis=-1, keepdims=True)
            acc = alpha * acc + jnp.dot(p.astype(BF16), vj, preferred_element_type=F32)
            new.append((m_new, l, acc))
        return tuple(new)

    def body(jj, carry):
        for r in range(ratio):
            carry = step(jj * ratio + r, carry)
        return carry

    one = (jnp.full((tq, 1), -jnp.inf, F32), jnp.zeros((tq, 1), F32), jnp.zeros((tq, LANES), F32))
    ratio = tq // tk
    carry = lax.fori_loop(0, i, body, (one, one))
    for r in range(ratio):
        carry = step(i * ratio + r, carry, diag=r * tk)
    (_, l0, acc0), (_, l1, acc1) = carry
    o_ref[0] = jnp.where(lane < MLA_V, acc0 / l0, acc1 / l1).astype(o_ref.dtype)


def _attention(q, k, v, cum_row=None):
    b, s, _ = v.shape
    fox = cum_row is not None
    qw = LANES if fox else 2 * LANES
    tq = _tile(s, ATTN_TQ)
    tk = _tile(tq, ATTN_TK)
    npairs = v.shape[2] // LANES
    in_specs = [pl.BlockSpec((1, tq, qw), lambda bi, hp, i: (bi, i, hp)),
                pl.BlockSpec((1, s, qw), lambda bi, hp, i: (bi, 0, hp)),
                pl.BlockSpec((1, s, LANES), lambda bi, hp, i: (bi, 0, hp))]
    args = [q, k, v]
    if fox:
        in_specs += [pl.BlockSpec((1, FOX_HEADS, s), lambda bi, hp, i: (bi, 0, 0))]
        args += [cum_row]
    return pl.pallas_call(
        functools.partial(_attn_kernel, tq=tq, tk=tk, fox=fox),
        grid=(b, npairs, s // tq),
        in_specs=in_specs,
        out_specs=pl.BlockSpec((1, tq, LANES), lambda bi, hp, i: (bi, i, hp)),
        out_shape=jax.ShapeDtypeStruct((b, s, npairs * LANES), BF16),
        compiler_params=_params(("arbitrary", "arbitrary", "arbitrary")),
        name="fox_attention" if fox else "mla_attention",
    )(*args)


def _proj2_kernel(a_ref, b_ref, wa_ref, wb_ref, h_ref, o_ref):
    o_ref[0] = (h_ref[0] + jnp.dot(a_ref[0], wa_ref[...], preferred_element_type=F32)
                + jnp.dot(b_ref[0], wb_ref[...], preferred_element_type=F32))


def _proj2_residual(a, bb, w_out, h):
    b, s, d = h.shape
    na = a.shape[2]
    nb = bb.shape[2]
    tm = _tile(s, 512)
    wa = w_out[:na].astype(BF16)
    wb = w_out[na:].astype(BF16)
    row = lambda n: pl.BlockSpec((1, tm, n), lambda bi, ti: (bi, ti, 0))
    return pl.pallas_call(
        _proj2_kernel,
        grid=(b, s // tm),
        in_specs=[row(na), row(nb), _full(wa.shape), _full(wb.shape), row(d)],
        out_specs=row(d),
        out_shape=jax.ShapeDtypeStruct((b, s, d), F32),
        compiler_params=_params(("arbitrary", "arbitrary")),
        name="out_proj",
    )(a, bb, wa, wb, h)


def _odd_pre_kernel(h_ref, nmix_ref, win_ref, u_o, qkv_o, z_o, gb_o):
    a = _rms(h_ref[0], nmix_ref[...])
    proj = _bdot(a, win_ref[...])
    o1 = S5_CH
    o2 = o1 + 3 * GDN_W
    o3 = o2 + GDN_W
    u_o[0] = proj[:, :o1]
    qkv_o[0] = proj[:, o1:o2]
    z_o[0] = proj[:, o2:o3]
    gb_o[0] = proj[:, o3:]


def _odd_pre(h, norm_mix, w_in):
    b, s, d = h.shape
    tm = _tile(s, 256)
    sizes = (S5_CH, 3 * GDN_W, GDN_HEADS, GDN_HEADS, GDN_W)
    offs = np.concatenate([[0], np.cumsum(sizes)])
    parts = [w_in[:, offs[i]:offs[i + 1]] for i in range(len(sizes))]
    pad = jnp.zeros((d, LANES - 2 * GDN_HEADS), w_in.dtype)
    win = jnp.concatenate([parts[0], parts[1], parts[4], parts[2], parts[3], pad], axis=1).astype(BF16)
    row = lambda n: pl.BlockSpec((1, tm, n), lambda bi, ti: (bi, ti, 0))
    widths = (S5_CH, 3 * GDN_W, GDN_W, LANES)
    return pl.pallas_call(
        _odd_pre_kernel,
        grid=(b, s // tm),
        in_specs=[row(d), _full((1, d)), _full(win.shape)],
        out_specs=[row(n) for n in widths],
        out_shape=[jax.ShapeDtypeStruct((b, s, n), F32) for n in widths],
        compiler_params=_params(("arbitrary", "arbitrary")),
        name="odd_pre",
    )(h, norm_mix[None, :], win)


def _s5_kernel(u_ref, perm_ref, unperm_ref, bbd_ref, cbd_ref, ar_ref, ai_ref, asr_ref, asi_ref, pwr_ref, pwi_ref,
               d_ref, wglu_ref, bglu_ref, o_ref, x_ref, sr_ref, si_ref):
    t = pl.program_id(1)

    @pl.when(t == 0)
    def _():
        sr_ref[...] = jnp.zeros_like(sr_ref)
        si_ref[...] = jnp.zeros_like(si_ref)

    tm = u_ref.shape[1]
    nseg = 8
    seg = tm // nseg
    u = u_ref[0]
    u1 = u.astype(BF16)
    u2 = (u - u1.astype(F32)).astype(BF16)
    u3 = (u - u1.astype(F32) - u2.astype(F32)).astype(BF16)
    perm = perm_ref[...]
    u = (jnp.dot(perm, u1, preferred_element_type=F32) + jnp.dot(perm, u2, preferred_element_type=F32)
         + jnp.dot(perm, u3, preferred_element_type=F32))
    hc = S5_CH // 2
    hn = S5_N // 2
    ub = u.astype(BF16)
    for part in range(2):
        for base in (0, S5_N):
            cols = slice(base + part * hn, base + (part + 1) * hn)
            x_ref[:, cols] = jnp.dot(ub[:, part * hc:(part + 1) * hc], bbd_ref[part * hc:(part + 1) * hc, cols],
                                     preferred_element_type=F32)
    ar = ar_ref[...]
    ai = ai_ref[...]
    re = slice(0, S5_N)
    im = slice(S5_N, 2 * S5_N)

    def local(i, carry):
        xr, xi = carry
        rows = pl.ds(pl.multiple_of(i * nseg, nseg), nseg)
        nr = ar * xr - ai * xi + x_ref[rows, re]
        ni = ar * xi + ai * xr + x_ref[rows, im]
        x_ref[rows, re] = nr
        x_ref[rows, im] = ni
        return nr, ni

    zero = jnp.zeros((nseg, S5_N), F32)
    er, ei = lax.fori_loop(0, seg, local, (zero, zero), unroll=4)

    asr = asr_ref[...]
    asi = asi_ref[...]
    cr = [sr_ref[...]]
    ci = [si_ref[...]]
    for s in range(nseg):
        cr.append(asr * cr[s] - asi * ci[s] + er[s:s + 1, :])
        ci.append(asr * ci[s] + asi * cr[s] + ei[s:s + 1, :])
    sr_ref[...] = cr[nseg]
    si_ref[...] = ci[nseg]
    ent_r = jnp.concatenate(cr[:nseg], axis=0)
    ent_i = jnp.concatenate(ci[:nseg], axis=0)

    def fix(i, c):
        rows = pl.ds(pl.multiple_of(i * nseg, nseg), nseg)
        pr = pwr_ref[pl.ds(i, 1), :]
        pi = pwi_ref[pl.ds(i, 1), :]
        x_ref[rows, re] += pr * ent_r - pi * ent_i
        x_ref[rows, im] += pr * ent_i + pi * ent_r
        return c

    lax.fori_loop(0, seg, fix, 0, unroll=4)
    ys = []
    for part in range(2):
        oc = slice(part * hc, (part + 1) * hc)
        acc = None
        for base in (0, S5_N):
            rows = slice(base + part * hn, base + (part + 1) * hn)
            term = _bdot(x_ref[:, rows], cbd_ref[rows, oc])
            acc = term if acc is None else acc + term
        ys.append(acc)
    y = jnp.concatenate(ys, axis=1) + d_ref[...] * u
    hg = jax.nn.gelu(y)
    out = (hg * jax.nn.sigmoid(_bdot(hg, wglu_ref[...]) + bglu_ref[...])).astype(BF16)
    o_ref[0] = jnp.dot(unperm_ref[...], out, preferred_element_type=F32).astype(o_ref.dtype)


def _s5(u, a_re, a_im, b_re, b_im, c_re, c_im, d_skip, log_step, w_glu, b_glu):
    b, s, _ = u.shape
    tm = _tile(s, 256)
    lam_re = jnp.minimum(a_re, -1e-4)
    lam_im = a_im
    dt = jnp.exp(log_step)[:, None]
    mag = jnp.exp(lam_re * dt)
    ab_re = mag * jnp.cos(lam_im * dt)
    ab_im = mag * jnp.sin(lam_im * dt)
    den = lam_re * lam_re + lam_im * lam_im
    nr, ni = ab_re - 1.0, ab_im
    gam_re = (nr * lam_re + ni * lam_im) / den
    gam_im = (ni * lam_re - nr * lam_im) / den
    bb_re = gam_re[..., None] * b_re - gam_im[..., None] * b_im
    bb_im = gam_re[..., None] * b_im + gam_im[..., None] * b_re
    eye = jnp.eye(S5_GROUPS, dtype=F32)
    bd_in = lambda m: jnp.einsum('gpc,gh->gchp', m, eye).reshape(S5_CH, S5_N)
    bd_out = lambda m: jnp.einsum('gcp,gh->gphc', m, eye).reshape(S5_N, S5_CH)
    bbd = jnp.concatenate([bd_in(bb_re), bd_in(bb_im)], axis=1).astype(BF16)
    cbd = jnp.concatenate([bd_out(c_re), -bd_out(c_im)], axis=0).astype(BF16)
    seg = tm // 8
    steps = jnp.arange(1, seg + 1, dtype=F32)[:, None, None] * dt[None]
    pmag = jnp.exp(lam_re[None] * steps)
    pw_re = (pmag * jnp.cos(lam_im[None] * steps)).reshape(seg, S5_N)
    pw_im = (pmag * jnp.sin(lam_im[None] * steps)).reshape(seg, S5_N)
    src = (np.arange(tm) % 8) * seg + np.arange(tm) // 8
    perm = np.zeros((tm, tm), np.float32)
    perm[np.arange(tm), src] = 1.0
    consts = [jnp.asarray(perm, BF16), jnp.asarray(perm.T, BF16),
              bbd, cbd, ab_re.reshape(1, S5_N), ab_im.reshape(1, S5_N), pw_re[seg - 1:seg], pw_im[seg - 1:seg],
              pw_re, pw_im, d_skip[None, :], w_glu.astype(BF16), b_glu[None, :]]
    row = pl.BlockSpec((1, tm, S5_CH), lambda bi, ti: (bi, ti, 0))
    return pl.pallas_call(
        _s5_kernel,
        grid=(b, s // tm),
        in_specs=[row] + [_full(c.shape) for c in consts],
        out_specs=row,
        out_shape=jax.ShapeDtypeStruct((b, s, S5_CH), BF16),
        scratch_shapes=[pltpu.VMEM((tm, 2 * S5_N), F32), pltpu.VMEM((1, S5_N), F32),
                        pltpu.VMEM((1, S5_N), F32)],
        compiler_params=_params(("arbitrary", "arbitrary")),
        name="s5",
    )(u, *consts)


def _gdn_kernel(x_ref, z_ref, gb_ref, cw_ref, nega_ref, dtb_ref, onorm_ref, tril_ref, triu_ref, o_ref,
                xpad_ref, state_ref):
    t = pl.program_id(1)
    tm = x_ref.shape[1]
    c = GDN_CHUNK
    hd = GDN_HEAD_DIM

    @pl.when(t == 0)
    def _():
        xpad_ref[0:8, :] = jnp.zeros((8, xpad_ref.shape[1]), F32)
        state_ref[...] = jnp.zeros_like(state_ref)

    @pl.when(t > 0)
    def _():
        xpad_ref[0:8, :] = xpad_ref[tm:tm + 8, :]

    xpad_ref[8:tm + 8, :] = x_ref[0]
    conv = cw_ref[0:1, :] * xpad_ref[pl.ds(8 - (GDN_CONV - 1), tm), :]
    for i in range(1, GDN_CONV):
        conv = conv + cw_ref[i:i + 1, :] * xpad_ref[pl.ds(8 - (GDN_CONV - 1) + i, tm), :]
    act = _silu(conv)

    def l2n(x):
        return x * lax.rsqrt(jnp.sum(x * x, axis=-1, keepdims=True) + RMS_EPS)

    gb = gb_ref[0]
    g = nega_ref[...] * _softplus(gb + dtb_ref[...])
    beta = jax.nn.sigmoid(gb)
    gc = _hdot(tril_ref[...], g)
    gct = _hdot(g.T, triu_ref[...])

    ri = lax.broadcasted_iota(jnp.int32, (tm, tm), 0)
    ci = lax.broadcasted_iota(jnp.int32, (tm, tm), 1)
    same = (ri // c) == (ci // c)
    incl = same & (ri >= ci)
    strict = same & (ri > ci)
    eye = (ri == ci).astype(F32)
    offs = []
    bs = 1
    while bs < c:
        offs.append(((ri // (2 * bs)) == (ci // (2 * bs))) & ((ri % (2 * bs)) >= bs) & ((ci % (2 * bs)) < bs))
        bs *= 2
    z = z_ref[0]
    nchunks = tm // c

    heads = range(GDN_HEADS)
    q = [l2n(act[:, hh * hd:(hh + 1) * hd]) * (hd ** -0.5) for hh in heads]
    k = [l2n(act[:, GDN_W + hh * hd:GDN_W + (hh + 1) * hd]) for hh in heads]
    v = [act[:, 2 * GDN_W + hh * hd:2 * GDN_W + (hh + 1) * hd] for hh in heads]
    bcol = [beta[:, GDN_HEADS + hh:GDN_HEADS + hh + 1] for hh in heads]
    gcol = [gc[:, hh:hh + 1] for hh in heads]
    decay = [jnp.where(incl, jnp.exp(jnp.where(incl, gcol[hh] - gct[hh:hh + 1, :], 0.0)), 0.0) for hh in heads]
    kb = [k[hh] * bcol[hh] for hh in heads]
    a_mat = [jnp.where(strict, _bdot_nt(kb[hh], k[hh]) * decay[hh], 0.0) for hh in heads]
    t_mat = [eye for _ in heads]
    for off in offs:
        pa = [_bdot(t_mat[hh], jnp.where(off, a_mat[hh], 0.0)) for hh in heads]
        t_mat = [t_mat[hh] - _bdot(pa[hh], t_mat[hh]) for hh in heads]
    th = [t_mat[hh].astype(BF16) for hh in heads]
    tl = [(t_mat[hh] - th[hh].astype(F32)).astype(BF16) for hh in heads]
    ah = [a_mat[hh].astype(BF16) for hh in heads]
    al = [(a_mat[hh] - ah[hh].astype(F32)).astype(BF16) for hh in heads]
    a_t = [jnp.dot(ah[hh], th[hh], preferred_element_type=F32) + jnp.dot(ah[hh], tl[hh], preferred_element_type=F32)
           + jnp.dot(al[hh], th[hh], preferred_element_type=F32) for hh in heads]
    t_mat = [t_mat[hh] + jnp.dot(th[hh], (eye - t_mat[hh] - a_t[hh]).astype(BF16), preferred_element_type=F32)
             for hh in heads]
    eg = [jnp.exp(gcol[hh]) for hh in heads]
    u = [_bdot(t_mat[hh], v[hh] * bcol[hh]) for hh in heads]
    w = [_bdot(t_mat[hh], kb[hh] * eg[hh]) for hh in heads]
    intra = [jnp.where(incl, _bdot_nt(q[hh], k[hh]) * decay[hh], 0.0).astype(BF16) for hh in heads]
    qd = [q[hh] * eg[hh] for hh in heads]
    state = [state_ref[hh] for hh in heads]
    for n in range(nchunks):
        r0 = n * c
        for hh in heads:
            lo = hh * hd
            gcn = gcol[hh][r0:r0 + c, :]
            glast = gcol[hh][r0 + c - 1:r0 + c, :]
            v_new = u[hh][r0:r0 + c, :] - _bdot(w[hh][r0:r0 + c, :], state[hh])
            v_rep = jnp.concatenate([v_new.astype(BF16)] * nchunks, axis=0)
            o = _bdot(qd[hh][r0:r0 + c, :], state[hh]) + jnp.dot(intra[hh][r0:r0 + c, :], v_rep,
                                                                  preferred_element_type=F32)
            state[hh] = state[hh] * jnp.exp(glast) + _bdot_tn(k[hh][r0:r0 + c, :] * jnp.exp(glast - gcn), v_new)
            on = o * lax.rsqrt(jnp.mean(o * o, axis=-1, keepdims=True) + RMS_EPS) * onorm_ref[...]
            o_ref[0, r0:r0 + c, lo:lo + hd] = (on * _silu(z[r0:r0 + c, lo:lo + hd])).astype(o_ref.dtype)
    for hh in heads:
        state_ref[hh] = state[hh]


def _gdn(qkv, z, gb, conv_w, a_log, dt_bias, o_norm):
    b, s, cw = qkv.shape
    tm = _tile(s, 256)
    nega = jnp.zeros((1, LANES), F32).at[0, :GDN_HEADS].set(-jnp.exp(a_log))
    dtb = jnp.zeros((1, LANES), F32).at[0, :GDN_HEADS].set(dt_bias)
    cwp = jnp.pad(conv_w, ((0, 8 - GDN_CONV), (0, 0)))
    ridx = np.arange(tm)
    same = (ridx[:, None] // GDN_CHUNK) == (ridx[None, :] // GDN_CHUNK)
    tril = jnp.asarray((same & (ridx[:, None] >= ridx[None, :])).astype(np.float32))
    triu = jnp.asarray((same & (ridx[:, None] <= ridx[None, :])).astype(np.float32))
    consts = [cwp, nega, dtb, o_norm[None, :], tril, triu]
    row = lambda n: pl.BlockSpec((1, tm, n), lambda bi, ti: (bi, ti, 0))
    return pl.pallas_call(
        _gdn_kernel,
        grid=(b, s // tm),
        in_specs=[row(cw), row(GDN_W), row(LANES)] + [_full(c.shape) for c in consts],
        out_specs=row(GDN_W),
        out_shape=jax.ShapeDtypeStruct((b, s, GDN_W), BF16),
        scratch_shapes=[pltpu.VMEM((tm + 8, cw), F32), pltpu.VMEM((GDN_HEADS, GDN_HEAD_DIM, GDN_HEAD_DIM), F32)],
        compiler_params=_params(("arbitrary", "arbitrary")),
        name="gdn",
    )(qkv, z, gb, *consts)


def _router_kernel(h_ref, g_ref, wr_ref, br_ref, tri_ref, xs_o, keyt_o, wt_o, cnt_o, sel_ref):
    tm = h_ref.shape[0]
    m = _rms(h_ref[...], g_ref[...])
    logits = _hdot(m, wr_ref[...]) + br_ref[...]
    lane = lax.broadcasted_iota(jnp.int32, (tm, LANES), 1)
    neg = -jnp.inf

    def first_argmax(x):
        mx = jnp.max(x, axis=-1, keepdims=True)
        idx = jnp.min(jnp.where(x == mx, lane, LANES), axis=-1, keepdims=True)
        return mx, idx

    is_g = (lane >= N_EXPERTS) & (lane < N_EXPERTS + MOE_GROUPS)
    gl = jnp.where(is_g, logits, neg)
    gmax, gidx = first_argmax(gl)
    g_w = 1.0 / jnp.sum(jnp.where(is_g, jnp.exp(gl - gmax), 0.0), axis=-1, keepdims=True)
    in_group = (lane // MOE_PER_GROUP) == (gidx - N_EXPERTS)
    el = jnp.where(in_group & (lane < N_EXPERTS), logits, neg)
    m1, i1 = first_argmax(el)
    el2 = jnp.where(lane == i1, neg, el)
    m2, i2 = first_argmax(el2)
    r = jnp.exp(m2 - m1)
    w1 = g_w / (1.0 + r)
    w2 = g_w * r / (1.0 + r)
    chose = (lane == i1) | (lane == i2)
    wmat = jnp.where(lane == i1, w1, jnp.where(lane == i2, w2, 0.0))
    ch = chose.astype(F32)
    rank = jnp.dot(tri_ref[...], ch.astype(BF16), preferred_element_type=F32)
    keyt = jnp.where(chose, rank, -1.0).T
    keyt_o[0] = keyt
    wt_o[0] = wmat.T
    cnt_o[0] = jnp.sum(ch, axis=0, keepdims=True).astype(jnp.int32)
    riota = lax.broadcasted_iota(jnp.int32, (MOE_CAP, tm), 0).astype(F32)
    for e in range(N_EXPERTS):
        sel_ref[e * MOE_CAP:(e + 1) * MOE_CAP, :] = jnp.where(keyt[e:e + 1, :] == riota, 1.0, 0.0).astype(BF16)
    xg = jnp.dot(sel_ref[...], m.astype(BF16), preferred_element_type=F32)
    xs_o[...] = xg.astype(BF16).reshape(xs_o.shape)


def _moe_router(hf, norm_g, w_group, b_group, w_expert, b_expert, tb):
    n, d = hf.shape
    nblk = n // tb
    wr = jnp.zeros((d, LANES), F32).at[:, :N_EXPERTS].set(w_expert)
    wr = wr.at[:, N_EXPERTS:N_EXPERTS + MOE_GROUPS].set(w_group)
    br = jnp.zeros((1, LANES), F32).at[0, :N_EXPERTS].set(b_expert)
    br = br.at[0, N_EXPERTS:N_EXPERTS + MOE_GROUPS].set(b_group)
    tri = jnp.asarray(np.tril(np.ones((tb, tb), np.float32), -1), BF16)
    blk = pl.BlockSpec((1, LANES, tb), lambda i: (i, 0, 0))
    return pl.pallas_call(
        _router_kernel,
        grid=(nblk,),
        in_specs=[pl.BlockSpec((tb, d), lambda i: (i, 0)), _full((1, d)), _full(wr.shape), _full(br.shape),
                  _full(tri.shape)],
        out_specs=[pl.BlockSpec((N_EXPERTS, MOE_CAP, d), lambda i: (0, i, 0)), blk, blk,
                   pl.BlockSpec((1, 1, LANES), lambda i: (i, 0, 0))],
        out_shape=[jax.ShapeDtypeStruct((N_EXPERTS, nblk * MOE_CAP, d), BF16),
                   jax.ShapeDtypeStruct((nblk, LANES, tb), F32),
                   jax.ShapeDtypeStruct((nblk, LANES, tb), F32),
                   jax.ShapeDtypeStruct((nblk, 1, LANES), jnp.int32)],
        scratch_shapes=[pltpu.VMEM((N_EXPERTS * MOE_CAP, tb), BF16)],
        compiler_params=_params(("arbitrary",)),
        name="moe_router",
    )(hf, norm_g[None, :], wr, br, tri)


def _expert_mlp_kernel(x_ref, wg_ref, wu_ref, wd_ref, y_ref, wg_sc, wu_sc, wd_sc):
    @pl.when(pl.program_id(1) == 0)
    def _():
        wg_sc[...] = wg_ref[0, 0].astype(BF16)
        wu_sc[...] = wu_ref[0, 0].astype(BF16)
        wd_sc[...] = wd_ref[0, 0].astype(BF16)

    x = x_ref[0]
    hid = _silu(jnp.dot(x, wg_sc[...], preferred_element_type=F32)) * jnp.dot(
        x, wu_sc[...], preferred_element_type=F32)
    y_ref[0] = jnp.dot(hid.astype(BF16), wd_sc[...], preferred_element_type=F32).astype(BF16)


def _expert_mlp(xs, w_gate, w_up, w_down, layer):
    ne, rows, d = xs.shape
    ff = w_gate.shape[3]
    tr = _tile(rows, 1024)
    return pl.pallas_call(
        _expert_mlp_kernel,
        grid=(ne, rows // tr),
        in_specs=[pl.BlockSpec((1, tr, d), lambda e, i: (e, i, 0)),
                  pl.BlockSpec((1, 1, d, ff), lambda e, i: (layer, e, 0, 0)),
                  pl.BlockSpec((1, 1, d, ff), lambda e, i: (layer, e, 0, 0)),
                  pl.BlockSpec((1, 1, ff, d), lambda e, i: (layer, e, 0, 0))],
        out_specs=pl.BlockSpec((1, tr, d), lambda e, i: (e, i, 0)),
        out_shape=jax.ShapeDtypeStruct((ne, rows, d), BF16),
        scratch_shapes=[pltpu.VMEM((d, ff), BF16), pltpu.VMEM((d, ff), BF16), pltpu.VMEM((ff, d), BF16)],
        compiler_params=_params(("arbitrary", "arbitrary")),
        name="moe_expert_mlp",
    )(xs, w_gate, w_up, w_down)


def _combine_kernel(y_ref, keyt_ref, wt_ref, h_ref, o_ref, sel_ref):
    tb = h_ref.shape[0]
    riota = lax.broadcasted_iota(jnp.int32, (MOE_CAP, tb), 0).astype(F32)
    for e in range(N_EXPERTS):
        hit = keyt_ref[0, e:e + 1, :] == riota
        sel_ref[e * MOE_CAP:(e + 1) * MOE_CAP, :] = jnp.where(hit, wt_ref[0, e:e + 1, :], 0.0).astype(BF16)
    y = y_ref[...].reshape(N_EXPERTS * MOE_CAP, y_ref.shape[2])
    o_ref[...] = h_ref[...] + lax.dot_general(sel_ref[...], y, (((0,), (0,)), ((), ())),
                                              preferred_element_type=F32)


def _moe_combine(ys, keyt, wt, hf, tb):
    n, d = hf.shape
    nblk = n // tb
    blk = pl.BlockSpec((1, LANES, tb), lambda i: (i, 0, 0))
    return pl.pallas_call(
        _combine_kernel,
        grid=(nblk,),
        in_specs=[pl.BlockSpec((N_EXPERTS, MOE_CAP, d), lambda i: (0, i, 0)), blk, blk,
                  pl.BlockSpec((tb, d), lambda i: (i, 0))],
        out_specs=pl.BlockSpec((tb, d), lambda i: (i, 0)),
        out_shape=jax.ShapeDtypeStruct((n, d), F32),
        scratch_shapes=[pltpu.VMEM((N_EXPERTS * MOE_CAP, tb), BF16)],
        compiler_params=_params(("arbitrary",)),
        name="moe_combine",
    )(ys, keyt, wt, hf)


def _overflow_kernel(cnt_ref, h_ref, g_ref, keyt_ref, wt_ref, base_ref, wg_ref, wu_ref, wd_ref, o_ref,
                     acc_ref, m_ref):
    blk = pl.program_id(0)
    e = pl.program_id(1)
    tb = h_ref.shape[0]

    @pl.when(e == 0)
    def _():
        acc_ref[...] = base_ref[...]
        m_ref[...] = _rms(h_ref[...], g_ref[...]).astype(BF16)

    extra = jnp.maximum(cnt_ref[blk * LANES + e] - MOE_CAP, 0)
    krow = keyt_ref[0, pl.ds(e, 1), :]
    wrow = wt_ref[0, pl.ds(e, 1), :]
    riota = lax.broadcasted_iota(jnp.int32, (MOE_ROWS, tb), 0).astype(F32)

    def chunk(ci, carry):
        hit = krow == (riota + (MOE_CAP + ci * MOE_ROWS).astype(F32))
        sel = jnp.where(hit, 1.0, 0.0).astype(BF16)
        xg = jnp.dot(sel, m_ref[...], preferred_element_type=F32).astype(BF16)
        hid = _silu(_bdot(xg, wg_ref[0, 0])) * _bdot(xg, wu_ref[0, 0])
        y = _bdot(hid, wd_ref[0, 0]).astype(BF16)
        acc_ref[...] += _bdot_tn(jnp.where(hit, wrow, 0.0), y)
        return carry

    lax.fori_loop(0, (extra + MOE_ROWS - 1) // MOE_ROWS, chunk, 0)

    @pl.when(e == N_EXPERTS - 1)
    def _():
        o_ref[...] = acc_ref[...]


def _moe_overflow(cnt, hf, norm_g, keyt, wt, base, w_gate, w_up, w_down, layer, tb):
    n, d = hf.shape
    ff = w_gate.shape[3]
    nblk = n // tb
    tok = pl.BlockSpec((tb, d), lambda i, e, c: (i, 0))
    blk = pl.BlockSpec((1, LANES, tb), lambda i, e, c: (i, 0, 0))
    grid_spec = pltpu.PrefetchScalarGridSpec(
        num_scalar_prefetch=1,
        grid=(nblk, N_EXPERTS),
        in_specs=[tok, pl.BlockSpec((1, d), lambda i, e, c: (0, 0)), blk, blk, tok,
                  pl.BlockSpec((1, 1, d, ff), lambda i, e, c: (layer, e, 0, 0)),
                  pl.BlockSpec((1, 1, d, ff), lambda i, e, c: (layer, e, 0, 0)),
                  pl.BlockSpec((1, 1, ff, d), lambda i, e, c: (layer, e, 0, 0))],
        out_specs=tok,
        scratch_shapes=[pltpu.VMEM((tb, d), F32), pltpu.VMEM((tb, d), BF16)],
    )
    return pl.pallas_call(
        _overflow_kernel,
        grid_spec=grid_spec,
        out_shape=jax.ShapeDtypeStruct((n, d), F32),
        compiler_params=_params(("arbitrary", "arbitrary")),
        name="moe_overflow",
    )(cnt, hf, norm_g[None, :], keyt, wt, base, w_gate, w_up, w_down)


def _hier_moe_residual(h, norm_g, w_group, b_group, w_expert, b_expert, w_gate, w_up, w_down, layer):
    b, s, d = h.shape
    hf = h.reshape(b * s, d)
    tb = _tile(b * s, MOE_TB)
    xs, keyt, wt, cnt = _moe_router(hf, norm_g, w_group, b_group, w_expert, b_expert, tb)
    ys = _expert_mlp(xs, w_gate, w_up, w_down, layer)
    out = _moe_combine(ys, keyt, wt, hf, tb)
    out = lax.cond(jnp.max(cnt[:, 0, :N_EXPERTS]) > MOE_CAP,
                   lambda o: _moe_overflow(cnt.reshape(-1), hf, norm_g, keyt, wt, o, w_gate, w_up, w_down, layer, tb),
                   lambda o: o, out)
    return out.reshape(b, s, d)


def _ple_kernel(h_ref, p_ref, g_ref, wg_ref, bg_ref, wp_ref, o_ref):
    h = h_ref[0]
    gate = jax.nn.sigmoid(_bdot(_rms(h, g_ref[...]), wg_ref[...]) + bg_ref[...])
    o_ref[0] = h + gate * _bdot(p_ref[0, 0], wp_ref[...])


def _ple_residual(h, p_all, layer, w_proj, norm_g, w_gate, b_gate):
    b, s, d = h.shape
    pd = p_all.shape[3]
    tm = _tile(s, 512)
    row = lambda n: pl.BlockSpec((1, tm, n), lambda bi, ti: (bi, ti, 0))
    return pl.pallas_call(
        _ple_kernel,
        grid=(b, s // tm),
        in_specs=[row(d), pl.BlockSpec((1, 1, tm, pd), lambda bi, ti: (layer, bi, ti, 0)), _full((1, d)),
                  _full((d, d)), _full((1, d)), _full((pd, d))],
        out_specs=row(d),
        out_shape=jax.ShapeDtypeStruct((b, s, d), F32),
        compiler_params=_params(("arbitrary", "arbitrary")),
        name="ple",
    )(h, p_all, norm_g[None, :], w_gate.astype(BF16), b_gate[None, :], w_proj.astype(BF16))


def _even_layer(h, positions, norm_mix, w_in, b_f, fox_qn, fox_kn, q_a_norm, w_q_up, kv_a_norm, w_kv_up,
                mla_qn, mla_kn, w_out):
    fq, fk, fv, cum, mq, mk, mv = _even_pre(h, positions, norm_mix, w_in, b_f, fox_qn, fox_kn, q_a_norm,
                                            w_q_up, kv_a_norm, w_kv_up, mla_qn, mla_kn)
    o_fox = _attention(fq, fk, fv, cum)
    o_mla = _attention(mq, mk, mv)
    return _proj2_residual(o_fox, o_mla, w_out, h)


def _odd_layer(h, norm_mix, w_in, a_re, a_im, b_re, b_im, c_re, c_im, d_skip, log_step, w_glu, b_glu,
               conv_w, a_log, dt_bias, o_norm, w_out):
    u, qkv, z, gb = _odd_pre(h, norm_mix, w_in)
    y_ssm = _s5(u, a_re, a_im, b_re, b_im, c_re, c_im, d_skip, log_step, w_glu, b_glu)
    o_gdn = _gdn(qkv, z, gb, conv_w, a_log, dt_bias, jnp.tile(o_norm, 1))
    return _proj2_residual(y_ssm, o_gdn, w_out, h)


def kernel(x, p, positions, norm_mix, norm_ffn, ev_w_in, fox_b_f, fox_q_norm, fox_k_norm, mla_q_a_norm, mla_w_q_up, mla_kv_a_norm, mla_w_kv_up, mla_q_norm, mla_k_norm, ev_w_out, od_w_in, s5_a_re, s5_a_im, s5_b_re, s5_b_im, s5_c_re, s5_c_im, s5_d, s5_log_step, s5_w_glu, s5_b_glu, gdn_conv_w, gdn_a_log, gdn_dt_bias, gdn_o_norm, od_w_out, moe_w_group, moe_b_group, moe_w_expert, moe_b_expert, moe_w_gate, moe_w_up, moe_w_down, ple_w_proj, ple_norm, ple_w_gate, ple_b_gate):
    h = x
    depth = p.shape[0]
    for i in range(depth):
        j = i // 2
        if i % 2 == 0:
            h = _even_layer(h, positions, norm_mix[i], ev_w_in[j], fox_b_f[j], fox_q_norm[j], fox_k_norm[j],
                            mla_q_a_norm[j], mla_w_q_up[j], mla_kv_a_norm[j], mla_w_kv_up[j], mla_q_norm[j],
                            mla_k_norm[j], ev_w_out[j])
        else:
            h = _odd_layer(h, norm_mix[i], od_w_in[j], s5_a_re[j], s5_a_im[j], s5_b_re[j], s5_b_im[j],
                           s5_c_re[j], s5_c_im[j], s5_d[j], s5_log_step[j], s5_w_glu[j], s5_b_glu[j],
                           gdn_conv_w[j], gdn_a_log[j], gdn_dt_bias[j], gdn_o_norm[j], od_w_out[j])
        h = _hier_moe_residual(h, norm_ffn[i], moe_w_group[i], moe_b_group[i], moe_w_expert[i],
                               moe_b_expert[i], moe_w_gate, moe_w_up, moe_w_down, i)
        h = _ple_residual(h, p, i, ple_w_proj[i], ple_norm[i], ple_w_gate[i], ple_b_gate[i])
    return h
```

```python
import functools
import math

import numpy as np
import jax
import jax.numpy as jnp
from jax import lax
from jax.experimental import pallas as pl
from jax.experimental.pallas import tpu as pltpu

F32 = jnp.float32
BF16 = jnp.bfloat16
HI = lax.Precision.HIGHEST

LANES = 128
RMS_EPS = 1e-6
ROPE_THETA = 10000.0
LOG2E = math.log2(math.e)

FOX_HEADS = 8
FOX_HEAD_DIM = 64
MLA_HEADS = 8
MLA_Q_LORA = 384
MLA_KV_LORA = 256
MLA_NOPE = 64
MLA_ROPE = 32
MLA_V = 64
MLA_QK = MLA_NOPE + MLA_ROPE

S5_CH = 512
S5_GROUP_CH = 16
S5_GROUPS = S5_CH // S5_GROUP_CH
S5_STATE = 64
S5_N = S5_GROUPS * S5_STATE

GDN_HEADS = 4
GDN_HEAD_DIM = 128
GDN_W = GDN_HEADS * GDN_HEAD_DIM
GDN_CONV = 4
GDN_CHUNK = 64

MOE_GROUPS = 4
MOE_PER_GROUP = 8
N_EXPERTS = MOE_GROUPS * MOE_PER_GROUP
MOE_TB = 512
MOE_CAP = 64
MOE_ROWS = 128
ATTN_TQ = 2048
ATTN_TK = 512

VMEM_LIMIT = 56 * 1024 * 1024


def _tile(n, pref):
    t = min(n, pref)
    assert n % t == 0, (n, t)
    return t


def _params(sem):
    return pltpu.CompilerParams(dimension_semantics=sem, vmem_limit_bytes=VMEM_LIMIT)


def _full(shape):
    nd = len(shape)
    return pl.BlockSpec(shape, lambda *_: (0,) * nd)


def _rms(x, g):
    return x * lax.rsqrt(jnp.mean(x * x, axis=-1, keepdims=True) + RMS_EPS) * g


def _bdot(a, b):
    return jnp.dot(a.astype(BF16), b.astype(BF16), preferred_element_type=F32)


def _bdot_nt(a, b):
    return lax.dot_general(a.astype(BF16), b.astype(BF16), (((1,), (1,)), ((), ())),
                           preferred_element_type=F32)


def _bdot_tn(a, b):
    return lax.dot_general(a.astype(BF16), b.astype(BF16), (((0,), (0,)), ((), ())),
                           preferred_element_type=F32)


def _hdot(a, b):
    return jnp.dot(a, b, precision=HI, preferred_element_type=F32)


def _split_dot(x, ind):
    hi = x.astype(BF16)
    lo = (x - hi.astype(F32)).astype(BF16)
    return (jnp.dot(hi, ind, preferred_element_type=F32)
            + jnp.dot(lo, ind, preferred_element_type=F32))


def _log_sigmoid(x):
    return jnp.minimum(x, 0.0) - jnp.log(1.0 + jnp.exp(-jnp.abs(x)))


def _softplus(x):
    return jnp.maximum(x, 0.0) + jnp.log(1.0 + jnp.exp(-jnp.abs(x)))


def _silu(x):
    return x * jax.nn.sigmoid(x)


def _head_norm128(x, nheads, denom, gain):
    outs = []
    for hh in range(nheads):
        xh = x[:, LANES * hh:LANES * (hh + 1)]
        ss = jnp.sum(xh * xh, axis=-1, keepdims=True)
        outs.append(xh * lax.rsqrt(ss / denom + RMS_EPS))
    return jnp.concatenate(outs, axis=1) * gain


def _even_pre_kernel(h_ref, pos_ref, nmix_ref, win_ref, ind_ref, fqn_ref, fkn_ref, bf_ref,
                     qan_ref, wq_ref, kvan_ref, wkv_ref, mqn_ref, mkn_ref, freq_ref, s1_ref, s2_ref,
                     tri_ref, vone_ref, fq_o, fk_o, fv_o, cum_o, mq_o, mk_o, mv_o, carry_ref):
    t = pl.program_id(1)

    @pl.when(t == 0)
    def _():
        carry_ref[...] = jnp.zeros_like(carry_ref)

    tm = h_ref.shape[1]
    a = _rms(h_ref[0], nmix_ref[...])
    proj = _bdot(a, win_ref[...])
    nf = FOX_HEADS * FOX_HEAD_DIM
    fq = proj[:, 0:nf]
    fk = proj[:, nf:2 * nf]
    nv = FOX_HEADS * LANES
    fv = proj[:, 2 * nf:2 * nf + nv]
    o_cq = 2 * nf + nv
    cq = proj[:, o_cq:o_cq + MLA_Q_LORA]
    o_ckv = o_cq + MLA_Q_LORA
    ckv = proj[:, o_ckv:o_ckv + MLA_KV_LORA]
    misc = proj[:, o_ckv + MLA_KV_LORA:]

    ind = ind_ref[...]
    fq_n = fq * lax.rsqrt(_split_dot(fq * fq, ind) / FOX_HEAD_DIM + RMS_EPS) * fqn_ref[...]
    fk_n = fk * lax.rsqrt(_split_dot(fk * fk, ind) / FOX_HEAD_DIM + RMS_EPS) * fkn_ref[...]
    fq_o[0] = (fq_n * (FOX_HEAD_DIM ** -0.5 * LOG2E)).astype(BF16)
    fk_o[0] = fk_n.astype(BF16)
    fv_o[0] = (fv + vone_ref[...]).astype(BF16)

    lane = lax.broadcasted_iota(jnp.int32, (tm, LANES), 1)
    logf = jnp.where(lane < FOX_HEADS, _log_sigmoid(misc + bf_ref[...]), 0.0)
    cum = _hdot(tri_ref[...], logf) + carry_ref[...]
    carry_ref[...] = cum[tm - 1:tm, :]
    cum_o[0] = (cum * LOG2E).T[:FOX_HEADS, :]

    ang = pos_ref[0].astype(F32) * freq_ref[...]
    cos1 = jnp.cos(ang)
    sin1 = jnp.sin(ang)
    cos = jnp.concatenate([cos1] * MLA_HEADS, axis=1)
    sin_a = jnp.concatenate([sin1 * s1_ref[...]] * MLA_HEADS, axis=1)
    sin_b = jnp.concatenate([sin1 * s2_ref[...]] * MLA_HEADS, axis=1)
    width = MLA_HEADS * LANES
    half = MLA_ROPE // 2

    def rope(x):
        return (x * cos + pltpu.roll(x, width - half, 1) * sin_a + pltpu.roll(x, half, 1) * sin_b)

    q = _bdot(_rms(cq, qan_ref[...]), wq_ref[...])
    q = rope(_head_norm128(q, MLA_HEADS, MLA_QK, mqn_ref[...]))
    mq_o[0] = (q * (MLA_QK ** -0.5 * LOG2E)).astype(BF16)

    kv = _bdot(_rms(ckv, kvan_ref[...]), wkv_ref[...])
    kr = pltpu.roll(misc, MLA_NOPE - FOX_HEADS, 1)
    kr = jnp.where((lane >= MLA_NOPE) & (lane < MLA_QK), kr, 0.0)
    k = kv[:, :width] + jnp.concatenate([kr] * MLA_HEADS, axis=1)
    k = rope(_head_norm128(k, MLA_HEADS, MLA_QK, mkn_ref[...]))
    mk_o[0] = k.astype(BF16)
    mv_o[0] = (kv[:, width:] + vone_ref[...]).astype(BF16)


def _even_pre(h, positions, norm_mix, w_in, b_f, fox_qn, fox_kn, q_a_norm, w_q_up, kv_a_norm, w_kv_up,
              mla_qn, mla_kn):
    b, s, d = h.shape
    tm = _tile(s, 256)
    nf = FOX_HEADS * FOX_HEAD_DIM
    sizes = (nf, nf, nf, FOX_HEADS, MLA_Q_LORA, MLA_KV_LORA, MLA_ROPE)
    offs = np.concatenate([[0], np.cumsum(sizes)])
    parts = [w_in[:, offs[i]:offs[i + 1]] for i in range(len(sizes))]
    pad = jnp.zeros((d, LANES - FOX_HEADS - MLA_ROPE), w_in.dtype)
    slot_pad = ((0, 0), (0, 0), (0, LANES - FOX_HEAD_DIM))
    wfv = jnp.pad(parts[2].reshape(d, FOX_HEADS, FOX_HEAD_DIM), slot_pad).reshape(d, FOX_HEADS * LANES)
    win = jnp.concatenate([parts[0], parts[1], wfv, parts[4], parts[5], parts[3], parts[6], pad],
                          axis=1).astype(BF16)
    gidx = np.arange(nf) // FOX_HEAD_DIM
    ind = jnp.asarray(gidx[:, None] == gidx[None, :], BF16)
    fqn = jnp.tile(fox_qn, FOX_HEADS)[None, :]
    fkn = jnp.tile(fox_kn, FOX_HEADS)[None, :]
    bf = jnp.zeros((1, LANES), F32).at[0, :FOX_HEADS].set(b_f)
    padq = LANES - MLA_QK
    wq = jnp.pad(w_q_up.reshape(MLA_Q_LORA, MLA_HEADS, MLA_QK), ((0, 0), (0, 0), (0, padq)))
    wq = wq.reshape(MLA_Q_LORA, MLA_HEADS * LANES).astype(BF16)
    wkv3 = w_kv_up.reshape(MLA_KV_LORA, MLA_HEADS, MLA_NOPE + MLA_V)
    wk = jnp.pad(wkv3[:, :, :MLA_NOPE], ((0, 0), (0, 0), (0, LANES - MLA_NOPE)))
    wv = jnp.pad(wkv3[:, :, MLA_NOPE:], ((0, 0), (0, 0), (0, LANES - MLA_V)))
    wkv = jnp.concatenate([wk.reshape(MLA_KV_LORA, MLA_HEADS * LANES),
                           wv.reshape(MLA_KV_LORA, MLA_HEADS * LANES)], axis=1).astype(BF16)
    vone = jnp.tile(jnp.zeros((LANES,), F32).at[MLA_V].set(1.0), MLA_HEADS)[None, :]
    mqn = jnp.tile(jnp.pad(mla_qn, (0, padq)), MLA_HEADS)[None, :]
    mkn = jnp.tile(jnp.pad(mla_kn, (0, padq)), MLA_HEADS)[None, :]
    half = MLA_ROPE // 2
    inv = ROPE_THETA ** (-jnp.arange(half, dtype=F32) * 2.0 / MLA_ROPE)
    freq = jnp.zeros((1, LANES), F32).at[0, MLA_NOPE:MLA_NOPE + half].set(inv)
    freq = freq.at[0, MLA_NOPE + half:MLA_QK].set(inv)
    s1 = jnp.zeros((1, LANES), F32).at[0, MLA_NOPE:MLA_NOPE + half].set(-1.0)
    s2 = jnp.zeros((1, LANES), F32).at[0, MLA_NOPE + half:MLA_QK].set(1.0)
    tri = jnp.asarray(np.tril(np.ones((tm, tm), np.float32)))
    pos3 = positions.reshape(b, s, 1)

    row = lambda n: pl.BlockSpec((1, tm, n), lambda bi, ti: (bi, ti, 0))
    consts = [norm_mix[None, :], win, ind, fqn, fkn, bf, q_a_norm[None, :], wq, kv_a_norm[None, :], wkv,
              mqn, mkn, freq, s1, s2, tri, vone]
    nv = FOX_HEADS * LANES
    out_shape = [jax.ShapeDtypeStruct((b, s, nf), BF16)] * 2 + [jax.ShapeDtypeStruct((b, s, nv), BF16)] + [
        jax.ShapeDtypeStruct((b, FOX_HEADS, s), F32),
        jax.ShapeDtypeStruct((b, s, MLA_HEADS * LANES), BF16),
        jax.ShapeDtypeStruct((b, s, MLA_HEADS * LANES), BF16),
        jax.ShapeDtypeStruct((b, s, MLA_HEADS * LANES), BF16)]
    return pl.pallas_call(
        _even_pre_kernel,
        grid=(b, s // tm),
        in_specs=[row(d), row(1)] + [_full(c.shape) for c in consts],
        out_specs=[row(nf), row(nf), row(nv), pl.BlockSpec((1, FOX_HEADS, tm), lambda bi, ti: (bi, 0, ti)),
                   row(MLA_HEADS * LANES), row(MLA_HEADS * LANES), row(MLA_HEADS * LANES)],
        out_shape=out_shape,
        scratch_shapes=[pltpu.VMEM((1, LANES), F32)],
        compiler_params=_params(("arbitrary", "arbitrary")),
        name="even_pre",
    )(h, pos3, *consts)


def _attn_kernel(*refs, tq, tk, fox):
    if fox:
        q_ref, k_ref, v_ref, cr_ref, o_ref = refs
    else:
        q_ref, k_ref, v_ref, o_ref = refs
    hp = pl.program_id(1)
    i = pl.program_id(2)
    lane = lax.broadcasted_iota(jnp.int32, (tq, LANES), 1)
    rowi = lax.broadcasted_iota(jnp.int32, (tq, tk), 0)
    coli = lax.broadcasted_iota(jnp.int32, (tq, tk), 1)
    qs = []
    for hh in range(2):
        if fox:
            in_head = (lane >= FOX_HEAD_DIM * hh) & (lane < FOX_HEAD_DIM * (hh + 1))
            qs.append(jnp.where(in_head, q_ref[0], jnp.zeros((), BF16)))
        else:
            qs.append(q_ref[0, :, LANES * hh:LANES * (hh + 1)])

    def scores(hh, koff, diag):
        if fox:
            kj = k_ref[0, pl.ds(koff, tk), :]
        else:
            kj = k_ref[0, pl.ds(koff, tk), LANES * hh:LANES * (hh + 1)]
        sc = lax.dot_general(qs[hh], kj, (((1,), (1,)), ((), ())), preferred_element_type=F32)
        if fox:
            sc = sc - cr_ref[0, pl.ds(2 * hp + hh, 1), pl.ds(koff, tk)]
        if diag is not None:
            sc = jnp.where(coli + diag <= rowi, sc, -jnp.inf)
        return sc

    def step(j, carry, diag=None):
        koff = pl.multiple_of(j * tk, tk)
        new = []
        for hh in range(2):
            m, acc = carry[hh]
            sc = scores(hh, koff, diag)
            m_new = jnp.maximum(m, jnp.max(sc, axis=-1, keepdims=True))
            alpha = jnp.exp2(m - m_new)
            p = jnp.exp2((sc - m_new).astype(BF16))
            vj = v_ref[0, pl.ds(koff, tk), LANES * hh:LANES * (hh + 1)]
            new.append((m_new, alpha * acc + jnp.dot(p, vj, preferred_element_type=F32)))
        return tuple(new)

    def body(jj, carry):
        for r in range(ratio):
            carry = step(jj * ratio + r, carry)
        return carry

    one = (jnp.full((tq, 1), -jnp.inf, F32), jnp.zeros((tq, LANES), F32))
    ratio = tq // tk
    carry = lax.fori_loop(0, i, body, (one, one))
    for r in range(ratio):
        carry = step(i * ratio + r, carry, diag=r * tk)
    outs = [acc / acc[:, MLA_V:MLA_V + 1] for _, acc in carry]
    o_ref[0] = jnp.where(lane < MLA_V, outs[0], pltpu.roll(outs[1], MLA_V, 1)).astype(o_ref.dtype)


def _attention(q, k, v, cum_row=None):
    b, s, _ = v.shape
    fox = cum_row is not None
    qw = LANES if fox else 2 * LANES
    tq = _tile(s, ATTN_TQ)
    tk = _tile(tq, ATTN_TK)
    npairs = v.shape[2] // (2 * LANES)
    in_specs = [pl.BlockSpec((1, tq, qw), lambda bi, hp, i: (bi, i, hp)),
                pl.BlockSpec((1, s, qw), lambda bi, hp, i: (bi, 0, hp)),
                pl.BlockSpec((1, s, 2 * LANES), lambda bi, hp, i: (bi, 0, hp))]
    args = [q, k, v]
    if fox:
        in_specs += [pl.BlockSpec((1, FOX_HEADS, s), lambda bi, hp, i: (bi, 0, 0))]
        args += [cum_row]
    return pl.pallas_call(
        functools.partial(_attn_kernel, tq=tq, tk=tk, fox=fox),
        grid=(b, npairs, s // tq),
        in_specs=in_specs,
        out_specs=pl.BlockSpec((1, tq, LANES), lambda bi, hp, i: (bi, i, hp)),
        out_shape=jax.ShapeDtypeStruct((b, s, npairs * LANES), BF16),
        compiler_params=_params(("arbitrary", "arbitrary", "arbitrary")),
        name="fox_attention" if fox else "mla_attention",
    )(*args)


def _proj2_kernel(a_ref, b_ref, wa_ref, wb_ref, h_ref, o_ref):
    o_ref[0] = (h_ref[0] + jnp.dot(a_ref[0], wa_ref[...], preferred_element_type=F32)
                + jnp.dot(b_ref[0], wb_ref[...], preferred_element_type=F32))


def _proj2_residual(a, bb, w_out, h):
    b, s, d = h.shape
    na = a.shape[2]
    nb = bb.shape[2]
    tm = _tile(s, 512)
    wa = w_out[:na].astype(BF16)
    wb = w_out[na:].astype(BF16)
    row = lambda n: pl.BlockSpec((1, tm, n), lambda bi, ti: (bi, ti, 0))
    return pl.pallas_call(
        _proj2_kernel,
        grid=(b, s // tm),
        in_specs=[row(na), row(nb), _full(wa.shape), _full(wb.shape), row(d)],
        out_specs=row(d),
        out_shape=jax.ShapeDtypeStruct((b, s, d), F32),
        compiler_params=_params(("arbitrary", "arbitrary")),
        name="out_proj",
    )(a, bb, wa, wb, h)


def _odd_pre_kernel(h_ref, nmix_ref, win_ref, u_o, qkv_o, z_o, gb_o):
    a = _rms(h_ref[0], nmix_ref[...])
    proj = _bdot(a, win_ref[...])
    o1 = S5_CH
    o2 = o1 + 3 * GDN_W
    o3 = o2 + GDN_W
    u_o[0] = proj[:, :o1]
    qkv_o[0] = proj[:, o1:o2]
    z_o[0] = proj[:, o2:o3]
    gb_o[0] = proj[:, o3:]


def _odd_pre(h, norm_mix, w_in):
    b, s, d = h.shape
    tm = _tile(s, 256)
    sizes = (S5_CH, 3 * GDN_W, GDN_HEADS, GDN_HEADS, GDN_W)
    offs = np.concatenate([[0], np.cumsum(sizes)])
    parts = [w_in[:, offs[i]:offs[i + 1]] for i in range(len(sizes))]
    pad = jnp.zeros((d, LANES - 2 * GDN_HEADS), w_in.dtype)
    win = jnp.concatenate([parts[0], parts[1], parts[4], parts[2], parts[3], pad], axis=1).astype(BF16)
    row = lambda n: pl.BlockSpec((1, tm, n), lambda bi, ti: (bi, ti, 0))
    widths = (S5_CH, 3 * GDN_W, GDN_W, LANES)
    return pl.pallas_call(
        _odd_pre_kernel,
        grid=(b, s // tm),
        in_specs=[row(d), _full((1, d)), _full(win.shape)],
        out_specs=[row(n) for n in widths],
        out_shape=[jax.ShapeDtypeStruct((b, s, n), F32) for n in widths],
        compiler_params=_params(("arbitrary", "arbitrary")),
        name="odd_pre",
    )(h, norm_mix[None, :], win)


def _s5_kernel(u_ref, perm_ref, unperm_ref, bbd_ref, cbd_ref, ar_ref, ai_ref, asr_ref, asi_ref, pwr_ref, pwi_ref,
               d_ref, wglu_ref, bglu_ref, o_ref, x_ref, sr_ref, si_ref):
    t = pl.program_id(1)

    @pl.when(t == 0)
    def _():
        sr_ref[...] = jnp.zeros_like(sr_ref)
        si_ref[...] = jnp.zeros_like(si_ref)

    tm = u_ref.shape[1]
    nseg = 8
    seg = tm // nseg
    u = u_ref[0]
    u1 = u.astype(BF16)
    u2 = (u - u1.astype(F32)).astype(BF16)
    u3 = (u - u1.astype(F32) - u2.astype(F32)).astype(BF16)
    perm = perm_ref[...]
    u = (jnp.dot(perm, u1, preferred_element_type=F32) + jnp.dot(perm, u2, preferred_element_type=F32)
         + jnp.dot(perm, u3, preferred_element_type=F32))
    hc = S5_CH // 2
    hn = S5_N // 2
    ub = u.astype(BF16)
    for part in range(2):
        for base in (0, S5_N):
            cols = slice(base + part * hn, base + (part + 1) * hn)
            x_ref[:, cols] = jnp.dot(ub[:, part * hc:(part + 1) * hc], bbd_ref[part * hc:(part + 1) * hc, cols],
                                     preferred_element_type=F32)
    ar = ar_ref[...]
    ai = ai_ref[...]
    re = slice(0, S5_N)
    im = slice(S5_N, 2 * S5_N)

    def local(i, carry):
        xr, xi = carry
        rows = pl.ds(pl.multiple_of(i * nseg, nseg), nseg)
        nr = ar * xr - ai * xi + x_ref[rows, re]
        ni = ar * xi + ai * xr + x_ref[rows, im]
        x_ref[rows, re] = nr
        x_ref[rows, im] = ni
        return nr, ni

    zero = jnp.zeros((nseg, S5_N), F32)
    er, ei = lax.fori_loop(0, seg, local, (zero, zero), unroll=4)

    asr = asr_ref[...]
    asi = asi_ref[...]
    cr = [sr_ref[...]]
    ci = [si_ref[...]]
    for s in range(nseg):
        cr.append(asr * cr[s] - asi * ci[s] + er[s:s + 1, :])
        ci.append(asr * ci[s] + asi * cr[s] + ei[s:s + 1, :])
    sr_ref[...] = cr[nseg]
    si_ref[...] = ci[nseg]
    ent_r = jnp.concatenate(cr[:nseg], axis=0)
    ent_i = jnp.concatenate(ci[:nseg], axis=0)

    def fix(i, c):
        rows = pl.ds(pl.multiple_of(i * nseg, nseg), nseg)
        pr = pwr_ref[pl.ds(i, 1), :]
        pi = pwi_ref[pl.ds(i, 1), :]
        x_ref[rows, re] += pr * ent_r - pi * ent_i
        x_ref[rows, im] += pr * ent_i + pi * ent_r
        return c

    lax.fori_loop(0, seg, fix, 0, unroll=4)
    ys = []
    for part in range(2):
        oc = slice(part * hc, (part + 1) * hc)
        acc = None
        for base in (0, S5_N):
            rows = slice(base + part * hn, base + (part + 1) * hn)
            term = _bdot(x_ref[:, rows], cbd_ref[rows, oc])
            acc = term if acc is None else acc + term
        ys.append(acc)
    y = jnp.concatenate(ys, axis=1) + d_ref[...] * u
    hg = jax.nn.gelu(y)
    out = (hg * jax.nn.sigmoid(_bdot(hg, wglu_ref[...]) + bglu_ref[...])).astype(BF16)
    o_ref[0] = jnp.dot(unperm_ref[...], out, preferred_element_type=F32).astype(o_ref.dtype)


def _s5(u, a_re, a_im, b_re, b_im, c_re, c_im, d_skip, log_step, w_glu, b_glu):
    b, s, _ = u.shape
    tm = _tile(s, 256)
    lam_re = jnp.minimum(a_re, -1e-4)
    lam_im = a_im
    dt = jnp.exp(log_step)[:, None]
    mag = jnp.exp(lam_re * dt)
    ab_re = mag * jnp.cos(lam_im * dt)
    ab_im = mag * jnp.sin(lam_im * dt)
    den = lam_re * lam_re + lam_im * lam_im
    nr, ni = ab_re - 1.0, ab_im
    gam_re = (nr * lam_re + ni * lam_im) / den
    gam_im = (ni * lam_re - nr * lam_im) / den
    bb_re = gam_re[..., None] * b_re - gam_im[..., None] * b_im
    bb_im = gam_re[..., None] * b_im + gam_im[..., None] * b_re
    eye = jnp.eye(S5_GROUPS, dtype=F32)
    bd_in = lambda m: jnp.einsum('gpc,gh->gchp', m, eye).reshape(S5_CH, S5_N)
    bd_out = lambda m: jnp.einsum('gcp,gh->gphc', m, eye).reshape(S5_N, S5_CH)
    bbd = jnp.concatenate([bd_in(bb_re), bd_in(bb_im)], axis=1).astype(BF16)
    cbd = jnp.concatenate([bd_out(c_re), -bd_out(c_im)], axis=0).astype(BF16)
    seg = tm // 8
    steps = jnp.arange(1, seg + 1, dtype=F32)[:, None, None] * dt[None]
    pmag = jnp.exp(lam_re[None] * steps)
    pw_re = (pmag * jnp.cos(lam_im[None] * steps)).reshape(seg, S5_N)
    pw_im = (pmag * jnp.sin(lam_im[None] * steps)).reshape(seg, S5_N)
    src = (np.arange(tm) % 8) * seg + np.arange(tm) // 8
    perm = np.zeros((tm, tm), np.float32)
    perm[np.arange(tm), src] = 1.0
    consts = [jnp.asarray(perm, BF16), jnp.asarray(perm.T, BF16),
              bbd, cbd, ab_re.reshape(1, S5_N), ab_im.reshape(1, S5_N), pw_re[seg - 1:seg], pw_im[seg - 1:seg],
              pw_re, pw_im, d_skip[None, :], w_glu.astype(BF16), b_glu[None, :]]
    row = pl.BlockSpec((1, tm, S5_CH), lambda bi, ti: (bi, ti, 0))
    return pl.pallas_call(
        _s5_kernel,
        grid=(b, s // tm),
        in_specs=[row] + [_full(c.shape) for c in consts],
        out_specs=row,
        out_shape=jax.ShapeDtypeStruct((b, s, S5_CH), BF16),
        scratch_shapes=[pltpu.VMEM((tm, 2 * S5_N), F32), pltpu.VMEM((1, S5_N), F32),
                        pltpu.VMEM((1, S5_N), F32)],
        compiler_params=_params(("arbitrary", "arbitrary")),
        name="s5",
    )(u, *consts)


def _gdn_kernel(x_ref, z_ref, gb_ref, cw_ref, nega_ref, dtb_ref, onorm_ref, tril_ref, triu_ref, o_ref,
                xpad_ref, state_ref):
    t = pl.program_id(1)
    tm = x_ref.shape[1]
    c = GDN_CHUNK
    hd = GDN_HEAD_DIM

    @pl.when(t == 0)
    def _():
        xpad_ref[0:8, :] = jnp.zeros((8, xpad_ref.shape[1]), F32)
        state_ref[...] = jnp.zeros_like(state_ref)

    @pl.when(t > 0)
    def _():
        xpad_ref[0:8, :] = xpad_ref[tm:tm + 8, :]

    xpad_ref[8:tm + 8, :] = x_ref[0]
    conv = cw_ref[0:1, :] * xpad_ref[pl.ds(8 - (GDN_CONV - 1), tm), :]
    for i in range(1, GDN_CONV):
        conv = conv + cw_ref[i:i + 1, :] * xpad_ref[pl.ds(8 - (GDN_CONV - 1) + i, tm), :]
    act = _silu(conv)

    def l2n(x):
        return x * lax.rsqrt(jnp.sum(x * x, axis=-1, keepdims=True) + RMS_EPS)

    gb = gb_ref[0]
    g = nega_ref[...] * _softplus(gb + dtb_ref[...])
    beta = jax.nn.sigmoid(gb)
    gc = _hdot(tril_ref[...], g)
    gct = _hdot(g.T, triu_ref[...])

    ri = lax.broadcasted_iota(jnp.int32, (tm, tm), 0)
    ci = lax.broadcasted_iota(jnp.int32, (tm, tm), 1)
    same = (ri // c) == (ci // c)
    incl = same & (ri >= ci)
    strict = same & (ri > ci)
    eye = (ri == ci).astype(F32)
    offs = []
    bs = 1
    while bs < c:
        offs.append(((ri // (2 * bs)) == (ci // (2 * bs))) & ((ri % (2 * bs)) >= bs) & ((ci % (2 * bs)) < bs))
        bs *= 2
    z = z_ref[0]
    nchunks = tm // c

    heads = range(GDN_HEADS)
    q = [l2n(act[:, hh * hd:(hh + 1) * hd]) * (hd ** -0.5) for hh in heads]
    k = [l2n(act[:, GDN_W + hh * hd:GDN_W + (hh + 1) * hd]) for hh in heads]
    v = [act[:, 2 * GDN_W + hh * hd:2 * GDN_W + (hh + 1) * hd] for hh in heads]
    bcol = [beta[:, GDN_HEADS + hh:GDN_HEADS + hh + 1] for hh in heads]
    gcol = [gc[:, hh:hh + 1] for hh in heads]
    decay = [jnp.where(incl, jnp.exp(jnp.where(incl, gcol[hh] - gct[hh:hh + 1, :], 0.0)), 0.0) for hh in heads]
    kb = [k[hh] * bcol[hh] for hh in heads]
    a_mat = [jnp.where(strict, _bdot_nt(kb[hh], k[hh]) * decay[hh], 0.0) for hh in heads]
    t_mat = [eye for _ in heads]
    for off in offs:
        pa = [_bdot(t_mat[hh], jnp.where(off, a_mat[hh], 0.0)) for hh in heads]
        t_mat = [t_mat[hh] - _bdot(pa[hh], t_mat[hh]) for hh in heads]
    th = [t_mat[hh].astype(BF16) for hh in heads]
    tl = [(t_mat[hh] - th[hh].astype(F32)).astype(BF16) for hh in heads]
    ah = [a_mat[hh].astype(BF16) for hh in heads]
    al = [(a_mat[hh] - ah[hh].astype(F32)).astype(BF16) for hh in heads]
    a_t = [jnp.dot(ah[hh], th[hh], preferred_element_type=F32) + jnp.dot(ah[hh], tl[hh], preferred_element_type=F32)
           + jnp.dot(al[hh], th[hh], preferred_element_type=F32) for hh in heads]
    t_mat = [t_mat[hh] + jnp.dot(th[hh], (eye - t_mat[hh] - a_t[hh]).astype(BF16), preferred_element_type=F32)
             for hh in heads]
    eg = [jnp.exp(gcol[hh]) for hh in heads]
    u = [_bdot(t_mat[hh], v[hh] * bcol[hh]) for hh in heads]
    w = [_bdot(t_mat[hh], kb[hh] * eg[hh]) for hh in heads]
    intra = [jnp.where(incl, _bdot_nt(q[hh], k[hh]) * decay[hh], 0.0).astype(BF16) for hh in heads]
    qd = [q[hh] * eg[hh] for hh in heads]
    state = [state_ref[hh] for hh in heads]
    for n in range(nchunks):
        r0 = n * c
        for hh in heads:
            lo = hh * hd
            gcn = gcol[hh][r0:r0 + c, :]
            glast = gcol[hh][r0 + c - 1:r0 + c, :]
            v_new = u[hh][r0:r0 + c, :] - _bdot(w[hh][r0:r0 + c, :], state[hh])
            v_rep = jnp.concatenate([v_new.astype(BF16)] * nchunks, axis=0)
            o = _bdot(qd[hh][r0:r0 + c, :], state[hh]) + jnp.dot(intra[hh][r0:r0 + c, :], v_rep,
                                                                  preferred_element_type=F32)
            state[hh] = state[hh] * jnp.exp(glast) + _bdot_tn(k[hh][r0:r0 + c, :] * jnp.exp(glast - gcn), v_new)
            on = o * lax.rsqrt(jnp.mean(o * o, axis=-1, keepdims=True) + RMS_EPS) * onorm_ref[...]
            o_ref[0, r0:r0 + c, lo:lo + hd] = (on * _silu(z[r0:r0 + c, lo:lo + hd])).astype(o_ref.dtype)
    for hh in heads:
        state_ref[hh] = state[hh]


def _gdn(qkv, z, gb, conv_w, a_log, dt_bias, o_norm):
    b, s, cw = qkv.shape
    tm = _tile(s, 256)
    nega = jnp.zeros((1, LANES), F32).at[0, :GDN_HEADS].set(-jnp.exp(a_log))
    dtb = jnp.zeros((1, LANES), F32).at[0, :GDN_HEADS].set(dt_bias)
    cwp = jnp.pad(conv_w, ((0, 8 - GDN_CONV), (0, 0)))
    ridx = np.arange(tm)
    same = (ridx[:, None] // GDN_CHUNK) == (ridx[None, :] // GDN_CHUNK)
    tril = jnp.asarray((same & (ridx[:, None] >= ridx[None, :])).astype(np.float32))
    triu = jnp.asarray((same & (ridx[:, None] <= ridx[None, :])).astype(np.float32))
    consts = [cwp, nega, dtb, o_norm[None, :], tril, triu]
    row = lambda n: pl.BlockSpec((1, tm, n), lambda bi, ti: (bi, ti, 0))
    return pl.pallas_call(
        _gdn_kernel,
        grid=(b, s // tm),
        in_specs=[row(cw), row(GDN_W), row(LANES)] + [_full(c.shape) for c in consts],
        out_specs=row(GDN_W),
        out_shape=jax.ShapeDtypeStruct((b, s, GDN_W), BF16),
        scratch_shapes=[pltpu.VMEM((tm + 8, cw), F32), pltpu.VMEM((GDN_HEADS, GDN_HEAD_DIM, GDN_HEAD_DIM), F32)],
        compiler_params=_params(("arbitrary", "arbitrary")),
        name="gdn",
    )(qkv, z, gb, *consts)


def _router_kernel(h_ref, g_ref, wr_ref, br_ref, tri_ref, xs_o, keyt_o, wt_o, cnt_o, sel_ref):
    tm = h_ref.shape[0]
    m = _rms(h_ref[...], g_ref[...])
    logits = _hdot(m, wr_ref[...]) + br_ref[...]
    lane = lax.broadcasted_iota(jnp.int32, (tm, LANES), 1)
    neg = -jnp.inf

    def first_argmax(x):
        mx = jnp.max(x, axis=-1, keepdims=True)
        idx = jnp.min(jnp.where(x == mx, lane, LANES), axis=-1, keepdims=True)
        return mx, idx

    is_g = (lane >= N_EXPERTS) & (lane < N_EXPERTS + MOE_GROUPS)
    gl = jnp.where(is_g, logits, neg)
    gmax, gidx = first_argmax(gl)
    g_w = 1.0 / jnp.sum(jnp.where(is_g, jnp.exp(gl - gmax), 0.0), axis=-1, keepdims=True)
    in_group = (lane // MOE_PER_GROUP) == (gidx - N_EXPERTS)
    el = jnp.where(in_group & (lane < N_EXPERTS), logits, neg)
    m1, i1 = first_argmax(el)
    el2 = jnp.where(lane == i1, neg, el)
    m2, i2 = first_argmax(el2)
    r = jnp.exp(m2 - m1)
    w1 = g_w / (1.0 + r)
    w2 = g_w * r / (1.0 + r)
    chose = (lane == i1) | (lane == i2)
    wmat = jnp.where(lane == i1, w1, jnp.where(lane == i2, w2, 0.0))
    ch = chose.astype(F32)
    rank = jnp.dot(tri_ref[...], ch.astype(BF16), preferred_element_type=F32)
    keyt = jnp.where(chose, rank, -1.0).T
    keyt_o[0] = keyt
    wt_o[0] = wmat.T
    cnt_o[0] = jnp.sum(ch, axis=0, keepdims=True).astype(jnp.int32)
    riota = lax.broadcasted_iota(jnp.int32, (MOE_CAP, tm), 0).astype(F32)
    for e in range(N_EXPERTS):
        sel_ref[e * MOE_CAP:(e + 1) * MOE_CAP, :] = jnp.where(keyt[e:e + 1, :] == riota, 1.0, 0.0).astype(BF16)
    xg = jnp.dot(sel_ref[...], m.astype(BF16), preferred_element_type=F32)
    xs_o[...] = xg.astype(BF16).reshape(xs_o.shape)


def _moe_router(hf, norm_g, w_group, b_group, w_expert, b_expert, tb):
    n, d = hf.shape
    nblk = n // tb
    wr = jnp.zeros((d, LANES), F32).at[:, :N_EXPERTS].set(w_expert)
    wr = wr.at[:, N_EXPERTS:N_EXPERTS + MOE_GROUPS].set(w_group)
    br = jnp.zeros((1, LANES), F32).at[0, :N_EXPERTS].set(b_expert)
    br = br.at[0, N_EXPERTS:N_EXPERTS + MOE_GROUPS].set(b_group)
    tri = jnp.asarray(np.tril(np.ones((tb, tb), np.float32), -1), BF16)
    blk = pl.BlockSpec((1, LANES, tb), lambda i: (i, 0, 0))
    return pl.pallas_call(
        _router_kernel,
        grid=(nblk,),
        in_specs=[pl.BlockSpec((tb, d), lambda i: (i, 0)), _full((1, d)), _full(wr.shape), _full(br.shape),
                  _full(tri.shape)],
        out_specs=[pl.BlockSpec((N_EXPERTS, MOE_CAP, d), lambda i: (0, i, 0)), blk, blk,
                   pl.BlockSpec((1, 1, LANES), lambda i: (i, 0, 0))],
        out_shape=[jax.ShapeDtypeStruct((N_EXPERTS, nblk * MOE_CAP, d), BF16),
                   jax.ShapeDtypeStruct((nblk, LANES, tb), F32),
                   jax.ShapeDtypeStruct((nblk, LANES, tb), F32),
                   jax.ShapeDtypeStruct((nblk, 1, LANES), jnp.int32)],
        scratch_shapes=[pltpu.VMEM((N_EXPERTS * MOE_CAP, tb), BF16)],
        compiler_params=_params(("arbitrary",)),
        name="moe_router",
    )(hf, norm_g[None, :], wr, br, tri)


def _expert_mlp_kernel(x_ref, wg_ref, wu_ref, wd_ref, y_ref, wg_sc, wu_sc, wd_sc):
    @pl.when(pl.program_id(1) == 0)
    def _():
        wg_sc[...] = wg_ref[0, 0].astype(BF16)
        wu_sc[...] = wu_ref[0, 0].astype(BF16)
        wd_sc[...] = wd_ref[0, 0].astype(BF16)

    x = x_ref[0]
    hid = _silu(jnp.dot(x, wg_sc[...], preferred_element_type=F32)) * jnp.dot(
        x, wu_sc[...], preferred_element_type=F32)
    y_ref[0] = jnp.dot(hid.astype(BF16), wd_sc[...], preferred_element_type=F32).astype(BF16)


def _expert_mlp(xs, w_gate, w_up, w_down, layer):
    ne, rows, d = xs.shape
    ff = w_gate.shape[3]
    tr = _tile(rows, 1024)
    return pl.pallas_call(
        _expert_mlp_kernel,
        grid=(ne, rows // tr),
        in_specs=[pl.BlockSpec((1, tr, d), lambda e, i: (e, i, 0)),
                  pl.BlockSpec((1, 1, d, ff), lambda e, i: (layer, e, 0, 0)),
                  pl.BlockSpec((1, 1, d, ff), lambda e, i: (layer, e, 0, 0)),
                  pl.BlockSpec((1, 1, ff, d), lambda e, i: (layer, e, 0, 0))],
        out_specs=pl.BlockSpec((1, tr, d), lambda e, i: (e, i, 0)),
        out_shape=jax.ShapeDtypeStruct((ne, rows, d), BF16),
        scratch_shapes=[pltpu.VMEM((d, ff), BF16), pltpu.VMEM((d, ff), BF16), pltpu.VMEM((ff, d), BF16)],
        compiler_params=_params(("arbitrary", "arbitrary")),
        name="moe_expert_mlp",
    )(xs, w_gate, w_up, w_down)


def _combine_kernel(y_ref, keyt_ref, wt_ref, h_ref, o_ref, sel_ref):
    tb = h_ref.shape[0]
    riota = lax.broadcasted_iota(jnp.int32, (MOE_CAP, tb), 0).astype(F32)
    for e in range(N_EXPERTS):
        hit = keyt_ref[0, e:e + 1, :] == riota
        sel_ref[e * MOE_CAP:(e + 1) * MOE_CAP, :] = jnp.where(hit, wt_ref[0, e:e + 1, :], 0.0).astype(BF16)
    y = y_ref[...].reshape(N_EXPERTS * MOE_CAP, y_ref.shape[2])
    o_ref[...] = h_ref[...] + lax.dot_general(sel_ref[...], y, (((0,), (0,)), ((), ())),
                                              preferred_element_type=F32)


def _moe_combine(ys, keyt, wt, hf, tb):
    n, d = hf.shape
    nblk = n // tb
    blk = pl.BlockSpec((1, LANES, tb), lambda i: (i, 0, 0))
    return pl.pallas_call(
        _combine_kernel,
        grid=(nblk,),
        in_specs=[pl.BlockSpec((N_EXPERTS, MOE_CAP, d), lambda i: (0, i, 0)), blk, blk,
                  pl.BlockSpec((tb, d), lambda i: (i, 0))],
        out_specs=pl.BlockSpec((tb, d), lambda i: (i, 0)),
        out_shape=jax.ShapeDtypeStruct((n, d), F32),
        scratch_shapes=[pltpu.VMEM((N_EXPERTS * MOE_CAP, tb), BF16)],
        compiler_params=_params(("arbitrary",)),
        name="moe_combine",
    )(ys, keyt, wt, hf)


def _overflow_kernel(cnt_ref, h_ref, g_ref, keyt_ref, wt_ref, base_ref, wg_ref, wu_ref, wd_ref, o_ref,
                     acc_ref, m_ref):
    blk = pl.program_id(0)
    e = pl.program_id(1)
    tb = h_ref.shape[0]

    @pl.when(e == 0)
    def _():
        acc_ref[...] = base_ref[...]
        m_ref[...] = _rms(h_ref[...], g_ref[...]).astype(BF16)

    extra = jnp.maximum(cnt_ref[blk * LANES + e] - MOE_CAP, 0)
    krow = keyt_ref[0, pl.ds(e, 1), :]
    wrow = wt_ref[0, pl.ds(e, 1), :]
    riota = lax.broadcasted_iota(jnp.int32, (MOE_ROWS, tb), 0).astype(F32)

    def chunk(ci, carry):
        hit = krow == (riota + (MOE_CAP + ci * MOE_ROWS).astype(F32))
        sel = jnp.where(hit, 1.0, 0.0).astype(BF16)
        xg = jnp.dot(sel, m_ref[...], preferred_element_type=F32).astype(BF16)
        hid = _silu(_bdot(xg, wg_ref[0, 0])) * _bdot(xg, wu_ref[0, 0])
        y = _bdot(hid, wd_ref[0, 0]).astype(BF16)
        acc_ref[...] += _bdot_tn(jnp.where(hit, wrow, 0.0), y)
        return carry

    lax.fori_loop(0, (extra + MOE_ROWS - 1) // MOE_ROWS, chunk, 0)

    @pl.when(e == N_EXPERTS - 1)
    def _():
        o_ref[...] = acc_ref[...]


def _moe_overflow(cnt, hf, norm_g, keyt, wt, base, w_gate, w_up, w_down, layer, tb):
    n, d = hf.shape
    ff = w_gate.shape[3]
    nblk = n // tb
    tok = pl.BlockSpec((tb, d), lambda i, e, c: (i, 0))
    blk = pl.BlockSpec((1, LANES, tb), lambda i, e, c: (i, 0, 0))
    grid_spec = pltpu.PrefetchScalarGridSpec(
        num_scalar_prefetch=1,
        grid=(nblk, N_EXPERTS),
        in_specs=[tok, pl.BlockSpec((1, d), lambda i, e, c: (0, 0)), blk, blk, tok,
                  pl.BlockSpec((1, 1, d, ff), lambda i, e, c: (layer, e, 0, 0)),
                  pl.BlockSpec((1, 1, d, ff), lambda i, e, c: (layer, e, 0, 0)),
                  pl.BlockSpec((1, 1, ff, d), lambda i, e, c: (layer, e, 0, 0))],
        out_specs=tok,
        scratch_shapes=[pltpu.VMEM((tb, d), F32), pltpu.VMEM((tb, d), BF16)],
    )
    return pl.pallas_call(
        _overflow_kernel,
        grid_spec=grid_spec,
        out_shape=jax.ShapeDtypeStruct((n, d), F32),
        compiler_params=_params(("arbitrary", "arbitrary")),
        name="moe_overflow",
    )(cnt, hf, norm_g[None, :], keyt, wt, base, w_gate, w_up, w_down)


def _hier_moe_residual(h, norm_g, w_group, b_group, w_expert, b_expert, w_gate, w_up, w_down, layer):
    b, s, d = h.shape
    hf = h.reshape(b * s, d)
    tb = _tile(b * s, MOE_TB)
    xs, keyt, wt, cnt = _moe_router(hf, norm_g, w_group, b_group, w_expert, b_expert, tb)
    ys = _expert_mlp(xs, w_gate, w_up, w_down, layer)
    out = _moe_combine(ys, keyt, wt, hf, tb)
    out = lax.cond(jnp.max(cnt[:, 0, :N_EXPERTS]) > MOE_CAP,
                   lambda o: _moe_overflow(cnt.reshape(-1), hf, norm_g, keyt, wt, o, w_gate, w_up, w_down, layer, tb),
                   lambda o: o, out)
    return out.reshape(b, s, d)


def _ple_kernel(h_ref, p_ref, g_ref, wg_ref, bg_ref, wp_ref, o_ref):
    h = h_ref[0]
    gate = jax.nn.sigmoid(_bdot(_rms(h, g_ref[...]), wg_ref[...]) + bg_ref[...])
    o_ref[0] = h + gate * _bdot(p_ref[0, 0], wp_ref[...])


def _ple_residual(h, p_all, layer, w_proj, norm_g, w_gate, b_gate):
    b, s, d = h.shape
    pd = p_all.shape[3]
    tm = _tile(s, 512)
    row = lambda n: pl.BlockSpec((1, tm, n), lambda bi, ti: (bi, ti, 0))
    return pl.pallas_call(
        _ple_kernel,
        grid=(b, s // tm),
        in_specs=[row(d), pl.BlockSpec((1, 1, tm, pd), lambda bi, ti: (layer, bi, ti, 0)), _full((1, d)),
                  _full((d, d)), _full((1, d)), _full((pd, d))],
        out_specs=row(d),
        out_shape=jax.ShapeDtypeStruct((b, s, d), F32),
        compiler_params=_params(("arbitrary", "arbitrary")),
        name="ple",
    )(h, p_all, norm_g[None, :], w_gate.astype(BF16), b_gate[None, :], w_proj.astype(BF16))


def _even_layer(h, positions, norm_mix, w_in, b_f, fox_qn, fox_kn, q_a_norm, w_q_up, kv_a_norm, w_kv_up,
                mla_qn, mla_kn, w_out):
    fq, fk, fv, cum, mq, mk, mv = _even_pre(h, positions, norm_mix, w_in, b_f, fox_qn, fox_kn, q_a_norm,
                                            w_q_up, kv_a_norm, w_kv_up, mla_qn, mla_kn)
    o_fox = _attention(fq, fk, fv, cum)
    o_mla = _attention(mq, mk, mv)
    return _proj2_residual(o_fox, o_mla, w_out, h)


def _odd_layer(h, norm_mix, w_in, a_re, a_im, b_re, b_im, c_re, c_im, d_skip, log_step, w_glu, b_glu,
               conv_w, a_log, dt_bias, o_norm, w_out):
    u, qkv, z, gb = _odd_pre(h, norm_mix, w_in)
    y_ssm = _s5(u, a_re, a_im, b_re, b_im, c_re, c_im, d_skip, log_step, w_glu, b_glu)
    o_gdn = _gdn(qkv, z, gb, conv_w, a_log, dt_bias, jnp.tile(o_norm, 1))
    return _proj2_residual(y_ssm, o_gdn, w_out, h)


def kernel(x, p, positions, norm_mix, norm_ffn, ev_w_in, fox_b_f, fox_q_norm, fox_k_norm, mla_q_a_norm, mla_w_q_up, mla_kv_a_norm, mla_w_kv_up, mla_q_norm, mla_k_norm, ev_w_out, od_w_in, s5_a_re, s5_a_im, s5_b_re, s5_b_im, s5_c_re, s5_c_im, s5_d, s5_log_step, s5_w_glu, s5_b_glu, gdn_conv_w, gdn_a_log, gdn_dt_bias, gdn_o_norm, od_w_out, moe_w_group, moe_b_group, moe_w_expert, moe_b_expert, moe_w_gate, moe_w_up, moe_w_down, ple_w_proj, ple_norm, ple_w_gate, ple_b_gate):
    h = x
    depth = p.shape[0]
    for i in range(depth):
        j = i // 2
        if i % 2 == 0:
            h = _even_layer(h, positions, norm_mix[i], ev_w_in[j], fox_b_f[j], fox_q_norm[j], fox_k_norm[j],
                            mla_q_a_norm[j], mla_w_q_up[j], mla_kv_a_norm[j], mla_w_kv_up[j], mla_q_norm[j],
                            mla_k_norm[j], ev_w_out[j])
        else:
            h = _odd_layer(h, norm_mix[i], od_w_in[j], s5_a_re[j], s5_a_im[j], s5_b_re[j], s5_b_im[j],
                           s5_c_re[j], s5_c_im[j], s5_d[j], s5_log_step[j], s5_w_glu[j], s5_b_glu[j],
                           gdn_conv_w[j], gdn_a_log[j], gdn_dt_bias[j], gdn_o_norm[j], od_w_out[j])
        h = _hier_moe_residual(h, norm_ffn[i], moe_w_group[i], moe_b_group[i], moe_w_expert[i],
                               moe_b_expert[i], moe_w_gate, moe_w_up, moe_w_down, i)
        h = _ple_residual(h, p, i, ple_w_proj[i], ple_norm[i], ple_w_gate[i], ple_b_gate[i])
    return h
```

```python
import functools
import math

import numpy as np
import jax
import jax.numpy as jnp
from jax import lax
from jax.experimental import pallas as pl
from jax.experimental.pallas import tpu as pltpu

F32 = jnp.float32
BF16 = jnp.bfloat16
HI = lax.Precision.HIGHEST

LANES = 128
RMS_EPS = 1e-6
ROPE_THETA = 10000.0
LOG2E = math.log2(math.e)

FOX_HEADS = 8
FOX_HEAD_DIM = 64
MLA_HEADS = 8
MLA_Q_LORA = 384
MLA_KV_LORA = 256
MLA_NOPE = 64
MLA_ROPE = 32
MLA_V = 64
MLA_QK = MLA_NOPE + MLA_ROPE

S5_CH = 512
S5_GROUP_CH = 16
S5_GROUPS = S5_CH // S5_GROUP_CH
S5_STATE = 64
S5_N = S5_GROUPS * S5_STATE

GDN_HEADS = 4
GDN_HEAD_DIM = 128
GDN_W = GDN_HEADS * GDN_HEAD_DIM
GDN_CONV = 4
GDN_CHUNK = 64

MOE_GROUPS = 4
MOE_PER_GROUP = 8
N_EXPERTS = MOE_GROUPS * MOE_PER_GROUP
MOE_TB = 512
MOE_CAP = 64
MOE_ROWS = 128
ATTN_TQ = 2048
ATTN_TK = 512

VMEM_LIMIT = 56 * 1024 * 1024


def _tile(n, pref):
    t = min(n, pref)
    assert n % t == 0, (n, t)
    return t


def _params(sem):
    return pltpu.CompilerParams(dimension_semantics=sem, vmem_limit_bytes=VMEM_LIMIT)


def _full(shape):
    nd = len(shape)
    return pl.BlockSpec(shape, lambda *_: (0,) * nd)


def _rms(x, g):
    return x * lax.rsqrt(jnp.mean(x * x, axis=-1, keepdims=True) + RMS_EPS) * g


def _bdot(a, b):
    return jnp.dot(a.astype(BF16), b.astype(BF16), preferred_element_type=F32)


def _bdot_nt(a, b):
    return lax.dot_general(a.astype(BF16), b.astype(BF16), (((1,), (1,)), ((), ())),
                           preferred_element_type=F32)


def _bdot_tn(a, b):
    return lax.dot_general(a.astype(BF16), b.astype(BF16), (((0,), (0,)), ((), ())),
                           preferred_element_type=F32)


def _hdot(a, b):
    return jnp.dot(a, b, precision=HI, preferred_element_type=F32)


def _split_dot(x, ind):
    hi = x.astype(BF16)
    lo = (x - hi.astype(F32)).astype(BF16)
    return (jnp.dot(hi, ind, preferred_element_type=F32)
            + jnp.dot(lo, ind, preferred_element_type=F32))


def _log_sigmoid(x):
    return jnp.minimum(x, 0.0) - jnp.log(1.0 + jnp.exp(-jnp.abs(x)))


def _softplus(x):
    return jnp.maximum(x, 0.0) + jnp.log(1.0 + jnp.exp(-jnp.abs(x)))


def _silu(x):
    return x * jax.nn.sigmoid(x)


def _head_norm128(x, nheads, denom, gain):
    outs = []
    for hh in range(nheads):
        xh = x[:, LANES * hh:LANES * (hh + 1)]
        ss = jnp.sum(xh * xh, axis=-1, keepdims=True)
        outs.append(xh * lax.rsqrt(ss / denom + RMS_EPS))
    return jnp.concatenate(outs, axis=1) * gain


def _even_pre_kernel(h_ref, pos_ref, nmix_ref, win_ref, ind_ref, fqn_ref, fkn_ref, bf_ref,
                     qan_ref, wq_ref, kvan_ref, wkv_ref, mqn_ref, mkn_ref, freq_ref, s1_ref, s2_ref,
                     tri_ref, vone_ref, fq_o, fk_o, fv_o, cum_o, mq_o, mk_o, mv_o, carry_ref):
    t = pl.program_id(1)

    @pl.when(t == 0)
    def _():
        carry_ref[...] = jnp.zeros_like(carry_ref)

    tm = h_ref.shape[1]
    a = _rms(h_ref[0], nmix_ref[...])
    proj = _bdot(a, win_ref[...])
    nf = FOX_HEADS * FOX_HEAD_DIM
    fq = proj[:, 0:nf]
    fk = proj[:, nf:2 * nf]
    nv = FOX_HEADS * LANES
    fv = proj[:, 2 * nf:2 * nf + nv]
    o_cq = 2 * nf + nv
    cq = proj[:, o_cq:o_cq + MLA_Q_LORA]
    o_ckv = o_cq + MLA_Q_LORA
    ckv = proj[:, o_ckv:o_ckv + MLA_KV_LORA]
    misc = proj[:, o_ckv + MLA_KV_LORA:]

    ind = ind_ref[...]
    fq_n = fq * lax.rsqrt(_split_dot(fq * fq, ind) / FOX_HEAD_DIM + RMS_EPS) * fqn_ref[...]
    fk_n = fk * lax.rsqrt(_split_dot(fk * fk, ind) / FOX_HEAD_DIM + RMS_EPS) * fkn_ref[...]
    fq_o[0] = (fq_n * (FOX_HEAD_DIM ** -0.5 * LOG2E)).astype(BF16)
    fk_o[0] = fk_n.astype(BF16)
    fv_o[0] = (fv + vone_ref[...]).astype(BF16)

    lane = lax.broadcasted_iota(jnp.int32, (tm, LANES), 1)
    logf = jnp.where(lane < FOX_HEADS, _log_sigmoid(misc + bf_ref[...]), 0.0)
    cum = _hdot(tri_ref[...], logf) + carry_ref[...]
    carry_ref[...] = cum[tm - 1:tm, :]
    cum_o[0] = (cum * LOG2E).T[:FOX_HEADS, :]

    ang = pos_ref[0].astype(F32) * freq_ref[...]
    cos1 = jnp.cos(ang)
    sin1 = jnp.sin(ang)
    cos = jnp.concatenate([cos1] * MLA_HEADS, axis=1)
    sin_a = jnp.concatenate([sin1 * s1_ref[...]] * MLA_HEADS, axis=1)
    sin_b = jnp.concatenate([sin1 * s2_ref[...]] * MLA_HEADS, axis=1)
    width = MLA_HEADS * LANES
    half = MLA_ROPE // 2

    def rope(x):
        return (x * cos + pltpu.roll(x, width - half, 1) * sin_a + pltpu.roll(x, half, 1) * sin_b)

    q = _bdot(_rms(cq, qan_ref[...]), wq_ref[...])
    q = rope(_head_norm128(q, MLA_HEADS, MLA_QK, mqn_ref[...]))
    mq_o[0] = (q * (MLA_QK ** -0.5 * LOG2E)).astype(BF16)

    kv = _bdot(_rms(ckv, kvan_ref[...]), wkv_ref[...])
    kr = pltpu.roll(misc, MLA_NOPE - FOX_HEADS, 1)
    kr = jnp.where((lane >= MLA_NOPE) & (lane < MLA_QK), kr, 0.0)
    k = kv[:, :width] + jnp.concatenate([kr] * MLA_HEADS, axis=1)
    k = rope(_head_norm128(k, MLA_HEADS, MLA_QK, mkn_ref[...]))
    mk_o[0] = k.astype(BF16)
    mv_o[0] = (kv[:, width:] + vone_ref[...]).astype(BF16)


def _even_pre(h, positions, norm_mix, w_in, b_f, fox_qn, fox_kn, q_a_norm, w_q_up, kv_a_norm, w_kv_up,
              mla_qn, mla_kn):
    b, s, d = h.shape
    tm = _tile(s, 256)
    nf = FOX_HEADS * FOX_HEAD_DIM
    sizes = (nf, nf, nf, FOX_HEADS, MLA_Q_LORA, MLA_KV_LORA, MLA_ROPE)
    offs = np.concatenate([[0], np.cumsum(sizes)])
    parts = [w_in[:, offs[i]:offs[i + 1]] for i in range(len(sizes))]
    pad = jnp.zeros((d, LANES - FOX_HEADS - MLA_ROPE), w_in.dtype)
    slot_pad = ((0, 0), (0, 0), (0, LANES - FOX_HEAD_DIM))
    wfv = jnp.pad(parts[2].reshape(d, FOX_HEADS, FOX_HEAD_DIM), slot_pad).reshape(d, FOX_HEADS * LANES)
    win = jnp.concatenate([parts[0], parts[1], wfv, parts[4], parts[5], parts[3], parts[6], pad],
                          axis=1).astype(BF16)
    gidx = np.arange(nf) // FOX_HEAD_DIM
    ind = jnp.asarray(gidx[:, None] == gidx[None, :], BF16)
    fqn = jnp.tile(fox_qn, FOX_HEADS)[None, :]
    fkn = jnp.tile(fox_kn, FOX_HEADS)[None, :]
    bf = jnp.zeros((1, LANES), F32).at[0, :FOX_HEADS].set(b_f)
    padq = LANES - MLA_QK
    wq = jnp.pad(w_q_up.reshape(MLA_Q_LORA, MLA_HEADS, MLA_QK), ((0, 0), (0, 0), (0, padq)))
    wq = wq.reshape(MLA_Q_LORA, MLA_HEADS * LANES).astype(BF16)
    wkv3 = w_kv_up.reshape(MLA_KV_LORA, MLA_HEADS, MLA_NOPE + MLA_V)
    wk = jnp.pad(wkv3[:, :, :MLA_NOPE], ((0, 0), (0, 0), (0, LANES - MLA_NOPE)))
    wv = jnp.pad(wkv3[:, :, MLA_NOPE:], ((0, 0), (0, 0), (0, LANES - MLA_V)))
    wkv = jnp.concatenate([wk.reshape(MLA_KV_LORA, MLA_HEADS * LANES),
                           wv.reshape(MLA_KV_LORA, MLA_HEADS * LANES)], axis=1).astype(BF16)
    vone = jnp.tile(jnp.zeros((LANES,), F32).at[MLA_V].set(1.0), MLA_HEADS)[None, :]
    mqn = jnp.tile(jnp.pad(mla_qn, (0, padq)), MLA_HEADS)[None, :]
    mkn = jnp.tile(jnp.pad(mla_kn, (0, padq)), MLA_HEADS)[None, :]
    half = MLA_ROPE // 2
    inv = ROPE_THETA ** (-jnp.arange(half, dtype=F32) * 2.0 / MLA_ROPE)
    freq = jnp.zeros((1, LANES), F32).at[0, MLA_NOPE:MLA_NOPE + half].set(inv)
    freq = freq.at[0, MLA_NOPE + half:MLA_QK].set(inv)
    s1 = jnp.zeros((1, LANES), F32).at[0, MLA_NOPE:MLA_NOPE + half].set(-1.0)
    s2 = jnp.zeros((1, LANES), F32).at[0, MLA_NOPE + half:MLA_QK].set(1.0)
    tri = jnp.asarray(np.tril(np.ones((tm, tm), np.float32)))
    pos3 = positions.reshape(b, s, 1)

    row = lambda n: pl.BlockSpec((1, tm, n), lambda bi, ti: (bi, ti, 0))
    consts = [norm_mix[None, :], win, ind, fqn, fkn, bf, q_a_norm[None, :], wq, kv_a_norm[None, :], wkv,
              mqn, mkn, freq, s1, s2, tri, vone]
    nv = FOX_HEADS * LANES
    out_shape = [jax.ShapeDtypeStruct((b, s, nf), BF16)] * 2 + [jax.ShapeDtypeStruct((b, s, nv), BF16)] + [
        jax.ShapeDtypeStruct((b, FOX_HEADS, s), F32),
        jax.ShapeDtypeStruct((b, s, MLA_HEADS * LANES), BF16),
        jax.ShapeDtypeStruct((b, s, MLA_HEADS * LANES), BF16),
        jax.ShapeDtypeStruct((b, s, MLA_HEADS * LANES), BF16)]
    return pl.pallas_call(
        _even_pre_kernel,
        grid=(b, s // tm),
        in_specs=[row(d), row(1)] + [_full(c.shape) for c in consts],
        out_specs=[row(nf), row(nf), row(nv), pl.BlockSpec((1, FOX_HEADS, tm), lambda bi, ti: (bi, 0, ti)),
                   row(MLA_HEADS * LANES), row(MLA_HEADS * LANES), row(MLA_HEADS * LANES)],
        out_shape=out_shape,
        scratch_shapes=[pltpu.VMEM((1, LANES), F32)],
        compiler_params=_params(("arbitrary", "arbitrary")),
        name="even_pre",
    )(h, pos3, *consts)


def _attn_kernel(*refs, tq, tk, fox):
    if fox:
        q_ref, k_ref, v_ref, cr_ref, o_ref = refs
    else:
        q_ref, k_ref, v_ref, o_ref = refs
    hp = pl.program_id(1)
    i = pl.program_id(2)
    lane = lax.broadcasted_iota(jnp.int32, (tq, LANES), 1)
    qs = []
    for hh in range(2):
        if fox:
            in_head = (lane >= FOX_HEAD_DIM * hh) & (lane < FOX_HEAD_DIM * (hh + 1))
            qs.append(jnp.where(in_head, q_ref[0], jnp.zeros((), BF16)))
        else:
            qs.append(q_ref[0, :, LANES * hh:LANES * (hh + 1)])

    def step(j, carry, lo=None):
        koff = pl.multiple_of(j * tk, tk)
        top = 0 if lo is None else lo
        new = []
        for hh in range(2):
            m, acc = carry[hh]
            if fox:
                kj = k_ref[0, pl.ds(koff, tk), :]
            else:
                kj = k_ref[0, pl.ds(koff, tk), LANES * hh:LANES * (hh + 1)]
            sc = lax.dot_general(qs[hh][top:], kj, (((1,), (1,)), ((), ())), preferred_element_type=F32)
            if fox:
                sc = sc - cr_ref[0, pl.ds(2 * hp + hh, 1), pl.ds(koff, tk)]
            if lo is not None:
                rowi = lax.broadcasted_iota(jnp.int32, sc.shape, 0)
                coli = lax.broadcasted_iota(jnp.int32, sc.shape, 1)
                sc = jnp.where(coli <= rowi, sc, -jnp.inf)
            m_new = jnp.maximum(m[top:], jnp.max(sc, axis=-1, keepdims=True))
            alpha = jnp.exp2(m[top:] - m_new)
            p = jnp.exp2((sc - jnp.concatenate([m_new] * (tk // LANES), axis=1)).astype(BF16))
            vj = v_ref[0, pl.ds(koff, tk), LANES * hh:LANES * (hh + 1)]
            acc_new = alpha * acc[top:] + jnp.dot(p, vj, preferred_element_type=F32)
            if top:
                m_new = jnp.concatenate([m[:top], m_new], axis=0)
                acc_new = jnp.concatenate([acc[:top], acc_new], axis=0)
            new.append((m_new, acc_new))
        return tuple(new)

    def body(jj, carry):
        for r in range(ratio):
            carry = step(jj * ratio + r, carry)
        return carry

    one = (jnp.full((tq, LANES), -jnp.inf, F32), jnp.zeros((tq, LANES), F32))
    ratio = tq // tk
    carry = lax.fori_loop(0, i, body, (one, one))
    for r in range(ratio):
        carry = step(i * ratio + r, carry, lo=r * tk)
    outs = [acc / acc[:, MLA_V:MLA_V + 1] for _, acc in carry]
    o_ref[0] = jnp.where(lane < MLA_V, outs[0], pltpu.roll(outs[1], MLA_V, 1)).astype(o_ref.dtype)


def _attention(q, k, v, cum_row=None):
    b, s, _ = v.shape
    fox = cum_row is not None
    qw = LANES if fox else 2 * LANES
    tq = _tile(s, ATTN_TQ)
    tk = _tile(tq, ATTN_TK)
    npairs = v.shape[2] // (2 * LANES)
    in_specs = [pl.BlockSpec((1, tq, qw), lambda bi, hp, i: (bi, i, hp)),
                pl.BlockSpec((1, s, qw), lambda bi, hp, i: (bi, 0, hp)),
                pl.BlockSpec((1, s, 2 * LANES), lambda bi, hp, i: (bi, 0, hp))]
    args = [q, k, v]
    if fox:
        in_specs += [pl.BlockSpec((1, FOX_HEADS, s), lambda bi, hp, i: (bi, 0, 0))]
        args += [cum_row]
    return pl.pallas_call(
        functools.partial(_attn_kernel, tq=tq, tk=tk, fox=fox),
        grid=(b, npairs, s // tq),
        in_specs=in_specs,
        out_specs=pl.BlockSpec((1, tq, LANES), lambda bi, hp, i: (bi, i, hp)),
        out_shape=jax.ShapeDtypeStruct((b, s, npairs * LANES), BF16),
        compiler_params=_params(("arbitrary", "arbitrary", "arbitrary")),
        name="fox_attention" if fox else "mla_attention",
    )(*args)


def _proj2_kernel(a_ref, b_ref, wa_ref, wb_ref, h_ref, o_ref):
    o_ref[0] = (h_ref[0] + jnp.dot(a_ref[0], wa_ref[...], preferred_element_type=F32)
                + jnp.dot(b_ref[0], wb_ref[...], preferred_element_type=F32))


def _proj2_residual(a, bb, w_out, h):
    b, s, d = h.shape
    na = a.shape[2]
    nb = bb.shape[2]
    tm = _tile(s, 512)
    wa = w_out[:na].astype(BF16)
    wb = w_out[na:].astype(BF16)
    row = lambda n: pl.BlockSpec((1, tm, n), lambda bi, ti: (bi, ti, 0))
    return pl.pallas_call(
        _proj2_kernel,
        grid=(b, s // tm),
        in_specs=[row(na), row(nb), _full(wa.shape), _full(wb.shape), row(d)],
        out_specs=row(d),
        out_shape=jax.ShapeDtypeStruct((b, s, d), F32),
        compiler_params=_params(("arbitrary", "arbitrary")),
        name="out_proj",
    )(a, bb, wa, wb, h)


def _odd_pre_kernel(h_ref, nmix_ref, win_ref, u_o, qkv_o, z_o, gb_o):
    a = _rms(h_ref[0], nmix_ref[...])
    proj = _bdot(a, win_ref[...])
    o1 = S5_CH
    o2 = o1 + 3 * GDN_W
    o3 = o2 + GDN_W
    u_o[0] = proj[:, :o1]
    qkv_o[0] = proj[:, o1:o2]
    z_o[0] = proj[:, o2:o3]
    gb_o[0] = proj[:, o3:]


def _odd_pre(h, norm_mix, w_in):
    b, s, d = h.shape
    tm = _tile(s, 256)
    sizes = (S5_CH, 3 * GDN_W, GDN_HEADS, GDN_HEADS, GDN_W)
    offs = np.concatenate([[0], np.cumsum(sizes)])
    parts = [w_in[:, offs[i]:offs[i + 1]] for i in range(len(sizes))]
    pad = jnp.zeros((d, LANES - 2 * GDN_HEADS), w_in.dtype)
    win = jnp.concatenate([parts[0], parts[1], parts[4], parts[2], parts[3], pad], axis=1).astype(BF16)
    row = lambda n: pl.BlockSpec((1, tm, n), lambda bi, ti: (bi, ti, 0))
    widths = (S5_CH, 3 * GDN_W, GDN_W, LANES)
    return pl.pallas_call(
        _odd_pre_kernel,
        grid=(b, s // tm),
        in_specs=[row(d), _full((1, d)), _full(win.shape)],
        out_specs=[row(n) for n in widths],
        out_shape=[jax.ShapeDtypeStruct((b, s, n), F32) for n in widths],
        compiler_params=_params(("arbitrary", "arbitrary")),
        name="odd_pre",
    )(h, norm_mix[None, :], win)


def _s5_kernel(u_ref, perm_ref, unperm_ref, bbd_ref, cbd_ref, ar_ref, ai_ref, asr_ref, asi_ref, pwr_ref, pwi_ref,
               d_ref, wglu_ref, bglu_ref, o_ref, x_ref, sr_ref, si_ref):
    t = pl.program_id(1)

    @pl.when(t == 0)
    def _():
        sr_ref[...] = jnp.zeros_like(sr_ref)
        si_ref[...] = jnp.zeros_like(si_ref)

    tm = u_ref.shape[1]
    nseg = 8
    seg = tm // nseg
    u = u_ref[0]
    u1 = u.astype(BF16)
    u2 = (u - u1.astype(F32)).astype(BF16)
    u3 = (u - u1.astype(F32) - u2.astype(F32)).astype(BF16)
    perm = perm_ref[...]
    u = (jnp.dot(perm, u1, preferred_element_type=F32) + jnp.dot(perm, u2, preferred_element_type=F32)
         + jnp.dot(perm, u3, preferred_element_type=F32))
    hc = S5_CH // 2
    hn = S5_N // 2
    ub = u.astype(BF16)
    for part in range(2):
        for base in (0, S5_N):
            cols = slice(base + part * hn, base + (part + 1) * hn)
            x_ref[:, cols] = jnp.dot(ub[:, part * hc:(part + 1) * hc], bbd_ref[part * hc:(part + 1) * hc, cols],
                                     preferred_element_type=F32)
    ar = ar_ref[...]
    ai = ai_ref[...]
    re = slice(0, S5_N)
    im = slice(S5_N, 2 * S5_N)

    def local(i, carry):
        xr, xi = carry
        rows = pl.ds(pl.multiple_of(i * nseg, nseg), nseg)
        nr = ar * xr - ai * xi + x_ref[rows, re]
        ni = ar * xi + ai * xr + x_ref[rows, im]
        x_ref[rows, re] = nr
        x_ref[rows, im] = ni
        return nr, ni

    zero = jnp.zeros((nseg, S5_N), F32)
    er, ei = lax.fori_loop(0, seg, local, (zero, zero), unroll=4)

    asr = asr_ref[...]
    asi = asi_ref[...]
    cr = [sr_ref[...]]
    ci = [si_ref[...]]
    for s in range(nseg):
        cr.append(asr * cr[s] - asi * ci[s] + er[s:s + 1, :])
        ci.append(asr * ci[s] + asi * cr[s] + ei[s:s + 1, :])
    sr_ref[...] = cr[nseg]
    si_ref[...] = ci[nseg]
    ent_r = jnp.concatenate(cr[:nseg], axis=0)
    ent_i = jnp.concatenate(ci[:nseg], axis=0)

    def fix(i, c):
        rows = pl.ds(pl.multiple_of(i * nseg, nseg), nseg)
        pr = pwr_ref[pl.ds(i, 1), :]
        pi = pwi_ref[pl.ds(i, 1), :]
        x_ref[rows, re] += pr * ent_r - pi * ent_i
        x_ref[rows, im] += pr * ent_i + pi * ent_r
        return c

    lax.fori_loop(0, seg, fix, 0, unroll=4)
    ys = []
    for part in range(2):
        oc = slice(part * hc, (part + 1) * hc)
        acc = None
        for base in (0, S5_N):
            rows = slice(base + part * hn, base + (part + 1) * hn)
            term = _bdot(x_ref[:, rows], cbd_ref[rows, oc])
            acc = term if acc is None else acc + term
        ys.append(acc)
    y = jnp.concatenate(ys, axis=1) + d_ref[...] * u
    hg = jax.nn.gelu(y)
    out = (hg * jax.nn.sigmoid(_bdot(hg, wglu_ref[...]) + bglu_ref[...])).astype(BF16)
    o_ref[0] = jnp.dot(unperm_ref[...], out, preferred_element_type=F32).astype(o_ref.dtype)


def _s5(u, a_re, a_im, b_re, b_im, c_re, c_im, d_skip, log_step, w_glu, b_glu):
    b, s, _ = u.shape
    tm = _tile(s, 256)
    lam_re = jnp.minimum(a_re, -1e-4)
    lam_im = a_im
    dt = jnp.exp(log_step)[:, None]
    mag = jnp.exp(lam_re * dt)
    ab_re = mag * jnp.cos(lam_im * dt)
    ab_im = mag * jnp.sin(lam_im * dt)
    den = lam_re * lam_re + lam_im * lam_im
    nr, ni = ab_re - 1.0, ab_im
    gam_re = (nr * lam_re + ni * lam_im) / den
    gam_im = (ni * lam_re - nr * lam_im) / den
    bb_re = gam_re[..., None] * b_re - gam_im[..., None] * b_im
    bb_im = gam_re[..., None] * b_im + gam_im[..., None] * b_re
    eye = jnp.eye(S5_GROUPS, dtype=F32)
    bd_in = lambda m: jnp.einsum('gpc,gh->gchp', m, eye).reshape(S5_CH, S5_N)
    bd_out = lambda m: jnp.einsum('gcp,gh->gphc', m, eye).reshape(S5_N, S5_CH)
    bbd = jnp.concatenate([bd_in(bb_re), bd_in(bb_im)], axis=1).astype(BF16)
    cbd = jnp.concatenate([bd_out(c_re), -bd_out(c_im)], axis=0).astype(BF16)
    seg = tm // 8
    steps = jnp.arange(1, seg + 1, dtype=F32)[:, None, None] * dt[None]
    pmag = jnp.exp(lam_re[None] * steps)
    pw_re = (pmag * jnp.cos(lam_im[None] * steps)).reshape(seg, S5_N)
    pw_im = (pmag * jnp.sin(lam_im[None] * steps)).reshape(seg, S5_N)
    src = (np.arange(tm) % 8) * seg + np.arange(tm) // 8
    perm = np.zeros((tm, tm), np.float32)
    perm[np.arange(tm), src] = 1.0
    consts = [jnp.asarray(perm, BF16), jnp.asarray(perm.T, BF16),
              bbd, cbd, ab_re.reshape(1, S5_N), ab_im.reshape(1, S5_N), pw_re[seg - 1:seg], pw_im[seg - 1:seg],
              pw_re, pw_im, d_skip[None, :], w_glu.astype(BF16), b_glu[None, :]]
    row = pl.BlockSpec((1, tm, S5_CH), lambda bi, ti: (bi, ti, 0))
    return pl.pallas_call(
        _s5_kernel,
        grid=(b, s // tm),
        in_specs=[row] + [_full(c.shape) for c in consts],
        out_specs=row,
        out_shape=jax.ShapeDtypeStruct((b, s, S5_CH), BF16),
        scratch_shapes=[pltpu.VMEM((tm, 2 * S5_N), F32), pltpu.VMEM((1, S5_N), F32),
                        pltpu.VMEM((1, S5_N), F32)],
        compiler_params=_params(("arbitrary", "arbitrary")),
        name="s5",
    )(u, *consts)


def _gdn_kernel(x_ref, z_ref, gb_ref, cw_ref, nega_ref, dtb_ref, onorm_ref, tril_ref, triu_ref, o_ref,
                xpad_ref, state_ref):
    t = pl.program_id(1)
    tm = x_ref.shape[1]
    c = GDN_CHUNK
    hd = GDN_HEAD_DIM

    @pl.when(t == 0)
    def _():
        xpad_ref[0:8, :] = jnp.zeros((8, xpad_ref.shape[1]), F32)
        state_ref[...] = jnp.zeros_like(state_ref)

    @pl.when(t > 0)
    def _():
        xpad_ref[0:8, :] = xpad_ref[tm:tm + 8, :]

    xpad_ref[8:tm + 8, :] = x_ref[0]
    conv = cw_ref[0:1, :] * xpad_ref[pl.ds(8 - (GDN_CONV - 1), tm), :]
    for i in range(1, GDN_CONV):
        conv = conv + cw_ref[i:i + 1, :] * xpad_ref[pl.ds(8 - (GDN_CONV - 1) + i, tm), :]
    act = _silu(conv)

    def l2n(x):
        return x * lax.rsqrt(jnp.sum(x * x, axis=-1, keepdims=True) + RMS_EPS)

    gb = gb_ref[0]
    g = nega_ref[...] * _softplus(gb + dtb_ref[...])
    beta = jax.nn.sigmoid(gb)
    gc = _hdot(tril_ref[...], g)
    gct = _hdot(g.T, triu_ref[...])

    ri = lax.broadcasted_iota(jnp.int32, (tm, tm), 0)
    ci = lax.broadcasted_iota(jnp.int32, (tm, tm), 1)
    same = (ri // c) == (ci // c)
    incl = same & (ri >= ci)
    strict = same & (ri > ci)
    eye = (ri == ci).astype(F32)
    offs = []
    bs = 1
    while bs < c:
        offs.append(((ri // (2 * bs)) == (ci // (2 * bs))) & ((ri % (2 * bs)) >= bs) & ((ci % (2 * bs)) < bs))
        bs *= 2
    z = z_ref[0]
    nchunks = tm // c

    heads = range(GDN_HEADS)
    q = [l2n(act[:, hh * hd:(hh + 1) * hd]) * (hd ** -0.5) for hh in heads]
    k = [l2n(act[:, GDN_W + hh * hd:GDN_W + (hh + 1) * hd]) for hh in heads]
    v = [act[:, 2 * GDN_W + hh * hd:2 * GDN_W + (hh + 1) * hd] for hh in heads]
    bcol = [beta[:, GDN_HEADS + hh:GDN_HEADS + hh + 1] for hh in heads]
    gcol = [gc[:, hh:hh + 1] for hh in heads]
    decay = [jnp.where(incl, jnp.exp(jnp.where(incl, gcol[hh] - gct[hh:hh + 1, :], 0.0)), 0.0) for hh in heads]
    kb = [k[hh] * bcol[hh] for hh in heads]
    a_mat = [jnp.where(strict, _bdot_nt(kb[hh], k[hh]) * decay[hh], 0.0) for hh in heads]
    t_mat = [eye for _ in heads]
    for off in offs:
        pa = [_bdot(t_mat[hh], jnp.where(off, a_mat[hh], 0.0)) for hh in heads]
        t_mat = [t_mat[hh] - _bdot(pa[hh], t_mat[hh]) for hh in heads]
    th = [t_mat[hh].astype(BF16) for hh in heads]
    tl = [(t_mat[hh] - th[hh].astype(F32)).astype(BF16) for hh in heads]
    ah = [a_mat[hh].astype(BF16) for hh in heads]
    al = [(a_mat[hh] - ah[hh].astype(F32)).astype(BF16) for hh in heads]
    a_t = [jnp.dot(ah[hh], th[hh], preferred_element_type=F32) + jnp.dot(ah[hh], tl[hh], preferred_element_type=F32)
           + jnp.dot(al[hh], th[hh], preferred_element_type=F32) for hh in heads]
    t_mat = [t_mat[hh] + jnp.dot(th[hh], (eye - t_mat[hh] - a_t[hh]).astype(BF16), preferred_element_type=F32)
             for hh in heads]
    eg = [jnp.exp(gcol[hh]) for hh in heads]
    u = [_bdot(t_mat[hh], v[hh] * bcol[hh]) for hh in heads]
    w = [_bdot(t_mat[hh], kb[hh] * eg[hh]) for hh in heads]
    intra = [jnp.where(incl, _bdot_nt(q[hh], k[hh]) * decay[hh], 0.0).astype(BF16) for hh in heads]
    qd = [q[hh] * eg[hh] for hh in heads]
    state = [state_ref[hh] for hh in heads]
    for n in range(nchunks):
        r0 = n * c
        for hh in heads:
            lo = hh * hd
            gcn = gcol[hh][r0:r0 + c, :]
            glast = gcol[hh][r0 + c - 1:r0 + c, :]
            v_new = u[hh][r0:r0 + c, :] - _bdot(w[hh][r0:r0 + c, :], state[hh])
            v_rep = jnp.concatenate([v_new.astype(BF16)] * nchunks, axis=0)
            o = _bdot(qd[hh][r0:r0 + c, :], state[hh]) + jnp.dot(intra[hh][r0:r0 + c, :], v_rep,
                                                                  preferred_element_type=F32)
            state[hh] = state[hh] * jnp.exp(glast) + _bdot_tn(k[hh][r0:r0 + c, :] * jnp.exp(glast - gcn), v_new)
            on = o * lax.rsqrt(jnp.mean(o * o, axis=-1, keepdims=True) + RMS_EPS) * onorm_ref[...]
            o_ref[0, r0:r0 + c, lo:lo + hd] = (on * _silu(z[r0:r0 + c, lo:lo + hd])).astype(o_ref.dtype)
    for hh in heads:
        state_ref[hh] = state[hh]


def _gdn(qkv, z, gb, conv_w, a_log, dt_bias, o_norm):
    b, s, cw = qkv.shape
    tm = _tile(s, 256)
    nega = jnp.zeros((1, LANES), F32).at[0, :GDN_HEADS].set(-jnp.exp(a_log))
    dtb = jnp.zeros((1, LANES), F32).at[0, :GDN_HEADS].set(dt_bias)
    cwp = jnp.pad(conv_w, ((0, 8 - GDN_CONV), (0, 0)))
    ridx = np.arange(tm)
    same = (ridx[:, None] // GDN_CHUNK) == (ridx[None, :] // GDN_CHUNK)
    tril = jnp.asarray((same & (ridx[:, None] >= ridx[None, :])).astype(np.float32))
    triu = jnp.asarray((same & (ridx[:, None] <= ridx[None, :])).astype(np.float32))
    consts = [cwp, nega, dtb, o_norm[None, :], tril, triu]
    row = lambda n: pl.BlockSpec((1, tm, n), lambda bi, ti: (bi, ti, 0))
    return pl.pallas_call(
        _gdn_kernel,
        grid=(b, s // tm),
        in_specs=[row(cw), row(GDN_W), row(LANES)] + [_full(c.shape) for c in consts],
        out_specs=row(GDN_W),
        out_shape=jax.ShapeDtypeStruct((b, s, GDN_W), BF16),
        scratch_shapes=[pltpu.VMEM((tm + 8, cw), F32), pltpu.VMEM((GDN_HEADS, GDN_HEAD_DIM, GDN_HEAD_DIM), F32)],
        compiler_params=_params(("arbitrary", "arbitrary")),
        name="gdn",
    )(qkv, z, gb, *consts)


def _router_kernel(h_ref, g_ref, wrh_ref, wrl_ref, br_ref, tri_ref, xs_o, keyt_o, wt_o, cnt_o, sel_ref):
    tm = h_ref.shape[0]
    m = _rms(h_ref[...], g_ref[...])
    mh = m.astype(BF16)
    ml = (m - mh.astype(F32)).astype(BF16)
    logits = (jnp.dot(mh, wrh_ref[...], preferred_element_type=F32) + jnp.dot(mh, wrl_ref[...], preferred_element_type=F32)
              + jnp.dot(ml, wrh_ref[...], preferred_element_type=F32)) + br_ref[...]
    lane = lax.broadcasted_iota(jnp.int32, (tm, LANES), 1)
    neg = -jnp.inf

    def first_argmax(x):
        mx = jnp.max(x, axis=-1, keepdims=True)
        idx = jnp.min(jnp.where(x == mx, lane, LANES), axis=-1, keepdims=True)
        return mx, idx

    is_g = (lane >= N_EXPERTS) & (lane < N_EXPERTS + MOE_GROUPS)
    gl = jnp.where(is_g, logits, neg)
    gmax, gidx = first_argmax(gl)
    g_w = 1.0 / jnp.sum(jnp.where(is_g, jnp.exp(gl - gmax), 0.0), axis=-1, keepdims=True)
    in_group = (lane // MOE_PER_GROUP) == (gidx - N_EXPERTS)
    el = jnp.where(in_group & (lane < N_EXPERTS), logits, neg)
    m1, i1 = first_argmax(el)
    el2 = jnp.where(lane == i1, neg, el)
    m2, i2 = first_argmax(el2)
    r = jnp.exp(m2 - m1)
    w1 = g_w / (1.0 + r)
    w2 = g_w * r / (1.0 + r)
    chose = (lane == i1) | (lane == i2)
    wmat = jnp.where(lane == i1, w1, jnp.where(lane == i2, w2, 0.0))
    ch = chose.astype(F32)
    rank = jnp.dot(tri_ref[...], ch.astype(BF16), preferred_element_type=F32)
    keyt = jnp.where(chose, rank, -1.0).T
    keyt_o[0] = keyt
    wt_o[0] = wmat.T
    cnt_o[0] = jnp.sum(ch, axis=0, keepdims=True).astype(jnp.int32)
    riota = lax.broadcasted_iota(jnp.int32, (MOE_CAP, tm), 0).astype(F32)
    for e in range(N_EXPERTS):
        sel_ref[e * MOE_CAP:(e + 1) * MOE_CAP, :] = jnp.where(keyt[e:e + 1, :] == riota, 1.0, 0.0).astype(BF16)
    xg = jnp.dot(sel_ref[...], mh, preferred_element_type=F32)
    xs_o[...] = xg.astype(BF16).reshape(xs_o.shape)


def _moe_router(hf, norm_g, w_group, b_group, w_expert, b_expert, tb):
    n, d = hf.shape
    nblk = n // tb
    wr = jnp.zeros((d, LANES), F32).at[:, :N_EXPERTS].set(w_expert)
    wr = wr.at[:, N_EXPERTS:N_EXPERTS + MOE_GROUPS].set(w_group)
    wrh = wr.astype(BF16)
    wrl = (wr - wrh.astype(F32)).astype(BF16)
    br = jnp.zeros((1, LANES), F32).at[0, :N_EXPERTS].set(b_expert)
    br = br.at[0, N_EXPERTS:N_EXPERTS + MOE_GROUPS].set(b_group)
    tri = jnp.asarray(np.tril(np.ones((tb, tb), np.float32), -1), BF16)
    blk = pl.BlockSpec((1, LANES, tb), lambda i: (i, 0, 0))
    return pl.pallas_call(
        _router_kernel,
        grid=(nblk,),
        in_specs=[pl.BlockSpec((tb, d), lambda i: (i, 0)), _full((1, d)), _full(wr.shape), _full(wr.shape), _full(br.shape),
                  _full(tri.shape)],
        out_specs=[pl.BlockSpec((N_EXPERTS, MOE_CAP, d), lambda i: (0, i, 0)), blk, blk,
                   pl.BlockSpec((1, 1, LANES), lambda i: (i, 0, 0))],
        out_shape=[jax.ShapeDtypeStruct((N_EXPERTS, nblk * MOE_CAP, d), BF16),
                   jax.ShapeDtypeStruct((nblk, LANES, tb), F32),
                   jax.ShapeDtypeStruct((nblk, LANES, tb), F32),
                   jax.ShapeDtypeStruct((nblk, 1, LANES), jnp.int32)],
        scratch_shapes=[pltpu.VMEM((N_EXPERTS * MOE_CAP, tb), BF16)],
        compiler_params=_params(("arbitrary",)),
        name="moe_router",
    )(hf, norm_g[None, :], wrh, wrl, br, tri)


def _expert_mlp_kernel(x_ref, wg_ref, wu_ref, wd_ref, y_ref, wg_sc, wu_sc, wd_sc):
    @pl.when(pl.program_id(1) == 0)
    def _():
        wg_sc[...] = wg_ref[0, 0].astype(BF16)
        wu_sc[...] = wu_ref[0, 0].astype(BF16)
        wd_sc[...] = wd_ref[0, 0].astype(BF16)

    x = x_ref[0]
    hid = _silu(jnp.dot(x, wg_sc[...], preferred_element_type=F32)) * jnp.dot(
        x, wu_sc[...], preferred_element_type=F32)
    y_ref[0] = jnp.dot(hid.astype(BF16), wd_sc[...], preferred_element_type=F32).astype(BF16)


def _expert_mlp(xs, w_gate, w_up, w_down, layer):
    ne, rows, d = xs.shape
    ff = w_gate.shape[3]
    tr = _tile(rows, 1024)
    return pl.pallas_call(
        _expert_mlp_kernel,
        grid=(ne, rows // tr),
        in_specs=[pl.BlockSpec((1, tr, d), lambda e, i: (e, i, 0)),
                  pl.BlockSpec((1, 1, d, ff), lambda e, i: (layer, e, 0, 0)),
                  pl.BlockSpec((1, 1, d, ff), lambda e, i: (layer, e, 0, 0)),
                  pl.BlockSpec((1, 1, ff, d), lambda e, i: (layer, e, 0, 0))],
        out_specs=pl.BlockSpec((1, tr, d), lambda e, i: (e, i, 0)),
        out_shape=jax.ShapeDtypeStruct((ne, rows, d), BF16),
        scratch_shapes=[pltpu.VMEM((d, ff), BF16), pltpu.VMEM((d, ff), BF16), pltpu.VMEM((ff, d), BF16)],
        compiler_params=_params(("arbitrary", "arbitrary")),
        name="moe_expert_mlp",
    )(xs, w_gate, w_up, w_down)


def _combine_kernel(y_ref, keyt_ref, wt_ref, h_ref, o_ref, sel_ref):
    tb = h_ref.shape[0]
    riota = lax.broadcasted_iota(jnp.int32, (MOE_CAP, tb), 0).astype(F32)
    for e in range(N_EXPERTS):
        hit = keyt_ref[0, e:e + 1, :] == riota
        sel_ref[e * MOE_CAP:(e + 1) * MOE_CAP, :] = jnp.where(hit, wt_ref[0, e:e + 1, :], 0.0).astype(BF16)
    y = y_ref[...].reshape(N_EXPERTS * MOE_CAP, y_ref.shape[2])
    o_ref[...] = h_ref[...] + lax.dot_general(sel_ref[...], y, (((0,), (0,)), ((), ())),
                                              preferred_element_type=F32)


def _moe_combine(ys, keyt, wt, hf, tb):
    n, d = hf.shape
    nblk = n // tb
    blk = pl.BlockSpec((1, LANES, tb), lambda i: (i, 0, 0))
    return pl.pallas_call(
        _combine_kernel,
        grid=(nblk,),
        in_specs=[pl.BlockSpec((N_EXPERTS, MOE_CAP, d), lambda i: (0, i, 0)), blk, blk,
                  pl.BlockSpec((tb, d), lambda i: (i, 0))],
        out_specs=pl.BlockSpec((tb, d), lambda i: (i, 0)),
        out_shape=jax.ShapeDtypeStruct((n, d), F32),
        scratch_shapes=[pltpu.VMEM((N_EXPERTS * MOE_CAP, tb), BF16)],
        compiler_params=_params(("arbitrary",)),
        name="moe_combine",
    )(ys, keyt, wt, hf)


def _overflow_kernel(cnt_ref, h_ref, g_ref, keyt_ref, wt_ref, base_ref, wg_ref, wu_ref, wd_ref, o_ref,
                     acc_ref, m_ref):
    blk = pl.program_id(0)
    e = pl.program_id(1)
    tb = h_ref.shape[0]

    @pl.when(e == 0)
    def _():
        acc_ref[...] = base_ref[...]
        m_ref[...] = _rms(h_ref[...], g_ref[...]).astype(BF16)

    extra = jnp.maximum(cnt_ref[blk * LANES + e] - MOE_CAP, 0)
    krow = keyt_ref[0, pl.ds(e, 1), :]
    wrow = wt_ref[0, pl.ds(e, 1), :]
    riota = lax.broadcasted_iota(jnp.int32, (MOE_ROWS, tb), 0).astype(F32)

    def chunk(ci, carry):
        hit = krow == (riota + (MOE_CAP + ci * MOE_ROWS).astype(F32))
        sel = jnp.where(hit, 1.0, 0.0).astype(BF16)
        xg = jnp.dot(sel, m_ref[...], preferred_element_type=F32).astype(BF16)
        hid = _silu(_bdot(xg, wg_ref[0, 0])) * _bdot(xg, wu_ref[0, 0])
        y = _bdot(hid, wd_ref[0, 0]).astype(BF16)
        acc_ref[...] += _bdot_tn(jnp.where(hit, wrow, 0.0), y)
        return carry

    lax.fori_loop(0, (extra + MOE_ROWS - 1) // MOE_ROWS, chunk, 0)

    @pl.when(e == N_EXPERTS - 1)
    def _():
        o_ref[...] = acc_ref[...]


def _moe_overflow(cnt, hf, norm_g, keyt, wt, base, w_gate, w_up, w_down, layer, tb):
    n, d = hf.shape
    ff = w_gate.shape[3]
    nblk = n // tb
    tok = pl.BlockSpec((tb, d), lambda i, e, c: (i, 0))
    blk = pl.BlockSpec((1, LANES, tb), lambda i, e, c: (i, 0, 0))
    grid_spec = pltpu.PrefetchScalarGridSpec(
        num_scalar_prefetch=1,
        grid=(nblk, N_EXPERTS),
        in_specs=[tok, pl.BlockSpec((1, d), lambda i, e, c: (0, 0)), blk, blk, tok,
                  pl.BlockSpec((1, 1, d, ff), lambda i, e, c: (layer, e, 0, 0)),
                  pl.BlockSpec((1, 1, d, ff), lambda i, e, c: (layer, e, 0, 0)),
                  pl.BlockSpec((1, 1, ff, d), lambda i, e, c: (layer, e, 0, 0))],
        out_specs=tok,
        scratch_shapes=[pltpu.VMEM((tb, d), F32), pltpu.VMEM((tb, d), BF16)],
    )
    return pl.pallas_call(
        _overflow_kernel,
        grid_spec=grid_spec,
        out_shape=jax.ShapeDtypeStruct((n, d), F32),
        input_output_aliases={5: 0},
        compiler_params=_params(("arbitrary", "arbitrary")),
        name="moe_overflow",
    )(cnt, hf, norm_g[None, :], keyt, wt, base, w_gate, w_up, w_down)


def _hier_moe_residual(h, norm_g, w_group, b_group, w_expert, b_expert, w_gate, w_up, w_down, layer):
    b, s, d = h.shape
    hf = h.reshape(b * s, d)
    tb = _tile(b * s, MOE_TB)
    xs, keyt, wt, cnt = _moe_router(hf, norm_g, w_group, b_group, w_expert, b_expert, tb)
    ys = _expert_mlp(xs, w_gate, w_up, w_down, layer)
    out = _moe_combine(ys, keyt, wt, hf, tb)
    out = lax.cond(jnp.max(cnt[:, 0, :N_EXPERTS]) > MOE_CAP,
                   lambda o: _moe_overflow(cnt.reshape(-1), hf, norm_g, keyt, wt, o, w_gate, w_up, w_down, layer, tb),
                   lambda o: o, out)
    return out.reshape(b, s, d)


def _ple_kernel(h_ref, p_ref, g_ref, wg_ref, bg_ref, wp_ref, o_ref):
    h = h_ref[0]
    gate = jax.nn.sigmoid(_bdot(_rms(h, g_ref[...]), wg_ref[...]) + bg_ref[...])
    o_ref[0] = h + gate * _bdot(p_ref[0, 0], wp_ref[...])


def _ple_residual(h, p_all, layer, w_proj, norm_g, w_gate, b_gate):
    b, s, d = h.shape
    pd = p_all.shape[3]
    tm = _tile(s, 512)
    row = lambda n: pl.BlockSpec((1, tm, n), lambda bi, ti: (bi, ti, 0))
    return pl.pallas_call(
        _ple_kernel,
        grid=(b, s // tm),
        in_specs=[row(d), pl.BlockSpec((1, 1, tm, pd), lambda bi, ti: (layer, bi, ti, 0)), _full((1, d)),
                  _full((d, d)), _full((1, d)), _full((pd, d))],
        out_specs=row(d),
        out_shape=jax.ShapeDtypeStruct((b, s, d), F32),
        compiler_params=_params(("arbitrary", "arbitrary")),
        name="ple",
    )(h, p_all, norm_g[None, :], w_gate.astype(BF16), b_gate[None, :], w_proj.astype(BF16))


def _even_layer(h, positions, norm_mix, w_in, b_f, fox_qn, fox_kn, q_a_norm, w_q_up, kv_a_norm, w_kv_up,
                mla_qn, mla_kn, w_out):
    fq, fk, fv, cum, mq, mk, mv = _even_pre(h, positions, norm_mix, w_in, b_f, fox_qn, fox_kn, q_a_norm,
                                            w_q_up, kv_a_norm, w_kv_up, mla_qn, mla_kn)
    o_fox = _attention(fq, fk, fv, cum)
    o_mla = _attention(mq, mk, mv)
    return _proj2_residual(o_fox, o_mla, w_out, h)


def _odd_layer(h, norm_mix, w_in, a_re, a_im, b_re, b_im, c_re, c_im, d_skip, log_step, w_glu, b_glu,
               conv_w, a_log, dt_bias, o_norm, w_out):
    u, qkv, z, gb = _odd_pre(h, norm_mix, w_in)
    y_ssm = _s5(u, a_re, a_im, b_re, b_im, c_re, c_im, d_skip, log_step, w_glu, b_glu)
    o_gdn = _gdn(qkv, z, gb, conv_w, a_log, dt_bias, jnp.tile(o_norm, 1))
    return _proj2_residual(y_ssm, o_gdn, w_out, h)


def kernel(x, p, positions, norm_mix, norm_ffn, ev_w_in, fox_b_f, fox_q_norm, fox_k_norm, mla_q_a_norm, mla_w_q_up, mla_kv_a_norm, mla_w_kv_up, mla_q_norm, mla_k_norm, ev_w_out, od_w_in, s5_a_re, s5_a_im, s5_b_re, s5_b_im, s5_c_re, s5_c_im, s5_d, s5_log_step, s5_w_glu, s5_b_glu, gdn_conv_w, gdn_a_log, gdn_dt_bias, gdn_o_norm, od_w_out, moe_w_group, moe_b_group, moe_w_expert, moe_b_expert, moe_w_gate, moe_w_up, moe_w_down, ple_w_proj, ple_norm, ple_w_gate, ple_b_gate):
    h = x
    depth = p.shape[0]
    for i in range(depth):
        j = i // 2
        if i % 2 == 0:
            h = _even_layer(h, positions, norm_mix[i], ev_w_in[j], fox_b_f[j], fox_q_norm[j], fox_k_norm[j],
                            mla_q_a_norm[j], mla_w_q_up[j], mla_kv_a_norm[j], mla_w_kv_up[j], mla_q_norm[j],
                            mla_k_norm[j], ev_w_out[j])
        else:
            h = _odd_layer(h, norm_mix[i], od_w_in[j], s5_a_re[j], s5_a_im[j], s5_b_re[j], s5_b_im[j],
                           s5_c_re[j], s5_c_im[j], s5_d[j], s5_log_step[j], s5_w_glu[j], s5_b_glu[j],
                           gdn_conv_w[j], gdn_a_log[j], gdn_dt_bias[j], gdn_o_norm[j], od_w_out[j])
        h = _hier_moe_residual(h, norm_ffn[i], moe_w_group[i], moe_b_group[i], moe_w_expert[i],
                               moe_b_expert[i], moe_w_gate, moe_w_up, moe_w_down, i)
        h = _ple_residual(h, p, i, ple_w_proj[i], ple_norm[i], ple_w_gate[i], ple_b_gate[i])
    return h
```

```python
import functools
import math

import numpy as np
import jax
import jax.numpy as jnp
from jax import lax
from jax.experimental import pallas as pl
from jax.experimental.pallas import tpu as pltpu

F32 = jnp.float32
BF16 = jnp.bfloat16
HI = lax.Precision.HIGHEST

LANES = 128
RMS_EPS = 1e-6
ROPE_THETA = 10000.0
LOG2E = math.log2(math.e)

FOX_HEADS = 8
FOX_HEAD_DIM = 64
MLA_HEADS = 8
MLA_Q_LORA = 384
MLA_KV_LORA = 256
MLA_NOPE = 64
MLA_ROPE = 32
MLA_V = 64
MLA_QK = MLA_NOPE + MLA_ROPE

S5_CH = 512
S5_GROUP_CH = 16
S5_GROUPS = S5_CH // S5_GROUP_CH
S5_STATE = 64
S5_N = S5_GROUPS * S5_STATE

GDN_HEADS = 4
GDN_HEAD_DIM = 128
GDN_W = GDN_HEADS * GDN_HEAD_DIM
GDN_CONV = 4
GDN_CHUNK = 64

MOE_GROUPS = 4
MOE_PER_GROUP = 8
N_EXPERTS = MOE_GROUPS * MOE_PER_GROUP
MOE_TB = 512
MOE_CAP = 64
MOE_ROWS = 128
ATTN_TQ = 2048
ATTN_TK = 512

VMEM_LIMIT = 56 * 1024 * 1024


def _tile(n, pref):
    t = min(n, pref)
    assert n % t == 0, (n, t)
    return t


def _params(sem):
    return pltpu.CompilerParams(dimension_semantics=sem, vmem_limit_bytes=VMEM_LIMIT)


def _full(shape):
    nd = len(shape)
    return pl.BlockSpec(shape, lambda *_: (0,) * nd)


def _rms(x, g):
    return x * lax.rsqrt(jnp.mean(x * x, axis=-1, keepdims=True) + RMS_EPS) * g


def _bdot(a, b):
    return jnp.dot(a.astype(BF16), b.astype(BF16), preferred_element_type=F32)


def _bdot_nt(a, b):
    return lax.dot_general(a.astype(BF16), b.astype(BF16), (((1,), (1,)), ((), ())),
                           preferred_element_type=F32)


def _bdot_tn(a, b):
    return lax.dot_general(a.astype(BF16), b.astype(BF16), (((0,), (0,)), ((), ())),
                           preferred_element_type=F32)


def _hdot(a, b):
    return jnp.dot(a, b, precision=HI, preferred_element_type=F32)


def _split_dot(x, ind):
    hi = x.astype(BF16)
    lo = (x - hi.astype(F32)).astype(BF16)
    return (jnp.dot(hi, ind, preferred_element_type=F32)
            + jnp.dot(lo, ind, preferred_element_type=F32))


def _log_sigmoid(x):
    return jnp.minimum(x, 0.0) - jnp.log(1.0 + jnp.exp(-jnp.abs(x)))


def _softplus(x):
    return jnp.maximum(x, 0.0) + jnp.log(1.0 + jnp.exp(-jnp.abs(x)))


def _silu(x):
    return x * jax.nn.sigmoid(x)


def _head_norm128(x, nheads, denom, gain):
    outs = []
    for hh in range(nheads):
        xh = x[:, LANES * hh:LANES * (hh + 1)]
        ss = jnp.sum(xh * xh, axis=-1, keepdims=True)
        outs.append(xh * lax.rsqrt(ss / denom + RMS_EPS))
    return jnp.concatenate(outs, axis=1) * gain


def _even_pre_kernel(h_ref, pos_ref, nmix_ref, win_ref, ind_ref, fqn_ref, fkn_ref, bf_ref,
                     qan_ref, wq_ref, kvan_ref, wkv_ref, mqn_ref, mkn_ref, freq_ref, s1_ref, s2_ref,
                     tri_ref, vone_ref, fq_o, fk_o, fv_o, cum_o, mq_o, mk_o, mv_o, carry_ref):
    t = pl.program_id(1)

    @pl.when(t == 0)
    def _():
        carry_ref[...] = jnp.zeros_like(carry_ref)

    tm = h_ref.shape[1]
    a = _rms(h_ref[0], nmix_ref[...])
    proj = _bdot(a, win_ref[...])
    nf = FOX_HEADS * FOX_HEAD_DIM
    fq = proj[:, 0:nf]
    fk = proj[:, nf:2 * nf]
    nv = FOX_HEADS * LANES
    fv = proj[:, 2 * nf:2 * nf + nv]
    o_cq = 2 * nf + nv
    cq = proj[:, o_cq:o_cq + MLA_Q_LORA]
    o_ckv = o_cq + MLA_Q_LORA
    ckv = proj[:, o_ckv:o_ckv + MLA_KV_LORA]
    misc = proj[:, o_ckv + MLA_KV_LORA:]

    ind = ind_ref[...]
    fq_n = fq * lax.rsqrt(_split_dot(fq * fq, ind) / FOX_HEAD_DIM + RMS_EPS) * fqn_ref[...]
    fk_n = fk * lax.rsqrt(_split_dot(fk * fk, ind) / FOX_HEAD_DIM + RMS_EPS) * fkn_ref[...]
    fq_o[0] = (fq_n * (FOX_HEAD_DIM ** -0.5 * LOG2E)).astype(BF16)
    fk_o[0] = fk_n.astype(BF16)
    fv_o[0] = (fv + vone_ref[...]).astype(BF16)

    lane = lax.broadcasted_iota(jnp.int32, (tm, LANES), 1)
    logf = jnp.where(lane < FOX_HEADS, _log_sigmoid(misc + bf_ref[...]), 0.0)
    cum = _hdot(tri_ref[...], logf) + carry_ref[...]
    carry_ref[...] = cum[tm - 1:tm, :]
    cum_o[0] = (cum * LOG2E).T[:FOX_HEADS, :]

    ang = pos_ref[0].astype(F32) * freq_ref[...]
    cos1 = jnp.cos(ang)
    sin1 = jnp.sin(ang)
    cos = jnp.concatenate([cos1] * MLA_HEADS, axis=1)
    sin_a = jnp.concatenate([sin1 * s1_ref[...]] * MLA_HEADS, axis=1)
    sin_b = jnp.concatenate([sin1 * s2_ref[...]] * MLA_HEADS, axis=1)
    width = MLA_HEADS * LANES
    half = MLA_ROPE // 2

    def rope(x):
        return (x * cos + pltpu.roll(x, width - half, 1) * sin_a + pltpu.roll(x, half, 1) * sin_b)

    q = _bdot(_rms(cq, qan_ref[...]), wq_ref[...])
    q = rope(_head_norm128(q, MLA_HEADS, MLA_QK, mqn_ref[...]))
    mq_o[0] = (q * (MLA_QK ** -0.5 * LOG2E)).astype(BF16)

    kv = _bdot(_rms(ckv, kvan_ref[...]), wkv_ref[...])
    kr = pltpu.roll(misc, MLA_NOPE - FOX_HEADS, 1)
    kr = jnp.where((lane >= MLA_NOPE) & (lane < MLA_QK), kr, 0.0)
    k = kv[:, :width] + jnp.concatenate([kr] * MLA_HEADS, axis=1)
    k = rope(_head_norm128(k, MLA_HEADS, MLA_QK, mkn_ref[...]))
    mk_o[0] = k.astype(BF16)
    mv_o[0] = (kv[:, width:] + vone_ref[...]).astype(BF16)


def _even_pre(h, positions, norm_mix, w_in, b_f, fox_qn, fox_kn, q_a_norm, w_q_up, kv_a_norm, w_kv_up,
              mla_qn, mla_kn):
    b, s, d = h.shape
    tm = _tile(s, 256)
    nf = FOX_HEADS * FOX_HEAD_DIM
    sizes = (nf, nf, nf, FOX_HEADS, MLA_Q_LORA, MLA_KV_LORA, MLA_ROPE)
    offs = np.concatenate([[0], np.cumsum(sizes)])
    parts = [w_in[:, offs[i]:offs[i + 1]] for i in range(len(sizes))]
    pad = jnp.zeros((d, LANES - FOX_HEADS - MLA_ROPE), w_in.dtype)
    slot_pad = ((0, 0), (0, 0), (0, LANES - FOX_HEAD_DIM))
    wfv = jnp.pad(parts[2].reshape(d, FOX_HEADS, FOX_HEAD_DIM), slot_pad).reshape(d, FOX_HEADS * LANES)
    win = jnp.concatenate([parts[0], parts[1], wfv, parts[4], parts[5], parts[3], parts[6], pad],
                          axis=1).astype(BF16)
    gidx = np.arange(nf) // FOX_HEAD_DIM
    ind = jnp.asarray(gidx[:, None] == gidx[None, :], BF16)
    fqn = jnp.tile(fox_qn, FOX_HEADS)[None, :]
    fkn = jnp.tile(fox_kn, FOX_HEADS)[None, :]
    bf = jnp.zeros((1, LANES), F32).at[0, :FOX_HEADS].set(b_f)
    padq = LANES - MLA_QK
    wq = jnp.pad(w_q_up.reshape(MLA_Q_LORA, MLA_HEADS, MLA_QK), ((0, 0), (0, 0), (0, padq)))
    wq = wq.reshape(MLA_Q_LORA, MLA_HEADS * LANES).astype(BF16)
    wkv3 = w_kv_up.reshape(MLA_KV_LORA, MLA_HEADS, MLA_NOPE + MLA_V)
    wk = jnp.pad(wkv3[:, :, :MLA_NOPE], ((0, 0), (0, 0), (0, LANES - MLA_NOPE)))
    wv = jnp.pad(wkv3[:, :, MLA_NOPE:], ((0, 0), (0, 0), (0, LANES - MLA_V)))
    wkv = jnp.concatenate([wk.reshape(MLA_KV_LORA, MLA_HEADS * LANES),
                           wv.reshape(MLA_KV_LORA, MLA_HEADS * LANES)], axis=1).astype(BF16)
    vone = jnp.tile(jnp.zeros((LANES,), F32).at[MLA_V].set(1.0), MLA_HEADS)[None, :]
    mqn = jnp.tile(jnp.pad(mla_qn, (0, padq)), MLA_HEADS)[None, :]
    mkn = jnp.tile(jnp.pad(mla_kn, (0, padq)), MLA_HEADS)[None, :]
    half = MLA_ROPE // 2
    inv = ROPE_THETA ** (-jnp.arange(half, dtype=F32) * 2.0 / MLA_ROPE)
    freq = jnp.zeros((1, LANES), F32).at[0, MLA_NOPE:MLA_NOPE + half].set(inv)
    freq = freq.at[0, MLA_NOPE + half:MLA_QK].set(inv)
    s1 = jnp.zeros((1, LANES), F32).at[0, MLA_NOPE:MLA_NOPE + half].set(-1.0)
    s2 = jnp.zeros((1, LANES), F32).at[0, MLA_NOPE + half:MLA_QK].set(1.0)
    tri = jnp.asarray(np.tril(np.ones((tm, tm), np.float32)))
    pos3 = positions.reshape(b, s, 1)

    row = lambda n: pl.BlockSpec((1, tm, n), lambda bi, ti: (bi, ti, 0))
    consts = [norm_mix[None, :], win, ind, fqn, fkn, bf, q_a_norm[None, :], wq, kv_a_norm[None, :], wkv,
              mqn, mkn, freq, s1, s2, tri, vone]
    nv = FOX_HEADS * LANES
    out_shape = [jax.ShapeDtypeStruct((b, s, nf), BF16)] * 2 + [jax.ShapeDtypeStruct((b, s, nv), BF16)] + [
        jax.ShapeDtypeStruct((b, FOX_HEADS, s), F32),
        jax.ShapeDtypeStruct((b, s, MLA_HEADS * LANES), BF16),
        jax.ShapeDtypeStruct((b, s, MLA_HEADS * LANES), BF16),
        jax.ShapeDtypeStruct((b, s, MLA_HEADS * LANES), BF16)]
    return pl.pallas_call(
        _even_pre_kernel,
        grid=(b, s // tm),
        in_specs=[row(d), row(1)] + [_full(c.shape) for c in consts],
        out_specs=[row(nf), row(nf), row(nv), pl.BlockSpec((1, FOX_HEADS, tm), lambda bi, ti: (bi, 0, ti)),
                   row(MLA_HEADS * LANES), row(MLA_HEADS * LANES), row(MLA_HEADS * LANES)],
        out_shape=out_shape,
        scratch_shapes=[pltpu.VMEM((1, LANES), F32)],
        compiler_params=_params(("arbitrary", "arbitrary")),
        name="even_pre",
    )(h, pos3, *consts)


def _attn_kernel(*refs, tq, tk, fox):
    if fox:
        q_ref, k_ref, v_ref, cr_ref, o_ref = refs
    else:
        q_ref, k_ref, v_ref, o_ref = refs
    hp = pl.program_id(1)
    i = pl.program_id(2)
    lane = lax.broadcasted_iota(jnp.int32, (tq, LANES), 1)
    qs = []
    for hh in range(2):
        if fox:
            in_head = (lane >= FOX_HEAD_DIM * hh) & (lane < FOX_HEAD_DIM * (hh + 1))
            qs.append(jnp.where(in_head, q_ref[0], jnp.zeros((), BF16)))
        else:
            qs.append(q_ref[0, :, LANES * hh:LANES * (hh + 1)])

    def step(j, carry, lo=None):
        koff = pl.multiple_of(j * tk, tk)
        top = 0 if lo is None else lo
        new = []
        for hh in range(2):
            m, acc = carry[hh]
            if fox:
                kj = k_ref[0, pl.ds(koff, tk), :]
            else:
                kj = k_ref[0, pl.ds(koff, tk), LANES * hh:LANES * (hh + 1)]
            sc = lax.dot_general(qs[hh][top:], kj, (((1,), (1,)), ((), ())), preferred_element_type=F32)
            if fox:
                sc = sc - cr_ref[0, pl.ds(2 * hp + hh, 1), pl.ds(koff, tk)]
            if lo is not None:
                rowi = lax.broadcasted_iota(jnp.int32, sc.shape, 0)
                coli = lax.broadcasted_iota(jnp.int32, sc.shape, 1)
                sc = jnp.where(coli <= rowi, sc, -jnp.inf)
            m_new = jnp.maximum(m[top:], jnp.max(sc, axis=-1, keepdims=True))
            alpha = jnp.exp2(m[top:] - m_new)
            p = jnp.exp2((sc - jnp.concatenate([m_new] * (tk // LANES), axis=1)).astype(BF16))
            vj = v_ref[0, pl.ds(koff, tk), LANES * hh:LANES * (hh + 1)]
            acc_new = alpha * acc[top:] + jnp.dot(p, vj, preferred_element_type=F32)
            if top:
                m_new = jnp.concatenate([m[:top], m_new], axis=0)
                acc_new = jnp.concatenate([acc[:top], acc_new], axis=0)
            new.append((m_new, acc_new))
        return tuple(new)

    def body(jj, carry):
        for r in range(ratio):
            carry = step(jj * ratio + r, carry)
        return carry

    one = (jnp.full((tq, LANES), -jnp.inf, F32), jnp.zeros((tq, LANES), F32))
    ratio = tq // tk
    carry = lax.fori_loop(0, i, body, (one, one))
    for r in range(ratio):
        carry = step(i * ratio + r, carry, lo=r * tk)
    outs = [acc / acc[:, MLA_V:MLA_V + 1] for _, acc in carry]
    o_ref[0] = jnp.where(lane < MLA_V, outs[0], pltpu.roll(outs[1], MLA_V, 1)).astype(o_ref.dtype)


def _attention(q, k, v, cum_row=None):
    b, s, _ = v.shape
    fox = cum_row is not None
    qw = LANES if fox else 2 * LANES
    tq = _tile(s, ATTN_TQ)
    tk = _tile(tq, ATTN_TK)
    npairs = v.shape[2] // (2 * LANES)
    in_specs = [pl.BlockSpec((1, tq, qw), lambda bi, hp, i: (bi, i, hp)),
                pl.BlockSpec((1, s, qw), lambda bi, hp, i: (bi, 0, hp)),
                pl.BlockSpec((1, s, 2 * LANES), lambda bi, hp, i: (bi, 0, hp))]
    args = [q, k, v]
    if fox:
        in_specs += [pl.BlockSpec((1, FOX_HEADS, s), lambda bi, hp, i: (bi, 0, 0))]
        args += [cum_row]
    return pl.pallas_call(
        functools.partial(_attn_kernel, tq=tq, tk=tk, fox=fox),
        grid=(b, npairs, s // tq),
        in_specs=in_specs,
        out_specs=pl.BlockSpec((1, tq, LANES), lambda bi, hp, i: (bi, i, hp)),
        out_shape=jax.ShapeDtypeStruct((b, s, npairs * LANES), BF16),
        compiler_params=_params(("arbitrary", "arbitrary", "arbitrary")),
        name="fox_attention" if fox else "mla_attention",
    )(*args)


def _odd_pre_kernel(h_ref, nmix_ref, win_ref, u_o, qkv_o, z_o, gb_o):
    a = _rms(h_ref[0], nmix_ref[...])
    proj = _bdot(a, win_ref[...])
    o1 = S5_CH
    o2 = o1 + 3 * GDN_W
    o3 = o2 + GDN_W
    u_o[0] = proj[:, :o1]
    qkv_o[0] = proj[:, o1:o2]
    z_o[0] = proj[:, o2:o3]
    gb_o[0] = proj[:, o3:]


def _odd_pre(h, norm_mix, w_in):
    b, s, d = h.shape
    tm = _tile(s, 256)
    sizes = (S5_CH, 3 * GDN_W, GDN_HEADS, GDN_HEADS, GDN_W)
    offs = np.concatenate([[0], np.cumsum(sizes)])
    parts = [w_in[:, offs[i]:offs[i + 1]] for i in range(len(sizes))]
    pad = jnp.zeros((d, LANES - 2 * GDN_HEADS), w_in.dtype)
    win = jnp.concatenate([parts[0], parts[1], parts[4], parts[2], parts[3], pad], axis=1).astype(BF16)
    row = lambda n: pl.BlockSpec((1, tm, n), lambda bi, ti: (bi, ti, 0))
    widths = (S5_CH, 3 * GDN_W, GDN_W, LANES)
    return pl.pallas_call(
        _odd_pre_kernel,
        grid=(b, s // tm),
        in_specs=[row(d), _full((1, d)), _full(win.shape)],
        out_specs=[row(n) for n in widths],
        out_shape=[jax.ShapeDtypeStruct((b, s, n), F32) for n in widths],
        compiler_params=_params(("arbitrary", "arbitrary")),
        name="odd_pre",
    )(h, norm_mix[None, :], win)


def _s5_kernel(u_ref, perm_ref, unperm_ref, bbd_ref, cbd_ref, ar_ref, ai_ref, asr_ref, asi_ref, pwr_ref, pwi_ref,
               d_ref, wglu_ref, bglu_ref, o_ref, x_ref, sr_ref, si_ref):
    t = pl.program_id(1)

    @pl.when(t == 0)
    def _():
        sr_ref[...] = jnp.zeros_like(sr_ref)
        si_ref[...] = jnp.zeros_like(si_ref)

    tm = u_ref.shape[1]
    nseg = 8
    seg = tm // nseg
    u = u_ref[0]
    u1 = u.astype(BF16)
    u2 = (u - u1.astype(F32)).astype(BF16)
    u3 = (u - u1.astype(F32) - u2.astype(F32)).astype(BF16)
    perm = perm_ref[...]
    u = (jnp.dot(perm, u1, preferred_element_type=F32) + jnp.dot(perm, u2, preferred_element_type=F32)
         + jnp.dot(perm, u3, preferred_element_type=F32))
    hc = S5_CH // 2
    hn = S5_N // 2
    ub = u.astype(BF16)
    for part in range(2):
        for base in (0, S5_N):
            cols = slice(base + part * hn, base + (part + 1) * hn)
            x_ref[:, cols] = jnp.dot(ub[:, part * hc:(part + 1) * hc], bbd_ref[part * hc:(part + 1) * hc, cols],
                                     preferred_element_type=F32)
    ar = ar_ref[...]
    ai = ai_ref[...]
    re = slice(0, S5_N)
    im = slice(S5_N, 2 * S5_N)

    def local(i, carry):
        xr, xi = carry
        rows = pl.ds(pl.multiple_of(i * nseg, nseg), nseg)
        nr = ar * xr - ai * xi + x_ref[rows, re]
        ni = ar * xi + ai * xr + x_ref[rows, im]
        x_ref[rows, re] = nr
        x_ref[rows, im] = ni
        return nr, ni

    zero = jnp.zeros((nseg, S5_N), F32)
    er, ei = lax.fori_loop(0, seg, local, (zero, zero), unroll=4)

    asr = asr_ref[...]
    asi = asi_ref[...]
    cr = [sr_ref[...]]
    ci = [si_ref[...]]
    for s in range(nseg):
        cr.append(asr * cr[s] - asi * ci[s] + er[s:s + 1, :])
        ci.append(asr * ci[s] + asi * cr[s] + ei[s:s + 1, :])
    sr_ref[...] = cr[nseg]
    si_ref[...] = ci[nseg]
    ent_r = jnp.concatenate(cr[:nseg], axis=0)
    ent_i = jnp.concatenate(ci[:nseg], axis=0)

    def fix(i, c):
        rows = pl.ds(pl.multiple_of(i * nseg, nseg), nseg)
        pr = pwr_ref[pl.ds(i, 1), :]
        pi = pwi_ref[pl.ds(i, 1), :]
        x_ref[rows, re] += pr * ent_r - pi * ent_i
        x_ref[rows, im] += pr * ent_i + pi * ent_r
        return c

    lax.fori_loop(0, seg, fix, 0, unroll=4)
    ys = []
    for part in range(2):
        oc = slice(part * hc, (part + 1) * hc)
        acc = None
        for base in (0, S5_N):
            rows = slice(base + part * hn, base + (part + 1) * hn)
            term = _bdot(x_ref[:, rows], cbd_ref[rows, oc])
            acc = term if acc is None else acc + term
        ys.append(acc)
    y = jnp.concatenate(ys, axis=1) + d_ref[...] * u
    hg = jax.nn.gelu(y)
    out = (hg * jax.nn.sigmoid(_bdot(hg, wglu_ref[...]) + bglu_ref[...])).astype(BF16)
    o_ref[0] = jnp.dot(unperm_ref[...], out, preferred_element_type=F32).astype(o_ref.dtype)


def _s5(u, a_re, a_im, b_re, b_im, c_re, c_im, d_skip, log_step, w_glu, b_glu):
    b, s, _ = u.shape
    tm = _tile(s, 256)
    lam_re = jnp.minimum(a_re, -1e-4)
    lam_im = a_im
    dt = jnp.exp(log_step)[:, None]
    mag = jnp.exp(lam_re * dt)
    ab_re = mag * jnp.cos(lam_im * dt)
    ab_im = mag * jnp.sin(lam_im * dt)
    den = lam_re * lam_re + lam_im * lam_im
    nr, ni = ab_re - 1.0, ab_im
    gam_re = (nr * lam_re + ni * lam_im) / den
    gam_im = (ni * lam_re - nr * lam_im) / den
    bb_re = gam_re[..., None] * b_re - gam_im[..., None] * b_im
    bb_im = gam_re[..., None] * b_im + gam_im[..., None] * b_re
    eye = jnp.eye(S5_GROUPS, dtype=F32)
    bd_in = lambda m: jnp.einsum('gpc,gh->gchp', m, eye).reshape(S5_CH, S5_N)
    bd_out = lambda m: jnp.einsum('gcp,gh->gphc', m, eye).reshape(S5_N, S5_CH)
    bbd = jnp.concatenate([bd_in(bb_re), bd_in(bb_im)], axis=1).astype(BF16)
    cbd = jnp.concatenate([bd_out(c_re), -bd_out(c_im)], axis=0).astype(BF16)
    seg = tm // 8
    steps = jnp.arange(1, seg + 1, dtype=F32)[:, None, None] * dt[None]
    pmag = jnp.exp(lam_re[None] * steps)
    pw_re = (pmag * jnp.cos(lam_im[None] * steps)).reshape(seg, S5_N)
    pw_im = (pmag * jnp.sin(lam_im[None] * steps)).reshape(seg, S5_N)
    src = (np.arange(tm) % 8) * seg + np.arange(tm) // 8
    perm = np.zeros((tm, tm), np.float32)
    perm[np.arange(tm), src] = 1.0
    consts = [jnp.asarray(perm, BF16), jnp.asarray(perm.T, BF16),
              bbd, cbd, ab_re.reshape(1, S5_N), ab_im.reshape(1, S5_N), pw_re[seg - 1:seg], pw_im[seg - 1:seg],
              pw_re, pw_im, d_skip[None, :], w_glu.astype(BF16), b_glu[None, :]]
    row = pl.BlockSpec((1, tm, S5_CH), lambda bi, ti: (bi, ti, 0))
    return pl.pallas_call(
        _s5_kernel,
        grid=(b, s // tm),
        in_specs=[row] + [_full(c.shape) for c in consts],
        out_specs=row,
        out_shape=jax.ShapeDtypeStruct((b, s, S5_CH), BF16),
        scratch_shapes=[pltpu.VMEM((tm, 2 * S5_N), F32), pltpu.VMEM((1, S5_N), F32),
                        pltpu.VMEM((1, S5_N), F32)],
        compiler_params=_params(("arbitrary", "arbitrary")),
        name="s5",
    )(u, *consts)


def _gdn_kernel(x_ref, z_ref, gb_ref, cw_ref, nega_ref, dtb_ref, onorm_ref, tril_ref, triu_ref, o_ref,
                xpad_ref, state_ref):
    t = pl.program_id(1)
    tm = x_ref.shape[1]
    c = GDN_CHUNK
    hd = GDN_HEAD_DIM

    @pl.when(t == 0)
    def _():
        xpad_ref[0:8, :] = jnp.zeros((8, xpad_ref.shape[1]), F32)
        state_ref[...] = jnp.zeros_like(state_ref)

    @pl.when(t > 0)
    def _():
        xpad_ref[0:8, :] = xpad_ref[tm:tm + 8, :]

    xpad_ref[8:tm + 8, :] = x_ref[0]
    conv = cw_ref[0:1, :] * xpad_ref[pl.ds(8 - (GDN_CONV - 1), tm), :]
    for i in range(1, GDN_CONV):
        conv = conv + cw_ref[i:i + 1, :] * xpad_ref[pl.ds(8 - (GDN_CONV - 1) + i, tm), :]
    act = _silu(conv)

    def l2n(x):
        return x * lax.rsqrt(jnp.sum(x * x, axis=-1, keepdims=True) + RMS_EPS)

    gb = gb_ref[0]
    g = nega_ref[...] * _softplus(gb + dtb_ref[...])
    beta = jax.nn.sigmoid(gb)
    gc = _hdot(tril_ref[...], g)
    gct = _hdot(g.T, triu_ref[...])

    ri = lax.broadcasted_iota(jnp.int32, (tm, tm), 0)
    ci = lax.broadcasted_iota(jnp.int32, (tm, tm), 1)
    same = (ri // c) == (ci // c)
    incl = same & (ri >= ci)
    strict = same & (ri > ci)
    eye = (ri == ci).astype(F32)
    offs = []
    bs = 1
    while bs < c:
        offs.append(((ri // (2 * bs)) == (ci // (2 * bs))) & ((ri % (2 * bs)) >= bs) & ((ci % (2 * bs)) < bs))
        bs *= 2
    z = z_ref[0]
    nchunks = tm // c

    heads = range(GDN_HEADS)
    q = [l2n(act[:, hh * hd:(hh + 1) * hd]) * (hd ** -0.5) for hh in heads]
    k = [l2n(act[:, GDN_W + hh * hd:GDN_W + (hh + 1) * hd]) for hh in heads]
    v = [act[:, 2 * GDN_W + hh * hd:2 * GDN_W + (hh + 1) * hd] for hh in heads]
    bcol = [beta[:, GDN_HEADS + hh:GDN_HEADS + hh + 1] for hh in heads]
    gcol = [gc[:, hh:hh + 1] for hh in heads]
    decay = [jnp.where(incl, jnp.exp(jnp.where(incl, gcol[hh] - gct[hh:hh + 1, :], 0.0)), 0.0) for hh in heads]
    kb = [k[hh] * bcol[hh] for hh in heads]
    a_mat = [jnp.where(strict, _bdot_nt(kb[hh], k[hh]) * decay[hh], 0.0) for hh in heads]
    t_mat = [eye for _ in heads]
    for off in offs:
        pa = [_bdot(t_mat[hh], jnp.where(off, a_mat[hh], 0.0)) for hh in heads]
        t_mat = [t_mat[hh] - _bdot(pa[hh], t_mat[hh]) for hh in heads]
    th = [t_mat[hh].astype(BF16) for hh in heads]
    tl = [(t_mat[hh] - th[hh].astype(F32)).astype(BF16) for hh in heads]
    ah = [a_mat[hh].astype(BF16) for hh in heads]
    al = [(a_mat[hh] - ah[hh].astype(F32)).astype(BF16) for hh in heads]
    a_t = [jnp.dot(ah[hh], th[hh], preferred_element_type=F32) + jnp.dot(ah[hh], tl[hh], preferred_element_type=F32)
           + jnp.dot(al[hh], th[hh], preferred_element_type=F32) for hh in heads]
    t_mat = [t_mat[hh] + jnp.dot(th[hh], (eye - t_mat[hh] - a_t[hh]).astype(BF16), preferred_element_type=F32)
             for hh in heads]
    eg = [jnp.exp(gcol[hh]) for hh in heads]
    u = [_bdot(t_mat[hh], v[hh] * bcol[hh]) for hh in heads]
    w = [_bdot(t_mat[hh], kb[hh] * eg[hh]) for hh in heads]
    intra = [jnp.where(incl, _bdot_nt(q[hh], k[hh]) * decay[hh], 0.0).astype(BF16) for hh in heads]
    qd = [q[hh] * eg[hh] for hh in heads]
    state = [state_ref[hh] for hh in heads]
    for n in range(nchunks):
        r0 = n * c
        for hh in heads:
            lo = hh * hd
            gcn = gcol[hh][r0:r0 + c, :]
            glast = gcol[hh][r0 + c - 1:r0 + c, :]
            v_new = u[hh][r0:r0 + c, :] - _bdot(w[hh][r0:r0 + c, :], state[hh])
            v_rep = jnp.concatenate([v_new.astype(BF16)] * nchunks, axis=0)
            o = _bdot(qd[hh][r0:r0 + c, :], state[hh]) + jnp.dot(intra[hh][r0:r0 + c, :], v_rep,
                                                                  preferred_element_type=F32)
            state[hh] = state[hh] * jnp.exp(glast) + _bdot_tn(k[hh][r0:r0 + c, :] * jnp.exp(glast - gcn), v_new)
            on = o * lax.rsqrt(jnp.mean(o * o, axis=-1, keepdims=True) + RMS_EPS) * onorm_ref[...]
            o_ref[0, r0:r0 + c, lo:lo + hd] = (on * _silu(z[r0:r0 + c, lo:lo + hd])).astype(o_ref.dtype)
    for hh in heads:
        state_ref[hh] = state[hh]


def _gdn(qkv, z, gb, conv_w, a_log, dt_bias, o_norm):
    b, s, cw = qkv.shape
    tm = _tile(s, 256)
    nega = jnp.zeros((1, LANES), F32).at[0, :GDN_HEADS].set(-jnp.exp(a_log))
    dtb = jnp.zeros((1, LANES), F32).at[0, :GDN_HEADS].set(dt_bias)
    cwp = jnp.pad(conv_w, ((0, 8 - GDN_CONV), (0, 0)))
    ridx = np.arange(tm)
    same = (ridx[:, None] // GDN_CHUNK) == (ridx[None, :] // GDN_CHUNK)
    tril = jnp.asarray((same & (ridx[:, None] >= ridx[None, :])).astype(np.float32))
    triu = jnp.asarray((same & (ridx[:, None] <= ridx[None, :])).astype(np.float32))
    consts = [cwp, nega, dtb, o_norm[None, :], tril, triu]
    row = lambda n: pl.BlockSpec((1, tm, n), lambda bi, ti: (bi, ti, 0))
    return pl.pallas_call(
        _gdn_kernel,
        grid=(b, s // tm),
        in_specs=[row(cw), row(GDN_W), row(LANES)] + [_full(c.shape) for c in consts],
        out_specs=row(GDN_W),
        out_shape=jax.ShapeDtypeStruct((b, s, GDN_W), BF16),
        scratch_shapes=[pltpu.VMEM((tm + 8, cw), F32), pltpu.VMEM((GDN_HEADS, GDN_HEAD_DIM, GDN_HEAD_DIM), F32)],
        compiler_params=_params(("arbitrary", "arbitrary")),
        name="gdn",
    )(qkv, z, gb, *consts)


def _router_kernel(a_ref, b_ref, wa_ref, wb_ref, h_ref, g_ref, wrh_ref, wrl_ref, br_ref, tri_ref,
                   h_o, xs_o, keyt_o, wt_o, cnt_o, sel_ref):
    tm = h_ref.shape[0]
    h = (h_ref[...] + jnp.dot(a_ref[...], wa_ref[...], preferred_element_type=F32)
         + jnp.dot(b_ref[...], wb_ref[...], preferred_element_type=F32))
    h_o[...] = h
    m = _rms(h, g_ref[...])
    mh = m.astype(BF16)
    ml = (m - mh.astype(F32)).astype(BF16)
    logits = (jnp.dot(mh, wrh_ref[...], preferred_element_type=F32) + jnp.dot(mh, wrl_ref[...], preferred_element_type=F32)
              + jnp.dot(ml, wrh_ref[...], preferred_element_type=F32)) + br_ref[...]
    lane = lax.broadcasted_iota(jnp.int32, (tm, LANES), 1)
    neg = -jnp.inf

    def first_argmax(x):
        mx = jnp.max(x, axis=-1, keepdims=True)
        idx = jnp.min(jnp.where(x == mx, lane, LANES), axis=-1, keepdims=True)
        return mx, idx

    is_g = (lane >= N_EXPERTS) & (lane < N_EXPERTS + MOE_GROUPS)
    gl = jnp.where(is_g, logits, neg)
    gmax, gidx = first_argmax(gl)
    g_w = 1.0 / jnp.sum(jnp.where(is_g, jnp.exp(gl - gmax), 0.0), axis=-1, keepdims=True)
    in_group = (lane // MOE_PER_GROUP) == (gidx - N_EXPERTS)
    el = jnp.where(in_group & (lane < N_EXPERTS), logits, neg)
    m1, i1 = first_argmax(el)
    el2 = jnp.where(lane == i1, neg, el)
    m2, i2 = first_argmax(el2)
    r = jnp.exp(m2 - m1)
    w1 = g_w / (1.0 + r)
    w2 = g_w * r / (1.0 + r)
    chose = (lane == i1) | (lane == i2)
    wmat = jnp.where(lane == i1, w1, jnp.where(lane == i2, w2, 0.0))
    ch = chose.astype(F32)
    rank = jnp.dot(tri_ref[...], ch.astype(BF16), preferred_element_type=F32)
    keyt = jnp.where(chose, rank, -1.0).T
    keyt_o[0] = keyt
    wt_o[0] = wmat.T
    cnt_o[0] = jnp.sum(ch, axis=0, keepdims=True).astype(jnp.int32)
    riota = lax.broadcasted_iota(jnp.int32, (MOE_CAP, tm), 0).astype(F32)
    for e in range(N_EXPERTS):
        sel_ref[e * MOE_CAP:(e + 1) * MOE_CAP, :] = jnp.where(keyt[e:e + 1, :] == riota, 1.0, 0.0).astype(BF16)
    xg = jnp.dot(sel_ref[...], mh, preferred_element_type=F32)
    xs_o[...] = xg.astype(BF16).reshape(xs_o.shape)


def _moe_router(a, bb, w_out, hf, norm_g, w_group, b_group, w_expert, b_expert, tb):
    n, d = hf.shape
    nblk = n // tb
    na, nb = a.shape[1], bb.shape[1]
    wa = w_out[:na].astype(BF16)
    wb = w_out[na:].astype(BF16)
    tok = lambda w: pl.BlockSpec((tb, w), lambda i: (i, 0))
    wr = jnp.zeros((d, LANES), F32).at[:, :N_EXPERTS].set(w_expert)
    wr = wr.at[:, N_EXPERTS:N_EXPERTS + MOE_GROUPS].set(w_group)
    wrh = wr.astype(BF16)
    wrl = (wr - wrh.astype(F32)).astype(BF16)
    br = jnp.zeros((1, LANES), F32).at[0, :N_EXPERTS].set(b_expert)
    br = br.at[0, N_EXPERTS:N_EXPERTS + MOE_GROUPS].set(b_group)
    tri = jnp.asarray(np.tril(np.ones((tb, tb), np.float32), -1), BF16)
    blk = pl.BlockSpec((1, LANES, tb), lambda i: (i, 0, 0))
    return pl.pallas_call(
        _router_kernel,
        grid=(nblk,),
        in_specs=[tok(na), tok(nb), _full(wa.shape), _full(wb.shape), tok(d), _full((1, d)), _full(wr.shape),
                  _full(wr.shape), _full(br.shape), _full(tri.shape)],
        out_specs=[tok(d), pl.BlockSpec((N_EXPERTS, MOE_CAP, d), lambda i: (0, i, 0)), blk, blk,
                   pl.BlockSpec((1, 1, LANES), lambda i: (i, 0, 0))],
        out_shape=[jax.ShapeDtypeStruct((n, d), F32),
                   jax.ShapeDtypeStruct((N_EXPERTS, nblk * MOE_CAP, d), BF16),
                   jax.ShapeDtypeStruct((nblk, LANES, tb), F32),
                   jax.ShapeDtypeStruct((nblk, LANES, tb), F32),
                   jax.ShapeDtypeStruct((nblk, 1, LANES), jnp.int32)],
        scratch_shapes=[pltpu.VMEM((N_EXPERTS * MOE_CAP, tb), BF16)],
        compiler_params=_params(("arbitrary",)),
        name="moe_router",
    )(a, bb, wa, wb, hf, norm_g[None, :], wrh, wrl, br, tri)


def _expert_mlp_kernel(x_ref, wg_ref, wu_ref, wd_ref, y_ref, wg_sc, wu_sc, wd_sc):
    @pl.when(pl.program_id(1) == 0)
    def _():
        wg_sc[...] = wg_ref[0, 0].astype(BF16)
        wu_sc[...] = wu_ref[0, 0].astype(BF16)
        wd_sc[...] = wd_ref[0, 0].astype(BF16)

    x = x_ref[0]
    hid = _silu(jnp.dot(x, wg_sc[...], preferred_element_type=F32)) * jnp.dot(
        x, wu_sc[...], preferred_element_type=F32)
    y_ref[0] = jnp.dot(hid.astype(BF16), wd_sc[...], preferred_element_type=F32).astype(BF16)


def _expert_mlp(xs, w_gate, w_up, w_down, layer):
    ne, rows, d = xs.shape
    ff = w_gate.shape[3]
    tr = _tile(rows, 1024)
    return pl.pallas_call(
        _expert_mlp_kernel,
        grid=(ne, rows // tr),
        in_specs=[pl.BlockSpec((1, tr, d), lambda e, i: (e, i, 0)),
                  pl.BlockSpec((1, 1, d, ff), lambda e, i: (layer, e, 0, 0)),
                  pl.BlockSpec((1, 1, d, ff), lambda e, i: (layer, e, 0, 0)),
                  pl.BlockSpec((1, 1, ff, d), lambda e, i: (layer, e, 0, 0))],
        out_specs=pl.BlockSpec((1, tr, d), lambda e, i: (e, i, 0)),
        out_shape=jax.ShapeDtypeStruct((ne, rows, d), BF16),
        scratch_shapes=[pltpu.VMEM((d, ff), BF16), pltpu.VMEM((d, ff), BF16), pltpu.VMEM((ff, d), BF16)],
        compiler_params=_params(("arbitrary", "arbitrary")),
        name="moe_expert_mlp",
    )(xs, w_gate, w_up, w_down)


def _combine_rows(y_ref, keyt_ref, wt_ref, h_ref, sel_ref):
    tb = h_ref.shape[0]
    riota = lax.broadcasted_iota(jnp.int32, (MOE_CAP, tb), 0).astype(F32)
    for e in range(N_EXPERTS):
        hit = keyt_ref[0, e:e + 1, :] == riota
        sel_ref[e * MOE_CAP:(e + 1) * MOE_CAP, :] = jnp.where(hit, wt_ref[0, e:e + 1, :], 0.0).astype(BF16)
    y = y_ref[...].reshape(N_EXPERTS * MOE_CAP, y_ref.shape[2])
    return h_ref[...] + lax.dot_general(sel_ref[...], y, (((0,), (0,)), ((), ())), preferred_element_type=F32)


def _ple_rows(h, p, g_ref, wg_ref, bg_ref, wp_ref):
    gate = jax.nn.sigmoid(_bdot(_rms(h, g_ref[...]), wg_ref[...]) + bg_ref[...])
    return h + gate * _bdot(p, wp_ref[...])


def _combine_kernel(y_ref, keyt_ref, wt_ref, h_ref, o_ref, sel_ref):
    o_ref[...] = _combine_rows(y_ref, keyt_ref, wt_ref, h_ref, sel_ref)


def _combine_ple_kernel(y_ref, keyt_ref, wt_ref, h_ref, p_ref, g_ref, wg_ref, bg_ref, wp_ref, o_ref, sel_ref):
    hm = _combine_rows(y_ref, keyt_ref, wt_ref, h_ref, sel_ref)
    o_ref[...] = _ple_rows(hm, p_ref[0], g_ref, wg_ref, bg_ref, wp_ref)


def _moe_combine(ys, keyt, wt, hf, tb, ple=None):
    n, d = hf.shape
    nblk = n // tb
    blk = pl.BlockSpec((1, LANES, tb), lambda i: (i, 0, 0))
    in_specs = [pl.BlockSpec((N_EXPERTS, MOE_CAP, d), lambda i: (0, i, 0)), blk, blk,
                pl.BlockSpec((tb, d), lambda i: (i, 0))]
    args = [ys, keyt, wt, hf]
    kern = _combine_kernel
    if ple is not None:
        p_all, layer, norm_g, w_gate, b_gate, w_proj = ple
        pd = p_all.shape[-1]
        in_specs += [pl.BlockSpec((1, tb, pd), lambda i: (layer, i, 0)), _full((1, d)), _full((d, d)),
                     _full((1, d)), _full((pd, d))]
        args += [p_all.reshape(p_all.shape[0], n, pd), norm_g[None, :], w_gate.astype(BF16), b_gate[None, :],
                 w_proj.astype(BF16)]
        kern = _combine_ple_kernel
    return pl.pallas_call(
        kern,
        grid=(nblk,),
        in_specs=in_specs,
        out_specs=pl.BlockSpec((tb, d), lambda i: (i, 0)),
        out_shape=jax.ShapeDtypeStruct((n, d), F32),
        scratch_shapes=[pltpu.VMEM((N_EXPERTS * MOE_CAP, tb), BF16)],
        compiler_params=_params(("arbitrary",)),
        name="moe_combine" if ple is None else "moe_combine_ple",
    )(*args)


def _overflow_kernel(cnt_ref, h_ref, g_ref, keyt_ref, wt_ref, base_ref, wg_ref, wu_ref, wd_ref, o_ref,
                     acc_ref, m_ref):
    blk = pl.program_id(0)
    e = pl.program_id(1)
    tb = h_ref.shape[0]

    @pl.when(e == 0)
    def _():
        acc_ref[...] = base_ref[...]
        m_ref[...] = _rms(h_ref[...], g_ref[...]).astype(BF16)

    extra = jnp.maximum(cnt_ref[blk * LANES + e] - MOE_CAP, 0)
    krow = keyt_ref[0, pl.ds(e, 1), :]
    wrow = wt_ref[0, pl.ds(e, 1), :]
    riota = lax.broadcasted_iota(jnp.int32, (MOE_ROWS, tb), 0).astype(F32)

    def chunk(ci, carry):
        hit = krow == (riota + (MOE_CAP + ci * MOE_ROWS).astype(F32))
        sel = jnp.where(hit, 1.0, 0.0).astype(BF16)
        xg = jnp.dot(sel, m_ref[...], preferred_element_type=F32).astype(BF16)
        hid = _silu(_bdot(xg, wg_ref[0, 0])) * _bdot(xg, wu_ref[0, 0])
        y = _bdot(hid, wd_ref[0, 0]).astype(BF16)
        acc_ref[...] += _bdot_tn(jnp.where(hit, wrow, 0.0), y)
        return carry

    lax.fori_loop(0, (extra + MOE_ROWS - 1) // MOE_ROWS, chunk, 0)

    @pl.when(e == N_EXPERTS - 1)
    def _():
        o_ref[...] = acc_ref[...]


def _moe_overflow(cnt, hf, norm_g, keyt, wt, base, w_gate, w_up, w_down, layer, tb):
    n, d = hf.shape
    ff = w_gate.shape[3]
    nblk = n // tb
    tok = pl.BlockSpec((tb, d), lambda i, e, c: (i, 0))
    blk = pl.BlockSpec((1, LANES, tb), lambda i, e, c: (i, 0, 0))
    grid_spec = pltpu.PrefetchScalarGridSpec(
        num_scalar_prefetch=1,
        grid=(nblk, N_EXPERTS),
        in_specs=[tok, pl.BlockSpec((1, d), lambda i, e, c: (0, 0)), blk, blk, tok,
                  pl.BlockSpec((1, 1, d, ff), lambda i, e, c: (layer, e, 0, 0)),
                  pl.BlockSpec((1, 1, d, ff), lambda i, e, c: (layer, e, 0, 0)),
                  pl.BlockSpec((1, 1, ff, d), lambda i, e, c: (layer, e, 0, 0))],
        out_specs=tok,
        scratch_shapes=[pltpu.VMEM((tb, d), F32), pltpu.VMEM((tb, d), BF16)],
    )
    return pl.pallas_call(
        _overflow_kernel,
        grid_spec=grid_spec,
        out_shape=jax.ShapeDtypeStruct((n, d), F32),
        input_output_aliases={5: 0},
        compiler_params=_params(("arbitrary", "arbitrary")),
        name="moe_overflow",
    )(cnt, hf, norm_g[None, :], keyt, wt, base, w_gate, w_up, w_down)


def _proj_moe_ple(mix_a, mix_b, w_out, h, norm_g, w_group, b_group, w_expert, b_expert, w_gate, w_up, w_down,
                  layer, p_all, ple_w_proj, ple_norm, ple_w_gate, ple_b_gate):
    b, s, d = h.shape
    n = b * s
    tb = _tile(n, MOE_TB)
    hf, xs, keyt, wt, cnt = _moe_router(mix_a.reshape(n, -1), mix_b.reshape(n, -1), w_out, h.reshape(n, d),
                                        norm_g, w_group, b_group, w_expert, b_expert, tb)
    ys = _expert_mlp(xs, w_gate, w_up, w_down, layer)

    def fast():
        ple = (p_all, layer, ple_norm, ple_w_gate, ple_b_gate, ple_w_proj)
        return _moe_combine(ys, keyt, wt, hf, tb, ple).reshape(b, s, d)

    def with_overflow():
        out = _moe_combine(ys, keyt, wt, hf, tb)
        out = _moe_overflow(cnt.reshape(-1), hf, norm_g, keyt, wt, out, w_gate, w_up, w_down, layer, tb)
        return _ple_residual(out.reshape(b, s, d), p_all, layer, ple_w_proj, ple_norm, ple_w_gate, ple_b_gate)

    return lax.cond(jnp.max(cnt[:, 0, :N_EXPERTS]) > MOE_CAP, with_overflow, fast)


def _ple_kernel(h_ref, p_ref, g_ref, wg_ref, bg_ref, wp_ref, o_ref):
    o_ref[0] = _ple_rows(h_ref[0], p_ref[0, 0], g_ref, wg_ref, bg_ref, wp_ref)


def _ple_residual(h, p_all, layer, w_proj, norm_g, w_gate, b_gate):
    b, s, d = h.shape
    pd = p_all.shape[3]
    tm = _tile(s, 512)
    row = lambda n: pl.BlockSpec((1, tm, n), lambda bi, ti: (bi, ti, 0))
    return pl.pallas_call(
        _ple_kernel,
        grid=(b, s // tm),
        in_specs=[row(d), pl.BlockSpec((1, 1, tm, pd), lambda bi, ti: (layer, bi, ti, 0)), _full((1, d)),
                  _full((d, d)), _full((1, d)), _full((pd, d))],
        out_specs=row(d),
        out_shape=jax.ShapeDtypeStruct((b, s, d), F32),
        compiler_params=_params(("arbitrary", "arbitrary")),
        name="ple",
    )(h, p_all, norm_g[None, :], w_gate.astype(BF16), b_gate[None, :], w_proj.astype(BF16))


def _even_mixers(h, positions, norm_mix, w_in, b_f, fox_qn, fox_kn, q_a_norm, w_q_up, kv_a_norm, w_kv_up,
                 mla_qn, mla_kn):
    fq, fk, fv, cum, mq, mk, mv = _even_pre(h, positions, norm_mix, w_in, b_f, fox_qn, fox_kn, q_a_norm,
                                            w_q_up, kv_a_norm, w_kv_up, mla_qn, mla_kn)
    return _attention(fq, fk, fv, cum), _attention(mq, mk, mv)


def _odd_mixers(h, norm_mix, w_in, a_re, a_im, b_re, b_im, c_re, c_im, d_skip, log_step, w_glu, b_glu,
                conv_w, a_log, dt_bias, o_norm):
    u, qkv, z, gb = _odd_pre(h, norm_mix, w_in)
    y_ssm = _s5(u, a_re, a_im, b_re, b_im, c_re, c_im, d_skip, log_step, w_glu, b_glu)
    return y_ssm, _gdn(qkv, z, gb, conv_w, a_log, dt_bias, o_norm)


def kernel(x, p, positions, norm_mix, norm_ffn, ev_w_in, fox_b_f, fox_q_norm, fox_k_norm, mla_q_a_norm, mla_w_q_up, mla_kv_a_norm, mla_w_kv_up, mla_q_norm, mla_k_norm, ev_w_out, od_w_in, s5_a_re, s5_a_im, s5_b_re, s5_b_im, s5_c_re, s5_c_im, s5_d, s5_log_step, s5_w_glu, s5_b_glu, gdn_conv_w, gdn_a_log, gdn_dt_bias, gdn_o_norm, od_w_out, moe_w_group, moe_b_group, moe_w_expert, moe_b_expert, moe_w_gate, moe_w_up, moe_w_down, ple_w_proj, ple_norm, ple_w_gate, ple_b_gate):
    h = x
    depth = p.shape[0]
    for i in range(depth):
        j = i // 2
        if i % 2 == 0:
            mix = _even_mixers(h, positions, norm_mix[i], ev_w_in[j], fox_b_f[j], fox_q_norm[j], fox_k_norm[j],
                               mla_q_a_norm[j], mla_w_q_up[j], mla_kv_a_norm[j], mla_w_kv_up[j], mla_q_norm[j],
                               mla_k_norm[j])
            w_out = ev_w_out[j]
        else:
            mix = _odd_mixers(h, norm_mix[i], od_w_in[j], s5_a_re[j], s5_a_im[j], s5_b_re[j], s5_b_im[j],
                              s5_c_re[j], s5_c_im[j], s5_d[j], s5_log_step[j], s5_w_glu[j], s5_b_glu[j],
                              gdn_conv_w[j], gdn_a_log[j], gdn_dt_bias[j], gdn_o_norm[j])
            w_out = od_w_out[j]
        h = _proj_moe_ple(mix[0], mix[1], w_out, h, norm_ffn[i], moe_w_group[i], moe_b_group[i], moe_w_expert[i],
                          moe_b_expert[i], moe_w_gate, moe_w_up, moe_w_down, i, p, ple_w_proj[i], ple_norm[i],
                          ple_w_gate[i], ple_b_gate[i])
    return h
```

```python
import functools
import math

import numpy as np
import jax
import jax.numpy as jnp
from jax import lax
from jax.experimental import pallas as pl
from jax.experimental.pallas import tpu as pltpu

F32 = jnp.float32
BF16 = jnp.bfloat16
HI = lax.Precision.HIGHEST

LANES = 128
RMS_EPS = 1e-6
ROPE_THETA = 10000.0
LOG2E = math.log2(math.e)

FOX_HEADS = 8
FOX_HEAD_DIM = 64
MLA_HEADS = 8
MLA_Q_LORA = 384
MLA_KV_LORA = 256
MLA_NOPE = 64
MLA_ROPE = 32
MLA_V = 64
MLA_QK = MLA_NOPE + MLA_ROPE

S5_CH = 512
S5_GROUP_CH = 16
S5_GROUPS = S5_CH // S5_GROUP_CH
S5_STATE = 64
S5_N = S5_GROUPS * S5_STATE

GDN_HEADS = 4
GDN_HEAD_DIM = 128
GDN_W = GDN_HEADS * GDN_HEAD_DIM
GDN_CONV = 4
GDN_CHUNK = 64

MOE_GROUPS = 4
MOE_PER_GROUP = 8
N_EXPERTS = MOE_GROUPS * MOE_PER_GROUP
MOE_TB = 512
MOE_CAP = 48
MOE_ROWS = 128
ATTN_TQ = 2048
ATTN_TK = 512

VMEM_LIMIT = 56 * 1024 * 1024


def _tile(n, pref):
    t = min(n, pref)
    assert n % t == 0, (n, t)
    return t


def _params(sem):
    return pltpu.CompilerParams(dimension_semantics=sem, vmem_limit_bytes=VMEM_LIMIT)


def _full(shape):
    nd = len(shape)
    return pl.BlockSpec(shape, lambda *_: (0,) * nd)


def _rms(x, g):
    return x * lax.rsqrt(jnp.mean(x * x, axis=-1, keepdims=True) + RMS_EPS) * g


def _bdot(a, b):
    return jnp.dot(a.astype(BF16), b.astype(BF16), preferred_element_type=F32)


def _bdot_nt(a, b):
    return lax.dot_general(a.astype(BF16), b.astype(BF16), (((1,), (1,)), ((), ())),
                           preferred_element_type=F32)


def _bdot_tn(a, b):
    return lax.dot_general(a.astype(BF16), b.astype(BF16), (((0,), (0,)), ((), ())),
                           preferred_element_type=F32)


def _hdot(a, b):
    return jnp.dot(a, b, precision=HI, preferred_element_type=F32)


def _split_dot(x, ind):
    hi = x.astype(BF16)
    lo = (x - hi.astype(F32)).astype(BF16)
    return (jnp.dot(hi, ind, preferred_element_type=F32)
            + jnp.dot(lo, ind, preferred_element_type=F32))


def _log_sigmoid(x):
    return jnp.minimum(x, 0.0) - jnp.log(1.0 + jnp.exp(-jnp.abs(x)))


def _softplus(x):
    return jnp.maximum(x, 0.0) + jnp.log(1.0 + jnp.exp(-jnp.abs(x)))


def _silu(x):
    return x * jax.nn.sigmoid(x)


def _head_norm128(x, nheads, denom, gain):
    outs = []
    for hh in range(nheads):
        xh = x[:, LANES * hh:LANES * (hh + 1)]
        ss = jnp.sum(xh * xh, axis=-1, keepdims=True)
        outs.append(xh * lax.rsqrt(ss / denom + RMS_EPS))
    return jnp.concatenate(outs, axis=1) * gain


def _even_pre_kernel(h_ref, pos_ref, nmix_ref, win_ref, ind_ref, fqn_ref, fkn_ref, bf_ref,
                     qan_ref, wq_ref, kvan_ref, wkv_ref, mqn_ref, mkn_ref, freq_ref, s1_ref, s2_ref,
                     tri_ref, vone_ref, fq_o, fk_o, fv_o, cum_o, mq_o, mk_o, mv_o, carry_ref):
    t = pl.program_id(1)

    @pl.when(t == 0)
    def _():
        carry_ref[...] = jnp.zeros_like(carry_ref)

    tm = h_ref.shape[1]
    a = _rms(h_ref[0], nmix_ref[...])
    proj = _bdot(a, win_ref[...])
    nf = FOX_HEADS * FOX_HEAD_DIM
    fq = proj[:, 0:nf]
    fk = proj[:, nf:2 * nf]
    nv = FOX_HEADS * LANES
    fv = proj[:, 2 * nf:2 * nf + nv]
    o_cq = 2 * nf + nv
    cq = proj[:, o_cq:o_cq + MLA_Q_LORA]
    o_ckv = o_cq + MLA_Q_LORA
    ckv = proj[:, o_ckv:o_ckv + MLA_KV_LORA]
    misc = proj[:, o_ckv + MLA_KV_LORA:]

    ind = ind_ref[...]
    fq_n = fq * lax.rsqrt(_split_dot(fq * fq, ind) / FOX_HEAD_DIM + RMS_EPS) * fqn_ref[...]
    fk_n = fk * lax.rsqrt(_split_dot(fk * fk, ind) / FOX_HEAD_DIM + RMS_EPS) * fkn_ref[...]
    fq_o[0] = (fq_n * (FOX_HEAD_DIM ** -0.5 * LOG2E)).astype(BF16)
    fk_o[0] = fk_n.astype(BF16)
    fv_o[0] = (fv + vone_ref[...]).astype(BF16)

    lane = lax.broadcasted_iota(jnp.int32, (tm, LANES), 1)
    logf = jnp.where(lane < FOX_HEADS, _log_sigmoid(misc + bf_ref[...]), 0.0)
    cum = _hdot(tri_ref[...], logf) + carry_ref[...]
    carry_ref[...] = cum[tm - 1:tm, :]
    cum_o[0] = (cum * LOG2E).T[:FOX_HEADS, :]

    ang = pos_ref[0].astype(F32) * freq_ref[...]
    cos1 = jnp.cos(ang)
    sin1 = jnp.sin(ang)
    cos = jnp.concatenate([cos1] * MLA_HEADS, axis=1)
    sin_a = jnp.concatenate([sin1 * s1_ref[...]] * MLA_HEADS, axis=1)
    sin_b = jnp.concatenate([sin1 * s2_ref[...]] * MLA_HEADS, axis=1)
    width = MLA_HEADS * LANES
    half = MLA_ROPE // 2

    def rope(x):
        return (x * cos + pltpu.roll(x, width - half, 1) * sin_a + pltpu.roll(x, half, 1) * sin_b)

    q = _bdot(_rms(cq, qan_ref[...]), wq_ref[...])
    q = rope(_head_norm128(q, MLA_HEADS, MLA_QK, mqn_ref[...]))
    mq_o[0] = (q * (MLA_QK ** -0.5 * LOG2E)).astype(BF16)

    kv = _bdot(_rms(ckv, kvan_ref[...]), wkv_ref[...])
    kr = pltpu.roll(misc, MLA_NOPE - FOX_HEADS, 1)
    kr = jnp.where((lane >= MLA_NOPE) & (lane < MLA_QK), kr, 0.0)
    k = kv[:, :width] + jnp.concatenate([kr] * MLA_HEADS, axis=1)
    k = rope(_head_norm128(k, MLA_HEADS, MLA_QK, mkn_ref[...]))
    mk_o[0] = k.astype(BF16)
    mv_o[0] = (kv[:, width:] + vone_ref[...]).astype(BF16)


def _even_pre(h, positions, norm_mix, w_in, b_f, fox_qn, fox_kn, q_a_norm, w_q_up, kv_a_norm, w_kv_up,
              mla_qn, mla_kn):
    b, s, d = h.shape
    tm = _tile(s, 256)
    nf = FOX_HEADS * FOX_HEAD_DIM
    sizes = (nf, nf, nf, FOX_HEADS, MLA_Q_LORA, MLA_KV_LORA, MLA_ROPE)
    offs = np.concatenate([[0], np.cumsum(sizes)])
    parts = [w_in[:, offs[i]:offs[i + 1]] for i in range(len(sizes))]
    pad = jnp.zeros((d, LANES - FOX_HEADS - MLA_ROPE), w_in.dtype)
    slot_pad = ((0, 0), (0, 0), (0, LANES - FOX_HEAD_DIM))
    wfv = jnp.pad(parts[2].reshape(d, FOX_HEADS, FOX_HEAD_DIM), slot_pad).reshape(d, FOX_HEADS * LANES)
    win = jnp.concatenate([parts[0], parts[1], wfv, parts[4], parts[5], parts[3], parts[6], pad],
                          axis=1).astype(BF16)
    gidx = np.arange(nf) // FOX_HEAD_DIM
    ind = jnp.asarray(gidx[:, None] == gidx[None, :], BF16)
    fqn = jnp.tile(fox_qn, FOX_HEADS)[None, :]
    fkn = jnp.tile(fox_kn, FOX_HEADS)[None, :]
    bf = jnp.zeros((1, LANES), F32).at[0, :FOX_HEADS].set(b_f)
    padq = LANES - MLA_QK
    wq = jnp.pad(w_q_up.reshape(MLA_Q_LORA, MLA_HEADS, MLA_QK), ((0, 0), (0, 0), (0, padq)))
    wq = wq.reshape(MLA_Q_LORA, MLA_HEADS * LANES).astype(BF16)
    wkv3 = w_kv_up.reshape(MLA_KV_LORA, MLA_HEADS, MLA_NOPE + MLA_V)
    wk = jnp.pad(wkv3[:, :, :MLA_NOPE], ((0, 0), (0, 0), (0, LANES - MLA_NOPE)))
    wv = jnp.pad(wkv3[:, :, MLA_NOPE:], ((0, 0), (0, 0), (0, LANES - MLA_V)))
    wkv = jnp.concatenate([wk.reshape(MLA_KV_LORA, MLA_HEADS * LANES),
                           wv.reshape(MLA_KV_LORA, MLA_HEADS * LANES)], axis=1).astype(BF16)
    vone = jnp.tile(jnp.zeros((LANES,), F32).at[MLA_V].set(1.0), MLA_HEADS)[None, :]
    mqn = jnp.tile(jnp.pad(mla_qn, (0, padq)), MLA_HEADS)[None, :]
    mkn = jnp.tile(jnp.pad(mla_kn, (0, padq)), MLA_HEADS)[None, :]
    half = MLA_ROPE // 2
    inv = ROPE_THETA ** (-jnp.arange(half, dtype=F32) * 2.0 / MLA_ROPE)
    freq = jnp.zeros((1, LANES), F32).at[0, MLA_NOPE:MLA_NOPE + half].set(inv)
    freq = freq.at[0, MLA_NOPE + half:MLA_QK].set(inv)
    s1 = jnp.zeros((1, LANES), F32).at[0, MLA_NOPE:MLA_NOPE + half].set(-1.0)
    s2 = jnp.zeros((1, LANES), F32).at[0, MLA_NOPE + half:MLA_QK].set(1.0)
    tri = jnp.asarray(np.tril(np.ones((tm, tm), np.float32)))
    pos3 = positions.reshape(b, s, 1)

    row = lambda n: pl.BlockSpec((1, tm, n), lambda bi, ti: (bi, ti, 0))
    consts = [norm_mix[None, :], win, ind, fqn, fkn, bf, q_a_norm[None, :], wq, kv_a_norm[None, :], wkv,
              mqn, mkn, freq, s1, s2, tri, vone]
    nv = FOX_HEADS * LANES
    out_shape = [jax.ShapeDtypeStruct((b, s, nf), BF16)] * 2 + [jax.ShapeDtypeStruct((b, s, nv), BF16)] + [
        jax.ShapeDtypeStruct((b, FOX_HEADS, s), F32),
        jax.ShapeDtypeStruct((b, s, MLA_HEADS * LANES), BF16),
        jax.ShapeDtypeStruct((b, s, MLA_HEADS * LANES), BF16),
        jax.ShapeDtypeStruct((b, s, MLA_HEADS * LANES), BF16)]
    return pl.pallas_call(
        _even_pre_kernel,
        grid=(b, s // tm),
        in_specs=[row(d), row(1)] + [_full(c.shape) for c in consts],
        out_specs=[row(nf), row(nf), row(nv), pl.BlockSpec((1, FOX_HEADS, tm), lambda bi, ti: (bi, 0, ti)),
                   row(MLA_HEADS * LANES), row(MLA_HEADS * LANES), row(MLA_HEADS * LANES)],
        out_shape=out_shape,
        scratch_shapes=[pltpu.VMEM((1, LANES), F32)],
        compiler_params=_params(("arbitrary", "arbitrary")),
        name="even_pre",
    )(h, pos3, *consts)


def _attn_kernel(*refs, tq, tk, fox):
    if fox:
        q_ref, k_ref, v_ref, cr_ref, o_ref = refs
    else:
        q_ref, k_ref, v_ref, o_ref = refs
    hp = pl.program_id(1)
    i = pl.program_id(2)
    lane = lax.broadcasted_iota(jnp.int32, (tq, LANES), 1)
    qs = []
    for hh in range(2):
        if fox:
            in_head = (lane >= FOX_HEAD_DIM * hh) & (lane < FOX_HEAD_DIM * (hh + 1))
            qs.append(jnp.where(in_head, q_ref[0], jnp.zeros((), BF16)))
        else:
            qs.append(q_ref[0, :, LANES * hh:LANES * (hh + 1)])

    def step(j, carry, lo=None):
        koff = pl.multiple_of(j * tk, tk)
        top = 0 if lo is None else lo
        new = []
        for hh in range(2):
            m, acc = carry[hh]
            if fox:
                kj = k_ref[0, pl.ds(koff, tk), :]
            else:
                kj = k_ref[0, pl.ds(koff, tk), LANES * hh:LANES * (hh + 1)]
            sc = lax.dot_general(qs[hh][top:], kj, (((1,), (1,)), ((), ())), preferred_element_type=F32)
            if fox:
                sc = sc - cr_ref[0, pl.ds(2 * hp + hh, 1), pl.ds(koff, tk)]
            if lo is not None:
                rowi = lax.broadcasted_iota(jnp.int32, sc.shape, 0)
                coli = lax.broadcasted_iota(jnp.int32, sc.shape, 1)
                sc = jnp.where(coli <= rowi, sc, -jnp.inf)
            m_new = jnp.maximum(m[top:], jnp.max(sc, axis=-1, keepdims=True))
            alpha = jnp.exp2(m[top:] - m_new)
            p = jnp.exp2((sc - jnp.concatenate([m_new] * (tk // LANES), axis=1)).astype(BF16))
            vj = v_ref[0, pl.ds(koff, tk), LANES * hh:LANES * (hh + 1)]
            acc_new = alpha * acc[top:] + jnp.dot(p, vj, preferred_element_type=F32)
            if top:
                m_new = jnp.concatenate([m[:top], m_new], axis=0)
                acc_new = jnp.concatenate([acc[:top], acc_new], axis=0)
            new.append((m_new, acc_new))
        return tuple(new)

    def body(jj, carry):
        for r in range(ratio):
            carry = step(jj * ratio + r, carry)
        return carry

    one = (jnp.full((tq, LANES), -jnp.inf, F32), jnp.zeros((tq, LANES), F32))
    ratio = tq // tk
    carry = lax.fori_loop(0, i, body, (one, one))
    for r in range(ratio):
        carry = step(i * ratio + r, carry, lo=r * tk)
    outs = [acc / acc[:, MLA_V:MLA_V + 1] for _, acc in carry]
    o_ref[0] = jnp.where(lane < MLA_V, outs[0], pltpu.roll(outs[1], MLA_V, 1)).astype(o_ref.dtype)


def _attention(q, k, v, cum_row=None):
    b, s, _ = v.shape
    fox = cum_row is not None
    qw = LANES if fox else 2 * LANES
    tq = _tile(s, ATTN_TQ)
    tk = _tile(tq, ATTN_TK)
    npairs = v.shape[2] // (2 * LANES)
    in_specs = [pl.BlockSpec((1, tq, qw), lambda bi, hp, i: (bi, i, hp)),
                pl.BlockSpec((1, s, qw), lambda bi, hp, i: (bi, 0, hp)),
                pl.BlockSpec((1, s, 2 * LANES), lambda bi, hp, i: (bi, 0, hp))]
    args = [q, k, v]
    if fox:
        in_specs += [pl.BlockSpec((1, FOX_HEADS, s), lambda bi, hp, i: (bi, 0, 0))]
        args += [cum_row]
    return pl.pallas_call(
        functools.partial(_attn_kernel, tq=tq, tk=tk, fox=fox),
        grid=(b, npairs, s // tq),
        in_specs=in_specs,
        out_specs=pl.BlockSpec((1, tq, LANES), lambda bi, hp, i: (bi, i, hp)),
        out_shape=jax.ShapeDtypeStruct((b, s, npairs * LANES), BF16),
        compiler_params=_params(("arbitrary", "arbitrary", "arbitrary")),
        name="fox_attention" if fox else "mla_attention",
    )(*args)


def _odd_pre_kernel(h_ref, nmix_ref, win_ref, u_o, qkv_o, z_o, gb_o):
    a = _rms(h_ref[0], nmix_ref[...])
    proj = _bdot(a, win_ref[...])
    o1 = S5_CH
    o2 = o1 + 3 * GDN_W
    o3 = o2 + GDN_W
    u_o[0] = proj[:, :o1]
    qkv_o[0] = proj[:, o1:o2]
    z_o[0] = proj[:, o2:o3]
    gb_o[0] = proj[:, o3:]


def _odd_pre(h, norm_mix, w_in):
    b, s, d = h.shape
    tm = _tile(s, 256)
    sizes = (S5_CH, 3 * GDN_W, GDN_HEADS, GDN_HEADS, GDN_W)
    offs = np.concatenate([[0], np.cumsum(sizes)])
    parts = [w_in[:, offs[i]:offs[i + 1]] for i in range(len(sizes))]
    pad = jnp.zeros((d, LANES - 2 * GDN_HEADS), w_in.dtype)
    win = jnp.concatenate([parts[0], parts[1], parts[4], parts[2], parts[3], pad], axis=1).astype(BF16)
    row = lambda n: pl.BlockSpec((1, tm, n), lambda bi, ti: (bi, ti, 0))
    widths = (S5_CH, 3 * GDN_W, GDN_W, LANES)
    return pl.pallas_call(
        _odd_pre_kernel,
        grid=(b, s // tm),
        in_specs=[row(d), _full((1, d)), _full(win.shape)],
        out_specs=[row(n) for n in widths],
        out_shape=[jax.ShapeDtypeStruct((b, s, n), F32) for n in widths],
        compiler_params=_params(("arbitrary", "arbitrary")),
        name="odd_pre",
    )(h, norm_mix[None, :], win)


def _s5_kernel(u_ref, perm_ref, unperm_ref, bbd_ref, cbd_ref, ar_ref, ai_ref, asr_ref, asi_ref, pwr_ref, pwi_ref,
               d_ref, wglu_ref, bglu_ref, o_ref, x_ref, sr_ref, si_ref):
    t = pl.program_id(1)

    @pl.when(t == 0)
    def _():
        sr_ref[...] = jnp.zeros_like(sr_ref)
        si_ref[...] = jnp.zeros_like(si_ref)

    tm = u_ref.shape[1]
    nseg = 8
    seg = tm // nseg
    u = u_ref[0]
    u1 = u.astype(BF16)
    u2 = (u - u1.astype(F32)).astype(BF16)
    u3 = (u - u1.astype(F32) - u2.astype(F32)).astype(BF16)
    perm = perm_ref[...]
    u = (jnp.dot(perm, u1, preferred_element_type=F32) + jnp.dot(perm, u2, preferred_element_type=F32)
         + jnp.dot(perm, u3, preferred_element_type=F32))
    hc = S5_CH // 2
    hn = S5_N // 2
    ub = u.astype(BF16)
    for part in range(2):
        for base in (0, S5_N):
            cols = slice(base + part * hn, base + (part + 1) * hn)
            x_ref[:, cols] = jnp.dot(ub[:, part * hc:(part + 1) * hc], bbd_ref[part * hc:(part + 1) * hc, cols],
                                     preferred_element_type=F32)
    ar = ar_ref[...]
    ai = ai_ref[...]
    re = slice(0, S5_N)
    im = slice(S5_N, 2 * S5_N)

    def local(i, carry):
        xr, xi = carry
        rows = pl.ds(pl.multiple_of(i * nseg, nseg), nseg)
        nr = ar * xr - ai * xi + x_ref[rows, re]
        ni = ar * xi + ai * xr + x_ref[rows, im]
        x_ref[rows, re] = nr
        x_ref[rows, im] = ni
        return nr, ni

    zero = jnp.zeros((nseg, S5_N), F32)
    er, ei = lax.fori_loop(0, seg, local, (zero, zero), unroll=4)

    asr = asr_ref[...]
    asi = asi_ref[...]
    cr = [sr_ref[...]]
    ci = [si_ref[...]]
    for s in range(nseg):
        cr.append(asr * cr[s] - asi * ci[s] + er[s:s + 1, :])
        ci.append(asr * ci[s] + asi * cr[s] + ei[s:s + 1, :])
    sr_ref[...] = cr[nseg]
    si_ref[...] = ci[nseg]
    ent_r = jnp.concatenate(cr[:nseg], axis=0)
    ent_i = jnp.concatenate(ci[:nseg], axis=0)

    def fix(i, c):
        rows = pl.ds(pl.multiple_of(i * nseg, nseg), nseg)
        pr = pwr_ref[pl.ds(i, 1), :]
        pi = pwi_ref[pl.ds(i, 1), :]
        x_ref[rows, re] += pr * ent_r - pi * ent_i
        x_ref[rows, im] += pr * ent_i + pi * ent_r
        return c

    lax.fori_loop(0, seg, fix, 0, unroll=4)
    ys = []
    for part in range(2):
        oc = slice(part * hc, (part + 1) * hc)
        acc = None
        for base in (0, S5_N):
            rows = slice(base + part * hn, base + (part + 1) * hn)
            term = _bdot(x_ref[:, rows], cbd_ref[rows, oc])
            acc = term if acc is None else acc + term
        ys.append(acc)
    y = jnp.concatenate(ys, axis=1) + d_ref[...] * u
    hg = jax.nn.gelu(y)
    out = (hg * jax.nn.sigmoid(_bdot(hg, wglu_ref[...]) + bglu_ref[...])).astype(BF16)
    o_ref[0] = jnp.dot(unperm_ref[...], out, preferred_element_type=F32).astype(o_ref.dtype)


def _s5(u, a_re, a_im, b_re, b_im, c_re, c_im, d_skip, log_step, w_glu, b_glu):
    b, s, _ = u.shape
    tm = _tile(s, 256)
    lam_re = jnp.minimum(a_re, -1e-4)
    lam_im = a_im
    dt = jnp.exp(log_step)[:, None]
    mag = jnp.exp(lam_re * dt)
    ab_re = mag * jnp.cos(lam_im * dt)
    ab_im = mag * jnp.sin(lam_im * dt)
    den = lam_re * lam_re + lam_im * lam_im
    nr, ni = ab_re - 1.0, ab_im
    gam_re = (nr * lam_re + ni * lam_im) / den
    gam_im = (ni * lam_re - nr * lam_im) / den
    bb_re = gam_re[..., None] * b_re - gam_im[..., None] * b_im
    bb_im = gam_re[..., None] * b_im + gam_im[..., None] * b_re
    eye = jnp.eye(S5_GROUPS, dtype=F32)
    bd_in = lambda m: jnp.einsum('gpc,gh->gchp', m, eye).reshape(S5_CH, S5_N)
    bd_out = lambda m: jnp.einsum('gcp,gh->gphc', m, eye).reshape(S5_N, S5_CH)
    bbd = jnp.concatenate([bd_in(bb_re), bd_in(bb_im)], axis=1).astype(BF16)
    cbd = jnp.concatenate([bd_out(c_re), -bd_out(c_im)], axis=0).astype(BF16)
    seg = tm // 8
    steps = jnp.arange(1, seg + 1, dtype=F32)[:, None, None] * dt[None]
    pmag = jnp.exp(lam_re[None] * steps)
    pw_re = (pmag * jnp.cos(lam_im[None] * steps)).reshape(seg, S5_N)
    pw_im = (pmag * jnp.sin(lam_im[None] * steps)).reshape(seg, S5_N)
    src = (np.arange(tm) % 8) * seg + np.arange(tm) // 8
    perm = np.zeros((tm, tm), np.float32)
    perm[np.arange(tm), src] = 1.0
    consts = [jnp.asarray(perm, BF16), jnp.asarray(perm.T, BF16),
              bbd, cbd, ab_re.reshape(1, S5_N), ab_im.reshape(1, S5_N), pw_re[seg - 1:seg], pw_im[seg - 1:seg],
              pw_re, pw_im, d_skip[None, :], w_glu.astype(BF16), b_glu[None, :]]
    row = pl.BlockSpec((1, tm, S5_CH), lambda bi, ti: (bi, ti, 0))
    return pl.pallas_call(
        _s5_kernel,
        grid=(b, s // tm),
        in_specs=[row] + [_full(c.shape) for c in consts],
        out_specs=row,
        out_shape=jax.ShapeDtypeStruct((b, s, S5_CH), BF16),
        scratch_shapes=[pltpu.VMEM((tm, 2 * S5_N), F32), pltpu.VMEM((1, S5_N), F32),
                        pltpu.VMEM((1, S5_N), F32)],
        compiler_params=_params(("arbitrary", "arbitrary")),
        name="s5",
    )(u, *consts)


def _gdn_kernel(x_ref, z_ref, gb_ref, cw_ref, nega_ref, dtb_ref, onorm_ref, tril_ref, triu_ref, o_ref,
                xpad_ref, state_ref):
    t = pl.program_id(1)
    tm = x_ref.shape[1]
    c = GDN_CHUNK
    hd = GDN_HEAD_DIM

    @pl.when(t == 0)
    def _():
        xpad_ref[0:8, :] = jnp.zeros((8, xpad_ref.shape[1]), F32)
        state_ref[...] = jnp.zeros_like(state_ref)

    @pl.when(t > 0)
    def _():
        xpad_ref[0:8, :] = xpad_ref[tm:tm + 8, :]

    xpad_ref[8:tm + 8, :] = x_ref[0]
    conv = cw_ref[0:1, :] * xpad_ref[pl.ds(8 - (GDN_CONV - 1), tm), :]
    for i in range(1, GDN_CONV):
        conv = conv + cw_ref[i:i + 1, :] * xpad_ref[pl.ds(8 - (GDN_CONV - 1) + i, tm), :]
    act = _silu(conv)

    def l2n(x):
        return x * lax.rsqrt(jnp.sum(x * x, axis=-1, keepdims=True) + RMS_EPS)

    gb = gb_ref[0]
    g = nega_ref[...] * _softplus(gb + dtb_ref[...])
    beta = jax.nn.sigmoid(gb)
    gc = _hdot(tril_ref[...], g)
    gct = _hdot(g.T, triu_ref[...])

    ri = lax.broadcasted_iota(jnp.int32, (tm, tm), 0)
    ci = lax.broadcasted_iota(jnp.int32, (tm, tm), 1)
    same = (ri // c) == (ci // c)
    incl = same & (ri >= ci)
    strict = same & (ri > ci)
    eye = (ri == ci).astype(F32)
    offs = []
    bs = 1
    while bs < c:
        offs.append(((ri // (2 * bs)) == (ci // (2 * bs))) & ((ri % (2 * bs)) >= bs) & ((ci % (2 * bs)) < bs))
        bs *= 2
    z = z_ref[0]
    nchunks = tm // c

    heads = range(GDN_HEADS)
    q = [l2n(act[:, hh * hd:(hh + 1) * hd]) * (hd ** -0.5) for hh in heads]
    k = [l2n(act[:, GDN_W + hh * hd:GDN_W + (hh + 1) * hd]) for hh in heads]
    v = [act[:, 2 * GDN_W + hh * hd:2 * GDN_W + (hh + 1) * hd] for hh in heads]
    bcol = [beta[:, GDN_HEADS + hh:GDN_HEADS + hh + 1] for hh in heads]
    gcol = [gc[:, hh:hh + 1] for hh in heads]
    decay = [jnp.where(incl, jnp.exp(jnp.where(incl, gcol[hh] - gct[hh:hh + 1, :], 0.0)), 0.0) for hh in heads]
    kb = [k[hh] * bcol[hh] for hh in heads]
    a_mat = [jnp.where(strict, _bdot_nt(kb[hh], k[hh]) * decay[hh], 0.0) for hh in heads]
    t_mat = [eye for _ in heads]
    for off in offs:
        pa = [_bdot(t_mat[hh], jnp.where(off, a_mat[hh], 0.0)) for hh in heads]
        t_mat = [t_mat[hh] - _bdot(pa[hh], t_mat[hh]) for hh in heads]
    th = [t_mat[hh].astype(BF16) for hh in heads]
    tl = [(t_mat[hh] - th[hh].astype(F32)).astype(BF16) for hh in heads]
    ah = [a_mat[hh].astype(BF16) for hh in heads]
    al = [(a_mat[hh] - ah[hh].astype(F32)).astype(BF16) for hh in heads]
    a_t = [jnp.dot(ah[hh], th[hh], preferred_element_type=F32) + jnp.dot(ah[hh], tl[hh], preferred_element_type=F32)
           + jnp.dot(al[hh], th[hh], preferred_element_type=F32) for hh in heads]
    t_mat = [t_mat[hh] + jnp.dot(th[hh], (eye - t_mat[hh] - a_t[hh]).astype(BF16), preferred_element_type=F32)
             for hh in heads]
    eg = [jnp.exp(gcol[hh]) for hh in heads]
    u = [_bdot(t_mat[hh], v[hh] * bcol[hh]) for hh in heads]
    w = [_bdot(t_mat[hh], kb[hh] * eg[hh]) for hh in heads]
    intra = [jnp.where(incl, _bdot_nt(q[hh], k[hh]) * decay[hh], 0.0).astype(BF16) for hh in heads]
    qd = [q[hh] * eg[hh] for hh in heads]
    state = [state_ref[hh] for hh in heads]
    for n in range(nchunks):
        r0 = n * c
        for hh in heads:
            lo = hh * hd
            gcn = gcol[hh][r0:r0 + c, :]
            glast = gcol[hh][r0 + c - 1:r0 + c, :]
            v_new = u[hh][r0:r0 + c, :] - _bdot(w[hh][r0:r0 + c, :], state[hh])
            v_rep = jnp.concatenate([v_new.astype(BF16)] * nchunks, axis=0)
            o = _bdot(qd[hh][r0:r0 + c, :], state[hh]) + jnp.dot(intra[hh][r0:r0 + c, :], v_rep,
                                                                  preferred_element_type=F32)
            state[hh] = state[hh] * jnp.exp(glast) + _bdot_tn(k[hh][r0:r0 + c, :] * jnp.exp(glast - gcn), v_new)
            on = o * lax.rsqrt(jnp.mean(o * o, axis=-1, keepdims=True) + RMS_EPS) * onorm_ref[...]
            o_ref[0, r0:r0 + c, lo:lo + hd] = (on * _silu(z[r0:r0 + c, lo:lo + hd])).astype(o_ref.dtype)
    for hh in heads:
        state_ref[hh] = state[hh]


def _gdn(qkv, z, gb, conv_w, a_log, dt_bias, o_norm):
    b, s, cw = qkv.shape
    tm = _tile(s, 256)
    nega = jnp.zeros((1, LANES), F32).at[0, :GDN_HEADS].set(-jnp.exp(a_log))
    dtb = jnp.zeros((1, LANES), F32).at[0, :GDN_HEADS].set(dt_bias)
    cwp = jnp.pad(conv_w, ((0, 8 - GDN_CONV), (0, 0)))
    ridx = np.arange(tm)
    same = (ridx[:, None] // GDN_CHUNK) == (ridx[None, :] // GDN_CHUNK)
    tril = jnp.asarray((same & (ridx[:, None] >= ridx[None, :])).astype(np.float32))
    triu = jnp.asarray((same & (ridx[:, None] <= ridx[None, :])).astype(np.float32))
    consts = [cwp, nega, dtb, o_norm[None, :], tril, triu]
    row = lambda n: pl.BlockSpec((1, tm, n), lambda bi, ti: (bi, ti, 0))
    return pl.pallas_call(
        _gdn_kernel,
        grid=(b, s // tm),
        in_specs=[row(cw), row(GDN_W), row(LANES)] + [_full(c.shape) for c in consts],
        out_specs=row(GDN_W),
        out_shape=jax.ShapeDtypeStruct((b, s, GDN_W), BF16),
        scratch_shapes=[pltpu.VMEM((tm + 8, cw), F32), pltpu.VMEM((GDN_HEADS, GDN_HEAD_DIM, GDN_HEAD_DIM), F32)],
        compiler_params=_params(("arbitrary", "arbitrary")),
        name="gdn",
    )(qkv, z, gb, *consts)


def _router_kernel(a_ref, b_ref, wa_ref, wb_ref, h_ref, g_ref, wrh_ref, wrl_ref, br_ref, tri_ref,
                   h_o, xs_o, keyt_o, wt_o, cnt_o, sel_ref):
    tm = h_ref.shape[0]
    h = (h_ref[...] + jnp.dot(a_ref[...], wa_ref[...], preferred_element_type=F32)
         + jnp.dot(b_ref[...], wb_ref[...], preferred_element_type=F32))
    h_o[...] = h
    m = _rms(h, g_ref[...])
    mh = m.astype(BF16)
    ml = (m - mh.astype(F32)).astype(BF16)
    logits = (jnp.dot(mh, wrh_ref[...], preferred_element_type=F32) + jnp.dot(mh, wrl_ref[...], preferred_element_type=F32)
              + jnp.dot(ml, wrh_ref[...], preferred_element_type=F32)) + br_ref[...]
    lane = lax.broadcasted_iota(jnp.int32, (tm, LANES), 1)
    neg = -jnp.inf

    def first_argmax(x):
        mx = jnp.max(x, axis=-1, keepdims=True)
        idx = jnp.min(jnp.where(x == mx, lane, LANES), axis=-1, keepdims=True)
        return mx, idx

    is_g = (lane >= N_EXPERTS) & (lane < N_EXPERTS + MOE_GROUPS)
    gl = jnp.where(is_g, logits, neg)
    gmax, gidx = first_argmax(gl)
    g_w = 1.0 / jnp.sum(jnp.where(is_g, jnp.exp(gl - gmax), 0.0), axis=-1, keepdims=True)
    in_group = (lane // MOE_PER_GROUP) == (gidx - N_EXPERTS)
    el = jnp.where(in_group & (lane < N_EXPERTS), logits, neg)
    m1, i1 = first_argmax(el)
    el2 = jnp.where(lane == i1, neg, el)
    m2, i2 = first_argmax(el2)
    r = jnp.exp(m2 - m1)
    w1 = g_w / (1.0 + r)
    w2 = g_w * r / (1.0 + r)
    chose = (lane == i1) | (lane == i2)
    wmat = jnp.where(lane == i1, w1, jnp.where(lane == i2, w2, 0.0))
    ch = chose.astype(F32)
    rank = jnp.dot(tri_ref[...], ch.astype(BF16), preferred_element_type=F32)
    keyt = jnp.where(chose, rank, -1.0).T
    keyt_o[0] = keyt
    wt_o[0] = wmat.T
    cnt_o[0] = jnp.sum(ch, axis=0, keepdims=True).astype(jnp.int32)
    riota = lax.broadcasted_iota(jnp.int32, (MOE_CAP, tm), 0).astype(F32)
    for e in range(N_EXPERTS):
        sel_ref[e * MOE_CAP:(e + 1) * MOE_CAP, :] = jnp.where(keyt[e:e + 1, :] == riota, 1.0, 0.0).astype(BF16)
    xg = jnp.dot(sel_ref[...], mh, preferred_element_type=F32)
    xs_o[...] = xg.astype(BF16).reshape(xs_o.shape)


def _moe_router(a, bb, w_out, hf, norm_g, w_group, b_group, w_expert, b_expert, tb):
    n, d = hf.shape
    nblk = n // tb
    na, nb = a.shape[1], bb.shape[1]
    wa = w_out[:na].astype(BF16)
    wb = w_out[na:].astype(BF16)
    tok = lambda w: pl.BlockSpec((tb, w), lambda i: (i, 0))
    wr = jnp.zeros((d, LANES), F32).at[:, :N_EXPERTS].set(w_expert)
    wr = wr.at[:, N_EXPERTS:N_EXPERTS + MOE_GROUPS].set(w_group)
    wrh = wr.astype(BF16)
    wrl = (wr - wrh.astype(F32)).astype(BF16)
    br = jnp.zeros((1, LANES), F32).at[0, :N_EXPERTS].set(b_expert)
    br = br.at[0, N_EXPERTS:N_EXPERTS + MOE_GROUPS].set(b_group)
    tri = jnp.asarray(np.tril(np.ones((tb, tb), np.float32), -1), BF16)
    blk = pl.BlockSpec((1, LANES, tb), lambda i: (i, 0, 0))
    return pl.pallas_call(
        _router_kernel,
        grid=(nblk,),
        in_specs=[tok(na), tok(nb), _full(wa.shape), _full(wb.shape), tok(d), _full((1, d)), _full(wr.shape),
                  _full(wr.shape), _full(br.shape), _full(tri.shape)],
        out_specs=[tok(d), pl.BlockSpec((N_EXPERTS, MOE_CAP, d), lambda i: (0, i, 0)), blk, blk,
                   pl.BlockSpec((1, 1, LANES), lambda i: (i, 0, 0))],
        out_shape=[jax.ShapeDtypeStruct((n, d), F32),
                   jax.ShapeDtypeStruct((N_EXPERTS, nblk * MOE_CAP, d), BF16),
                   jax.ShapeDtypeStruct((nblk, LANES, tb), F32),
                   jax.ShapeDtypeStruct((nblk, LANES, tb), F32),
                   jax.ShapeDtypeStruct((nblk, 1, LANES), jnp.int32)],
        scratch_shapes=[pltpu.VMEM((N_EXPERTS * MOE_CAP, tb), BF16)],
        compiler_params=_params(("arbitrary",)),
        name="moe_router",
    )(a, bb, wa, wb, hf, norm_g[None, :], wrh, wrl, br, tri)


def _expert_mlp_kernel(x_ref, wg_ref, wu_ref, wd_ref, y_ref, wg_sc, wu_sc, wd_sc):
    @pl.when(pl.program_id(1) == 0)
    def _():
        wg_sc[...] = wg_ref[0, 0].astype(BF16)
        wu_sc[...] = wu_ref[0, 0].astype(BF16)
        wd_sc[...] = wd_ref[0, 0].astype(BF16)

    x = x_ref[0]
    hid = _silu(jnp.dot(x, wg_sc[...], preferred_element_type=F32)) * jnp.dot(
        x, wu_sc[...], preferred_element_type=F32)
    y_ref[0] = jnp.dot(hid.astype(BF16), wd_sc[...], preferred_element_type=F32).astype(BF16)


def _expert_mlp(xs, w_gate, w_up, w_down, layer):
    ne, rows, d = xs.shape
    ff = w_gate.shape[3]
    tr = _tile(rows, 1024)
    return pl.pallas_call(
        _expert_mlp_kernel,
        grid=(ne, rows // tr),
        in_specs=[pl.BlockSpec((1, tr, d), lambda e, i: (e, i, 0)),
                  pl.BlockSpec((1, 1, d, ff), lambda e, i: (layer, e, 0, 0)),
                  pl.BlockSpec((1, 1, d, ff), lambda e, i: (layer, e, 0, 0)),
                  pl.BlockSpec((1, 1, ff, d), lambda e, i: (layer, e, 0, 0))],
        out_specs=pl.BlockSpec((1, tr, d), lambda e, i: (e, i, 0)),
        out_shape=jax.ShapeDtypeStruct((ne, rows, d), BF16),
        scratch_shapes=[pltpu.VMEM((d, ff), BF16), pltpu.VMEM((d, ff), BF16), pltpu.VMEM((ff, d), BF16)],
        compiler_params=_params(("arbitrary", "arbitrary")),
        name="moe_expert_mlp",
    )(xs, w_gate, w_up, w_down)


def _ple_rows(h, p, g_ref, wg_ref, bg_ref, wp_ref):
    gate = jax.nn.sigmoid(_bdot(_rms(h, g_ref[...]), wg_ref[...]) + bg_ref[...])
    return h + gate * _bdot(p, wp_ref[...])


def _combine_ple_kernel(cnt_ref, y_ref, keyt_ref, wt_ref, h_ref, p_ref, gffn_ref, gple_ref, wgate_ref, bgate_ref,
                        wproj_ref, wg_hbm, wu_hbm, wd_hbm, o_ref, sel_ref, acc_ref, m_ref, wg_buf, wu_buf, wd_buf,
                        sems, *, layer):
    blk = pl.program_id(0)
    tb = h_ref.shape[0]
    riota = lax.broadcasted_iota(jnp.int32, (MOE_CAP, tb), 0).astype(F32)
    for e in range(N_EXPERTS):
        hit = keyt_ref[0, e:e + 1, :] == riota
        sel_ref[e * MOE_CAP:(e + 1) * MOE_CAP, :] = jnp.where(hit, wt_ref[0, e:e + 1, :], 0.0).astype(BF16)
    y = y_ref[...].reshape(N_EXPERTS * MOE_CAP, y_ref.shape[2])
    acc_ref[...] = h_ref[...] + lax.dot_general(sel_ref[...], y, (((0,), (0,)), ((), ())),
                                                preferred_element_type=F32)

    most = lax.fori_loop(0, N_EXPERTS, lambda e, mx: jnp.maximum(mx, cnt_ref[blk * LANES + e]), 0)

    @pl.when(most > MOE_CAP)
    def _():
        m_ref[...] = _rms(h_ref[...], gffn_ref[...]).astype(BF16)
        rows = lax.broadcasted_iota(jnp.int32, (MOE_ROWS, tb), 0).astype(F32)

        def expert(e, carry):
            extra = cnt_ref[blk * LANES + e] - MOE_CAP

            @pl.when(extra > 0)
            def _():
                copies = [pltpu.make_async_copy(src.at[layer, e], dst, sems.at[n])
                          for n, (src, dst) in enumerate(((wg_hbm, wg_buf), (wu_hbm, wu_buf), (wd_hbm, wd_buf)))]
                for c in copies:
                    c.start()
                for c in copies:
                    c.wait()
                krow = keyt_ref[0, pl.ds(e, 1), :]
                wrow = wt_ref[0, pl.ds(e, 1), :]

                def chunk(ci, c2):
                    hit = krow == (rows + (MOE_CAP + ci * MOE_ROWS).astype(F32))
                    sel = jnp.where(hit, 1.0, 0.0).astype(BF16)
                    xg = jnp.dot(sel, m_ref[...], preferred_element_type=F32).astype(BF16)
                    hid = _silu(_bdot(xg, wg_buf[...])) * _bdot(xg, wu_buf[...])
                    yo = _bdot(hid, wd_buf[...]).astype(BF16)
                    acc_ref[...] += _bdot_tn(jnp.where(hit, wrow, 0.0), yo)
                    return c2

                lax.fori_loop(0, (extra + MOE_ROWS - 1) // MOE_ROWS, chunk, 0)

            return carry

        lax.fori_loop(0, N_EXPERTS, expert, 0)

    o_ref[...] = _ple_rows(acc_ref[...], p_ref[0], gple_ref, wgate_ref, bgate_ref, wproj_ref)


def _moe_combine_ple(cnt, ys, keyt, wt, hf, norm_ffn, w_gate, w_up, w_down, layer, p_all, ple_norm, ple_w_gate,
                     ple_b_gate, ple_w_proj, tb):
    n, d = hf.shape
    ff = w_gate.shape[3]
    nblk = n // tb
    pd = p_all.shape[-1]
    blk = pl.BlockSpec((1, LANES, tb), lambda i, c: (i, 0, 0))
    tok = pl.BlockSpec((tb, d), lambda i, c: (i, 0))
    vec = pl.BlockSpec((1, d), lambda i, c: (0, 0))
    hbm = pl.BlockSpec(memory_space=pl.ANY)
    grid_spec = pltpu.PrefetchScalarGridSpec(
        num_scalar_prefetch=1,
        grid=(nblk,),
        in_specs=[pl.BlockSpec((N_EXPERTS, MOE_CAP, d), lambda i, c: (0, i, 0)), blk, blk, tok,
                  pl.BlockSpec((1, tb, pd), lambda i, c: (layer, i, 0)), vec, vec,
                  pl.BlockSpec((d, d), lambda i, c: (0, 0)), vec, pl.BlockSpec((pd, d), lambda i, c: (0, 0)),
                  hbm, hbm, hbm],
        out_specs=tok,
        scratch_shapes=[pltpu.VMEM((N_EXPERTS * MOE_CAP, tb), BF16), pltpu.VMEM((tb, d), F32),
                        pltpu.VMEM((tb, d), BF16), pltpu.VMEM((d, ff), F32), pltpu.VMEM((d, ff), F32),
                        pltpu.VMEM((ff, d), F32), pltpu.SemaphoreType.DMA((3,))],
    )
    return pl.pallas_call(
        functools.partial(_combine_ple_kernel, layer=layer),
        grid_spec=grid_spec,
        out_shape=jax.ShapeDtypeStruct((n, d), F32),
        compiler_params=_params(("arbitrary",)),
        name="moe_combine_ple",
    )(cnt, ys, keyt, wt, hf, p_all.reshape(p_all.shape[0], n, pd), norm_ffn[None, :], ple_norm[None, :],
      ple_w_gate.astype(BF16), ple_b_gate[None, :], ple_w_proj.astype(BF16), w_gate, w_up, w_down)


def _proj_moe_ple(mix_a, mix_b, w_out, h, norm_g, w_group, b_group, w_expert, b_expert, w_gate, w_up, w_down,
                  layer, p_all, ple_w_proj, ple_norm, ple_w_gate, ple_b_gate):
    b, s, d = h.shape
    n = b * s
    tb = _tile(n, MOE_TB)
    hf, xs, keyt, wt, cnt = _moe_router(mix_a.reshape(n, -1), mix_b.reshape(n, -1), w_out, h.reshape(n, d),
                                        norm_g, w_group, b_group, w_expert, b_expert, tb)
    ys = _expert_mlp(xs, w_gate, w_up, w_down, layer)
    out = _moe_combine_ple(cnt.reshape(-1), ys, keyt, wt, hf, norm_g, w_gate, w_up, w_down, layer, p_all,
                           ple_norm, ple_w_gate, ple_b_gate, ple_w_proj, tb)
    return out.reshape(b, s, d)


def _even_mixers(h, positions, norm_mix, w_in, b_f, fox_qn, fox_kn, q_a_norm, w_q_up, kv_a_norm, w_kv_up,
                 mla_qn, mla_kn):
    fq, fk, fv, cum, mq, mk, mv = _even_pre(h, positions, norm_mix, w_in, b_f, fox_qn, fox_kn, q_a_norm,
                                            w_q_up, kv_a_norm, w_kv_up, mla_qn, mla_kn)
    return _attention(fq, fk, fv, cum), _attention(mq, mk, mv)


def _odd_mixers(h, norm_mix, w_in, a_re, a_im, b_re, b_im, c_re, c_im, d_skip, log_step, w_glu, b_glu,
                conv_w, a_log, dt_bias, o_norm):
    u, qkv, z, gb = _odd_pre(h, norm_mix, w_in)
    y_ssm = _s5(u, a_re, a_im, b_re, b_im, c_re, c_im, d_skip, log_step, w_glu, b_glu)
    return y_ssm, _gdn(qkv, z, gb, conv_w, a_log, dt_bias, o_norm)


def kernel(x, p, positions, norm_mix, norm_ffn, ev_w_in, fox_b_f, fox_q_norm, fox_k_norm, mla_q_a_norm, mla_w_q_up, mla_kv_a_norm, mla_w_kv_up, mla_q_norm, mla_k_norm, ev_w_out, od_w_in, s5_a_re, s5_a_im, s5_b_re, s5_b_im, s5_c_re, s5_c_im, s5_d, s5_log_step, s5_w_glu, s5_b_glu, gdn_conv_w, gdn_a_log, gdn_dt_bias, gdn_o_norm, od_w_out, moe_w_group, moe_b_group, moe_w_expert, moe_b_expert, moe_w_gate, moe_w_up, moe_w_down, ple_w_proj, ple_norm, ple_w_gate, ple_b_gate):
    h = x
    depth = p.shape[0]
    for i in range(depth):
        j = i // 2
        if i % 2 == 0:
            mix = _even_mixers(h, positions, norm_mix[i], ev_w_in[j], fox_b_f[j], fox_q_norm[j], fox_k_norm[j],
                               mla_q_a_norm[j], mla_w_q_up[j], mla_kv_a_norm[j], mla_w_kv_up[j], mla_q_norm[j],
                               mla_k_norm[j])
            w_out = ev_w_out[j]
        else:
            mix = _odd_mixers(h, norm_mix[i], od_w_in[j], s5_a_re[j], s5_a_im[j], s5_b_re[j], s5_b_im[j],
                              s5_c_re[j], s5_c_im[j], s5_d[j], s5_log_step[j], s5_w_glu[j], s5_b_glu[j],
                              gdn_conv_w[j], gdn_a_log[j], gdn_dt_bias[j], gdn_o_norm[j])
            w_out = od_w_out[j]
        h = _proj_moe_ple(mix[0], mix[1], w_out, h, norm_ffn[i], moe_w_group[i], moe_b_group[i], moe_w_expert[i],
                          moe_b_expert[i], moe_w_gate, moe_w_up, moe_w_down, i, p, ple_w_proj[i], ple_norm[i],
                          ple_w_gate[i], ple_b_gate[i])
    return h
```

```python
import functools
import math

import numpy as np
import jax
import jax.numpy as jnp
from jax import lax
from jax.experimental import pallas as pl
from jax.experimental.pallas import tpu as pltpu

F32 = jnp.float32
BF16 = jnp.bfloat16

LANES = 128
RMS_EPS = 1e-6
ROPE_THETA = 10000.0
LOG2E = math.log2(math.e)

FOX_HEADS = 8
FOX_HEAD_DIM = 64
MLA_HEADS = 8
MLA_Q_LORA = 384
MLA_KV_LORA = 256
MLA_NOPE = 64
MLA_ROPE = 32
MLA_V = 64
MLA_QK = MLA_NOPE + MLA_ROPE

S5_CH = 512
S5_GROUP_CH = 16
S5_GROUPS = S5_CH // S5_GROUP_CH
S5_STATE = 64
S5_N = S5_GROUPS * S5_STATE

GDN_HEADS = 4
GDN_HEAD_DIM = 128
GDN_W = GDN_HEADS * GDN_HEAD_DIM
GDN_CONV = 4
GDN_CHUNK = 64

MOE_GROUPS = 4
MOE_PER_GROUP = 8
N_EXPERTS = MOE_GROUPS * MOE_PER_GROUP
MOE_TB = 512
MOE_CAP = 64
MOE_ROWS = 128
ATTN_TQ = 2048
ATTN_TK = 512

VMEM_LIMIT = 56 * 1024 * 1024


def _tile(n, pref):
    t = min(n, pref)
    assert n % t == 0, (n, t)
    return t


def _params(sem):
    return pltpu.CompilerParams(dimension_semantics=sem, vmem_limit_bytes=VMEM_LIMIT)


def _full(shape):
    nd = len(shape)
    return pl.BlockSpec(shape, lambda *_: (0,) * nd)


def _rms(x, g):
    return x * lax.rsqrt(jnp.mean(x * x, axis=-1, keepdims=True) + RMS_EPS) * g


def _bdot(a, b):
    return jnp.dot(a.astype(BF16), b.astype(BF16), preferred_element_type=F32)


def _bdot_nt(a, b):
    return lax.dot_general(a.astype(BF16), b.astype(BF16), (((1,), (1,)), ((), ())),
                           preferred_element_type=F32)


def _bdot_tn(a, b):
    return lax.dot_general(a.astype(BF16), b.astype(BF16), (((0,), (0,)), ((), ())),
                           preferred_element_type=F32)


def _split3(x):
    x1 = x.astype(BF16)
    r = x - x1.astype(F32)
    x2 = r.astype(BF16)
    return x1, x2, (r - x2.astype(F32)).astype(BF16)


def _sel_dot(sel, x):
    return sum(jnp.dot(sel, part, preferred_element_type=F32) for part in _split3(x))


def _dot_sel(x, sel):
    return sum(jnp.dot(part, sel, preferred_element_type=F32) for part in _split3(x))


def _split_dot(x, ind):
    hi = x.astype(BF16)
    lo = (x - hi.astype(F32)).astype(BF16)
    return (jnp.dot(hi, ind, preferred_element_type=F32)
            + jnp.dot(lo, ind, preferred_element_type=F32))


def _log_sigmoid(x):
    return jnp.minimum(x, 0.0) - jnp.log(1.0 + jnp.exp(-jnp.abs(x)))


def _softplus(x):
    return jnp.maximum(x, 0.0) + jnp.log(1.0 + jnp.exp(-jnp.abs(x)))


def _silu(x):
    return x * jax.nn.sigmoid(x)


def _head_norm128(x, nheads, denom, gain):
    outs = []
    for hh in range(nheads):
        xh = x[:, LANES * hh:LANES * (hh + 1)]
        ss = jnp.sum(xh * xh, axis=-1, keepdims=True)
        outs.append(xh * lax.rsqrt(ss / denom + RMS_EPS))
    return jnp.concatenate(outs, axis=1) * gain


def _even_pre_kernel(h_ref, pos_ref, nmix_ref, win_ref, ind_ref, fqn_ref, fkn_ref, bf_ref,
                     qan_ref, wq_ref, kvan_ref, wkv_ref, mqn_ref, mkn_ref, freq_ref, s1_ref, s2_ref,
                     tri_ref, vone_ref, fq_o, fk_o, fv_o, cum_o, mq_o, mk_o, mv_o, carry_ref):
    t = pl.program_id(1)

    @pl.when(t == 0)
    def _():
        carry_ref[...] = jnp.zeros_like(carry_ref)

    tm = h_ref.shape[1]
    a = _rms(h_ref[0], nmix_ref[...])
    proj = _bdot(a, win_ref[...])
    nf = FOX_HEADS * FOX_HEAD_DIM
    fq = proj[:, 0:nf]
    fk = proj[:, nf:2 * nf]
    nv = FOX_HEADS * LANES
    fv = proj[:, 2 * nf:2 * nf + nv]
    o_cq = 2 * nf + nv
    cq = proj[:, o_cq:o_cq + MLA_Q_LORA]
    o_ckv = o_cq + MLA_Q_LORA
    ckv = proj[:, o_ckv:o_ckv + MLA_KV_LORA]
    misc = proj[:, o_ckv + MLA_KV_LORA:]

    ind = ind_ref[...]
    fq_n = fq * lax.rsqrt(_split_dot(fq * fq, ind) / FOX_HEAD_DIM + RMS_EPS) * fqn_ref[...]
    fk_n = fk * lax.rsqrt(_split_dot(fk * fk, ind) / FOX_HEAD_DIM + RMS_EPS) * fkn_ref[...]
    fq_o[0] = (fq_n * (FOX_HEAD_DIM ** -0.5 * LOG2E)).astype(BF16)
    fk_o[0] = fk_n.astype(BF16)
    fv_o[0] = (fv + vone_ref[...]).astype(BF16)

    lane = lax.broadcasted_iota(jnp.int32, (tm, LANES), 1)
    logf = jnp.where(lane < FOX_HEADS, _log_sigmoid(misc + bf_ref[...]), 0.0)
    cum = _sel_dot(tri_ref[...], logf) + carry_ref[...]
    carry_ref[...] = cum[tm - 1:tm, :]
    cum_o[0] = (cum * LOG2E).T[:FOX_HEADS, :]

    ang = pos_ref[0].astype(F32) * freq_ref[...]
    cos1 = jnp.cos(ang)
    sin1 = jnp.sin(ang)
    cos = jnp.concatenate([cos1] * MLA_HEADS, axis=1)
    sin_a = jnp.concatenate([sin1 * s1_ref[...]] * MLA_HEADS, axis=1)
    sin_b = jnp.concatenate([sin1 * s2_ref[...]] * MLA_HEADS, axis=1)
    width = MLA_HEADS * LANES
    half = MLA_ROPE // 2

    def rope(x):
        return (x * cos + pltpu.roll(x, width - half, 1) * sin_a + pltpu.roll(x, half, 1) * sin_b)

    q = _bdot(_rms(cq, qan_ref[...]), wq_ref[...])
    q = rope(_head_norm128(q, MLA_HEADS, MLA_QK, mqn_ref[...]))
    mq_o[0] = (q * (MLA_QK ** -0.5 * LOG2E)).astype(BF16)

    kv = _bdot(_rms(ckv, kvan_ref[...]), wkv_ref[...])
    kr = pltpu.roll(misc, MLA_NOPE - FOX_HEADS, 1)
    kr = jnp.where((lane >= MLA_NOPE) & (lane < MLA_QK), kr, 0.0)
    k = kv[:, :width] + jnp.concatenate([kr] * MLA_HEADS, axis=1)
    k = rope(_head_norm128(k, MLA_HEADS, MLA_QK, mkn_ref[...]))
    mk_o[0] = k.astype(BF16)
    mv_o[0] = (kv[:, width:] + vone_ref[...]).astype(BF16)


def _even_pre(h, positions, norm_mix, w_in, b_f, fox_qn, fox_kn, q_a_norm, w_q_up, kv_a_norm, w_kv_up,
              mla_qn, mla_kn):
    b, s, d = h.shape
    tm = _tile(s, 256)
    nf = FOX_HEADS * FOX_HEAD_DIM
    sizes = (nf, nf, nf, FOX_HEADS, MLA_Q_LORA, MLA_KV_LORA, MLA_ROPE)
    offs = np.concatenate([[0], np.cumsum(sizes)])
    parts = [w_in[:, offs[i]:offs[i + 1]] for i in range(len(sizes))]
    pad = jnp.zeros((d, LANES - FOX_HEADS - MLA_ROPE), w_in.dtype)
    slot_pad = ((0, 0), (0, 0), (0, LANES - FOX_HEAD_DIM))
    wfv = jnp.pad(parts[2].reshape(d, FOX_HEADS, FOX_HEAD_DIM), slot_pad).reshape(d, FOX_HEADS * LANES)
    win = jnp.concatenate([parts[0], parts[1], wfv, parts[4], parts[5], parts[3], parts[6], pad],
                          axis=1).astype(BF16)
    gidx = np.arange(nf) // FOX_HEAD_DIM
    ind = jnp.asarray(gidx[:, None] == gidx[None, :], BF16)
    fqn = jnp.tile(fox_qn, FOX_HEADS)[None, :]
    fkn = jnp.tile(fox_kn, FOX_HEADS)[None, :]
    bf = jnp.zeros((1, LANES), F32).at[0, :FOX_HEADS].set(b_f)
    padq = LANES - MLA_QK
    wq = jnp.pad(w_q_up.reshape(MLA_Q_LORA, MLA_HEADS, MLA_QK), ((0, 0), (0, 0), (0, padq)))
    wq = wq.reshape(MLA_Q_LORA, MLA_HEADS * LANES).astype(BF16)
    wkv3 = w_kv_up.reshape(MLA_KV_LORA, MLA_HEADS, MLA_NOPE + MLA_V)
    wk = jnp.pad(wkv3[:, :, :MLA_NOPE], ((0, 0), (0, 0), (0, LANES - MLA_NOPE)))
    wv = jnp.pad(wkv3[:, :, MLA_NOPE:], ((0, 0), (0, 0), (0, LANES - MLA_V)))
    wkv = jnp.concatenate([wk.reshape(MLA_KV_LORA, MLA_HEADS * LANES),
                           wv.reshape(MLA_KV_LORA, MLA_HEADS * LANES)], axis=1).astype(BF16)
    vone = jnp.tile(jnp.zeros((LANES,), F32).at[MLA_V].set(1.0), MLA_HEADS)[None, :]
    mqn = jnp.tile(jnp.pad(mla_qn, (0, padq)), MLA_HEADS)[None, :]
    mkn = jnp.tile(jnp.pad(mla_kn, (0, padq)), MLA_HEADS)[None, :]
    half = MLA_ROPE // 2
    inv = ROPE_THETA ** (-jnp.arange(half, dtype=F32) * 2.0 / MLA_ROPE)
    freq = jnp.zeros((1, LANES), F32).at[0, MLA_NOPE:MLA_NOPE + half].set(inv)
    freq = freq.at[0, MLA_NOPE + half:MLA_QK].set(inv)
    s1 = jnp.zeros((1, LANES), F32).at[0, MLA_NOPE:MLA_NOPE + half].set(-1.0)
    s2 = jnp.zeros((1, LANES), F32).at[0, MLA_NOPE + half:MLA_QK].set(1.0)
    tri = jnp.asarray(np.tril(np.ones((tm, tm), np.float32)), BF16)
    pos3 = positions.reshape(b, s, 1)

    row = lambda n: pl.BlockSpec((1, tm, n), lambda bi, ti: (bi, ti, 0))
    consts = [norm_mix[None, :], win, ind, fqn, fkn, bf, q_a_norm[None, :], wq, kv_a_norm[None, :], wkv,
              mqn, mkn, freq, s1, s2, tri, vone]
    nv = FOX_HEADS * LANES
    out_shape = [jax.ShapeDtypeStruct((b, s, nf), BF16)] * 2 + [jax.ShapeDtypeStruct((b, s, nv), BF16)] + [
        jax.ShapeDtypeStruct((b, FOX_HEADS, s), F32),
        jax.ShapeDtypeStruct((b, s, MLA_HEADS * LANES), BF16),
        jax.ShapeDtypeStruct((b, s, MLA_HEADS * LANES), BF16),
        jax.ShapeDtypeStruct((b, s, MLA_HEADS * LANES), BF16)]
    return pl.pallas_call(
        _even_pre_kernel,
        grid=(b, s // tm),
        in_specs=[row(d), row(1)] + [_full(c.shape) for c in consts],
        out_specs=[row(nf), row(nf), row(nv), pl.BlockSpec((1, FOX_HEADS, tm), lambda bi, ti: (bi, 0, ti)),
                   row(MLA_HEADS * LANES), row(MLA_HEADS * LANES), row(MLA_HEADS * LANES)],
        out_shape=out_shape,
        scratch_shapes=[pltpu.VMEM((1, LANES), F32)],
        compiler_params=_params(("arbitrary", "arbitrary")),
        name="even_pre",
    )(h, pos3, *consts)


def _attn_kernel(*refs, tq, tk, fox):
    if fox:
        q_ref, k_ref, v_ref, cr_ref, o_ref = refs
    else:
        q_ref, k_ref, v_ref, o_ref = refs
    hp = pl.program_id(1)
    i = pl.program_id(2)
    lane = lax.broadcasted_iota(jnp.int32, (tq, LANES), 1)
    qs = []
    for hh in range(2):
        if fox:
            in_head = (lane >= FOX_HEAD_DIM * hh) & (lane < FOX_HEAD_DIM * (hh + 1))
            qs.append(jnp.where(in_head, q_ref[0], jnp.zeros((), BF16)))
        else:
            qs.append(q_ref[0, :, LANES * hh:LANES * (hh + 1)])

    def step(j, carry, lo=None):
        koff = pl.multiple_of(j * tk, tk)
        top = 0 if lo is None else lo
        new = []
        for hh in range(2):
            m, acc = carry[hh]
            if fox:
                kj = k_ref[0, pl.ds(koff, tk), :]
            else:
                kj = k_ref[0, pl.ds(koff, tk), LANES * hh:LANES * (hh + 1)]
            sc = lax.dot_general(qs[hh][top:], kj, (((1,), (1,)), ((), ())), preferred_element_type=F32)
            if fox:
                sc = sc - cr_ref[0, pl.ds(2 * hp + hh, 1), pl.ds(koff, tk)]
            if lo is not None:
                rowi = lax.broadcasted_iota(jnp.int32, sc.shape, 0)
                coli = lax.broadcasted_iota(jnp.int32, sc.shape, 1)
                sc = jnp.where(coli <= rowi, sc, -jnp.inf)
            m_new = jnp.maximum(m[top:], jnp.max(sc, axis=-1, keepdims=True))
            alpha = jnp.exp2(m[top:] - m_new)
            p = jnp.exp2((sc - jnp.concatenate([m_new] * (tk // LANES), axis=1)).astype(BF16))
            vj = v_ref[0, pl.ds(koff, tk), LANES * hh:LANES * (hh + 1)]
            acc_new = alpha * acc[top:] + jnp.dot(p, vj, preferred_element_type=F32)
            if top:
                m_new = jnp.concatenate([m[:top], m_new], axis=0)
                acc_new = jnp.concatenate([acc[:top], acc_new], axis=0)
            new.append((m_new, acc_new))
        return tuple(new)

    def body(jj, carry):
        for r in range(ratio):
            carry = step(jj * ratio + r, carry)
        return carry

    one = (jnp.full((tq, LANES), -jnp.inf, F32), jnp.zeros((tq, LANES), F32))
    ratio = tq // tk
    carry = lax.fori_loop(0, i, body, (one, one))
    for r in range(ratio):
        carry = step(i * ratio + r, carry, lo=r * tk)
    outs = [acc / acc[:, MLA_V:MLA_V + 1] for _, acc in carry]
    o_ref[0] = jnp.where(lane < MLA_V, outs[0], pltpu.roll(outs[1], MLA_V, 1)).astype(o_ref.dtype)


def _attention(q, k, v, cum_row=None):
    b, s, _ = v.shape
    fox = cum_row is not None
    qw = LANES if fox else 2 * LANES
    tq = _tile(s, ATTN_TQ)
    tk = _tile(tq, ATTN_TK)
    npairs = v.shape[2] // (2 * LANES)
    in_specs = [pl.BlockSpec((1, tq, qw), lambda bi, hp, i: (bi, i, hp)),
                pl.BlockSpec((1, s, qw), lambda bi, hp, i: (bi, 0, hp)),
                pl.BlockSpec((1, s, 2 * LANES), lambda bi, hp, i: (bi, 0, hp))]
    args = [q, k, v]
    if fox:
        in_specs += [pl.BlockSpec((1, FOX_HEADS, s), lambda bi, hp, i: (bi, 0, 0))]
        args += [cum_row]
    return pl.pallas_call(
        functools.partial(_attn_kernel, tq=tq, tk=tk, fox=fox),
        grid=(b, npairs, s // tq),
        in_specs=in_specs,
        out_specs=pl.BlockSpec((1, tq, LANES), lambda bi, hp, i: (bi, i, hp)),
        out_shape=jax.ShapeDtypeStruct((b, s, npairs * LANES), BF16),
        compiler_params=_params(("arbitrary", "arbitrary", "arbitrary")),
        name="fox_attention" if fox else "mla_attention",
    )(*args)


N_S5_CONSTS = 13
N_GDN_CONSTS = 6


def _odd_kernel(h_ref, nmix_ref, win_ref, *refs):
    s5_consts = refs[:N_S5_CONSTS]
    gdn_consts = refs[N_S5_CONSTS:N_S5_CONSTS + N_GDN_CONSTS]
    y_o, o_o, x_ref, sr_ref, si_ref, xpad_ref, state_ref = refs[N_S5_CONSTS + N_GDN_CONSTS:]
    a = _rms(h_ref[0], nmix_ref[...])
    proj = _bdot(a, win_ref[...])
    o1 = S5_CH
    o2 = o1 + 3 * GDN_W
    o3 = o2 + GDN_W
    _s5_body(proj[:, :o1], *s5_consts, y_o, x_ref, sr_ref, si_ref)
    _gdn_body(proj[:, o1:o2], proj[:, o2:o3], proj[:, o3:], *gdn_consts, o_o, xpad_ref, state_ref)


def _odd_mixers(h, norm_mix, w_in, a_re, a_im, b_re, b_im, c_re, c_im, d_skip, log_step, w_glu, b_glu,
                conv_w, a_log, dt_bias, o_norm):
    b, s, d = h.shape
    tm = _tile(s, 256)
    sizes = (S5_CH, 3 * GDN_W, GDN_HEADS, GDN_HEADS, GDN_W)
    offs = np.concatenate([[0], np.cumsum(sizes)])
    parts = [w_in[:, offs[i]:offs[i + 1]] for i in range(len(sizes))]
    pad = jnp.zeros((d, LANES - 2 * GDN_HEADS), w_in.dtype)
    win = jnp.concatenate([parts[0], parts[1], parts[4], parts[2], parts[3], pad], axis=1).astype(BF16)
    s5_consts = _s5_consts(tm, a_re, a_im, b_re, b_im, c_re, c_im, d_skip, log_step, w_glu, b_glu)
    gdn_consts = _gdn_consts(tm, conv_w, a_log, dt_bias, o_norm)
    assert len(s5_consts) == N_S5_CONSTS and len(gdn_consts) == N_GDN_CONSTS
    consts = [norm_mix[None, :], win] + s5_consts + gdn_consts
    row = lambda n: pl.BlockSpec((1, tm, n), lambda bi, ti: (bi, ti, 0))
    return pl.pallas_call(
        _odd_kernel,
        grid=(b, s // tm),
        in_specs=[row(d)] + [_full(c.shape) for c in consts],
        out_specs=[row(S5_CH), row(GDN_W)],
        out_shape=[jax.ShapeDtypeStruct((b, s, S5_CH), BF16), jax.ShapeDtypeStruct((b, s, GDN_W), BF16)],
        scratch_shapes=[pltpu.VMEM((tm, 2 * S5_N), F32), pltpu.VMEM((1, S5_N), F32), pltpu.VMEM((1, S5_N), F32),
                        pltpu.VMEM((tm + 8, 3 * GDN_W), F32),
                        pltpu.VMEM((GDN_HEADS, GDN_HEAD_DIM, GDN_HEAD_DIM), F32)],
        compiler_params=_params(("arbitrary", "arbitrary")),
        name="odd_mixers",
    )(h, *consts)


def _s5_body(u, perm_ref, unperm_ref, bbd_ref, cbd_ref, ar_ref, ai_ref, asr_ref, asi_ref, pwr_ref, pwi_ref,
             d_ref, wglu_ref, bglu_ref, o_ref, x_ref, sr_ref, si_ref):
    t = pl.program_id(1)

    @pl.when(t == 0)
    def _():
        sr_ref[...] = jnp.zeros_like(sr_ref)
        si_ref[...] = jnp.zeros_like(si_ref)

    tm = u.shape[0]
    nseg = 8
    seg = tm // nseg
    u = _sel_dot(perm_ref[...], u)
    hc = S5_CH // 2
    hn = S5_N // 2
    ub = u.astype(BF16)
    for part in range(2):
        for base in (0, S5_N):
            cols = slice(base + part * hn, base + (part + 1) * hn)
            x_ref[:, cols] = jnp.dot(ub[:, part * hc:(part + 1) * hc], bbd_ref[part * hc:(part + 1) * hc, cols],
                                     preferred_element_type=F32)
    ar = ar_ref[...]
    ai = ai_ref[...]
    re = slice(0, S5_N)
    im = slice(S5_N, 2 * S5_N)

    def local(i, carry):
        xr, xi = carry
        rows = pl.ds(pl.multiple_of(i * nseg, nseg), nseg)
        nr = ar * xr - ai * xi + x_ref[rows, re]
        ni = ar * xi + ai * xr + x_ref[rows, im]
        x_ref[rows, re] = nr
        x_ref[rows, im] = ni
        return nr, ni

    zero = jnp.zeros((nseg, S5_N), F32)
    er, ei = lax.fori_loop(0, seg, local, (zero, zero), unroll=4)

    asr = asr_ref[...]
    asi = asi_ref[...]
    cr = [sr_ref[...]]
    ci = [si_ref[...]]
    for s in range(nseg):
        cr.append(asr * cr[s] - asi * ci[s] + er[s:s + 1, :])
        ci.append(asr * ci[s] + asi * cr[s] + ei[s:s + 1, :])
    sr_ref[...] = cr[nseg]
    si_ref[...] = ci[nseg]
    ent_r = jnp.concatenate(cr[:nseg], axis=0)
    ent_i = jnp.concatenate(ci[:nseg], axis=0)

    def fix(i, c):
        rows = pl.ds(pl.multiple_of(i * nseg, nseg), nseg)
        pr = pwr_ref[pl.ds(i, 1), :]
        pi = pwi_ref[pl.ds(i, 1), :]
        x_ref[rows, re] += pr * ent_r - pi * ent_i
        x_ref[rows, im] += pr * ent_i + pi * ent_r
        return c

    lax.fori_loop(0, seg, fix, 0, unroll=4)
    ys = []
    for part in range(2):
        oc = slice(part * hc, (part + 1) * hc)
        acc = None
        for base in (0, S5_N):
            rows = slice(base + part * hn, base + (part + 1) * hn)
            term = _bdot(x_ref[:, rows], cbd_ref[rows, oc])
            acc = term if acc is None else acc + term
        ys.append(acc)
    y = jnp.concatenate(ys, axis=1) + d_ref[...] * u
    hg = jax.nn.gelu(y)
    out = (hg * jax.nn.sigmoid(_bdot(hg, wglu_ref[...]) + bglu_ref[...])).astype(BF16)
    o_ref[0] = jnp.dot(unperm_ref[...], out, preferred_element_type=F32).astype(o_ref.dtype)


def _s5_consts(tm, a_re, a_im, b_re, b_im, c_re, c_im, d_skip, log_step, w_glu, b_glu):
    lam_re = jnp.minimum(a_re, -1e-4)
    lam_im = a_im
    dt = jnp.exp(log_step)[:, None]
    mag = jnp.exp(lam_re * dt)
    ab_re = mag * jnp.cos(lam_im * dt)
    ab_im = mag * jnp.sin(lam_im * dt)
    den = lam_re * lam_re + lam_im * lam_im
    nr, ni = ab_re - 1.0, ab_im
    gam_re = (nr * lam_re + ni * lam_im) / den
    gam_im = (ni * lam_re - nr * lam_im) / den
    bb_re = gam_re[..., None] * b_re - gam_im[..., None] * b_im
    bb_im = gam_re[..., None] * b_im + gam_im[..., None] * b_re
    eye = jnp.eye(S5_GROUPS, dtype=F32)
    bd_in = lambda m: jnp.einsum('gpc,gh->gchp', m, eye).reshape(S5_CH, S5_N)
    bd_out = lambda m: jnp.einsum('gcp,gh->gphc', m, eye).reshape(S5_N, S5_CH)
    bbd = jnp.concatenate([bd_in(bb_re), bd_in(bb_im)], axis=1).astype(BF16)
    cbd = jnp.concatenate([bd_out(c_re), -bd_out(c_im)], axis=0).astype(BF16)
    seg = tm // 8
    steps = jnp.arange(1, seg + 1, dtype=F32)[:, None, None] * dt[None]
    pmag = jnp.exp(lam_re[None] * steps)
    pw_re = (pmag * jnp.cos(lam_im[None] * steps)).reshape(seg, S5_N)
    pw_im = (pmag * jnp.sin(lam_im[None] * steps)).reshape(seg, S5_N)
    src = (np.arange(tm) % 8) * seg + np.arange(tm) // 8
    perm = np.zeros((tm, tm), np.float32)
    perm[np.arange(tm), src] = 1.0
    return [jnp.asarray(perm, BF16), jnp.asarray(perm.T, BF16),
            bbd, cbd, ab_re.reshape(1, S5_N), ab_im.reshape(1, S5_N), pw_re[seg - 1:seg], pw_im[seg - 1:seg],
            pw_re, pw_im, d_skip[None, :], w_glu.astype(BF16), b_glu[None, :]]


def _gdn_body(x, z, gb, cw_ref, nega_ref, dtb_ref, onorm_ref, tril_ref, triu_ref, o_ref, xpad_ref, state_ref):
    t = pl.program_id(1)
    tm = x.shape[0]
    c = GDN_CHUNK
    hd = GDN_HEAD_DIM

    @pl.when(t == 0)
    def _():
        xpad_ref[0:8, :] = jnp.zeros((8, xpad_ref.shape[1]), F32)
        state_ref[...] = jnp.zeros_like(state_ref)

    @pl.when(t > 0)
    def _():
        xpad_ref[0:8, :] = xpad_ref[tm:tm + 8, :]

    xpad_ref[8:tm + 8, :] = x
    conv = cw_ref[0:1, :] * xpad_ref[pl.ds(8 - (GDN_CONV - 1), tm), :]
    for i in range(1, GDN_CONV):
        conv = conv + cw_ref[i:i + 1, :] * xpad_ref[pl.ds(8 - (GDN_CONV - 1) + i, tm), :]
    act = _silu(conv)

    def l2n(x):
        return x * lax.rsqrt(jnp.sum(x * x, axis=-1, keepdims=True) + RMS_EPS)

    g = nega_ref[...] * _softplus(gb + dtb_ref[...])
    beta = jax.nn.sigmoid(gb)
    gc = _sel_dot(tril_ref[...], g)
    gct = _dot_sel(g.T, triu_ref[...])

    ri = lax.broadcasted_iota(jnp.int32, (tm, tm), 0)
    ci = lax.broadcasted_iota(jnp.int32, (tm, tm), 1)
    same = (ri // c) == (ci // c)
    incl = same & (ri >= ci)
    strict = same & (ri > ci)
    eye = (ri == ci).astype(F32)
    offs = []
    bs = 1
    while bs < c:
        offs.append(((ri // (2 * bs)) == (ci // (2 * bs))) & ((ri % (2 * bs)) >= bs) & ((ci % (2 * bs)) < bs))
        bs *= 2
    nchunks = tm // c

    heads = range(GDN_HEADS)
    q = [l2n(act[:, hh * hd:(hh + 1) * hd]) * (hd ** -0.5) for hh in heads]
    k = [l2n(act[:, GDN_W + hh * hd:GDN_W + (hh + 1) * hd]) for hh in heads]
    v = [act[:, 2 * GDN_W + hh * hd:2 * GDN_W + (hh + 1) * hd] for hh in heads]
    bcol = [beta[:, GDN_HEADS + hh:GDN_HEADS + hh + 1] for hh in heads]
    gcol = [gc[:, hh:hh + 1] for hh in heads]
    decay = [jnp.where(incl, jnp.exp(jnp.where(incl, gcol[hh] - gct[hh:hh + 1, :], 0.0)), 0.0) for hh in heads]
    kb = [k[hh] * bcol[hh] for hh in heads]
    a_mat = [jnp.where(strict, _bdot_nt(kb[hh], k[hh]) * decay[hh], 0.0) for hh in heads]
    t_mat = [eye - jnp.where(offs[0], a_mat[hh], 0.0) for hh in heads]
    for off in offs[1:]:
        pa = [_bdot(t_mat[hh], jnp.where(off, a_mat[hh], 0.0)) for hh in heads]
        t_mat = [t_mat[hh] - _bdot(pa[hh], t_mat[hh]) for hh in heads]
    th = [t_mat[hh].astype(BF16) for hh in heads]
    tl = [(t_mat[hh] - th[hh].astype(F32)).astype(BF16) for hh in heads]
    ah = [a_mat[hh].astype(BF16) for hh in heads]
    al = [(a_mat[hh] - ah[hh].astype(F32)).astype(BF16) for hh in heads]
    a_t = [jnp.dot(ah[hh], th[hh], preferred_element_type=F32) + jnp.dot(ah[hh], tl[hh], preferred_element_type=F32)
           + jnp.dot(al[hh], th[hh], preferred_element_type=F32) for hh in heads]
    t_mat = [t_mat[hh] + jnp.dot(th[hh], (eye - t_mat[hh] - a_t[hh]).astype(BF16), preferred_element_type=F32)
             for hh in heads]
    eg = [jnp.exp(gcol[hh]) for hh in heads]
    u = [_bdot(t_mat[hh], v[hh] * bcol[hh]) for hh in heads]
    w = [_bdot(t_mat[hh], kb[hh] * eg[hh]) for hh in heads]
    intra = [jnp.where(incl, _bdot_nt(q[hh], k[hh]) * decay[hh], 0.0).astype(BF16) for hh in heads]
    qd = [q[hh] * eg[hh] for hh in heads]
    state = [state_ref[hh] for hh in heads]
    for n in range(nchunks):
        r0 = n * c
        for hh in heads:
            lo = hh * hd
            gcn = gcol[hh][r0:r0 + c, :]
            glast = gcol[hh][r0 + c - 1:r0 + c, :]
            v_new = u[hh][r0:r0 + c, :] - _bdot(w[hh][r0:r0 + c, :], state[hh])
            v_rep = jnp.concatenate([v_new.astype(BF16)] * nchunks, axis=0)
            o = _bdot(qd[hh][r0:r0 + c, :], state[hh]) + jnp.dot(intra[hh][r0:r0 + c, :], v_rep,
                                                                  preferred_element_type=F32)
            state[hh] = state[hh] * jnp.exp(glast) + _bdot_tn(k[hh][r0:r0 + c, :] * jnp.exp(glast - gcn), v_new)
            on = o * lax.rsqrt(jnp.mean(o * o, axis=-1, keepdims=True) + RMS_EPS) * onorm_ref[...]
            o_ref[0, r0:r0 + c, lo:lo + hd] = (on * _silu(z[r0:r0 + c, lo:lo + hd])).astype(o_ref.dtype)
    for hh in heads:
        state_ref[hh] = state[hh]


def _gdn_consts(tm, conv_w, a_log, dt_bias, o_norm):
    nega = jnp.zeros((1, LANES), F32).at[0, :GDN_HEADS].set(-jnp.exp(a_log))
    dtb = jnp.zeros((1, LANES), F32).at[0, :GDN_HEADS].set(dt_bias)
    cwp = jnp.pad(conv_w, ((0, 8 - GDN_CONV), (0, 0)))
    ridx = np.arange(tm)
    same = (ridx[:, None] // GDN_CHUNK) == (ridx[None, :] // GDN_CHUNK)
    tril = jnp.asarray((same & (ridx[:, None] >= ridx[None, :])).astype(np.float32), BF16)
    triu = jnp.asarray((same & (ridx[:, None] <= ridx[None, :])).astype(np.float32), BF16)
    return [cwp, nega, dtb, o_norm[None, :], tril, triu]


def _router_kernel(a_ref, b_ref, wa_ref, wb_ref, h_ref, g_ref, wrh_ref, wrl_ref, br_ref, tri_ref,
                   h_o, xs_o, keyt_o, wt_o, cnt_o, sel_ref):
    tm = h_ref.shape[0]
    h = (h_ref[...] + jnp.dot(a_ref[...], wa_ref[...], preferred_element_type=F32)
         + jnp.dot(b_ref[...], wb_ref[...], preferred_element_type=F32))
    h_o[...] = h
    m = _rms(h, g_ref[...])
    mh = m.astype(BF16)
    ml = (m - mh.astype(F32)).astype(BF16)
    logits = (jnp.dot(mh, wrh_ref[...], preferred_element_type=F32) + jnp.dot(mh, wrl_ref[...], preferred_element_type=F32)
              + jnp.dot(ml, wrh_ref[...], preferred_element_type=F32)) + br_ref[...]
    lane = lax.broadcasted_iota(jnp.int32, (tm, LANES), 1)
    neg = -jnp.inf

    def first_argmax(x):
        mx = jnp.max(x, axis=-1, keepdims=True)
        idx = jnp.min(jnp.where(x == mx, lane, LANES), axis=-1, keepdims=True)
        return mx, idx

    is_g = (lane >= N_EXPERTS) & (lane < N_EXPERTS + MOE_GROUPS)
    gl = jnp.where(is_g, logits, neg)
    gmax, gidx = first_argmax(gl)
    g_w = 1.0 / jnp.sum(jnp.where(is_g, jnp.exp(gl - gmax), 0.0), axis=-1, keepdims=True)
    in_group = (lane // MOE_PER_GROUP) == (gidx - N_EXPERTS)
    el = jnp.where(in_group & (lane < N_EXPERTS), logits, neg)
    m1, i1 = first_argmax(el)
    el2 = jnp.where(lane == i1, neg, el)
    m2, i2 = first_argmax(el2)
    r = jnp.exp(m2 - m1)
    w1 = g_w / (1.0 + r)
    w2 = g_w * r / (1.0 + r)
    chose = (lane == i1) | (lane == i2)
    wmat = jnp.where(lane == i1, w1, jnp.where(lane == i2, w2, 0.0))
    ch = chose.astype(F32)
    rank = jnp.dot(tri_ref[...], ch.astype(BF16), preferred_element_type=F32)
    keyt = jnp.where(chose, rank, -1.0).T
    keyt_o[0] = keyt
    wt_o[0] = wmat.T
    cnt_o[0] = jnp.sum(ch, axis=0, keepdims=True).astype(jnp.int32)
    riota = lax.broadcasted_iota(jnp.int32, (MOE_CAP, tm), 0).astype(F32)
    for e in range(N_EXPERTS):
        sel_ref[e * MOE_CAP:(e + 1) * MOE_CAP, :] = jnp.where(keyt[e:e + 1, :] == riota, 1.0, 0.0).astype(BF16)
    xg = jnp.dot(sel_ref[...], mh, preferred_element_type=F32)
    xs_o[...] = xg.astype(BF16).reshape(xs_o.shape)


def _moe_router(a, bb, w_out, hf, norm_g, w_group, b_group, w_expert, b_expert, tb):
    n, d = hf.shape
    nblk = n // tb
    na, nb = a.shape[1], bb.shape[1]
    wa = w_out[:na].astype(BF16)
    wb = w_out[na:].astype(BF16)
    tok = lambda w: pl.BlockSpec((tb, w), lambda i: (i, 0))
    wr = jnp.zeros((d, LANES), F32).at[:, :N_EXPERTS].set(w_expert)
    wr = wr.at[:, N_EXPERTS:N_EXPERTS + MOE_GROUPS].set(w_group)
    wrh = wr.astype(BF16)
    wrl = (wr - wrh.astype(F32)).astype(BF16)
    br = jnp.zeros((1, LANES), F32).at[0, :N_EXPERTS].set(b_expert)
    br = br.at[0, N_EXPERTS:N_EXPERTS + MOE_GROUPS].set(b_group)
    tri = jnp.asarray(np.tril(np.ones((tb, tb), np.float32), -1), BF16)
    blk = pl.BlockSpec((1, LANES, tb), lambda i: (i, 0, 0))
    return pl.pallas_call(
        _router_kernel,
        grid=(nblk,),
        in_specs=[tok(na), tok(nb), _full(wa.shape), _full(wb.shape), tok(d), _full((1, d)), _full(wr.shape),
                  _full(wr.shape), _full(br.shape), _full(tri.shape)],
        out_specs=[tok(d), pl.BlockSpec((N_EXPERTS, MOE_CAP, d), lambda i: (0, i, 0)), blk, blk,
                   pl.BlockSpec((1, 1, LANES), lambda i: (i, 0, 0))],
        out_shape=[jax.ShapeDtypeStruct((n, d), F32),
                   jax.ShapeDtypeStruct((N_EXPERTS, nblk * MOE_CAP, d), BF16),
                   jax.ShapeDtypeStruct((nblk, LANES, tb), F32),
                   jax.ShapeDtypeStruct((nblk, LANES, tb), F32),
                   jax.ShapeDtypeStruct((nblk, 1, LANES), jnp.int32)],
        scratch_shapes=[pltpu.VMEM((N_EXPERTS * MOE_CAP, tb), BF16)],
        compiler_params=_params(("arbitrary",)),
        name="moe_router",
    )(a, bb, wa, wb, hf, norm_g[None, :], wrh, wrl, br, tri)


def _expert_mlp_kernel(x_ref, wg_ref, wu_ref, wd_ref, y_ref, wg_sc, wu_sc, wd_sc):
    @pl.when(pl.program_id(1) == 0)
    def _():
        wg_sc[...] = wg_ref[0, 0].astype(BF16)
        wu_sc[...] = wu_ref[0, 0].astype(BF16)
        wd_sc[...] = wd_ref[0, 0].astype(BF16)

    x = x_ref[0]
    hid = _silu(jnp.dot(x, wg_sc[...], preferred_element_type=F32)) * jnp.dot(
        x, wu_sc[...], preferred_element_type=F32)
    y_ref[0] = jnp.dot(hid.astype(BF16), wd_sc[...], preferred_element_type=F32).astype(BF16)


def _expert_mlp(xs, w_gate, w_up, w_down, layer):
    ne, rows, d = xs.shape
    ff = w_gate.shape[3]
    tr = _tile(rows, 1024)
    return pl.pallas_call(
        _expert_mlp_kernel,
        grid=(ne, rows // tr),
        in_specs=[pl.BlockSpec((1, tr, d), lambda e, i: (e, i, 0)),
                  pl.BlockSpec((1, 1, d, ff), lambda e, i: (layer, e, 0, 0)),
                  pl.BlockSpec((1, 1, d, ff), lambda e, i: (layer, e, 0, 0)),
                  pl.BlockSpec((1, 1, ff, d), lambda e, i: (layer, e, 0, 0))],
        out_specs=pl.BlockSpec((1, tr, d), lambda e, i: (e, i, 0)),
        out_shape=jax.ShapeDtypeStruct((ne, rows, d), BF16),
        scratch_shapes=[pltpu.VMEM((d, ff), BF16), pltpu.VMEM((d, ff), BF16), pltpu.VMEM((ff, d), BF16)],
        compiler_params=_params(("arbitrary", "arbitrary")),
        name="moe_expert_mlp",
    )(xs, w_gate, w_up, w_down)


def _ple_rows(h, p, g_ref, wg_ref, bg_ref, wp_ref):
    gate = jax.nn.sigmoid(_bdot(_rms(h, g_ref[...]), wg_ref[...]) + bg_ref[...])
    return h + gate * _bdot(p, wp_ref[...])


def _combine_ple_kernel(cnt_ref, y_ref, keyt_ref, wt_ref, h_ref, p_ref, gffn_ref, gple_ref, wgate_ref, bgate_ref,
                        wproj_ref, wg_hbm, wu_hbm, wd_hbm, o_ref, sel_ref, acc_ref, m_ref, wg_buf, wu_buf, wd_buf,
                        sems, *, layer):
    blk = pl.program_id(0)
    tb = h_ref.shape[0]
    riota = lax.broadcasted_iota(jnp.int32, (MOE_CAP, tb), 0).astype(F32)
    for e in range(N_EXPERTS):
        hit = keyt_ref[0, e:e + 1, :] == riota
        sel_ref[e * MOE_CAP:(e + 1) * MOE_CAP, :] = jnp.where(hit, wt_ref[0, e:e + 1, :], 0.0).astype(BF16)
    y = y_ref[...].reshape(N_EXPERTS * MOE_CAP, y_ref.shape[2])
    acc_ref[...] = h_ref[...] + lax.dot_general(sel_ref[...], y, (((0,), (0,)), ((), ())),
                                                preferred_element_type=F32)

    most = lax.fori_loop(0, N_EXPERTS, lambda e, mx: jnp.maximum(mx, cnt_ref[blk * LANES + e]), 0)

    @pl.when(most > MOE_CAP)
    def _():
        m_ref[...] = _rms(h_ref[...], gffn_ref[...]).astype(BF16)
        rows = lax.broadcasted_iota(jnp.int32, (MOE_ROWS, tb), 0).astype(F32)

        def expert(e, carry):
            extra = cnt_ref[blk * LANES + e] - MOE_CAP

            @pl.when(extra > 0)
            def _():
                copies = [pltpu.make_async_copy(src.at[layer, e], dst, sems.at[n])
                          for n, (src, dst) in enumerate(((wg_hbm, wg_buf), (wu_hbm, wu_buf), (wd_hbm, wd_buf)))]
                for c in copies:
                    c.start()
                for c in copies:
                    c.wait()
                krow = keyt_ref[0, pl.ds(e, 1), :]
                wrow = wt_ref[0, pl.ds(e, 1), :]

                def chunk(ci, c2):
                    hit = krow == (rows + (MOE_CAP + ci * MOE_ROWS).astype(F32))
                    sel = jnp.where(hit, 1.0, 0.0).astype(BF16)
                    xg = jnp.dot(sel, m_ref[...], preferred_element_type=F32).astype(BF16)
                    hid = _silu(_bdot(xg, wg_buf[...])) * _bdot(xg, wu_buf[...])
                    yo = _bdot(hid, wd_buf[...]).astype(BF16)
                    acc_ref[...] += _bdot_tn(jnp.where(hit, wrow, 0.0), yo)
                    return c2

                lax.fori_loop(0, (extra + MOE_ROWS - 1) // MOE_ROWS, chunk, 0)

            return carry

        lax.fori_loop(0, N_EXPERTS, expert, 0)

    o_ref[...] = _ple_rows(acc_ref[...], p_ref[0], gple_ref, wgate_ref, bgate_ref, wproj_ref)


def _moe_combine_ple(cnt, ys, keyt, wt, hf, norm_ffn, w_gate, w_up, w_down, layer, p_all, ple_norm, ple_w_gate,
                     ple_b_gate, ple_w_proj, tb):
    n, d = hf.shape
    ff = w_gate.shape[3]
    nblk = n // tb
    pd = p_all.shape[-1]
    blk = pl.BlockSpec((1, LANES, tb), lambda i, c: (i, 0, 0))
    tok = pl.BlockSpec((tb, d), lambda i, c: (i, 0))
    vec = pl.BlockSpec((1, d), lambda i, c: (0, 0))
    hbm = pl.BlockSpec(memory_space=pl.ANY)
    grid_spec = pltpu.PrefetchScalarGridSpec(
        num_scalar_prefetch=1,
        grid=(nblk,),
        in_specs=[pl.BlockSpec((N_EXPERTS, MOE_CAP, d), lambda i, c: (0, i, 0)), blk, blk, tok,
                  pl.BlockSpec((1, tb, pd), lambda i, c: (layer, i, 0)), vec, vec,
                  pl.BlockSpec((d, d), lambda i, c: (0, 0)), vec, pl.BlockSpec((pd, d), lambda i, c: (0, 0)),
                  hbm, hbm, hbm],
        out_specs=tok,
        scratch_shapes=[pltpu.VMEM((N_EXPERTS * MOE_CAP, tb), BF16), pltpu.VMEM((tb, d), F32),
                        pltpu.VMEM((tb, d), BF16), pltpu.VMEM((d, ff), F32), pltpu.VMEM((d, ff), F32),
                        pltpu.VMEM((ff, d), F32), pltpu.SemaphoreType.DMA((3,))],
    )
    return pl.pallas_call(
        functools.partial(_combine_ple_kernel, layer=layer),
        grid_spec=grid_spec,
        out_shape=jax.ShapeDtypeStruct((n, d), F32),
        compiler_params=_params(("arbitrary",)),
        name="moe_combine_ple",
    )(cnt, ys, keyt, wt, hf, p_all.reshape(p_all.shape[0], n, pd), norm_ffn[None, :], ple_norm[None, :],
      ple_w_gate.astype(BF16), ple_b_gate[None, :], ple_w_proj.astype(BF16), w_gate, w_up, w_down)


def _proj_moe_ple(mix_a, mix_b, w_out, h, norm_g, w_group, b_group, w_expert, b_expert, w_gate, w_up, w_down,
                  layer, p_all, ple_w_proj, ple_norm, ple_w_gate, ple_b_gate):
    b, s, d = h.shape
    n = b * s
    tb = _tile(n, MOE_TB)
    hf, xs, keyt, wt, cnt = _moe_router(mix_a.reshape(n, -1), mix_b.reshape(n, -1), w_out, h.reshape(n, d),
                                        norm_g, w_group, b_group, w_expert, b_expert, tb)
    ys = _expert_mlp(xs, w_gate, w_up, w_down, layer)
    out = _moe_combine_ple(cnt.reshape(-1), ys, keyt, wt, hf, norm_g, w_gate, w_up, w_down, layer, p_all,
                           ple_norm, ple_w_gate, ple_b_gate, ple_w_proj, tb)
    return out.reshape(b, s, d)


def _even_mixers(h, positions, norm_mix, w_in, b_f, fox_qn, fox_kn, q_a_norm, w_q_up, kv_a_norm, w_kv_up,
                 mla_qn, mla_kn):
    fq, fk, fv, cum, mq, mk, mv = _even_pre(h, positions, norm_mix, w_in, b_f, fox_qn, fox_kn, q_a_norm,
                                            w_q_up, kv_a_norm, w_kv_up, mla_qn, mla_kn)
    return _attention(fq, fk, fv, cum), _attention(mq, mk, mv)


def kernel(x, p, positions, norm_mix, norm_ffn, ev_w_in, fox_b_f, fox_q_norm, fox_k_norm, mla_q_a_norm, mla_w_q_up, mla_kv_a_norm, mla_w_kv_up, mla_q_norm, mla_k_norm, ev_w_out, od_w_in, s5_a_re, s5_a_im, s5_b_re, s5_b_im, s5_c_re, s5_c_im, s5_d, s5_log_step, s5_w_glu, s5_b_glu, gdn_conv_w, gdn_a_log, gdn_dt_bias, gdn_o_norm, od_w_out, moe_w_group, moe_b_group, moe_w_expert, moe_b_expert, moe_w_gate, moe_w_up, moe_w_down, ple_w_proj, ple_norm, ple_w_gate, ple_b_gate):
    h = x
    depth = p.shape[0]
    for i in range(depth):
        j = i // 2
        if i % 2 == 0:
            mix = _even_mixers(h, positions, norm_mix[i], ev_w_in[j], fox_b_f[j], fox_q_norm[j], fox_k_norm[j],
                               mla_q_a_norm[j], mla_w_q_up[j], mla_kv_a_norm[j], mla_w_kv_up[j], mla_q_norm[j],
                               mla_k_norm[j])
            w_out = ev_w_out[j]
        else:
            mix = _odd_mixers(h, norm_mix[i], od_w_in[j], s5_a_re[j], s5_a_im[j], s5_b_re[j], s5_b_im[j],
                              s5_c_re[j], s5_c_im[j], s5_d[j], s5_log_step[j], s5_w_glu[j], s5_b_glu[j],
                              gdn_conv_w[j], gdn_a_log[j], gdn_dt_bias[j], gdn_o_norm[j])
            w_out = od_w_out[j]
        h = _proj_moe_ple(mix[0], mix[1], w_out, h, norm_ffn[i], moe_w_group[i], moe_b_group[i], moe_w_expert[i],
                          moe_b_expert[i], moe_w_gate, moe_w_up, moe_w_down, i, p, ple_w_proj[i], ple_norm[i],
                          ple_w_gate[i], ple_b_gate[i])
    return h
```

```python
import functools
import math

import numpy as np
import jax
import jax.numpy as jnp
from jax import lax
from jax.experimental import pallas as pl
from jax.experimental.pallas import tpu as pltpu

F32 = jnp.float32
BF16 = jnp.bfloat16

LANES = 128
RMS_EPS = 1e-6
ROPE_THETA = 10000.0
LOG2E = math.log2(math.e)

FOX_HEADS = 8
FOX_HEAD_DIM = 64
MLA_HEADS = 8
MLA_Q_LORA = 384
MLA_KV_LORA = 256
MLA_NOPE = 64
MLA_ROPE = 32
MLA_V = 64
MLA_QK = MLA_NOPE + MLA_ROPE

S5_CH = 512
S5_GROUP_CH = 16
S5_GROUPS = S5_CH // S5_GROUP_CH
S5_STATE = 64
S5_N = S5_GROUPS * S5_STATE

GDN_HEADS = 4
GDN_HEAD_DIM = 128
GDN_W = GDN_HEADS * GDN_HEAD_DIM
GDN_CONV = 4
GDN_CHUNK = 64

MOE_GROUPS = 4
MOE_PER_GROUP = 8
N_EXPERTS = MOE_GROUPS * MOE_PER_GROUP
MOE_TB = 512
MOE_CAP = 64
MOE_ROWS = 128
ATTN_TQ = 2048
ATTN_TK = 512
MIX_TM = 256
EXPERT_TR = 1024

VMEM_LIMIT = 56 * 1024 * 1024


def _tile(n, pref):
    t = min(n, pref)
    assert n % t == 0, (n, t)
    return t


def _params(sem):
    return pltpu.CompilerParams(dimension_semantics=sem, vmem_limit_bytes=VMEM_LIMIT)


def _full(shape):
    nd = len(shape)
    return pl.BlockSpec(shape, lambda *_: (0,) * nd)


def _rms(x, g):
    return x * lax.rsqrt(jnp.mean(x * x, axis=-1, keepdims=True) + RMS_EPS) * g


def _bdot(a, b):
    return jnp.dot(a.astype(BF16), b.astype(BF16), preferred_element_type=F32)


def _bdot_nt(a, b):
    return lax.dot_general(a.astype(BF16), b.astype(BF16), (((1,), (1,)), ((), ())),
                           preferred_element_type=F32)


def _bdot_tn(a, b):
    return lax.dot_general(a.astype(BF16), b.astype(BF16), (((0,), (0,)), ((), ())),
                           preferred_element_type=F32)


def _split3(x):
    x1 = x.astype(BF16)
    r = x - x1.astype(F32)
    x2 = r.astype(BF16)
    return x1, x2, (r - x2.astype(F32)).astype(BF16)


def _sel_dot(sel, x):
    return sum(jnp.dot(sel, part, preferred_element_type=F32) for part in _split3(x))


def _dot_sel(x, sel):
    return sum(jnp.dot(part, sel, preferred_element_type=F32) for part in _split3(x))


def _split_dot(x, ind):
    hi = x.astype(BF16)
    lo = (x - hi.astype(F32)).astype(BF16)
    return (jnp.dot(hi, ind, preferred_element_type=F32)
            + jnp.dot(lo, ind, preferred_element_type=F32))


def _log_sigmoid(x):
    return jnp.minimum(x, 0.0) - jnp.log(1.0 + jnp.exp(-jnp.abs(x)))


def _softplus(x):
    return jnp.maximum(x, 0.0) + jnp.log(1.0 + jnp.exp(-jnp.abs(x)))


def _silu(x):
    return x * jax.nn.sigmoid(x)


def _head_norm128(x, nheads, denom, gain):
    outs = []
    for hh in range(nheads):
        xh = x[:, LANES * hh:LANES * (hh + 1)]
        ss = jnp.sum(xh * xh, axis=-1, keepdims=True)
        outs.append(xh * lax.rsqrt(ss / denom + RMS_EPS))
    return jnp.concatenate(outs, axis=1) * gain


def _even_pre_kernel(h_ref, pos_ref, nmix_ref, win_ref, ind_ref, fqn_ref, fkn_ref, bf_ref,
                     qan_ref, wq_ref, kvan_ref, wkv_ref, mqn_ref, mkn_ref, freq_ref, s1_ref, s2_ref,
                     tri_ref, vone_ref, fq_o, fk_o, fv_o, cum_o, mq_o, mk_o, mv_o, carry_ref):
    t = pl.program_id(1)

    @pl.when(t == 0)
    def _():
        carry_ref[...] = jnp.zeros_like(carry_ref)

    tm = h_ref.shape[1]
    a = _rms(h_ref[0], nmix_ref[...])
    proj = _bdot(a, win_ref[...])
    nf = FOX_HEADS * FOX_HEAD_DIM
    fq = proj[:, 0:nf]
    fk = proj[:, nf:2 * nf]
    nv = FOX_HEADS * LANES
    fv = proj[:, 2 * nf:2 * nf + nv]
    o_cq = 2 * nf + nv
    cq = proj[:, o_cq:o_cq + MLA_Q_LORA]
    o_ckv = o_cq + MLA_Q_LORA
    ckv = proj[:, o_ckv:o_ckv + MLA_KV_LORA]
    misc = proj[:, o_ckv + MLA_KV_LORA:]

    ind = ind_ref[...]
    fq_n = fq * lax.rsqrt(_split_dot(fq * fq, ind) / FOX_HEAD_DIM + RMS_EPS) * fqn_ref[...]
    fk_n = fk * lax.rsqrt(_split_dot(fk * fk, ind) / FOX_HEAD_DIM + RMS_EPS) * fkn_ref[...]
    fq_o[0] = (fq_n * (FOX_HEAD_DIM ** -0.5 * LOG2E)).astype(BF16)
    fk_o[0] = fk_n.astype(BF16)
    fv_o[0] = (fv + vone_ref[...]).astype(BF16)

    lane = lax.broadcasted_iota(jnp.int32, (tm, LANES), 1)
    logf = jnp.where(lane < FOX_HEADS, _log_sigmoid(misc + bf_ref[...]), 0.0)
    cum = _sel_dot(tri_ref[...], logf) + carry_ref[...]
    carry_ref[...] = cum[tm - 1:tm, :]
    cum_o[0] = (cum * LOG2E).T[:FOX_HEADS, :]

    ang = pos_ref[0].astype(F32) * freq_ref[...]
    cos1 = jnp.cos(ang)
    sin1 = jnp.sin(ang)
    cos = jnp.concatenate([cos1] * MLA_HEADS, axis=1)
    sin_a = jnp.concatenate([sin1 * s1_ref[...]] * MLA_HEADS, axis=1)
    sin_b = jnp.concatenate([sin1 * s2_ref[...]] * MLA_HEADS, axis=1)
    width = MLA_HEADS * LANES
    half = MLA_ROPE // 2

    def rope(x):
        return (x * cos + pltpu.roll(x, width - half, 1) * sin_a + pltpu.roll(x, half, 1) * sin_b)

    q = _bdot(_rms(cq, qan_ref[...]), wq_ref[...])
    q = rope(_head_norm128(q, MLA_HEADS, MLA_QK, mqn_ref[...]))
    mq_o[0] = (q * (MLA_QK ** -0.5 * LOG2E)).astype(BF16)

    kv = _bdot(_rms(ckv, kvan_ref[...]), wkv_ref[...])
    kr = pltpu.roll(misc, MLA_NOPE - FOX_HEADS, 1)
    kr = jnp.where((lane >= MLA_NOPE) & (lane < MLA_QK), kr, 0.0)
    k = kv[:, :width] + jnp.concatenate([kr] * MLA_HEADS, axis=1)
    k = rope(_head_norm128(k, MLA_HEADS, MLA_QK, mkn_ref[...]))
    mk_o[0] = k.astype(BF16)
    mv_o[0] = (kv[:, width:] + vone_ref[...]).astype(BF16)


def _even_pre(h, positions, norm_mix, w_in, b_f, fox_qn, fox_kn, q_a_norm, w_q_up, kv_a_norm, w_kv_up,
              mla_qn, mla_kn):
    b, s, d = h.shape
    tm = _tile(s, MIX_TM)
    nf = FOX_HEADS * FOX_HEAD_DIM
    sizes = (nf, nf, nf, FOX_HEADS, MLA_Q_LORA, MLA_KV_LORA, MLA_ROPE)
    offs = np.concatenate([[0], np.cumsum(sizes)])
    parts = [w_in[:, offs[i]:offs[i + 1]] for i in range(len(sizes))]
    pad = jnp.zeros((d, LANES - FOX_HEADS - MLA_ROPE), w_in.dtype)
    slot_pad = ((0, 0), (0, 0), (0, LANES - FOX_HEAD_DIM))
    wfv = jnp.pad(parts[2].reshape(d, FOX_HEADS, FOX_HEAD_DIM), slot_pad).reshape(d, FOX_HEADS * LANES)
    win = jnp.concatenate([parts[0], parts[1], wfv, parts[4], parts[5], parts[3], parts[6], pad],
                          axis=1).astype(BF16)
    gidx = np.arange(nf) // FOX_HEAD_DIM
    ind = jnp.asarray(gidx[:, None] == gidx[None, :], BF16)
    fqn = jnp.tile(fox_qn, FOX_HEADS)[None, :]
    fkn = jnp.tile(fox_kn, FOX_HEADS)[None, :]
    bf = jnp.zeros((1, LANES), F32).at[0, :FOX_HEADS].set(b_f)
    padq = LANES - MLA_QK
    wq = jnp.pad(w_q_up.reshape(MLA_Q_LORA, MLA_HEADS, MLA_QK), ((0, 0), (0, 0), (0, padq)))
    wq = wq.reshape(MLA_Q_LORA, MLA_HEADS * LANES).astype(BF16)
    wkv3 = w_kv_up.reshape(MLA_KV_LORA, MLA_HEADS, MLA_NOPE + MLA_V)
    wk = jnp.pad(wkv3[:, :, :MLA_NOPE], ((0, 0), (0, 0), (0, LANES - MLA_NOPE)))
    wv = jnp.pad(wkv3[:, :, MLA_NOPE:], ((0, 0), (0, 0), (0, LANES - MLA_V)))
    wkv = jnp.concatenate([wk.reshape(MLA_KV_LORA, MLA_HEADS * LANES),
                           wv.reshape(MLA_KV_LORA, MLA_HEADS * LANES)], axis=1).astype(BF16)
    vone = jnp.tile(jnp.zeros((LANES,), F32).at[MLA_V].set(1.0), MLA_HEADS)[None, :]
    mqn = jnp.tile(jnp.pad(mla_qn, (0, padq)), MLA_HEADS)[None, :]
    mkn = jnp.tile(jnp.pad(mla_kn, (0, padq)), MLA_HEADS)[None, :]
    half = MLA_ROPE // 2
    inv = ROPE_THETA ** (-jnp.arange(half, dtype=F32) * 2.0 / MLA_ROPE)
    freq = jnp.zeros((1, LANES), F32).at[0, MLA_NOPE:MLA_NOPE + half].set(inv)
    freq = freq.at[0, MLA_NOPE + half:MLA_QK].set(inv)
    s1 = jnp.zeros((1, LANES), F32).at[0, MLA_NOPE:MLA_NOPE + half].set(-1.0)
    s2 = jnp.zeros((1, LANES), F32).at[0, MLA_NOPE + half:MLA_QK].set(1.0)
    tri = jnp.asarray(np.tril(np.ones((tm, tm), np.float32)), BF16)
    pos3 = positions.reshape(b, s, 1)

    row = lambda n: pl.BlockSpec((1, tm, n), lambda bi, ti: (bi, ti, 0))
    consts = [norm_mix[None, :], win, ind, fqn, fkn, bf, q_a_norm[None, :], wq, kv_a_norm[None, :], wkv,
              mqn, mkn, freq, s1, s2, tri, vone]
    nv = FOX_HEADS * LANES
    out_shape = [jax.ShapeDtypeStruct((b, s, nf), BF16)] * 2 + [jax.ShapeDtypeStruct((b, s, nv), BF16)] + [
        jax.ShapeDtypeStruct((b, FOX_HEADS, s), F32),
        jax.ShapeDtypeStruct((b, s, MLA_HEADS * LANES), BF16),
        jax.ShapeDtypeStruct((b, s, MLA_HEADS * LANES), BF16),
        jax.ShapeDtypeStruct((b, s, MLA_HEADS * LANES), BF16)]
    return pl.pallas_call(
        _even_pre_kernel,
        grid=(b, s // tm),
        in_specs=[row(d), row(1)] + [_full(c.shape) for c in consts],
        out_specs=[row(nf), row(nf), row(nv), pl.BlockSpec((1, FOX_HEADS, tm), lambda bi, ti: (bi, 0, ti)),
                   row(MLA_HEADS * LANES), row(MLA_HEADS * LANES), row(MLA_HEADS * LANES)],
        out_shape=out_shape,
        scratch_shapes=[pltpu.VMEM((1, LANES), F32)],
        compiler_params=_params(("arbitrary", "arbitrary")),
        name="even_pre",
    )(h, pos3, *consts)


def _attn_kernel(*refs, tq, tk, fox):
    if fox:
        q_ref, k_ref, v_ref, cr_ref, o_ref = refs
    else:
        q_ref, k_ref, v_ref, o_ref = refs
    hp = pl.program_id(1)
    i = pl.program_id(2)
    lane = lax.broadcasted_iota(jnp.int32, (tq, LANES), 1)
    qs = []
    for hh in range(2):
        if fox:
            in_head = (lane >= FOX_HEAD_DIM * hh) & (lane < FOX_HEAD_DIM * (hh + 1))
            qs.append(jnp.where(in_head, q_ref[0], jnp.zeros((), BF16)))
        else:
            qs.append(q_ref[0, :, LANES * hh:LANES * (hh + 1)])

    def step(j, carry, lo=None):
        koff = pl.multiple_of(j * tk, tk)
        top = 0 if lo is None else lo
        new = []
        for hh in range(2):
            m, acc = carry[hh]
            if fox:
                kj = k_ref[0, pl.ds(koff, tk), :]
            else:
                kj = k_ref[0, pl.ds(koff, tk), LANES * hh:LANES * (hh + 1)]
            sc = lax.dot_general(qs[hh][top:], kj, (((1,), (1,)), ((), ())), preferred_element_type=F32)
            if fox:
                sc = sc - cr_ref[0, pl.ds(2 * hp + hh, 1), pl.ds(koff, tk)]
            if lo is not None:
                rowi = lax.broadcasted_iota(jnp.int32, sc.shape, 0)
                coli = lax.broadcasted_iota(jnp.int32, sc.shape, 1)
                sc = jnp.where(coli <= rowi, sc, -jnp.inf)
            m_new = jnp.maximum(m[top:], jnp.max(sc, axis=-1, keepdims=True))
            alpha = jnp.exp2(m[top:] - m_new)
            p = jnp.exp2((sc - jnp.concatenate([m_new] * (tk // LANES), axis=1)).astype(BF16))
            vj = v_ref[0, pl.ds(koff, tk), LANES * hh:LANES * (hh + 1)]
            acc_new = alpha * acc[top:] + jnp.dot(p, vj, preferred_element_type=F32)
            if top:
                m_new = jnp.concatenate([m[:top], m_new], axis=0)
                acc_new = jnp.concatenate([acc[:top], acc_new], axis=0)
            new.append((m_new, acc_new))
        return tuple(new)

    def body(jj, carry):
        for r in range(ratio):
            carry = step(jj * ratio + r, carry)
        return carry

    one = (jnp.full((tq, LANES), -jnp.inf, F32), jnp.zeros((tq, LANES), F32))
    ratio = tq // tk
    carry = lax.fori_loop(0, i, body, (one, one))
    for r in range(ratio):
        carry = step(i * ratio + r, carry, lo=r * tk)
    outs = [acc / acc[:, MLA_V:MLA_V + 1] for _, acc in carry]
    o_ref[0] = jnp.where(lane < MLA_V, outs[0], pltpu.roll(outs[1], MLA_V, 1)).astype(o_ref.dtype)


def _attention(q, k, v, cum_row=None):
    b, s, _ = v.shape
    fox = cum_row is not None
    qw = LANES if fox else 2 * LANES
    tq = _tile(s, ATTN_TQ)
    tk = _tile(tq, ATTN_TK)
    npairs = v.shape[2] // (2 * LANES)
    in_specs = [pl.BlockSpec((1, tq, qw), lambda bi, hp, i: (bi, i, hp)),
                pl.BlockSpec((1, s, qw), lambda bi, hp, i: (bi, 0, hp)),
                pl.BlockSpec((1, s, 2 * LANES), lambda bi, hp, i: (bi, 0, hp))]
    args = [q, k, v]
    if fox:
        in_specs += [pl.BlockSpec((1, FOX_HEADS, s), lambda bi, hp, i: (bi, 0, 0))]
        args += [cum_row]
    return pl.pallas_call(
        functools.partial(_attn_kernel, tq=tq, tk=tk, fox=fox),
        grid=(b, npairs, s // tq),
        in_specs=in_specs,
        out_specs=pl.BlockSpec((1, tq, LANES), lambda bi, hp, i: (bi, i, hp)),
        out_shape=jax.ShapeDtypeStruct((b, s, npairs * LANES), BF16),
        compiler_params=_params(("arbitrary", "arbitrary", "arbitrary")),
        name="fox_attention" if fox else "mla_attention",
    )(*args)


N_S5_CONSTS = 13
N_GDN_CONSTS = 6


def _odd_kernel(h_ref, nmix_ref, win_ref, *refs):
    s5_consts = refs[:N_S5_CONSTS]
    gdn_consts = refs[N_S5_CONSTS:N_S5_CONSTS + N_GDN_CONSTS]
    y_o, o_o, x_ref, sr_ref, si_ref, xpad_ref, state_ref = refs[N_S5_CONSTS + N_GDN_CONSTS:]
    a = _rms(h_ref[0], nmix_ref[...])
    proj = _bdot(a, win_ref[...])
    o1 = S5_CH
    o2 = o1 + 3 * GDN_W
    o3 = o2 + GDN_W
    _s5_body(proj[:, :o1], *s5_consts, y_o, x_ref, sr_ref, si_ref)
    _gdn_body(proj[:, o1:o2], proj[:, o2:o3], proj[:, o3:], *gdn_consts, o_o, xpad_ref, state_ref)


def _odd_mixers(h, norm_mix, w_in, a_re, a_im, b_re, b_im, c_re, c_im, d_skip, log_step, w_glu, b_glu,
                conv_w, a_log, dt_bias, o_norm):
    b, s, d = h.shape
    tm = _tile(s, MIX_TM)
    sizes = (S5_CH, 3 * GDN_W, GDN_HEADS, GDN_HEADS, GDN_W)
    offs = np.concatenate([[0], np.cumsum(sizes)])
    parts = [w_in[:, offs[i]:offs[i + 1]] for i in range(len(sizes))]
    pad = jnp.zeros((d, LANES - 2 * GDN_HEADS), w_in.dtype)
    win = jnp.concatenate([parts[0], parts[1], parts[4], parts[2], parts[3], pad], axis=1).astype(BF16)
    s5_consts = _s5_consts(tm, a_re, a_im, b_re, b_im, c_re, c_im, d_skip, log_step, w_glu, b_glu)
    gdn_consts = _gdn_consts(tm, conv_w, a_log, dt_bias, o_norm)
    assert len(s5_consts) == N_S5_CONSTS and len(gdn_consts) == N_GDN_CONSTS
    consts = [norm_mix[None, :], win] + s5_consts + gdn_consts
    row = lambda n: pl.BlockSpec((1, tm, n), lambda bi, ti: (bi, ti, 0))
    return pl.pallas_call(
        _odd_kernel,
        grid=(b, s // tm),
        in_specs=[row(d)] + [_full(c.shape) for c in consts],
        out_specs=[row(S5_CH), row(GDN_W)],
        out_shape=[jax.ShapeDtypeStruct((b, s, S5_CH), BF16), jax.ShapeDtypeStruct((b, s, GDN_W), BF16)],
        scratch_shapes=[pltpu.VMEM((tm, 2 * S5_N), F32), pltpu.VMEM((1, S5_N), F32), pltpu.VMEM((1, S5_N), F32),
                        pltpu.VMEM((tm + 8, 3 * GDN_W), F32),
                        pltpu.VMEM((GDN_HEADS, GDN_HEAD_DIM, GDN_HEAD_DIM), F32)],
        compiler_params=_params(("arbitrary", "arbitrary")),
        name="odd_mixers",
    )(h, *consts)


def _s5_body(u, perm_ref, unperm_ref, bbd_ref, cbd_ref, ar_ref, ai_ref, asr_ref, asi_ref, pwr_ref, pwi_ref,
             d_ref, wglu_ref, bglu_ref, o_ref, x_ref, sr_ref, si_ref):
    t = pl.program_id(1)

    @pl.when(t == 0)
    def _():
        sr_ref[...] = jnp.zeros_like(sr_ref)
        si_ref[...] = jnp.zeros_like(si_ref)

    tm = u.shape[0]
    nseg = 8
    seg = tm // nseg
    u = _sel_dot(perm_ref[...], u)
    hc = S5_CH // 2
    hn = S5_N // 2
    ub = u.astype(BF16)
    for part in range(2):
        for base in (0, S5_N):
            cols = slice(base + part * hn, base + (part + 1) * hn)
            x_ref[:, cols] = jnp.dot(ub[:, part * hc:(part + 1) * hc], bbd_ref[part * hc:(part + 1) * hc, cols],
                                     preferred_element_type=F32)
    ar = ar_ref[...]
    ai = ai_ref[...]
    re = slice(0, S5_N)
    im = slice(S5_N, 2 * S5_N)

    def local(i, carry):
        xr, xi = carry
        rows = pl.ds(pl.multiple_of(i * nseg, nseg), nseg)
        nr = ar * xr - ai * xi + x_ref[rows, re]
        ni = ar * xi + ai * xr + x_ref[rows, im]
        x_ref[rows, re] = nr
        x_ref[rows, im] = ni
        return nr, ni

    zero = jnp.zeros((nseg, S5_N), F32)
    er, ei = lax.fori_loop(0, seg, local, (zero, zero), unroll=4)

    asr = asr_ref[...]
    asi = asi_ref[...]
    cr = [sr_ref[...]]
    ci = [si_ref[...]]
    for s in range(nseg):
        cr.append(asr * cr[s] - asi * ci[s] + er[s:s + 1, :])
        ci.append(asr * ci[s] + asi * cr[s] + ei[s:s + 1, :])
    sr_ref[...] = cr[nseg]
    si_ref[...] = ci[nseg]
    ent_r = jnp.concatenate(cr[:nseg], axis=0)
    ent_i = jnp.concatenate(ci[:nseg], axis=0)

    def fix(i, c):
        rows = pl.ds(pl.multiple_of(i * nseg, nseg), nseg)
        pr = pwr_ref[pl.ds(i, 1), :]
        pi = pwi_ref[pl.ds(i, 1), :]
        x_ref[rows, re] += pr * ent_r - pi * ent_i
        x_ref[rows, im] += pr * ent_i + pi * ent_r
        return c

    lax.fori_loop(0, seg, fix, 0, unroll=4)
    ys = []
    for part in range(2):
        oc = slice(part * hc, (part + 1) * hc)
        acc = None
        for base in (0, S5_N):
            rows = slice(base + part * hn, base + (part + 1) * hn)
            term = _bdot(x_ref[:, rows], cbd_ref[rows, oc])
            acc = term if acc is None else acc + term
        ys.append(acc)
    y = jnp.concatenate(ys, axis=1) + d_ref[...] * u
    hg = jax.nn.gelu(y)
    out = (hg * jax.nn.sigmoid(_bdot(hg, wglu_ref[...]) + bglu_ref[...])).astype(BF16)
    o_ref[0] = jnp.dot(unperm_ref[...], out, preferred_element_type=F32).astype(o_ref.dtype)


def _s5_consts(tm, a_re, a_im, b_re, b_im, c_re, c_im, d_skip, log_step, w_glu, b_glu):
    lam_re = jnp.minimum(a_re, -1e-4)
    lam_im = a_im
    dt = jnp.exp(log_step)[:, None]
    mag = jnp.exp(lam_re * dt)
    ab_re = mag * jnp.cos(lam_im * dt)
    ab_im = mag * jnp.sin(lam_im * dt)
    den = lam_re * lam_re + lam_im * lam_im
    nr, ni = ab_re - 1.0, ab_im
    gam_re = (nr * lam_re + ni * lam_im) / den
    gam_im = (ni * lam_re - nr * lam_im) / den
    bb_re = gam_re[..., None] * b_re - gam_im[..., None] * b_im
    bb_im = gam_re[..., None] * b_im + gam_im[..., None] * b_re
    eye = jnp.eye(S5_GROUPS, dtype=F32)
    bd_in = lambda m: jnp.einsum('gpc,gh->gchp', m, eye).reshape(S5_CH, S5_N)
    bd_out = lambda m: jnp.einsum('gcp,gh->gphc', m, eye).reshape(S5_N, S5_CH)
    bbd = jnp.concatenate([bd_in(bb_re), bd_in(bb_im)], axis=1).astype(BF16)
    cbd = jnp.concatenate([bd_out(c_re), -bd_out(c_im)], axis=0).astype(BF16)
    seg = tm // 8
    steps = jnp.arange(1, seg + 1, dtype=F32)[:, None, None] * dt[None]
    pmag = jnp.exp(lam_re[None] * steps)
    pw_re = (pmag * jnp.cos(lam_im[None] * steps)).reshape(seg, S5_N)
    pw_im = (pmag * jnp.sin(lam_im[None] * steps)).reshape(seg, S5_N)
    src = (np.arange(tm) % 8) * seg + np.arange(tm) // 8
    perm = np.zeros((tm, tm), np.float32)
    perm[np.arange(tm), src] = 1.0
    return [jnp.asarray(perm, BF16), jnp.asarray(perm.T, BF16),
            bbd, cbd, ab_re.reshape(1, S5_N), ab_im.reshape(1, S5_N), pw_re[seg - 1:seg], pw_im[seg - 1:seg],
            pw_re, pw_im, d_skip[None, :], w_glu.astype(BF16), b_glu[None, :]]


def _gdn_body(x, z, gb, cw_ref, nega_ref, dtb_ref, onorm_ref, tril_ref, triu_ref, o_ref, xpad_ref, state_ref):
    t = pl.program_id(1)
    tm = x.shape[0]
    c = GDN_CHUNK
    hd = GDN_HEAD_DIM

    @pl.when(t == 0)
    def _():
        xpad_ref[0:8, :] = jnp.zeros((8, xpad_ref.shape[1]), F32)
        state_ref[...] = jnp.zeros_like(state_ref)

    @pl.when(t > 0)
    def _():
        xpad_ref[0:8, :] = xpad_ref[tm:tm + 8, :]

    xpad_ref[8:tm + 8, :] = x
    conv = cw_ref[0:1, :] * xpad_ref[pl.ds(8 - (GDN_CONV - 1), tm), :]
    for i in range(1, GDN_CONV):
        conv = conv + cw_ref[i:i + 1, :] * xpad_ref[pl.ds(8 - (GDN_CONV - 1) + i, tm), :]
    act = _silu(conv)

    def l2n(x):
        return x * lax.rsqrt(jnp.sum(x * x, axis=-1, keepdims=True) + RMS_EPS)

    g = nega_ref[...] * _softplus(gb + dtb_ref[...])
    beta = jax.nn.sigmoid(gb)
    gc = _sel_dot(tril_ref[...], g)
    gct = _dot_sel(g.T, triu_ref[...])

    ri = lax.broadcasted_iota(jnp.int32, (tm, tm), 0)
    ci = lax.broadcasted_iota(jnp.int32, (tm, tm), 1)
    same = (ri // c) == (ci // c)
    incl = same & (ri >= ci)
    strict = same & (ri > ci)
    eye = (ri == ci).astype(F32)
    offs = []
    bs = 1
    while bs < c:
        offs.append(((ri // (2 * bs)) == (ci // (2 * bs))) & ((ri % (2 * bs)) >= bs) & ((ci % (2 * bs)) < bs))
        bs *= 2
    nchunks = tm // c

    heads = range(GDN_HEADS)
    q = [l2n(act[:, hh * hd:(hh + 1) * hd]) * (hd ** -0.5) for hh in heads]
    k = [l2n(act[:, GDN_W + hh * hd:GDN_W + (hh + 1) * hd]) for hh in heads]
    v = [act[:, 2 * GDN_W + hh * hd:2 * GDN_W + (hh + 1) * hd] for hh in heads]
    bcol = [beta[:, GDN_HEADS + hh:GDN_HEADS + hh + 1] for hh in heads]
    gcol = [gc[:, hh:hh + 1] for hh in heads]
    decay = [jnp.where(incl, jnp.exp(jnp.where(incl, gcol[hh] - gct[hh:hh + 1, :], 0.0)), 0.0) for hh in heads]
    kb = [k[hh] * bcol[hh] for hh in heads]
    a_mat = [jnp.where(strict, _bdot_nt(kb[hh], k[hh]) * decay[hh], 0.0) for hh in heads]
    t_mat = [eye - jnp.where(offs[0], a_mat[hh], 0.0) for hh in heads]
    for off in offs[1:]:
        pa = [_bdot(t_mat[hh], jnp.where(off, a_mat[hh], 0.0)) for hh in heads]
        t_mat = [t_mat[hh] - _bdot(pa[hh], t_mat[hh]) for hh in heads]
    th = [t_mat[hh].astype(BF16) for hh in heads]
    tl = [(t_mat[hh] - th[hh].astype(F32)).astype(BF16) for hh in heads]
    ah = [a_mat[hh].astype(BF16) for hh in heads]
    al = [(a_mat[hh] - ah[hh].astype(F32)).astype(BF16) for hh in heads]
    a_t = [jnp.dot(ah[hh], th[hh], preferred_element_type=F32) + jnp.dot(ah[hh], tl[hh], preferred_element_type=F32)
           + jnp.dot(al[hh], th[hh], preferred_element_type=F32) for hh in heads]
    t_mat = [t_mat[hh] + jnp.dot(th[hh], (eye - t_mat[hh] - a_t[hh]).astype(BF16), preferred_element_type=F32)
             for hh in heads]
    eg = [jnp.exp(gcol[hh]) for hh in heads]
    u = [_bdot(t_mat[hh], v[hh] * bcol[hh]) for hh in heads]
    w = [_bdot(t_mat[hh], kb[hh] * eg[hh]) for hh in heads]
    intra = [jnp.where(incl, _bdot_nt(q[hh], k[hh]) * decay[hh], 0.0).astype(BF16) for hh in heads]
    qd = [q[hh] * eg[hh] for hh in heads]
    state = [state_ref[hh] for hh in heads]
    for n in range(nchunks):
        r0 = n * c
        for hh in heads:
            lo = hh * hd
            gcn = gcol[hh][r0:r0 + c, :]
            glast = gcol[hh][r0 + c - 1:r0 + c, :]
            v_new = u[hh][r0:r0 + c, :] - _bdot(w[hh][r0:r0 + c, :], state[hh])
            v_rep = jnp.concatenate([v_new.astype(BF16)] * nchunks, axis=0)
            o = _bdot(qd[hh][r0:r0 + c, :], state[hh]) + jnp.dot(intra[hh][r0:r0 + c, :], v_rep,
                                                                  preferred_element_type=F32)
            state[hh] = state[hh] * jnp.exp(glast) + _bdot_tn(k[hh][r0:r0 + c, :] * jnp.exp(glast - gcn), v_new)
            on = o * lax.rsqrt(jnp.mean(o * o, axis=-1, keepdims=True) + RMS_EPS) * onorm_ref[...]
            o_ref[0, r0:r0 + c, lo:lo + hd] = (on * _silu(z[r0:r0 + c, lo:lo + hd])).astype(o_ref.dtype)
    for hh in heads:
        state_ref[hh] = state[hh]


def _gdn_consts(tm, conv_w, a_log, dt_bias, o_norm):
    nega = jnp.zeros((1, LANES), F32).at[0, :GDN_HEADS].set(-jnp.exp(a_log))
    dtb = jnp.zeros((1, LANES), F32).at[0, :GDN_HEADS].set(dt_bias)
    cwp = jnp.pad(conv_w, ((0, 8 - GDN_CONV), (0, 0)))
    ridx = np.arange(tm)
    same = (ridx[:, None] // GDN_CHUNK) == (ridx[None, :] // GDN_CHUNK)
    tril = jnp.asarray((same & (ridx[:, None] >= ridx[None, :])).astype(np.float32), BF16)
    triu = jnp.asarray((same & (ridx[:, None] <= ridx[None, :])).astype(np.float32), BF16)
    return [cwp, nega, dtb, o_norm[None, :], tril, triu]


def _router_kernel(a_ref, b_ref, wa_ref, wb_ref, h_ref, g_ref, wrh_ref, wrl_ref, br_ref, tri_ref,
                   h_o, xs_o, keyt_o, wt_o, cnt_o, sel_ref):
    tm = h_ref.shape[0]
    h = (h_ref[...] + jnp.dot(a_ref[...], wa_ref[...], preferred_element_type=F32)
         + jnp.dot(b_ref[...], wb_ref[...], preferred_element_type=F32))
    h_o[...] = h
    m = _rms(h, g_ref[...])
    mh = m.astype(BF16)
    ml = (m - mh.astype(F32)).astype(BF16)
    logits = (jnp.dot(mh, wrh_ref[...], preferred_element_type=F32) + jnp.dot(mh, wrl_ref[...], preferred_element_type=F32)
              + jnp.dot(ml, wrh_ref[...], preferred_element_type=F32)) + br_ref[...]
    lane = lax.broadcasted_iota(jnp.int32, (tm, LANES), 1)
    neg = -jnp.inf

    def first_argmax(x):
        mx = jnp.max(x, axis=-1, keepdims=True)
        idx = jnp.min(jnp.where(x == mx, lane, LANES), axis=-1, keepdims=True)
        return mx, idx

    is_g = (lane >= N_EXPERTS) & (lane < N_EXPERTS + MOE_GROUPS)
    gl = jnp.where(is_g, logits, neg)
    gmax, gidx = first_argmax(gl)
    g_w = 1.0 / jnp.sum(jnp.where(is_g, jnp.exp(gl - gmax), 0.0), axis=-1, keepdims=True)
    in_group = (lane // MOE_PER_GROUP) == (gidx - N_EXPERTS)
    el = jnp.where(in_group & (lane < N_EXPERTS), logits, neg)
    m1, i1 = first_argmax(el)
    el2 = jnp.where(lane == i1, neg, el)
    m2, i2 = first_argmax(el2)
    r = jnp.exp(m2 - m1)
    w1 = g_w / (1.0 + r)
    w2 = g_w * r / (1.0 + r)
    chose = (lane == i1) | (lane == i2)
    wmat = jnp.where(lane == i1, w1, jnp.where(lane == i2, w2, 0.0))
    ch = chose.astype(F32)
    rank = jnp.dot(tri_ref[...], ch.astype(BF16), preferred_element_type=F32)
    keyt = jnp.where(chose, rank, -1.0).T
    keyt_o[0] = keyt
    wt_o[0] = wmat.T
    cnt_o[0] = jnp.sum(ch, axis=0, keepdims=True).astype(jnp.int32)
    riota = lax.broadcasted_iota(jnp.int32, (MOE_CAP, tm), 0).astype(F32)
    for e in range(N_EXPERTS):
        sel_ref[e * MOE_CAP:(e + 1) * MOE_CAP, :] = jnp.where(keyt[e:e + 1, :] == riota, 1.0, 0.0).astype(BF16)
    xg = jnp.dot(sel_ref[...], mh, preferred_element_type=F32)
    xs_o[...] = xg.astype(BF16).reshape(xs_o.shape)


def _moe_router(a, bb, w_out, hf, norm_g, w_group, b_group, w_expert, b_expert, tb):
    n, d = hf.shape
    nblk = n // tb
    na, nb = a.shape[1], bb.shape[1]
    wa = w_out[:na].astype(BF16)
    wb = w_out[na:].astype(BF16)
    tok = lambda w: pl.BlockSpec((tb, w), lambda i: (i, 0))
    wr = jnp.zeros((d, LANES), F32).at[:, :N_EXPERTS].set(w_expert)
    wr = wr.at[:, N_EXPERTS:N_EXPERTS + MOE_GROUPS].set(w_group)
    wrh = wr.astype(BF16)
    wrl = (wr - wrh.astype(F32)).astype(BF16)
    br = jnp.zeros((1, LANES), F32).at[0, :N_EXPERTS].set(b_expert)
    br = br.at[0, N_EXPERTS:N_EXPERTS + MOE_GROUPS].set(b_group)
    tri = jnp.asarray(np.tril(np.ones((tb, tb), np.float32), -1), BF16)
    blk = pl.BlockSpec((1, LANES, tb), lambda i: (i, 0, 0))
    return pl.pallas_call(
        _router_kernel,
        grid=(nblk,),
        in_specs=[tok(na), tok(nb), _full(wa.shape), _full(wb.shape), tok(d), _full((1, d)), _full(wr.shape),
                  _full(wr.shape), _full(br.shape), _full(tri.shape)],
        out_specs=[tok(d), pl.BlockSpec((N_EXPERTS, MOE_CAP, d), lambda i: (0, i, 0)), blk, blk,
                   pl.BlockSpec((1, 1, LANES), lambda i: (i, 0, 0))],
        out_shape=[jax.ShapeDtypeStruct((n, d), F32),
                   jax.ShapeDtypeStruct((N_EXPERTS, nblk * MOE_CAP, d), BF16),
                   jax.ShapeDtypeStruct((nblk, LANES, tb), F32),
                   jax.ShapeDtypeStruct((nblk, LANES, tb), F32),
                   jax.ShapeDtypeStruct((nblk, 1, LANES), jnp.int32)],
        scratch_shapes=[pltpu.VMEM((N_EXPERTS * MOE_CAP, tb), BF16)],
        compiler_params=_params(("arbitrary",)),
        name="moe_router",
    )(a, bb, wa, wb, hf, norm_g[None, :], wrh, wrl, br, tri)


def _expert_mlp_kernel(x_ref, wg_ref, wu_ref, wd_ref, y_ref, wg_sc, wu_sc, wd_sc):
    @pl.when(pl.program_id(1) == 0)
    def _():
        wg_sc[...] = wg_ref[0, 0].astype(BF16)
        wu_sc[...] = wu_ref[0, 0].astype(BF16)
        wd_sc[...] = wd_ref[0, 0].astype(BF16)

    x = x_ref[0]
    hid = _silu(jnp.dot(x, wg_sc[...], preferred_element_type=F32)) * jnp.dot(
        x, wu_sc[...], preferred_element_type=F32)
    y_ref[0] = jnp.dot(hid.astype(BF16), wd_sc[...], preferred_element_type=F32).astype(BF16)


def _expert_mlp(xs, w_gate, w_up, w_down, layer):
    ne, rows, d = xs.shape
    ff = w_gate.shape[3]
    tr = _tile(rows, EXPERT_TR)
    return pl.pallas_call(
        _expert_mlp_kernel,
        grid=(ne, rows // tr),
        in_specs=[pl.BlockSpec((1, tr, d), lambda e, i: (e, i, 0)),
                  pl.BlockSpec((1, 1, d, ff), lambda e, i: (layer, e, 0, 0)),
                  pl.BlockSpec((1, 1, d, ff), lambda e, i: (layer, e, 0, 0)),
                  pl.BlockSpec((1, 1, ff, d), lambda e, i: (layer, e, 0, 0))],
        out_specs=pl.BlockSpec((1, tr, d), lambda e, i: (e, i, 0)),
        out_shape=jax.ShapeDtypeStruct((ne, rows, d), BF16),
        scratch_shapes=[pltpu.VMEM((d, ff), BF16), pltpu.VMEM((d, ff), BF16), pltpu.VMEM((ff, d), BF16)],
        compiler_params=_params(("arbitrary", "arbitrary")),
        name="moe_expert_mlp",
    )(xs, w_gate, w_up, w_down)


def _ple_rows(h, p, g_ref, wg_ref, bg_ref, wp_ref):
    gate = jax.nn.sigmoid(_bdot(_rms(h, g_ref[...]), wg_ref[...]) + bg_ref[...])
    return h + gate * _bdot(p, wp_ref[...])


def _combine_ple_kernel(cnt_ref, y_ref, keyt_ref, wt_ref, h_ref, p_ref, gffn_ref, gple_ref, wgate_ref, bgate_ref,
                        wproj_ref, wg_hbm, wu_hbm, wd_hbm, o_ref, sel_ref, acc_ref, m_ref, wg_buf, wu_buf, wd_buf,
                        sems, *, layer):
    blk = pl.program_id(0)
    tb = h_ref.shape[0]
    riota = lax.broadcasted_iota(jnp.int32, (MOE_CAP, tb), 0).astype(F32)
    for e in range(N_EXPERTS):
        hit = keyt_ref[0, e:e + 1, :] == riota
        sel_ref[e * MOE_CAP:(e + 1) * MOE_CAP, :] = jnp.where(hit, wt_ref[0, e:e + 1, :], 0.0).astype(BF16)
    y = y_ref[...].reshape(N_EXPERTS * MOE_CAP, y_ref.shape[2])
    hm = h_ref[...] + lax.dot_general(sel_ref[...], y, (((0,), (0,)), ((), ())), preferred_element_type=F32)

    most = lax.fori_loop(0, N_EXPERTS, lambda e, mx: jnp.maximum(mx, cnt_ref[blk * LANES + e]), 0)

    @pl.when(most <= MOE_CAP)
    def _():
        o_ref[...] = _ple_rows(hm, p_ref[0], gple_ref, wgate_ref, bgate_ref, wproj_ref)

    @pl.when(most > MOE_CAP)
    def _():
        acc_ref[...] = hm
        m_ref[...] = _rms(h_ref[...], gffn_ref[...]).astype(BF16)
        rows = lax.broadcasted_iota(jnp.int32, (MOE_ROWS, tb), 0).astype(F32)

        def expert(e, carry):
            extra = cnt_ref[blk * LANES + e] - MOE_CAP

            @pl.when(extra > 0)
            def _():
                copies = [pltpu.make_async_copy(src.at[layer, e], dst, sems.at[n])
                          for n, (src, dst) in enumerate(((wg_hbm, wg_buf), (wu_hbm, wu_buf), (wd_hbm, wd_buf)))]
                for c in copies:
                    c.start()
                for c in copies:
                    c.wait()
                krow = keyt_ref[0, pl.ds(e, 1), :]
                wrow = wt_ref[0, pl.ds(e, 1), :]

                def chunk(ci, c2):
                    hit = krow == (rows + (MOE_CAP + ci * MOE_ROWS).astype(F32))
                    sel = jnp.where(hit, 1.0, 0.0).astype(BF16)
                    xg = jnp.dot(sel, m_ref[...], preferred_element_type=F32).astype(BF16)
                    hid = _silu(_bdot(xg, wg_buf[...])) * _bdot(xg, wu_buf[...])
                    yo = _bdot(hid, wd_buf[...]).astype(BF16)
                    acc_ref[...] += _bdot_tn(jnp.where(hit, wrow, 0.0), yo)
                    return c2

                lax.fori_loop(0, (extra + MOE_ROWS - 1) // MOE_ROWS, chunk, 0)

            return carry

        lax.fori_loop(0, N_EXPERTS, expert, 0)
        o_ref[...] = _ple_rows(acc_ref[...], p_ref[0], gple_ref, wgate_ref, bgate_ref, wproj_ref)


def _moe_combine_ple(cnt, ys, keyt, wt, hf, norm_ffn, w_gate, w_up, w_down, layer, p_all, ple_norm, ple_w_gate,
                     ple_b_gate, ple_w_proj, tb):
    n, d = hf.shape
    ff = w_gate.shape[3]
    nblk = n // tb
    pd = p_all.shape[-1]
    blk = pl.BlockSpec((1, LANES, tb), lambda i, c: (i, 0, 0))
    tok = pl.BlockSpec((tb, d), lambda i, c: (i, 0))
    vec = pl.BlockSpec((1, d), lambda i, c: (0, 0))
    hbm = pl.BlockSpec(memory_space=pl.ANY)
    grid_spec = pltpu.PrefetchScalarGridSpec(
        num_scalar_prefetch=1,
        grid=(nblk,),
        in_specs=[pl.BlockSpec((N_EXPERTS, MOE_CAP, d), lambda i, c: (0, i, 0)), blk, blk, tok,
                  pl.BlockSpec((1, tb, pd), lambda i, c: (layer, i, 0)), vec, vec,
                  pl.BlockSpec((d, d), lambda i, c: (0, 0)), vec, pl.BlockSpec((pd, d), lambda i, c: (0, 0)),
                  hbm, hbm, hbm],
        out_specs=tok,
        scratch_shapes=[pltpu.VMEM((N_EXPERTS * MOE_CAP, tb), BF16), pltpu.VMEM((tb, d), F32),
                        pltpu.VMEM((tb, d), BF16), pltpu.VMEM((d, ff), F32), pltpu.VMEM((d, ff), F32),
                        pltpu.VMEM((ff, d), F32), pltpu.SemaphoreType.DMA((3,))],
    )
    return pl.pallas_call(
        functools.partial(_combine_ple_kernel, layer=layer),
        grid_spec=grid_spec,
        out_shape=jax.ShapeDtypeStruct((n, d), F32),
        compiler_params=_params(("arbitrary",)),
        name="moe_combine_ple",
    )(cnt, ys, keyt, wt, hf, p_all.reshape(p_all.shape[0], n, pd), norm_ffn[None, :], ple_norm[None, :],
      ple_w_gate.astype(BF16), ple_b_gate[None, :], ple_w_proj.astype(BF16), w_gate, w_up, w_down)


def _proj_moe_ple(mix_a, mix_b, w_out, h, norm_g, w_group, b_group, w_expert, b_expert, w_gate, w_up, w_down,
                  layer, p_all, ple_w_proj, ple_norm, ple_w_gate, ple_b_gate):
    b, s, d = h.shape
    n = b * s
    tb = _tile(n, MOE_TB)
    hf, xs, keyt, wt, cnt = _moe_router(mix_a.reshape(n, -1), mix_b.reshape(n, -1), w_out, h.reshape(n, d),
                                        norm_g, w_group, b_group, w_expert, b_expert, tb)
    ys = _expert_mlp(xs, w_gate, w_up, w_down, layer)
    out = _moe_combine_ple(cnt.reshape(-1), ys, keyt, wt, hf, norm_g, w_gate, w_up, w_down, layer, p_all,
                           ple_norm, ple_w_gate, ple_b_gate, ple_w_proj, tb)
    return out.reshape(b, s, d)


def _even_mixers(h, positions, norm_mix, w_in, b_f, fox_qn, fox_kn, q_a_norm, w_q_up, kv_a_norm, w_kv_up,
                 mla_qn, mla_kn):
    fq, fk, fv, cum, mq, mk, mv = _even_pre(h, positions, norm_mix, w_in, b_f, fox_qn, fox_kn, q_a_norm,
                                            w_q_up, kv_a_norm, w_kv_up, mla_qn, mla_kn)
    return _attention(fq, fk, fv, cum), _attention(mq, mk, mv)


def kernel(x, p, positions, norm_mix, norm_ffn, ev_w_in, fox_b_f, fox_q_norm, fox_k_norm, mla_q_a_norm, mla_w_q_up, mla_kv_a_norm, mla_w_kv_up, mla_q_norm, mla_k_norm, ev_w_out, od_w_in, s5_a_re, s5_a_im, s5_b_re, s5_b_im, s5_c_re, s5_c_im, s5_d, s5_log_step, s5_w_glu, s5_b_glu, gdn_conv_w, gdn_a_log, gdn_dt_bias, gdn_o_norm, od_w_out, moe_w_group, moe_b_group, moe_w_expert, moe_b_expert, moe_w_gate, moe_w_up, moe_w_down, ple_w_proj, ple_norm, ple_w_gate, ple_b_gate):
    h = x
    depth = p.shape[0]
    for i in range(depth):
        j = i // 2
        if i % 2 == 0:
            mix = _even_mixers(h, positions, norm_mix[i], ev_w_in[j], fox_b_f[j], fox_q_norm[j], fox_k_norm[j],
                               mla_q_a_norm[j], mla_w_q_up[j], mla_kv_a_norm[j], mla_w_kv_up[j], mla_q_norm[j],
                               mla_k_norm[j])
            w_out = ev_w_out[j]
        else:
            mix = _odd_mixers(h, norm_mix[i], od_w_in[j], s5_a_re[j], s5_a_im[j], s5_b_re[j], s5_b_im[j],
                              s5_c_re[j], s5_c_im[j], s5_d[j], s5_log_step[j], s5_w_glu[j], s5_b_glu[j],
                              gdn_conv_w[j], gdn_a_log[j], gdn_dt_bias[j], gdn_o_norm[j])
            w_out = od_w_out[j]
        h = _proj_moe_ple(mix[0], mix[1], w_out, h, norm_ffn[i], moe_w_group[i], moe_b_group[i], moe_w_expert[i],
                          moe_b_expert[i], moe_w_gate, moe_w_up, moe_w_down, i, p, ple_w_proj[i], ple_norm[i],
                          ple_w_gate[i], ple_b_gate[i])
    return h
```

```python
import functools
import math

import numpy as np
import jax
import jax.numpy as jnp
from jax import lax
from jax.experimental import pallas as pl
from jax.experimental.pallas import tpu as pltpu

F32 = jnp.float32
BF16 = jnp.bfloat16

LANES = 128
RMS_EPS = 1e-6
ROPE_THETA = 10000.0
LOG2E = math.log2(math.e)

FOX_HEADS = 8
FOX_HEAD_DIM = 64
MLA_HEADS = 8
MLA_Q_LORA = 384
MLA_KV_LORA = 256
MLA_NOPE = 64
MLA_ROPE = 32
MLA_V = 64
MLA_QK = MLA_NOPE + MLA_ROPE

S5_CH = 512
S5_GROUP_CH = 16
S5_GROUPS = S5_CH // S5_GROUP_CH
S5_STATE = 64
S5_N = S5_GROUPS * S5_STATE

GDN_HEADS = 4
GDN_HEAD_DIM = 128
GDN_W = GDN_HEADS * GDN_HEAD_DIM
GDN_CONV = 4
GDN_CHUNK = 64

MOE_GROUPS = 4
MOE_PER_GROUP = 8
N_EXPERTS = MOE_GROUPS * MOE_PER_GROUP
MOE_TB = 512
MOE_CAP = 64
MOE_ROWS = 128
MOE_SLOT_LEVELS = (32, 48)
ATTN_TQ = 2048
ATTN_TK = 512
MIX_TM = 256
EXPERT_TR = 1024

VMEM_LIMIT = 56 * 1024 * 1024


def _tile(n, pref):
    t = min(n, pref)
    assert n % t == 0, (n, t)
    return t


def _params(sem):
    return pltpu.CompilerParams(dimension_semantics=sem, vmem_limit_bytes=VMEM_LIMIT)


def _full(shape):
    nd = len(shape)
    return pl.BlockSpec(shape, lambda *_: (0,) * nd)


def _rms(x, g):
    return x * lax.rsqrt(jnp.mean(x * x, axis=-1, keepdims=True) + RMS_EPS) * g


def _bdot(a, b):
    return jnp.dot(a.astype(BF16), b.astype(BF16), preferred_element_type=F32)


def _bdot_nt(a, b):
    return lax.dot_general(a.astype(BF16), b.astype(BF16), (((1,), (1,)), ((), ())),
                           preferred_element_type=F32)


def _bdot_tn(a, b):
    return lax.dot_general(a.astype(BF16), b.astype(BF16), (((0,), (0,)), ((), ())),
                           preferred_element_type=F32)


def _split3(x):
    x1 = x.astype(BF16)
    r = x - x1.astype(F32)
    x2 = r.astype(BF16)
    return x1, x2, (r - x2.astype(F32)).astype(BF16)


def _sel_dot(sel, x):
    return sum(jnp.dot(sel, part, preferred_element_type=F32) for part in _split3(x))


def _dot_sel(x, sel):
    return sum(jnp.dot(part, sel, preferred_element_type=F32) for part in _split3(x))


def _split_dot(x, ind):
    hi = x.astype(BF16)
    lo = (x - hi.astype(F32)).astype(BF16)
    return (jnp.dot(hi, ind, preferred_element_type=F32)
            + jnp.dot(lo, ind, preferred_element_type=F32))


def _log_sigmoid(x):
    return jnp.minimum(x, 0.0) - jnp.log(1.0 + jnp.exp(-jnp.abs(x)))


def _softplus(x):
    return jnp.maximum(x, 0.0) + jnp.log(1.0 + jnp.exp(-jnp.abs(x)))


def _silu(x):
    return x * jax.nn.sigmoid(x)


def _head_norm128(x, nheads, denom, gain):
    outs = []
    for hh in range(nheads):
        xh = x[:, LANES * hh:LANES * (hh + 1)]
        ss = jnp.sum(xh * xh, axis=-1, keepdims=True)
        outs.append(xh * lax.rsqrt(ss / denom + RMS_EPS))
    return jnp.concatenate(outs, axis=1) * gain


def _even_pre_kernel(h_ref, pos_ref, nmix_ref, win_ref, ind_ref, fqn_ref, fkn_ref, bf_ref,
                     qan_ref, wq_ref, kvan_ref, wkv_ref, mqn_ref, mkn_ref, freq_ref, s1_ref, s2_ref,
                     tri_ref, vone_ref, fq_o, fk_o, fv_o, cum_o, mq_o, mk_o, mv_o, carry_ref):
    t = pl.program_id(1)

    @pl.when(t == 0)
    def _():
        carry_ref[...] = jnp.zeros_like(carry_ref)

    tm = h_ref.shape[1]
    a = _rms(h_ref[0], nmix_ref[...])
    proj = _bdot(a, win_ref[...])
    nf = FOX_HEADS * FOX_HEAD_DIM
    fq = proj[:, 0:nf]
    fk = proj[:, nf:2 * nf]
    nv = FOX_HEADS * LANES
    fv = proj[:, 2 * nf:2 * nf + nv]
    o_cq = 2 * nf + nv
    cq = proj[:, o_cq:o_cq + MLA_Q_LORA]
    o_ckv = o_cq + MLA_Q_LORA
    ckv = proj[:, o_ckv:o_ckv + MLA_KV_LORA]
    misc = proj[:, o_ckv + MLA_KV_LORA:]

    ind = ind_ref[...]
    fq_n = fq * lax.rsqrt(_split_dot(fq * fq, ind) / FOX_HEAD_DIM + RMS_EPS) * fqn_ref[...]
    fk_n = fk * lax.rsqrt(_split_dot(fk * fk, ind) / FOX_HEAD_DIM + RMS_EPS) * fkn_ref[...]
    fq_o[0] = (fq_n * (FOX_HEAD_DIM ** -0.5 * LOG2E)).astype(BF16)
    fk_o[0] = fk_n.astype(BF16)
    fv_o[0] = (fv + vone_ref[...]).astype(BF16)

    lane = lax.broadcasted_iota(jnp.int32, (tm, LANES), 1)
    logf = jnp.where(lane < FOX_HEADS, _log_sigmoid(misc + bf_ref[...]), 0.0)
    cum = _sel_dot(tri_ref[...], logf) + carry_ref[...]
    carry_ref[...] = cum[tm - 1:tm, :]
    cum_o[0] = (cum * LOG2E).T[:FOX_HEADS, :]

    ang = pos_ref[0].astype(F32) * freq_ref[...]
    cos1 = jnp.cos(ang)
    sin1 = jnp.sin(ang)
    cos = jnp.concatenate([cos1] * MLA_HEADS, axis=1)
    sin_a = jnp.concatenate([sin1 * s1_ref[...]] * MLA_HEADS, axis=1)
    sin_b = jnp.concatenate([sin1 * s2_ref[...]] * MLA_HEADS, axis=1)
    width = MLA_HEADS * LANES
    half = MLA_ROPE // 2

    def rope(x):
        return (x * cos + pltpu.roll(x, width - half, 1) * sin_a + pltpu.roll(x, half, 1) * sin_b)

    q = _bdot(_rms(cq, qan_ref[...]), wq_ref[...])
    q = rope(_head_norm128(q, MLA_HEADS, MLA_QK, mqn_ref[...]))
    mq_o[0] = (q * (MLA_QK ** -0.5 * LOG2E)).astype(BF16)

    kv = _bdot(_rms(ckv, kvan_ref[...]), wkv_ref[...])
    kr = pltpu.roll(misc, MLA_NOPE - FOX_HEADS, 1)
    kr = jnp.where((lane >= MLA_NOPE) & (lane < MLA_QK), kr, 0.0)
    k = kv[:, :width] + jnp.concatenate([kr] * MLA_HEADS, axis=1)
    k = rope(_head_norm128(k, MLA_HEADS, MLA_QK, mkn_ref[...]))
    mk_o[0] = k.astype(BF16)
    mv_o[0] = (kv[:, width:] + vone_ref[...]).astype(BF16)


def _even_pre(h, positions, norm_mix, w_in, b_f, fox_qn, fox_kn, q_a_norm, w_q_up, kv_a_norm, w_kv_up,
              mla_qn, mla_kn):
    b, s, d = h.shape
    tm = _tile(s, MIX_TM)
    nf = FOX_HEADS * FOX_HEAD_DIM
    sizes = (nf, nf, nf, FOX_HEADS, MLA_Q_LORA, MLA_KV_LORA, MLA_ROPE)
    offs = np.concatenate([[0], np.cumsum(sizes)])
    parts = [w_in[:, offs[i]:offs[i + 1]] for i in range(len(sizes))]
    pad = jnp.zeros((d, LANES - FOX_HEADS - MLA_ROPE), w_in.dtype)
    slot_pad = ((0, 0), (0, 0), (0, LANES - FOX_HEAD_DIM))
    wfv = jnp.pad(parts[2].reshape(d, FOX_HEADS, FOX_HEAD_DIM), slot_pad).reshape(d, FOX_HEADS * LANES)
    win = jnp.concatenate([parts[0], parts[1], wfv, parts[4], parts[5], parts[3], parts[6], pad],
                          axis=1).astype(BF16)
    gidx = np.arange(nf) // FOX_HEAD_DIM
    ind = jnp.asarray(gidx[:, None] == gidx[None, :], BF16)
    fqn = jnp.tile(fox_qn, FOX_HEADS)[None, :]
    fkn = jnp.tile(fox_kn, FOX_HEADS)[None, :]
    bf = jnp.zeros((1, LANES), F32).at[0, :FOX_HEADS].set(b_f)
    padq = LANES - MLA_QK
    wq = jnp.pad(w_q_up.reshape(MLA_Q_LORA, MLA_HEADS, MLA_QK), ((0, 0), (0, 0), (0, padq)))
    wq = wq.reshape(MLA_Q_LORA, MLA_HEADS * LANES).astype(BF16)
    wkv3 = w_kv_up.reshape(MLA_KV_LORA, MLA_HEADS, MLA_NOPE + MLA_V)
    wk = jnp.pad(wkv3[:, :, :MLA_NOPE], ((0, 0), (0, 0), (0, LANES - MLA_NOPE)))
    wv = jnp.pad(wkv3[:, :, MLA_NOPE:], ((0, 0), (0, 0), (0, LANES - MLA_V)))
    wkv = jnp.concatenate([wk.reshape(MLA_KV_LORA, MLA_HEADS * LANES),
                           wv.reshape(MLA_KV_LORA, MLA_HEADS * LANES)], axis=1).astype(BF16)
    vone = jnp.tile(jnp.zeros((LANES,), F32).at[MLA_V].set(1.0), MLA_HEADS)[None, :]
    mqn = jnp.tile(jnp.pad(mla_qn, (0, padq)), MLA_HEADS)[None, :]
    mkn = jnp.tile(jnp.pad(mla_kn, (0, padq)), MLA_HEADS)[None, :]
    half = MLA_ROPE // 2
    inv = ROPE_THETA ** (-jnp.arange(half, dtype=F32) * 2.0 / MLA_ROPE)
    freq = jnp.zeros((1, LANES), F32).at[0, MLA_NOPE:MLA_NOPE + half].set(inv)
    freq = freq.at[0, MLA_NOPE + half:MLA_QK].set(inv)
    s1 = jnp.zeros((1, LANES), F32).at[0, MLA_NOPE:MLA_NOPE + half].set(-1.0)
    s2 = jnp.zeros((1, LANES), F32).at[0, MLA_NOPE + half:MLA_QK].set(1.0)
    tri = jnp.asarray(np.tril(np.ones((tm, tm), np.float32)), BF16)
    pos3 = positions.reshape(b, s, 1)

    row = lambda n: pl.BlockSpec((1, tm, n), lambda bi, ti: (bi, ti, 0))
    consts = [norm_mix[None, :], win, ind, fqn, fkn, bf, q_a_norm[None, :], wq, kv_a_norm[None, :], wkv,
              mqn, mkn, freq, s1, s2, tri, vone]
    nv = FOX_HEADS * LANES
    out_shape = [jax.ShapeDtypeStruct((b, s, nf), BF16)] * 2 + [jax.ShapeDtypeStruct((b, s, nv), BF16)] + [
        jax.ShapeDtypeStruct((b, FOX_HEADS, s), F32),
        jax.ShapeDtypeStruct((b, s, MLA_HEADS * LANES), BF16),
        jax.ShapeDtypeStruct((b, s, MLA_HEADS * LANES), BF16),
        jax.ShapeDtypeStruct((b, s, MLA_HEADS * LANES), BF16)]
    return pl.pallas_call(
        _even_pre_kernel,
        grid=(b, s // tm),
        in_specs=[row(d), row(1)] + [_full(c.shape) for c in consts],
        out_specs=[row(nf), row(nf), row(nv), pl.BlockSpec((1, FOX_HEADS, tm), lambda bi, ti: (bi, 0, ti)),
                   row(MLA_HEADS * LANES), row(MLA_HEADS * LANES), row(MLA_HEADS * LANES)],
        out_shape=out_shape,
        scratch_shapes=[pltpu.VMEM((1, LANES), F32)],
        compiler_params=_params(("arbitrary", "arbitrary")),
        name="even_pre",
    )(h, pos3, *consts)


def _attn_kernel(*refs, tq, tk, fox):
    if fox:
        q_ref, k_ref, v_ref, cr_ref, o_ref = refs
    else:
        q_ref, k_ref, v_ref, o_ref = refs
    hp = pl.program_id(1)
    i = pl.program_id(2)
    lane = lax.broadcasted_iota(jnp.int32, (tq, LANES), 1)
    qs = []
    for hh in range(2):
        if fox:
            in_head = (lane >= FOX_HEAD_DIM * hh) & (lane < FOX_HEAD_DIM * (hh + 1))
            qs.append(jnp.where(in_head, q_ref[0], jnp.zeros((), BF16)))
        else:
            qs.append(q_ref[0, :, LANES * hh:LANES * (hh + 1)])

    def step(j, carry, lo=None):
        koff = pl.multiple_of(j * tk, tk)
        top = 0 if lo is None else lo
        new = []
        for hh in range(2):
            m, acc = carry[hh]
            if fox:
                kj = k_ref[0, pl.ds(koff, tk), :]
            else:
                kj = k_ref[0, pl.ds(koff, tk), LANES * hh:LANES * (hh + 1)]
            sc = lax.dot_general(qs[hh][top:], kj, (((1,), (1,)), ((), ())), preferred_element_type=F32)
            if fox:
                sc = sc - cr_ref[0, pl.ds(2 * hp + hh, 1), pl.ds(koff, tk)]
            if lo is not None:
                rowi = lax.broadcasted_iota(jnp.int32, sc.shape, 0)
                coli = lax.broadcasted_iota(jnp.int32, sc.shape, 1)
                sc = jnp.where(coli <= rowi, sc, -jnp.inf)
            m_new = jnp.maximum(m[top:], jnp.max(sc, axis=-1, keepdims=True))
            alpha = jnp.exp2(m[top:] - m_new)
            p = jnp.exp2((sc - jnp.concatenate([m_new] * (tk // LANES), axis=1)).astype(BF16))
            vj = v_ref[0, pl.ds(koff, tk), LANES * hh:LANES * (hh + 1)]
            acc_new = alpha * acc[top:] + jnp.dot(p, vj, preferred_element_type=F32)
            if top:
                m_new = jnp.concatenate([m[:top], m_new], axis=0)
                acc_new = jnp.concatenate([acc[:top], acc_new], axis=0)
            new.append((m_new, acc_new))
        return tuple(new)

    def body(jj, carry):
        for r in range(ratio):
            carry = step(jj * ratio + r, carry)
        return carry

    one = (jnp.full((tq, LANES), -jnp.inf, F32), jnp.zeros((tq, LANES), F32))
    ratio = tq // tk
    carry = lax.fori_loop(0, i, body, (one, one))
    for r in range(ratio):
        carry = step(i * ratio + r, carry, lo=r * tk)
    outs = [acc / acc[:, MLA_V:MLA_V + 1] for _, acc in carry]
    o_ref[0] = jnp.where(lane < MLA_V, outs[0], pltpu.roll(outs[1], MLA_V, 1)).astype(o_ref.dtype)


def _attention(q, k, v, cum_row=None):
    b, s, _ = v.shape
    fox = cum_row is not None
    qw = LANES if fox else 2 * LANES
    tq = _tile(s, ATTN_TQ)
    tk = _tile(tq, ATTN_TK)
    npairs = v.shape[2] // (2 * LANES)
    in_specs = [pl.BlockSpec((1, tq, qw), lambda bi, hp, i: (bi, i, hp)),
                pl.BlockSpec((1, s, qw), lambda bi, hp, i: (bi, 0, hp)),
                pl.BlockSpec((1, s, 2 * LANES), lambda bi, hp, i: (bi, 0, hp))]
    args = [q, k, v]
    if fox:
        in_specs += [pl.BlockSpec((1, FOX_HEADS, s), lambda bi, hp, i: (bi, 0, 0))]
        args += [cum_row]
    return pl.pallas_call(
        functools.partial(_attn_kernel, tq=tq, tk=tk, fox=fox),
        grid=(b, npairs, s // tq),
        in_specs=in_specs,
        out_specs=pl.BlockSpec((1, tq, LANES), lambda bi, hp, i: (bi, i, hp)),
        out_shape=jax.ShapeDtypeStruct((b, s, npairs * LANES), BF16),
        compiler_params=_params(("arbitrary", "arbitrary", "arbitrary")),
        name="fox_attention" if fox else "mla_attention",
    )(*args)


N_S5_CONSTS = 13
N_GDN_CONSTS = 6


def _odd_kernel(h_ref, nmix_ref, win_ref, *refs):
    s5_consts = refs[:N_S5_CONSTS]
    gdn_consts = refs[N_S5_CONSTS:N_S5_CONSTS + N_GDN_CONSTS]
    y_o, o_o, x_ref, sr_ref, si_ref, xpad_ref, state_ref = refs[N_S5_CONSTS + N_GDN_CONSTS:]
    a = _rms(h_ref[0], nmix_ref[...])
    proj = _bdot(a, win_ref[...])
    o1 = S5_CH
    o2 = o1 + 3 * GDN_W
    o3 = o2 + GDN_W
    _s5_body(proj[:, :o1], *s5_consts, y_o, x_ref, sr_ref, si_ref)
    _gdn_body(proj[:, o1:o2], proj[:, o2:o3], proj[:, o3:], *gdn_consts, o_o, xpad_ref, state_ref)


def _odd_mixers(h, norm_mix, w_in, a_re, a_im, b_re, b_im, c_re, c_im, d_skip, log_step, w_glu, b_glu,
                conv_w, a_log, dt_bias, o_norm):
    b, s, d = h.shape
    tm = _tile(s, MIX_TM)
    sizes = (S5_CH, 3 * GDN_W, GDN_HEADS, GDN_HEADS, GDN_W)
    offs = np.concatenate([[0], np.cumsum(sizes)])
    parts = [w_in[:, offs[i]:offs[i + 1]] for i in range(len(sizes))]
    pad = jnp.zeros((d, LANES - 2 * GDN_HEADS), w_in.dtype)
    win = jnp.concatenate([parts[0], parts[1], parts[4], parts[2], parts[3], pad], axis=1).astype(BF16)
    s5_consts = _s5_consts(tm, a_re, a_im, b_re, b_im, c_re, c_im, d_skip, log_step, w_glu, b_glu)
    gdn_consts = _gdn_consts(tm, conv_w, a_log, dt_bias, o_norm)
    assert len(s5_consts) == N_S5_CONSTS and len(gdn_consts) == N_GDN_CONSTS
    consts = [norm_mix[None, :], win] + s5_consts + gdn_consts
    row = lambda n: pl.BlockSpec((1, tm, n), lambda bi, ti: (bi, ti, 0))
    return pl.pallas_call(
        _odd_kernel,
        grid=(b, s // tm),
        in_specs=[row(d)] + [_full(c.shape) for c in consts],
        out_specs=[row(S5_CH), row(GDN_W)],
        out_shape=[jax.ShapeDtypeStruct((b, s, S5_CH), BF16), jax.ShapeDtypeStruct((b, s, GDN_W), BF16)],
        scratch_shapes=[pltpu.VMEM((tm, 2 * S5_N), F32), pltpu.VMEM((1, S5_N), F32), pltpu.VMEM((1, S5_N), F32),
                        pltpu.VMEM((tm + 8, 3 * GDN_W), F32),
                        pltpu.VMEM((GDN_HEADS, GDN_HEAD_DIM, GDN_HEAD_DIM), F32)],
        compiler_params=_params(("arbitrary", "arbitrary")),
        name="odd_mixers",
    )(h, *consts)


def _s5_body(u, perm_ref, unperm_ref, bbd_ref, cbd_ref, ar_ref, ai_ref, asr_ref, asi_ref, pwr_ref, pwi_ref,
             d_ref, wglu_ref, bglu_ref, o_ref, x_ref, sr_ref, si_ref):
    t = pl.program_id(1)

    @pl.when(t == 0)
    def _():
        sr_ref[...] = jnp.zeros_like(sr_ref)
        si_ref[...] = jnp.zeros_like(si_ref)

    tm = u.shape[0]
    nseg = 8
    seg = tm // nseg
    u = _sel_dot(perm_ref[...], u)
    hc = S5_CH // 2
    hn = S5_N // 2
    ub = u.astype(BF16)
    for part in range(2):
        for base in (0, S5_N):
            cols = slice(base + part * hn, base + (part + 1) * hn)
            x_ref[:, cols] = jnp.dot(ub[:, part * hc:(part + 1) * hc], bbd_ref[part * hc:(part + 1) * hc, cols],
                                     preferred_element_type=F32)
    ar = ar_ref[...]
    ai = ai_ref[...]
    re = slice(0, S5_N)
    im = slice(S5_N, 2 * S5_N)

    def local(i, carry):
        xr, xi = carry
        rows = pl.ds(pl.multiple_of(i * nseg, nseg), nseg)
        nr = ar * xr - ai * xi + x_ref[rows, re]
        ni = ar * xi + ai * xr + x_ref[rows, im]
        x_ref[rows, re] = nr
        x_ref[rows, im] = ni
        return nr, ni

    zero = jnp.zeros((nseg, S5_N), F32)
    er, ei = lax.fori_loop(0, seg, local, (zero, zero), unroll=4)

    asr = asr_ref[...]
    asi = asi_ref[...]
    cr = [sr_ref[...]]
    ci = [si_ref[...]]
    for s in range(nseg):
        cr.append(asr * cr[s] - asi * ci[s] + er[s:s + 1, :])
        ci.append(asr * ci[s] + asi * cr[s] + ei[s:s + 1, :])
    sr_ref[...] = cr[nseg]
    si_ref[...] = ci[nseg]
    ent_r = jnp.concatenate(cr[:nseg], axis=0)
    ent_i = jnp.concatenate(ci[:nseg], axis=0)

    def fix(i, c):
        rows = pl.ds(pl.multiple_of(i * nseg, nseg), nseg)
        pr = pwr_ref[pl.ds(i, 1), :]
        pi = pwi_ref[pl.ds(i, 1), :]
        x_ref[rows, re] += pr * ent_r - pi * ent_i
        x_ref[rows, im] += pr * ent_i + pi * ent_r
        return c

    lax.fori_loop(0, seg, fix, 0, unroll=4)
    ys = []
    for part in range(2):
        oc = slice(part * hc, (part + 1) * hc)
        acc = None
        for base in (0, S5_N):
            rows = slice(base + part * hn, base + (part + 1) * hn)
            term = _bdot(x_ref[:, rows], cbd_ref[rows, oc])
            acc = term if acc is None else acc + term
        ys.append(acc)
    y = jnp.concatenate(ys, axis=1) + d_ref[...] * u
    hg = jax.nn.gelu(y)
    out = (hg * jax.nn.sigmoid(_bdot(hg, wglu_ref[...]) + bglu_ref[...])).astype(BF16)
    o_ref[0] = jnp.dot(unperm_ref[...], out, preferred_element_type=F32).astype(o_ref.dtype)


def _s5_consts(tm, a_re, a_im, b_re, b_im, c_re, c_im, d_skip, log_step, w_glu, b_glu):
    lam_re = jnp.minimum(a_re, -1e-4)
    lam_im = a_im
    dt = jnp.exp(log_step)[:, None]
    mag = jnp.exp(lam_re * dt)
    ab_re = mag * jnp.cos(lam_im * dt)
    ab_im = mag * jnp.sin(lam_im * dt)
    den = lam_re * lam_re + lam_im * lam_im
    nr, ni = ab_re - 1.0, ab_im
    gam_re = (nr * lam_re + ni * lam_im) / den
    gam_im = (ni * lam_re - nr * lam_im) / den
    bb_re = gam_re[..., None] * b_re - gam_im[..., None] * b_im
    bb_im = gam_re[..., None] * b_im + gam_im[..., None] * b_re
    eye = jnp.eye(S5_GROUPS, dtype=F32)
    bd_in = lambda m: jnp.einsum('gpc,gh->gchp', m, eye).reshape(S5_CH, S5_N)
    bd_out = lambda m: jnp.einsum('gcp,gh->gphc', m, eye).reshape(S5_N, S5_CH)
    bbd = jnp.concatenate([bd_in(bb_re), bd_in(bb_im)], axis=1).astype(BF16)
    cbd = jnp.concatenate([bd_out(c_re), -bd_out(c_im)], axis=0).astype(BF16)
    seg = tm // 8
    steps = jnp.arange(1, seg + 1, dtype=F32)[:, None, None] * dt[None]
    pmag = jnp.exp(lam_re[None] * steps)
    pw_re = (pmag * jnp.cos(lam_im[None] * steps)).reshape(seg, S5_N)
    pw_im = (pmag * jnp.sin(lam_im[None] * steps)).reshape(seg, S5_N)
    src = (np.arange(tm) % 8) * seg + np.arange(tm) // 8
    perm = np.zeros((tm, tm), np.float32)
    perm[np.arange(tm), src] = 1.0
    return [jnp.asarray(perm, BF16), jnp.asarray(perm.T, BF16),
            bbd, cbd, ab_re.reshape(1, S5_N), ab_im.reshape(1, S5_N), pw_re[seg - 1:seg], pw_im[seg - 1:seg],
            pw_re, pw_im, d_skip[None, :], w_glu.astype(BF16), b_glu[None, :]]


def _gdn_body(x, z, gb, cw_ref, nega_ref, dtb_ref, onorm_ref, tril_ref, triu_ref, o_ref, xpad_ref, state_ref):
    t = pl.program_id(1)
    tm = x.shape[0]
    c = GDN_CHUNK
    hd = GDN_HEAD_DIM

    @pl.when(t == 0)
    def _():
        xpad_ref[0:8, :] = jnp.zeros((8, xpad_ref.shape[1]), F32)
        state_ref[...] = jnp.zeros_like(state_ref)

    @pl.when(t > 0)
    def _():
        xpad_ref[0:8, :] = xpad_ref[tm:tm + 8, :]

    xpad_ref[8:tm + 8, :] = x
    conv = cw_ref[0:1, :] * xpad_ref[pl.ds(8 - (GDN_CONV - 1), tm), :]
    for i in range(1, GDN_CONV):
        conv = conv + cw_ref[i:i + 1, :] * xpad_ref[pl.ds(8 - (GDN_CONV - 1) + i, tm), :]
    act = _silu(conv)

    def l2n(x):
        return x * lax.rsqrt(jnp.sum(x * x, axis=-1, keepdims=True) + RMS_EPS)

    g = nega_ref[...] * _softplus(gb + dtb_ref[...])
    beta = jax.nn.sigmoid(gb)
    gc = _sel_dot(tril_ref[...], g)
    gct = _dot_sel(g.T, triu_ref[...])

    ri = lax.broadcasted_iota(jnp.int32, (tm, tm), 0)
    ci = lax.broadcasted_iota(jnp.int32, (tm, tm), 1)
    same = (ri // c) == (ci // c)
    incl = same & (ri >= ci)
    strict = same & (ri > ci)
    eye = (ri == ci).astype(F32)
    offs = []
    bs = 1
    while bs < c:
        offs.append(((ri // (2 * bs)) == (ci // (2 * bs))) & ((ri % (2 * bs)) >= bs) & ((ci % (2 * bs)) < bs))
        bs *= 2
    nchunks = tm // c

    heads = range(GDN_HEADS)
    q = [l2n(act[:, hh * hd:(hh + 1) * hd]) * (hd ** -0.5) for hh in heads]
    k = [l2n(act[:, GDN_W + hh * hd:GDN_W + (hh + 1) * hd]) for hh in heads]
    v = [act[:, 2 * GDN_W + hh * hd:2 * GDN_W + (hh + 1) * hd] for hh in heads]
    bcol = [beta[:, GDN_HEADS + hh:GDN_HEADS + hh + 1] for hh in heads]
    gcol = [gc[:, hh:hh + 1] for hh in heads]
    decay = [jnp.where(incl, jnp.exp(jnp.where(incl, gcol[hh] - gct[hh:hh + 1, :], 0.0)), 0.0) for hh in heads]
    kb = [k[hh] * bcol[hh] for hh in heads]
    a_mat = [jnp.where(strict, _bdot_nt(kb[hh], k[hh]) * decay[hh], 0.0) for hh in heads]
    t_mat = [eye - jnp.where(offs[0], a_mat[hh], 0.0) for hh in heads]
    for off in offs[1:]:
        pa = [_bdot(t_mat[hh], jnp.where(off, a_mat[hh], 0.0)) for hh in heads]
        t_mat = [t_mat[hh] - _bdot(pa[hh], t_mat[hh]) for hh in heads]
    th = [t_mat[hh].astype(BF16) for hh in heads]
    tl = [(t_mat[hh] - th[hh].astype(F32)).astype(BF16) for hh in heads]
    ah = [a_mat[hh].astype(BF16) for hh in heads]
    al = [(a_mat[hh] - ah[hh].astype(F32)).astype(BF16) for hh in heads]
    a_t = [jnp.dot(ah[hh], th[hh], preferred_element_type=F32) + jnp.dot(ah[hh], tl[hh], preferred_element_type=F32)
           + jnp.dot(al[hh], th[hh], preferred_element_type=F32) for hh in heads]
    t_mat = [t_mat[hh] + jnp.dot(th[hh], (eye - t_mat[hh] - a_t[hh]).astype(BF16), preferred_element_type=F32)
             for hh in heads]
    eg = [jnp.exp(gcol[hh]) for hh in heads]
    u = [_bdot(t_mat[hh], v[hh] * bcol[hh]) for hh in heads]
    w = [_bdot(t_mat[hh], kb[hh] * eg[hh]) for hh in heads]
    intra = [jnp.where(incl, _bdot_nt(q[hh], k[hh]) * decay[hh], 0.0).astype(BF16) for hh in heads]
    qd = [q[hh] * eg[hh] for hh in heads]
    state = [state_ref[hh] for hh in heads]
    for n in range(nchunks):
        r0 = n * c
        for hh in heads:
            lo = hh * hd
            gcn = gcol[hh][r0:r0 + c, :]
            glast = gcol[hh][r0 + c - 1:r0 + c, :]
            v_new = u[hh][r0:r0 + c, :] - _bdot(w[hh][r0:r0 + c, :], state[hh])
            v_rep = jnp.concatenate([v_new.astype(BF16)] * nchunks, axis=0)
            o = _bdot(qd[hh][r0:r0 + c, :], state[hh]) + jnp.dot(intra[hh][r0:r0 + c, :], v_rep,
                                                                  preferred_element_type=F32)
            state[hh] = state[hh] * jnp.exp(glast) + _bdot_tn(k[hh][r0:r0 + c, :] * jnp.exp(glast - gcn), v_new)
            on = o * lax.rsqrt(jnp.mean(o * o, axis=-1, keepdims=True) + RMS_EPS) * onorm_ref[...]
            o_ref[0, r0:r0 + c, lo:lo + hd] = (on * _silu(z[r0:r0 + c, lo:lo + hd])).astype(o_ref.dtype)
    for hh in heads:
        state_ref[hh] = state[hh]


def _gdn_consts(tm, conv_w, a_log, dt_bias, o_norm):
    nega = jnp.zeros((1, LANES), F32).at[0, :GDN_HEADS].set(-jnp.exp(a_log))
    dtb = jnp.zeros((1, LANES), F32).at[0, :GDN_HEADS].set(dt_bias)
    cwp = jnp.pad(conv_w, ((0, 8 - GDN_CONV), (0, 0)))
    ridx = np.arange(tm)
    same = (ridx[:, None] // GDN_CHUNK) == (ridx[None, :] // GDN_CHUNK)
    tril = jnp.asarray((same & (ridx[:, None] >= ridx[None, :])).astype(np.float32), BF16)
    triu = jnp.asarray((same & (ridx[:, None] <= ridx[None, :])).astype(np.float32), BF16)
    return [cwp, nega, dtb, o_norm[None, :], tril, triu]


def _router_kernel(a_ref, b_ref, wa_ref, wb_ref, h_ref, g_ref, wrh_ref, wrl_ref, br_ref, tri_ref,
                   h_o, xs_o, keyt_o, wt_o, cnt_o, sel_ref):
    tm = h_ref.shape[0]
    h = (h_ref[...] + jnp.dot(a_ref[...], wa_ref[...], preferred_element_type=F32)
         + jnp.dot(b_ref[...], wb_ref[...], preferred_element_type=F32))
    h_o[...] = h
    m = _rms(h, g_ref[...])
    mh = m.astype(BF16)
    ml = (m - mh.astype(F32)).astype(BF16)
    logits = (jnp.dot(mh, wrh_ref[...], preferred_element_type=F32) + jnp.dot(mh, wrl_ref[...], preferred_element_type=F32)
              + jnp.dot(ml, wrh_ref[...], preferred_element_type=F32)) + br_ref[...]
    lane = lax.broadcasted_iota(jnp.int32, (tm, LANES), 1)
    neg = -jnp.inf

    def first_argmax(x):
        mx = jnp.max(x, axis=-1, keepdims=True)
        idx = jnp.min(jnp.where(x == mx, lane, LANES), axis=-1, keepdims=True)
        return mx, idx

    is_g = (lane >= N_EXPERTS) & (lane < N_EXPERTS + MOE_GROUPS)
    gl = jnp.where(is_g, logits, neg)
    gmax, gidx = first_argmax(gl)
    g_w = 1.0 / jnp.sum(jnp.where(is_g, jnp.exp(gl - gmax), 0.0), axis=-1, keepdims=True)
    in_group = (lane // MOE_PER_GROUP) == (gidx - N_EXPERTS)
    el = jnp.where(in_group & (lane < N_EXPERTS), logits, neg)
    m1, i1 = first_argmax(el)
    el2 = jnp.where(lane == i1, neg, el)
    m2, i2 = first_argmax(el2)
    r = jnp.exp(m2 - m1)
    w1 = g_w / (1.0 + r)
    w2 = g_w * r / (1.0 + r)
    chose = (lane == i1) | (lane == i2)
    wmat = jnp.where(lane == i1, w1, jnp.where(lane == i2, w2, 0.0))
    ch = chose.astype(F32)
    rank = jnp.dot(tri_ref[...], ch.astype(BF16), preferred_element_type=F32)
    keyt = jnp.where(chose, rank, -1.0).T
    keyt_o[0] = keyt
    wt_o[0] = wmat.T
    cnt_o[0] = jnp.sum(ch, axis=0, keepdims=True).astype(jnp.int32)
    riota = lax.broadcasted_iota(jnp.int32, (MOE_CAP, tm), 0).astype(F32)
    for e in range(N_EXPERTS):
        sel_ref[e * MOE_CAP:(e + 1) * MOE_CAP, :] = jnp.where(keyt[e:e + 1, :] == riota, 1.0, 0.0).astype(BF16)
    xg = jnp.dot(sel_ref[...], mh, preferred_element_type=F32)
    xs_o[...] = xg.astype(BF16).reshape(xs_o.shape)


def _moe_router(a, bb, w_out, hf, norm_g, w_group, b_group, w_expert, b_expert, tb):
    n, d = hf.shape
    nblk = n // tb
    na, nb = a.shape[1], bb.shape[1]
    wa = w_out[:na].astype(BF16)
    wb = w_out[na:].astype(BF16)
    tok = lambda w: pl.BlockSpec((tb, w), lambda i: (i, 0))
    wr = jnp.zeros((d, LANES), F32).at[:, :N_EXPERTS].set(w_expert)
    wr = wr.at[:, N_EXPERTS:N_EXPERTS + MOE_GROUPS].set(w_group)
    wrh = wr.astype(BF16)
    wrl = (wr - wrh.astype(F32)).astype(BF16)
    br = jnp.zeros((1, LANES), F32).at[0, :N_EXPERTS].set(b_expert)
    br = br.at[0, N_EXPERTS:N_EXPERTS + MOE_GROUPS].set(b_group)
    tri = jnp.asarray(np.tril(np.ones((tb, tb), np.float32), -1), BF16)
    blk = pl.BlockSpec((1, LANES, tb), lambda i: (i, 0, 0))
    return pl.pallas_call(
        _router_kernel,
        grid=(nblk,),
        in_specs=[tok(na), tok(nb), _full(wa.shape), _full(wb.shape), tok(d), _full((1, d)), _full(wr.shape),
                  _full(wr.shape), _full(br.shape), _full(tri.shape)],
        out_specs=[tok(d), pl.BlockSpec((N_EXPERTS, MOE_CAP, d), lambda i: (0, i, 0)), blk, blk,
                   pl.BlockSpec((1, 1, LANES), lambda i: (i, 0, 0))],
        out_shape=[jax.ShapeDtypeStruct((n, d), F32),
                   jax.ShapeDtypeStruct((N_EXPERTS, nblk * MOE_CAP, d), BF16),
                   jax.ShapeDtypeStruct((nblk, LANES, tb), F32),
                   jax.ShapeDtypeStruct((nblk, LANES, tb), F32),
                   jax.ShapeDtypeStruct((nblk, 1, LANES), jnp.int32)],
        scratch_shapes=[pltpu.VMEM((N_EXPERTS * MOE_CAP, tb), BF16)],
        compiler_params=_params(("arbitrary",)),
        name="moe_router",
    )(a, bb, wa, wb, hf, norm_g[None, :], wrh, wrl, br, tri)


def _expert_mlp_kernel(top_ref, x_ref, wg_ref, wu_ref, wd_ref, y_ref, wg_sc, wu_sc, wd_sc):
    e = pl.program_id(0)
    i = pl.program_id(1)

    @pl.when(i == 0)
    def _():
        wg_sc[...] = wg_ref[0, 0].astype(BF16)
        wu_sc[...] = wu_ref[0, 0].astype(BF16)
        wd_sc[...] = wd_ref[0, 0].astype(BF16)

    tr, d = x_ref.shape[1], x_ref.shape[2]
    nb = tr // MOE_CAP
    top = top_ref[e * pl.num_programs(1) + i]

    def run(slots):
        x = x_ref[0]
        if slots < MOE_CAP:
            x = x.reshape(nb, MOE_CAP, d)[:, :slots].reshape(nb * slots, d)
        hid = _silu(jnp.dot(x, wg_sc[...], preferred_element_type=F32)) * jnp.dot(
            x, wu_sc[...], preferred_element_type=F32)
        y = jnp.dot(hid.astype(BF16), wd_sc[...], preferred_element_type=F32).astype(BF16)
        if slots < MOE_CAP:
            pad = jnp.zeros((nb, MOE_CAP - slots, d), BF16)
            y = jnp.concatenate([y.reshape(nb, slots, d), pad], axis=1).reshape(tr, d)
        y_ref[0] = y

    levels = [lv for lv in MOE_SLOT_LEVELS if lv < MOE_CAP] + [MOE_CAP]
    for n, slots in enumerate(levels):
        above = top > levels[n - 1] if n else top >= 0
        below = top <= slots if n + 1 < len(levels) else top >= 0
        pl.when(above & below)(functools.partial(run, slots))


def _expert_mlp(xs, cnt, w_gate, w_up, w_down, layer):
    ne, rows, d = xs.shape
    ff = w_gate.shape[3]
    tr = _tile(rows, EXPERT_TR)
    ntiles = rows // tr
    top = jnp.max(cnt[:, 0, :ne].reshape(ntiles, -1, ne), axis=1).T.reshape(-1)
    grid_spec = pltpu.PrefetchScalarGridSpec(
        num_scalar_prefetch=1,
        grid=(ne, ntiles),
        in_specs=[pl.BlockSpec((1, tr, d), lambda e, i, t: (e, i, 0)),
                  pl.BlockSpec((1, 1, d, ff), lambda e, i, t: (layer, e, 0, 0)),
                  pl.BlockSpec((1, 1, d, ff), lambda e, i, t: (layer, e, 0, 0)),
                  pl.BlockSpec((1, 1, ff, d), lambda e, i, t: (layer, e, 0, 0))],
        out_specs=pl.BlockSpec((1, tr, d), lambda e, i, t: (e, i, 0)),
        scratch_shapes=[pltpu.VMEM((d, ff), BF16), pltpu.VMEM((d, ff), BF16), pltpu.VMEM((ff, d), BF16)],
    )
    return pl.pallas_call(
        _expert_mlp_kernel,
        grid_spec=grid_spec,
        out_shape=jax.ShapeDtypeStruct((ne, rows, d), BF16),
        compiler_params=_params(("arbitrary", "arbitrary")),
        name="moe_expert_mlp",
    )(top, xs, w_gate, w_up, w_down)


def _ple_rows(h, p, g_ref, wg_ref, bg_ref, wp_ref):
    gate = jax.nn.sigmoid(_bdot(_rms(h, g_ref[...]), wg_ref[...]) + bg_ref[...])
    return h + gate * _bdot(p, wp_ref[...])


def _combine_ple_kernel(cnt_ref, y_ref, keyt_ref, wt_ref, h_ref, p_ref, gffn_ref, gple_ref, wgate_ref, bgate_ref,
                        wproj_ref, wg_hbm, wu_hbm, wd_hbm, o_ref, sel_ref, acc_ref, m_ref, wg_buf, wu_buf, wd_buf,
                        sems, *, layer):
    blk = pl.program_id(0)
    tb = h_ref.shape[0]
    riota = lax.broadcasted_iota(jnp.int32, (MOE_CAP, tb), 0).astype(F32)
    for e in range(N_EXPERTS):
        hit = keyt_ref[0, e:e + 1, :] == riota
        sel_ref[e * MOE_CAP:(e + 1) * MOE_CAP, :] = jnp.where(hit, wt_ref[0, e:e + 1, :], 0.0).astype(BF16)
    y = y_ref[...].reshape(N_EXPERTS * MOE_CAP, y_ref.shape[2])
    hm = h_ref[...] + lax.dot_general(sel_ref[...], y, (((0,), (0,)), ((), ())), preferred_element_type=F32)

    most = lax.fori_loop(0, N_EXPERTS, lambda e, mx: jnp.maximum(mx, cnt_ref[blk * LANES + e]), 0)

    @pl.when(most <= MOE_CAP)
    def _():
        o_ref[...] = _ple_rows(hm, p_ref[0], gple_ref, wgate_ref, bgate_ref, wproj_ref)

    @pl.when(most > MOE_CAP)
    def _():
        acc_ref[...] = hm
        m_ref[...] = _rms(h_ref[...], gffn_ref[...]).astype(BF16)
        rows = lax.broadcasted_iota(jnp.int32, (MOE_ROWS, tb), 0).astype(F32)

        def expert(e, carry):
            extra = cnt_ref[blk * LANES + e] - MOE_CAP

            @pl.when(extra > 0)
            def _():
                copies = [pltpu.make_async_copy(src.at[layer, e], dst, sems.at[n])
                          for n, (src, dst) in enumerate(((wg_hbm, wg_buf), (wu_hbm, wu_buf), (wd_hbm, wd_buf)))]
                for c in copies:
                    c.start()
                for c in copies:
                    c.wait()
                krow = keyt_ref[0, pl.ds(e, 1), :]
                wrow = wt_ref[0, pl.ds(e, 1), :]

                def chunk(ci, c2):
                    hit = krow == (rows + (MOE_CAP + ci * MOE_ROWS).astype(F32))
                    sel = jnp.where(hit, 1.0, 0.0).astype(BF16)
                    xg = jnp.dot(sel, m_ref[...], preferred_element_type=F32).astype(BF16)
                    hid = _silu(_bdot(xg, wg_buf[...])) * _bdot(xg, wu_buf[...])
                    yo = _bdot(hid, wd_buf[...]).astype(BF16)
                    acc_ref[...] += _bdot_tn(jnp.where(hit, wrow, 0.0), yo)
                    return c2

                lax.fori_loop(0, (extra + MOE_ROWS - 1) // MOE_ROWS, chunk, 0)

            return carry

        lax.fori_loop(0, N_EXPERTS, expert, 0)
        o_ref[...] = _ple_rows(acc_ref[...], p_ref[0], gple_ref, wgate_ref, bgate_ref, wproj_ref)


def _moe_combine_ple(cnt, ys, keyt, wt, hf, norm_ffn, w_gate, w_up, w_down, layer, p_all, ple_norm, ple_w_gate,
                     ple_b_gate, ple_w_proj, tb):
    n, d = hf.shape
    ff = w_gate.shape[3]
    nblk = n // tb
    pd = p_all.shape[-1]
    blk = pl.BlockSpec((1, LANES, tb), lambda i, c: (i, 0, 0))
    tok = pl.BlockSpec((tb, d), lambda i, c: (i, 0))
    vec = pl.BlockSpec((1, d), lambda i, c: (0, 0))
    hbm = pl.BlockSpec(memory_space=pl.ANY)
    grid_spec = pltpu.PrefetchScalarGridSpec(
        num_scalar_prefetch=1,
        grid=(nblk,),
        in_specs=[pl.BlockSpec((N_EXPERTS, MOE_CAP, d), lambda i, c: (0, i, 0)), blk, blk, tok,
                  pl.BlockSpec((1, tb, pd), lambda i, c: (layer, i, 0)), vec, vec,
                  pl.BlockSpec((d, d), lambda i, c: (0, 0)), vec, pl.BlockSpec((pd, d), lambda i, c: (0, 0)),
                  hbm, hbm, hbm],
        out_specs=tok,
        scratch_shapes=[pltpu.VMEM((N_EXPERTS * MOE_CAP, tb), BF16), pltpu.VMEM((tb, d), F32),
                        pltpu.VMEM((tb, d), BF16), pltpu.VMEM((d, ff), F32), pltpu.VMEM((d, ff), F32),
                        pltpu.VMEM((ff, d), F32), pltpu.SemaphoreType.DMA((3,))],
    )
    return pl.pallas_call(
        functools.partial(_combine_ple_kernel, layer=layer),
        grid_spec=grid_spec,
        out_shape=jax.ShapeDtypeStruct((n, d), F32),
        compiler_params=_params(("arbitrary",)),
        name="moe_combine_ple",
    )(cnt, ys, keyt, wt, hf, p_all.reshape(p_all.shape[0], n, pd), norm_ffn[None, :], ple_norm[None, :],
      ple_w_gate.astype(BF16), ple_b_gate[None, :], ple_w_proj.astype(BF16), w_gate, w_up, w_down)


def _proj_moe_ple(mix_a, mix_b, w_out, h, norm_g, w_group, b_group, w_expert, b_expert, w_gate, w_up, w_down,
                  layer, p_all, ple_w_proj, ple_norm, ple_w_gate, ple_b_gate):
    b, s, d = h.shape
    n = b * s
    tb = _tile(n, MOE_TB)
    hf, xs, keyt, wt, cnt = _moe_router(mix_a.reshape(n, -1), mix_b.reshape(n, -1), w_out, h.reshape(n, d),
                                        norm_g, w_group, b_group, w_expert, b_expert, tb)
    ys = _expert_mlp(xs, cnt, w_gate, w_up, w_down, layer)
    out = _moe_combine_ple(cnt.reshape(-1), ys, keyt, wt, hf, norm_g, w_gate, w_up, w_down, layer, p_all,
                           ple_norm, ple_w_gate, ple_b_gate, ple_w_proj, tb)
    return out.reshape(b, s, d)


def _even_mixers(h, positions, norm_mix, w_in, b_f, fox_qn, fox_kn, q_a_norm, w_q_up, kv_a_norm, w_kv_up,
                 mla_qn, mla_kn):
    fq, fk, fv, cum, mq, mk, mv = _even_pre(h, positions, norm_mix, w_in, b_f, fox_qn, fox_kn, q_a_norm,
                                            w_q_up, kv_a_norm, w_kv_up, mla_qn, mla_kn)
    return _attention(fq, fk, fv, cum), _attention(mq, mk, mv)


def kernel(x, p, positions, norm_mix, norm_ffn, ev_w_in, fox_b_f, fox_q_norm, fox_k_norm, mla_q_a_norm, mla_w_q_up, mla_kv_a_norm, mla_w_kv_up, mla_q_norm, mla_k_norm, ev_w_out, od_w_in, s5_a_re, s5_a_im, s5_b_re, s5_b_im, s5_c_re, s5_c_im, s5_d, s5_log_step, s5_w_glu, s5_b_glu, gdn_conv_w, gdn_a_log, gdn_dt_bias, gdn_o_norm, od_w_out, moe_w_group, moe_b_group, moe_w_expert, moe_b_expert, moe_w_gate, moe_w_up, moe_w_down, ple_w_proj, ple_norm, ple_w_gate, ple_b_gate):
    h = x
    depth = p.shape[0]
    for i in range(depth):
        j = i // 2
        if i % 2 == 0:
            mix = _even_mixers(h, positions, norm_mix[i], ev_w_in[j], fox_b_f[j], fox_q_norm[j], fox_k_norm[j],
                               mla_q_a_norm[j], mla_w_q_up[j], mla_kv_a_norm[j], mla_w_kv_up[j], mla_q_norm[j],
                               mla_k_norm[j])
            w_out = ev_w_out[j]
        else:
            mix = _odd_mixers(h, norm_mix[i], od_w_in[j], s5_a_re[j], s5_a_im[j], s5_b_re[j], s5_b_im[j],
                              s5_c_re[j], s5_c_im[j], s5_d[j], s5_log_step[j], s5_w_glu[j], s5_b_glu[j],
                              gdn_conv_w[j], gdn_a_log[j], gdn_dt_bias[j], gdn_o_norm[j])
            w_out = od_w_out[j]
        h = _proj_moe_ple(mix[0], mix[1], w_out, h, norm_ffn[i], moe_w_group[i], moe_b_group[i], moe_w_expert[i],
                          moe_b_expert[i], moe_w_gate, moe_w_up, moe_w_down, i, p, ple_w_proj[i], ple_norm[i],
                          ple_w_gate[i], ple_b_gate[i])
    return h
```

```python
import functools
import math

import numpy as np
import jax
import jax.numpy as jnp
from jax import lax
from jax.experimental import pallas as pl
from jax.experimental.pallas import tpu as pltpu

F32 = jnp.float32
BF16 = jnp.bfloat16

LANES = 128
RMS_EPS = 1e-6
ROPE_THETA = 10000.0
LOG2E = math.log2(math.e)

FOX_HEADS = 8
FOX_HEAD_DIM = 64
MLA_HEADS = 8
MLA_Q_LORA = 384
MLA_KV_LORA = 256
MLA_NOPE = 64
MLA_ROPE = 32
MLA_V = 64
MLA_QK = MLA_NOPE + MLA_ROPE

S5_CH = 512
S5_GROUP_CH = 16
S5_GROUPS = S5_CH // S5_GROUP_CH
S5_STATE = 64
S5_N = S5_GROUPS * S5_STATE

GDN_HEADS = 4
GDN_HEAD_DIM = 128
GDN_W = GDN_HEADS * GDN_HEAD_DIM
GDN_CONV = 4
GDN_CHUNK = 64

MOE_GROUPS = 4
MOE_PER_GROUP = 8
N_EXPERTS = MOE_GROUPS * MOE_PER_GROUP
MOE_TB = 512
MOE_CAP = 64
MOE_ROWS = 128
MOE_SLOT_LEVELS = (32, 48)
ATTN_TQ = 2048
ATTN_TK = 512
MIX_TM = 256
EXPERT_TR = 512

VMEM_LIMIT = 56 * 1024 * 1024


def _tile(n, pref):
    t = min(n, pref)
    assert n % t == 0, (n, t)
    return t


def _params(sem):
    return pltpu.CompilerParams(dimension_semantics=sem, vmem_limit_bytes=VMEM_LIMIT)


def _full(shape):
    nd = len(shape)
    return pl.BlockSpec(shape, lambda *_: (0,) * nd)


def _rms(x, g):
    return x * lax.rsqrt(jnp.mean(x * x, axis=-1, keepdims=True) + RMS_EPS) * g


def _bdot(a, b):
    return jnp.dot(a.astype(BF16), b.astype(BF16), preferred_element_type=F32)


def _bdot_nt(a, b):
    return lax.dot_general(a.astype(BF16), b.astype(BF16), (((1,), (1,)), ((), ())),
                           preferred_element_type=F32)


def _bdot_tn(a, b):
    return lax.dot_general(a.astype(BF16), b.astype(BF16), (((0,), (0,)), ((), ())),
                           preferred_element_type=F32)


def _split3(x):
    x1 = x.astype(BF16)
    r = x - x1.astype(F32)
    x2 = r.astype(BF16)
    return x1, x2, (r - x2.astype(F32)).astype(BF16)


def _sel_dot(sel, x):
    return sum(jnp.dot(sel, part, preferred_element_type=F32) for part in _split3(x))


def _dot_sel(x, sel):
    return sum(jnp.dot(part, sel, preferred_element_type=F32) for part in _split3(x))


def _split_dot(x, ind):
    hi = x.astype(BF16)
    lo = (x - hi.astype(F32)).astype(BF16)
    return (jnp.dot(hi, ind, preferred_element_type=F32)
            + jnp.dot(lo, ind, preferred_element_type=F32))


def _log_sigmoid(x):
    return jnp.minimum(x, 0.0) - jnp.log(1.0 + jnp.exp(-jnp.abs(x)))


def _softplus(x):
    return jnp.maximum(x, 0.0) + jnp.log(1.0 + jnp.exp(-jnp.abs(x)))


def _silu(x):
    return x * jax.nn.sigmoid(x)


def _head_norm128(x, nheads, denom, gain):
    outs = []
    for hh in range(nheads):
        xh = x[:, LANES * hh:LANES * (hh + 1)]
        ss = jnp.sum(xh * xh, axis=-1, keepdims=True)
        outs.append(xh * lax.rsqrt(ss / denom + RMS_EPS))
    return jnp.concatenate(outs, axis=1) * gain


def _even_pre_kernel(h_ref, pos_ref, nmix_ref, win_ref, ind_ref, fqn_ref, fkn_ref, bf_ref,
                     qan_ref, wq_ref, kvan_ref, wkv_ref, mqn_ref, mkn_ref, freq_ref, s1_ref, s2_ref,
                     tri_ref, vone_ref, fq_o, fk_o, fv_o, cum_o, mq_o, mk_o, mv_o, carry_ref):
    t = pl.program_id(1)

    @pl.when(t == 0)
    def _():
        carry_ref[...] = jnp.zeros_like(carry_ref)

    tm = h_ref.shape[1]
    a = _rms(h_ref[0], nmix_ref[...])
    proj = _bdot(a, win_ref[...])
    nf = FOX_HEADS * FOX_HEAD_DIM
    fq = proj[:, 0:nf]
    fk = proj[:, nf:2 * nf]
    nv = FOX_HEADS * LANES
    fv = proj[:, 2 * nf:2 * nf + nv]
    o_cq = 2 * nf + nv
    cq = proj[:, o_cq:o_cq + MLA_Q_LORA]
    o_ckv = o_cq + MLA_Q_LORA
    ckv = proj[:, o_ckv:o_ckv + MLA_KV_LORA]
    misc = proj[:, o_ckv + MLA_KV_LORA:]

    ind = ind_ref[...]
    fq_n = fq * lax.rsqrt(_split_dot(fq * fq, ind) / FOX_HEAD_DIM + RMS_EPS) * fqn_ref[...]
    fk_n = fk * lax.rsqrt(_split_dot(fk * fk, ind) / FOX_HEAD_DIM + RMS_EPS) * fkn_ref[...]
    fq_o[0] = (fq_n * (FOX_HEAD_DIM ** -0.5 * LOG2E)).astype(BF16)
    fk_o[0] = fk_n.astype(BF16)
    fv_o[0] = (fv + vone_ref[...]).astype(BF16)

    lane = lax.broadcasted_iota(jnp.int32, (tm, LANES), 1)
    logf = jnp.where(lane < FOX_HEADS, _log_sigmoid(misc + bf_ref[...]), 0.0)
    cum = _sel_dot(tri_ref[...], logf) + carry_ref[...]
    carry_ref[...] = cum[tm - 1:tm, :]
    cum_o[0] = (cum * LOG2E).T[:FOX_HEADS, :]

    ang = pos_ref[0].astype(F32) * freq_ref[...]
    cos1 = jnp.cos(ang)
    sin1 = jnp.sin(ang)
    cos = jnp.concatenate([cos1] * MLA_HEADS, axis=1)
    sin_a = jnp.concatenate([sin1 * s1_ref[...]] * MLA_HEADS, axis=1)
    sin_b = jnp.concatenate([sin1 * s2_ref[...]] * MLA_HEADS, axis=1)
    width = MLA_HEADS * LANES
    half = MLA_ROPE // 2

    def rope(x):
        return (x * cos + pltpu.roll(x, width - half, 1) * sin_a + pltpu.roll(x, half, 1) * sin_b)

    q = _bdot(_rms(cq, qan_ref[...]), wq_ref[...])
    q = rope(_head_norm128(q, MLA_HEADS, MLA_QK, mqn_ref[...]))
    mq_o[0] = (q * (MLA_QK ** -0.5 * LOG2E)).astype(BF16)

    kv = _bdot(_rms(ckv, kvan_ref[...]), wkv_ref[...])
    kr = pltpu.roll(misc, MLA_NOPE - FOX_HEADS, 1)
    kr = jnp.where((lane >= MLA_NOPE) & (lane < MLA_QK), kr, 0.0)
    k = kv[:, :width] + jnp.concatenate([kr] * MLA_HEADS, axis=1)
    k = rope(_head_norm128(k, MLA_HEADS, MLA_QK, mkn_ref[...]))
    mk_o[0] = k.astype(BF16)
    mv_o[0] = (kv[:, width:] + vone_ref[...]).astype(BF16)


def _even_pre(h, positions, norm_mix, w_in, b_f, fox_qn, fox_kn, q_a_norm, w_q_up, kv_a_norm, w_kv_up,
              mla_qn, mla_kn):
    b, s, d = h.shape
    tm = _tile(s, MIX_TM)
    nf = FOX_HEADS * FOX_HEAD_DIM
    sizes = (nf, nf, nf, FOX_HEADS, MLA_Q_LORA, MLA_KV_LORA, MLA_ROPE)
    offs = np.concatenate([[0], np.cumsum(sizes)])
    parts = [w_in[:, offs[i]:offs[i + 1]] for i in range(len(sizes))]
    pad = jnp.zeros((d, LANES - FOX_HEADS - MLA_ROPE), w_in.dtype)
    slot_pad = ((0, 0), (0, 0), (0, LANES - FOX_HEAD_DIM))
    wfv = jnp.pad(parts[2].reshape(d, FOX_HEADS, FOX_HEAD_DIM), slot_pad).reshape(d, FOX_HEADS * LANES)
    win = jnp.concatenate([parts[0], parts[1], wfv, parts[4], parts[5], parts[3], parts[6], pad],
                          axis=1).astype(BF16)
    gidx = np.arange(nf) // FOX_HEAD_DIM
    ind = jnp.asarray(gidx[:, None] == gidx[None, :], BF16)
    fqn = jnp.tile(fox_qn, FOX_HEADS)[None, :]
    fkn = jnp.tile(fox_kn, FOX_HEADS)[None, :]
    bf = jnp.zeros((1, LANES), F32).at[0, :FOX_HEADS].set(b_f)
    padq = LANES - MLA_QK
    wq = jnp.pad(w_q_up.reshape(MLA_Q_LORA, MLA_HEADS, MLA_QK), ((0, 0), (0, 0), (0, padq)))
    wq = wq.reshape(MLA_Q_LORA, MLA_HEADS * LANES).astype(BF16)
    wkv3 = w_kv_up.reshape(MLA_KV_LORA, MLA_HEADS, MLA_NOPE + MLA_V)
    wk = jnp.pad(wkv3[:, :, :MLA_NOPE], ((0, 0), (0, 0), (0, LANES - MLA_NOPE)))
    wv = jnp.pad(wkv3[:, :, MLA_NOPE:], ((0, 0), (0, 0), (0, LANES - MLA_V)))
    wkv = jnp.concatenate([wk.reshape(MLA_KV_LORA, MLA_HEADS * LANES),
                           wv.reshape(MLA_KV_LORA, MLA_HEADS * LANES)], axis=1).astype(BF16)
    vone = jnp.tile(jnp.zeros((LANES,), F32).at[MLA_V].set(1.0), MLA_HEADS)[None, :]
    mqn = jnp.tile(jnp.pad(mla_qn, (0, padq)), MLA_HEADS)[None, :]
    mkn = jnp.tile(jnp.pad(mla_kn, (0, padq)), MLA_HEADS)[None, :]
    half = MLA_ROPE // 2
    inv = ROPE_THETA ** (-jnp.arange(half, dtype=F32) * 2.0 / MLA_ROPE)
    freq = jnp.zeros((1, LANES), F32).at[0, MLA_NOPE:MLA_NOPE + half].set(inv)
    freq = freq.at[0, MLA_NOPE + half:MLA_QK].set(inv)
    s1 = jnp.zeros((1, LANES), F32).at[0, MLA_NOPE:MLA_NOPE + half].set(-1.0)
    s2 = jnp.zeros((1, LANES), F32).at[0, MLA_NOPE + half:MLA_QK].set(1.0)
    tri = jnp.asarray(np.tril(np.ones((tm, tm), np.float32)), BF16)
    pos3 = positions.reshape(b, s, 1)

    row = lambda n: pl.BlockSpec((1, tm, n), lambda bi, ti: (bi, ti, 0))
    consts = [norm_mix[None, :], win, ind, fqn, fkn, bf, q_a_norm[None, :], wq, kv_a_norm[None, :], wkv,
              mqn, mkn, freq, s1, s2, tri, vone]
    nv = FOX_HEADS * LANES
    out_shape = [jax.ShapeDtypeStruct((b, s, nf), BF16)] * 2 + [jax.ShapeDtypeStruct((b, s, nv), BF16)] + [
        jax.ShapeDtypeStruct((b, FOX_HEADS, s), F32),
        jax.ShapeDtypeStruct((b, s, MLA_HEADS * LANES), BF16),
        jax.ShapeDtypeStruct((b, s, MLA_HEADS * LANES), BF16),
        jax.ShapeDtypeStruct((b, s, MLA_HEADS * LANES), BF16)]
    return pl.pallas_call(
        _even_pre_kernel,
        grid=(b, s // tm),
        in_specs=[row(d), row(1)] + [_full(c.shape) for c in consts],
        out_specs=[row(nf), row(nf), row(nv), pl.BlockSpec((1, FOX_HEADS, tm), lambda bi, ti: (bi, 0, ti)),
                   row(MLA_HEADS * LANES), row(MLA_HEADS * LANES), row(MLA_HEADS * LANES)],
        out_shape=out_shape,
        scratch_shapes=[pltpu.VMEM((1, LANES), F32)],
        compiler_params=_params(("arbitrary", "arbitrary")),
        name="even_pre",
    )(h, pos3, *consts)


def _attn_kernel(*refs, tq, tk, fox):
    if fox:
        q_ref, k_ref, v_ref, cr_ref, o_ref = refs
    else:
        q_ref, k_ref, v_ref, o_ref = refs
    hp = pl.program_id(1)
    i = pl.program_id(2)
    lane = lax.broadcasted_iota(jnp.int32, (tq, LANES), 1)
    qs = []
    for hh in range(2):
        if fox:
            in_head = (lane >= FOX_HEAD_DIM * hh) & (lane < FOX_HEAD_DIM * (hh + 1))
            qs.append(jnp.where(in_head, q_ref[0], jnp.zeros((), BF16)))
        else:
            qs.append(q_ref[0, :, LANES * hh:LANES * (hh + 1)])

    def step(j, carry, lo=None):
        koff = pl.multiple_of(j * tk, tk)
        top = 0 if lo is None else lo
        new = []
        for hh in range(2):
            m, acc = carry[hh]
            if fox:
                kj = k_ref[0, pl.ds(koff, tk), :]
            else:
                kj = k_ref[0, pl.ds(koff, tk), LANES * hh:LANES * (hh + 1)]
            sc = lax.dot_general(qs[hh][top:], kj, (((1,), (1,)), ((), ())), preferred_element_type=F32)
            if fox:
                sc = sc - cr_ref[0, pl.ds(2 * hp + hh, 1), pl.ds(koff, tk)]
            if lo is not None:
                rowi = lax.broadcasted_iota(jnp.int32, sc.shape, 0)
                coli = lax.broadcasted_iota(jnp.int32, sc.shape, 1)
                sc = jnp.where(coli <= rowi, sc, -jnp.inf)
            m_new = jnp.maximum(m[top:], jnp.max(sc, axis=-1, keepdims=True))
            alpha = jnp.exp2(m[top:] - m_new)
            p = jnp.exp2((sc - jnp.concatenate([m_new] * (tk // LANES), axis=1)).astype(BF16))
            vj = v_ref[0, pl.ds(koff, tk), LANES * hh:LANES * (hh + 1)]
            acc_new = alpha * acc[top:] + jnp.dot(p, vj, preferred_element_type=F32)
            if top:
                m_new = jnp.concatenate([m[:top], m_new], axis=0)
                acc_new = jnp.concatenate([acc[:top], acc_new], axis=0)
            new.append((m_new, acc_new))
        return tuple(new)

    def body(jj, carry):
        for r in range(ratio):
            carry = step(jj * ratio + r, carry)
        return carry

    one = (jnp.full((tq, LANES), -jnp.inf, F32), jnp.zeros((tq, LANES), F32))
    ratio = tq // tk
    carry = lax.fori_loop(0, i, body, (one, one))
    for r in range(ratio):
        carry = step(i * ratio + r, carry, lo=r * tk)
    outs = [acc / acc[:, MLA_V:MLA_V + 1] for _, acc in carry]
    o_ref[0] = jnp.where(lane < MLA_V, outs[0], pltpu.roll(outs[1], MLA_V, 1)).astype(o_ref.dtype)


def _attention(q, k, v, cum_row=None):
    b, s, _ = v.shape
    fox = cum_row is not None
    qw = LANES if fox else 2 * LANES
    tq = _tile(s, ATTN_TQ)
    tk = _tile(tq, ATTN_TK)
    npairs = v.shape[2] // (2 * LANES)
    in_specs = [pl.BlockSpec((1, tq, qw), lambda bi, hp, i: (bi, i, hp)),
                pl.BlockSpec((1, s, qw), lambda bi, hp, i: (bi, 0, hp)),
                pl.BlockSpec((1, s, 2 * LANES), lambda bi, hp, i: (bi, 0, hp))]
    args = [q, k, v]
    if fox:
        in_specs += [pl.BlockSpec((1, FOX_HEADS, s), lambda bi, hp, i: (bi, 0, 0))]
        args += [cum_row]
    return pl.pallas_call(
        functools.partial(_attn_kernel, tq=tq, tk=tk, fox=fox),
        grid=(b, npairs, s // tq),
        in_specs=in_specs,
        out_specs=pl.BlockSpec((1, tq, LANES), lambda bi, hp, i: (bi, i, hp)),
        out_shape=jax.ShapeDtypeStruct((b, s, npairs * LANES), BF16),
        compiler_params=_params(("arbitrary", "arbitrary", "arbitrary")),
        name="fox_attention" if fox else "mla_attention",
    )(*args)


N_S5_CONSTS = 13
N_GDN_CONSTS = 6


def _odd_kernel(h_ref, nmix_ref, win_ref, *refs):
    s5_consts = refs[:N_S5_CONSTS]
    gdn_consts = refs[N_S5_CONSTS:N_S5_CONSTS + N_GDN_CONSTS]
    y_o, o_o, x_ref, sr_ref, si_ref, xpad_ref, state_ref = refs[N_S5_CONSTS + N_GDN_CONSTS:]
    a = _rms(h_ref[0], nmix_ref[...])
    proj = _bdot(a, win_ref[...])
    o1 = S5_CH
    o2 = o1 + 3 * GDN_W
    o3 = o2 + GDN_W
    _s5_body(proj[:, :o1], *s5_consts, y_o, x_ref, sr_ref, si_ref)
    _gdn_body(proj[:, o1:o2], proj[:, o2:o3], proj[:, o3:], *gdn_consts, o_o, xpad_ref, state_ref)


def _odd_mixers(h, norm_mix, w_in, a_re, a_im, b_re, b_im, c_re, c_im, d_skip, log_step, w_glu, b_glu,
                conv_w, a_log, dt_bias, o_norm):
    b, s, d = h.shape
    tm = _tile(s, MIX_TM)
    sizes = (S5_CH, 3 * GDN_W, GDN_HEADS, GDN_HEADS, GDN_W)
    offs = np.concatenate([[0], np.cumsum(sizes)])
    parts = [w_in[:, offs[i]:offs[i + 1]] for i in range(len(sizes))]
    pad = jnp.zeros((d, LANES - 2 * GDN_HEADS), w_in.dtype)
    win = jnp.concatenate([parts[0], parts[1], parts[4], parts[2], parts[3], pad], axis=1).astype(BF16)
    s5_consts = _s5_consts(tm, a_re, a_im, b_re, b_im, c_re, c_im, d_skip, log_step, w_glu, b_glu)
    gdn_consts = _gdn_consts(tm, conv_w, a_log, dt_bias, o_norm)
    assert len(s5_consts) == N_S5_CONSTS and len(gdn_consts) == N_GDN_CONSTS
    consts = [norm_mix[None, :], win] + s5_consts + gdn_consts
    row = lambda n: pl.BlockSpec((1, tm, n), lambda bi, ti: (bi, ti, 0))
    return pl.pallas_call(
        _odd_kernel,
        grid=(b, s // tm),
        in_specs=[row(d)] + [_full(c.shape) for c in consts],
        out_specs=[row(S5_CH), row(GDN_W)],
        out_shape=[jax.ShapeDtypeStruct((b, s, S5_CH), BF16), jax.ShapeDtypeStruct((b, s, GDN_W), BF16)],
        scratch_shapes=[pltpu.VMEM((tm, 2 * S5_N), F32), pltpu.VMEM((1, S5_N), F32), pltpu.VMEM((1, S5_N), F32),
                        pltpu.VMEM((tm + 8, 3 * GDN_W), F32),
                        pltpu.VMEM((GDN_HEADS, GDN_HEAD_DIM, GDN_HEAD_DIM), F32)],
        compiler_params=_params(("arbitrary", "arbitrary")),
        name="odd_mixers",
    )(h, *consts)


def _s5_body(u, perm_ref, unperm_ref, bbd_ref, cbd_ref, ar_ref, ai_ref, asr_ref, asi_ref, pwr_ref, pwi_ref,
             d_ref, wglu_ref, bglu_ref, o_ref, x_ref, sr_ref, si_ref):
    t = pl.program_id(1)

    @pl.when(t == 0)
    def _():
        sr_ref[...] = jnp.zeros_like(sr_ref)
        si_ref[...] = jnp.zeros_like(si_ref)

    tm = u.shape[0]
    nseg = 8
    seg = tm // nseg
    u = _sel_dot(perm_ref[...], u)
    hc = S5_CH // 2
    hn = S5_N // 2
    ub = u.astype(BF16)
    for part in range(2):
        for base in (0, S5_N):
            cols = slice(base + part * hn, base + (part + 1) * hn)
            x_ref[:, cols] = jnp.dot(ub[:, part * hc:(part + 1) * hc], bbd_ref[part * hc:(part + 1) * hc, cols],
                                     preferred_element_type=F32)
    ar = ar_ref[...]
    ai = ai_ref[...]
    re = slice(0, S5_N)
    im = slice(S5_N, 2 * S5_N)

    def local(i, carry):
        xr, xi = carry
        rows = pl.ds(pl.multiple_of(i * nseg, nseg), nseg)
        nr = ar * xr - ai * xi + x_ref[rows, re]
        ni = ar * xi + ai * xr + x_ref[rows, im]
        x_ref[rows, re] = nr
        x_ref[rows, im] = ni
        return nr, ni

    zero = jnp.zeros((nseg, S5_N), F32)
    er, ei = lax.fori_loop(0, seg, local, (zero, zero), unroll=4)

    asr = asr_ref[...]
    asi = asi_ref[...]
    cr = [sr_ref[...]]
    ci = [si_ref[...]]
    for s in range(nseg):
        cr.append(asr * cr[s] - asi * ci[s] + er[s:s + 1, :])
        ci.append(asr * ci[s] + asi * cr[s] + ei[s:s + 1, :])
    sr_ref[...] = cr[nseg]
    si_ref[...] = ci[nseg]
    ent_r = jnp.concatenate(cr[:nseg], axis=0)
    ent_i = jnp.concatenate(ci[:nseg], axis=0)

    def fix(i, c):
        rows = pl.ds(pl.multiple_of(i * nseg, nseg), nseg)
        pr = pwr_ref[pl.ds(i, 1), :]
        pi = pwi_ref[pl.ds(i, 1), :]
        x_ref[rows, re] += pr * ent_r - pi * ent_i
        x_ref[rows, im] += pr * ent_i + pi * ent_r
        return c

    lax.fori_loop(0, seg, fix, 0, unroll=4)
    ys = []
    for part in range(2):
        oc = slice(part * hc, (part + 1) * hc)
        acc = None
        for base in (0, S5_N):
            rows = slice(base + part * hn, base + (part + 1) * hn)
            term = _bdot(x_ref[:, rows], cbd_ref[rows, oc])
            acc = term if acc is None else acc + term
        ys.append(acc)
    y = jnp.concatenate(ys, axis=1) + d_ref[...] * u
    hg = jax.nn.gelu(y)
    out = (hg * jax.nn.sigmoid(_bdot(hg, wglu_ref[...]) + bglu_ref[...])).astype(BF16)
    o_ref[0] = jnp.dot(unperm_ref[...], out, preferred_element_type=F32).astype(o_ref.dtype)


def _s5_consts(tm, a_re, a_im, b_re, b_im, c_re, c_im, d_skip, log_step, w_glu, b_glu):
    lam_re = jnp.minimum(a_re, -1e-4)
    lam_im = a_im
    dt = jnp.exp(log_step)[:, None]
    mag = jnp.exp(lam_re * dt)
    ab_re = mag * jnp.cos(lam_im * dt)
    ab_im = mag * jnp.sin(lam_im * dt)
    den = lam_re * lam_re + lam_im * lam_im
    nr, ni = ab_re - 1.0, ab_im
    gam_re = (nr * lam_re + ni * lam_im) / den
    gam_im = (ni * lam_re - nr * lam_im) / den
    bb_re = gam_re[..., None] * b_re - gam_im[..., None] * b_im
    bb_im = gam_re[..., None] * b_im + gam_im[..., None] * b_re
    eye = jnp.eye(S5_GROUPS, dtype=F32)
    bd_in = lambda m: jnp.einsum('gpc,gh->gchp', m, eye).reshape(S5_CH, S5_N)
    bd_out = lambda m: jnp.einsum('gcp,gh->gphc', m, eye).reshape(S5_N, S5_CH)
    bbd = jnp.concatenate([bd_in(bb_re), bd_in(bb_im)], axis=1).astype(BF16)
    cbd = jnp.concatenate([bd_out(c_re), -bd_out(c_im)], axis=0).astype(BF16)
    seg = tm // 8
    steps = jnp.arange(1, seg + 1, dtype=F32)[:, None, None] * dt[None]
    pmag = jnp.exp(lam_re[None] * steps)
    pw_re = (pmag * jnp.cos(lam_im[None] * steps)).reshape(seg, S5_N)
    pw_im = (pmag * jnp.sin(lam_im[None] * steps)).reshape(seg, S5_N)
    src = (np.arange(tm) % 8) * seg + np.arange(tm) // 8
    perm = np.zeros((tm, tm), np.float32)
    perm[np.arange(tm), src] = 1.0
    return [jnp.asarray(perm, BF16), jnp.asarray(perm.T, BF16),
            bbd, cbd, ab_re.reshape(1, S5_N), ab_im.reshape(1, S5_N), pw_re[seg - 1:seg], pw_im[seg - 1:seg],
            pw_re, pw_im, d_skip[None, :], w_glu.astype(BF16), b_glu[None, :]]


def _gdn_body(x, z, gb, cw_ref, nega_ref, dtb_ref, onorm_ref, tril_ref, triu_ref, o_ref, xpad_ref, state_ref):
    t = pl.program_id(1)
    tm = x.shape[0]
    c = GDN_CHUNK
    hd = GDN_HEAD_DIM

    @pl.when(t == 0)
    def _():
        xpad_ref[0:8, :] = jnp.zeros((8, xpad_ref.shape[1]), F32)
        state_ref[...] = jnp.zeros_like(state_ref)

    @pl.when(t > 0)
    def _():
        xpad_ref[0:8, :] = xpad_ref[tm:tm + 8, :]

    xpad_ref[8:tm + 8, :] = x
    conv = cw_ref[0:1, :] * xpad_ref[pl.ds(8 - (GDN_CONV - 1), tm), :]
    for i in range(1, GDN_CONV):
        conv = conv + cw_ref[i:i + 1, :] * xpad_ref[pl.ds(8 - (GDN_CONV - 1) + i, tm), :]
    act = _silu(conv)

    def l2n(x):
        return x * lax.rsqrt(jnp.sum(x * x, axis=-1, keepdims=True) + RMS_EPS)

    g = nega_ref[...] * _softplus(gb + dtb_ref[...])
    beta = jax.nn.sigmoid(gb)
    gc = _sel_dot(tril_ref[...], g)
    gct = _dot_sel(g.T, triu_ref[...])

    ri = lax.broadcasted_iota(jnp.int32, (tm, tm), 0)
    ci = lax.broadcasted_iota(jnp.int32, (tm, tm), 1)
    same = (ri // c) == (ci // c)
    incl = same & (ri >= ci)
    strict = same & (ri > ci)
    eye = (ri == ci).astype(F32)
    offs = []
    bs = 1
    while bs < c:
        offs.append(((ri // (2 * bs)) == (ci // (2 * bs))) & ((ri % (2 * bs)) >= bs) & ((ci % (2 * bs)) < bs))
        bs *= 2
    nchunks = tm // c

    heads = range(GDN_HEADS)
    q = [l2n(act[:, hh * hd:(hh + 1) * hd]) * (hd ** -0.5) for hh in heads]
    k = [l2n(act[:, GDN_W + hh * hd:GDN_W + (hh + 1) * hd]) for hh in heads]
    v = [act[:, 2 * GDN_W + hh * hd:2 * GDN_W + (hh + 1) * hd] for hh in heads]
    bcol = [beta[:, GDN_HEADS + hh:GDN_HEADS + hh + 1] for hh in heads]
    gcol = [gc[:, hh:hh + 1] for hh in heads]
    decay = [jnp.where(incl, jnp.exp(jnp.where(incl, gcol[hh] - gct[hh:hh + 1, :], 0.0)), 0.0) for hh in heads]
    kb = [k[hh] * bcol[hh] for hh in heads]
    a_mat = [jnp.where(strict, _bdot_nt(kb[hh], k[hh]) * decay[hh], 0.0) for hh in heads]
    t_mat = [eye - jnp.where(offs[0], a_mat[hh], 0.0) for hh in heads]
    for off in offs[1:]:
        pa = [_bdot(t_mat[hh], jnp.where(off, a_mat[hh], 0.0)) for hh in heads]
        t_mat = [t_mat[hh] - _bdot(pa[hh], t_mat[hh]) for hh in heads]
    th = [t_mat[hh].astype(BF16) for hh in heads]
    tl = [(t_mat[hh] - th[hh].astype(F32)).astype(BF16) for hh in heads]
    ah = [a_mat[hh].astype(BF16) for hh in heads]
    al = [(a_mat[hh] - ah[hh].astype(F32)).astype(BF16) for hh in heads]
    a_t = [jnp.dot(ah[hh], th[hh], preferred_element_type=F32) + jnp.dot(ah[hh], tl[hh], preferred_element_type=F32)
           + jnp.dot(al[hh], th[hh], preferred_element_type=F32) for hh in heads]
    t_mat = [t_mat[hh] + jnp.dot(th[hh], (eye - t_mat[hh] - a_t[hh]).astype(BF16), preferred_element_type=F32)
             for hh in heads]
    eg = [jnp.exp(gcol[hh]) for hh in heads]
    u = [_bdot(t_mat[hh], v[hh] * bcol[hh]) for hh in heads]
    w = [_bdot(t_mat[hh], kb[hh] * eg[hh]) for hh in heads]
    intra = [jnp.where(incl, _bdot_nt(q[hh], k[hh]) * decay[hh], 0.0).astype(BF16) for hh in heads]
    qd = [q[hh] * eg[hh] for hh in heads]
    state = [state_ref[hh] for hh in heads]
    for n in range(nchunks):
        r0 = n * c
        for hh in heads:
            lo = hh * hd
            gcn = gcol[hh][r0:r0 + c, :]
            glast = gcol[hh][r0 + c - 1:r0 + c, :]
            v_new = u[hh][r0:r0 + c, :] - _bdot(w[hh][r0:r0 + c, :], state[hh])
            v_rep = jnp.concatenate([v_new.astype(BF16)] * nchunks, axis=0)
            o = _bdot(qd[hh][r0:r0 + c, :], state[hh]) + jnp.dot(intra[hh][r0:r0 + c, :], v_rep,
                                                                  preferred_element_type=F32)
            state[hh] = state[hh] * jnp.exp(glast) + _bdot_tn(k[hh][r0:r0 + c, :] * jnp.exp(glast - gcn), v_new)
            on = o * lax.rsqrt(jnp.mean(o * o, axis=-1, keepdims=True) + RMS_EPS) * onorm_ref[...]
            o_ref[0, r0:r0 + c, lo:lo + hd] = (on * _silu(z[r0:r0 + c, lo:lo + hd])).astype(o_ref.dtype)
    for hh in heads:
        state_ref[hh] = state[hh]


def _gdn_consts(tm, conv_w, a_log, dt_bias, o_norm):
    nega = jnp.zeros((1, LANES), F32).at[0, :GDN_HEADS].set(-jnp.exp(a_log))
    dtb = jnp.zeros((1, LANES), F32).at[0, :GDN_HEADS].set(dt_bias)
    cwp = jnp.pad(conv_w, ((0, 8 - GDN_CONV), (0, 0)))
    ridx = np.arange(tm)
    same = (ridx[:, None] // GDN_CHUNK) == (ridx[None, :] // GDN_CHUNK)
    tril = jnp.asarray((same & (ridx[:, None] >= ridx[None, :])).astype(np.float32), BF16)
    triu = jnp.asarray((same & (ridx[:, None] <= ridx[None, :])).astype(np.float32), BF16)
    return [cwp, nega, dtb, o_norm[None, :], tril, triu]


def _router_kernel(a_ref, b_ref, wa_ref, wb_ref, h_ref, g_ref, wrh_ref, wrl_ref, br_ref, tri_ref,
                   h_o, xs_o, keyt_o, wt_o, cnt_o, sel_ref):
    tm = h_ref.shape[0]
    h = (h_ref[...] + jnp.dot(a_ref[...], wa_ref[...], preferred_element_type=F32)
         + jnp.dot(b_ref[...], wb_ref[...], preferred_element_type=F32))
    h_o[...] = h
    m = _rms(h, g_ref[...])
    mh = m.astype(BF16)
    ml = (m - mh.astype(F32)).astype(BF16)
    logits = (jnp.dot(mh, wrh_ref[...], preferred_element_type=F32) + jnp.dot(mh, wrl_ref[...], preferred_element_type=F32)
              + jnp.dot(ml, wrh_ref[...], preferred_element_type=F32)) + br_ref[...]
    lane = lax.broadcasted_iota(jnp.int32, (tm, LANES), 1)
    neg = -jnp.inf

    def first_argmax(x):
        mx = jnp.max(x, axis=-1, keepdims=True)
        idx = jnp.min(jnp.where(x == mx, lane, LANES), axis=-1, keepdims=True)
        return mx, idx

    is_g = (lane >= N_EXPERTS) & (lane < N_EXPERTS + MOE_GROUPS)
    gl = jnp.where(is_g, logits, neg)
    gmax, gidx = first_argmax(gl)
    g_w = 1.0 / jnp.sum(jnp.where(is_g, jnp.exp(gl - gmax), 0.0), axis=-1, keepdims=True)
    in_group = (lane // MOE_PER_GROUP) == (gidx - N_EXPERTS)
    el = jnp.where(in_group & (lane < N_EXPERTS), logits, neg)
    m1, i1 = first_argmax(el)
    el2 = jnp.where(lane == i1, neg, el)
    m2, i2 = first_argmax(el2)
    r = jnp.exp(m2 - m1)
    w1 = g_w / (1.0 + r)
    w2 = g_w * r / (1.0 + r)
    chose = (lane == i1) | (lane == i2)
    wmat = jnp.where(lane == i1, w1, jnp.where(lane == i2, w2, 0.0))
    ch = chose.astype(F32)
    rank = jnp.dot(tri_ref[...], ch.astype(BF16), preferred_element_type=F32)
    keyt = jnp.where(chose, rank, -1.0).T
    keyt_o[0] = keyt
    wt_o[0] = wmat.T
    cnt_o[0] = jnp.sum(ch, axis=0, keepdims=True).astype(jnp.int32)
    riota = lax.broadcasted_iota(jnp.int32, (MOE_CAP, tm), 0).astype(F32)
    for e in range(N_EXPERTS):
        sel_ref[e * MOE_CAP:(e + 1) * MOE_CAP, :] = jnp.where(keyt[e:e + 1, :] == riota, 1.0, 0.0).astype(BF16)
    xg = jnp.dot(sel_ref[...], mh, preferred_element_type=F32)
    xs_o[...] = xg.astype(BF16).reshape(xs_o.shape)


def _moe_router(a, bb, w_out, hf, norm_g, w_group, b_group, w_expert, b_expert, tb):
    n, d = hf.shape
    nblk = n // tb
    na, nb = a.shape[1], bb.shape[1]
    wa = w_out[:na].astype(BF16)
    wb = w_out[na:].astype(BF16)
    tok = lambda w: pl.BlockSpec((tb, w), lambda i: (i, 0))
    wr = jnp.zeros((d, LANES), F32).at[:, :N_EXPERTS].set(w_expert)
    wr = wr.at[:, N_EXPERTS:N_EXPERTS + MOE_GROUPS].set(w_group)
    wrh = wr.astype(BF16)
    wrl = (wr - wrh.astype(F32)).astype(BF16)
    br = jnp.zeros((1, LANES), F32).at[0, :N_EXPERTS].set(b_expert)
    br = br.at[0, N_EXPERTS:N_EXPERTS + MOE_GROUPS].set(b_group)
    tri = jnp.asarray(np.tril(np.ones((tb, tb), np.float32), -1), BF16)
    blk = pl.BlockSpec((1, LANES, tb), lambda i: (i, 0, 0))
    return pl.pallas_call(
        _router_kernel,
        grid=(nblk,),
        in_specs=[tok(na), tok(nb), _full(wa.shape), _full(wb.shape), tok(d), _full((1, d)), _full(wr.shape),
                  _full(wr.shape), _full(br.shape), _full(tri.shape)],
        out_specs=[tok(d), pl.BlockSpec((N_EXPERTS, MOE_CAP, d), lambda i: (0, i, 0)), blk, blk,
                   pl.BlockSpec((1, 1, LANES), lambda i: (i, 0, 0))],
        out_shape=[jax.ShapeDtypeStruct((n, d), F32),
                   jax.ShapeDtypeStruct((N_EXPERTS, nblk * MOE_CAP, d), BF16),
                   jax.ShapeDtypeStruct((nblk, LANES, tb), F32),
                   jax.ShapeDtypeStruct((nblk, LANES, tb), F32),
                   jax.ShapeDtypeStruct((nblk, 1, LANES), jnp.int32)],
        scratch_shapes=[pltpu.VMEM((N_EXPERTS * MOE_CAP, tb), BF16)],
        compiler_params=_params(("arbitrary",)),
        name="moe_router",
    )(a, bb, wa, wb, hf, norm_g[None, :], wrh, wrl, br, tri)


def _expert_mlp_kernel(top_ref, x_ref, wg_ref, wu_ref, wd_ref, y_ref, wg_sc, wu_sc, wd_sc):
    e = pl.program_id(0)
    i = pl.program_id(1)

    @pl.when(i == 0)
    def _():
        wg_sc[...] = wg_ref[0, 0].astype(BF16)
        wu_sc[...] = wu_ref[0, 0].astype(BF16)
        wd_sc[...] = wd_ref[0, 0].astype(BF16)

    tr, d = x_ref.shape[1], x_ref.shape[2]
    nb = tr // MOE_CAP
    top = top_ref[e * pl.num_programs(1) + i]

    def run(slots):
        x = x_ref[0]
        if slots < MOE_CAP:
            x = x.reshape(nb, MOE_CAP, d)[:, :slots].reshape(nb * slots, d)
        hid = _silu(jnp.dot(x, wg_sc[...], preferred_element_type=F32)) * jnp.dot(
            x, wu_sc[...], preferred_element_type=F32)
        y = jnp.dot(hid.astype(BF16), wd_sc[...], preferred_element_type=F32).astype(BF16)
        if slots < MOE_CAP:
            pad = jnp.zeros((nb, MOE_CAP - slots, d), BF16)
            y = jnp.concatenate([y.reshape(nb, slots, d), pad], axis=1).reshape(tr, d)
        y_ref[0] = y

    levels = [lv for lv in MOE_SLOT_LEVELS if lv < MOE_CAP] + [MOE_CAP]
    for n, slots in enumerate(levels):
        above = top > levels[n - 1] if n else top >= 0
        below = top <= slots if n + 1 < len(levels) else top >= 0
        pl.when(above & below)(functools.partial(run, slots))


def _expert_mlp(xs, cnt, w_gate, w_up, w_down, layer):
    ne, rows, d = xs.shape
    ff = w_gate.shape[3]
    tr = _tile(rows, EXPERT_TR)
    ntiles = rows // tr
    top = jnp.max(cnt[:, 0, :ne].reshape(ntiles, -1, ne), axis=1).T.reshape(-1)
    grid_spec = pltpu.PrefetchScalarGridSpec(
        num_scalar_prefetch=1,
        grid=(ne, ntiles),
        in_specs=[pl.BlockSpec((1, tr, d), lambda e, i, t: (e, i, 0)),
                  pl.BlockSpec((1, 1, d, ff), lambda e, i, t: (layer, e, 0, 0)),
                  pl.BlockSpec((1, 1, d, ff), lambda e, i, t: (layer, e, 0, 0)),
                  pl.BlockSpec((1, 1, ff, d), lambda e, i, t: (layer, e, 0, 0))],
        out_specs=pl.BlockSpec((1, tr, d), lambda e, i, t: (e, i, 0)),
        scratch_shapes=[pltpu.VMEM((d, ff), BF16), pltpu.VMEM((d, ff), BF16), pltpu.VMEM((ff, d), BF16)],
    )
    return pl.pallas_call(
        _expert_mlp_kernel,
        grid_spec=grid_spec,
        out_shape=jax.ShapeDtypeStruct((ne, rows, d), BF16),
        compiler_params=_params(("arbitrary", "arbitrary")),
        name="moe_expert_mlp",
    )(top, xs, w_gate, w_up, w_down)


def _ple_rows(h, p, g_ref, wg_ref, bg_ref, wp_ref):
    gate = jax.nn.sigmoid(_bdot(_rms(h, g_ref[...]), wg_ref[...]) + bg_ref[...])
    return h + gate * _bdot(p, wp_ref[...])


def _combine_ple_kernel(cnt_ref, y_ref, keyt_ref, wt_ref, h_ref, p_ref, gffn_ref, gple_ref, wgate_ref, bgate_ref,
                        wproj_ref, wg_hbm, wu_hbm, wd_hbm, o_ref, sel_ref, acc_ref, m_ref, wg_buf, wu_buf, wd_buf,
                        sems, *, layer):
    blk = pl.program_id(0)
    tb = h_ref.shape[0]
    most = lax.fori_loop(0, N_EXPERTS, lambda e, mx: jnp.maximum(mx, cnt_ref[blk * LANES + e]), 0)

    def combine(slots):
        riota = lax.broadcasted_iota(jnp.int32, (slots, tb), 0).astype(F32)
        for e in range(N_EXPERTS):
            hit = keyt_ref[0, e:e + 1, :] == riota
            sel_ref[e * slots:(e + 1) * slots, :] = jnp.where(hit, wt_ref[0, e:e + 1, :], 0.0).astype(BF16)
        y = y_ref[:, 0:slots, :].reshape(N_EXPERTS * slots, y_ref.shape[2])
        acc_ref[...] = h_ref[...] + lax.dot_general(sel_ref[0:N_EXPERTS * slots, :], y, (((0,), (0,)), ((), ())),
                                                    preferred_element_type=F32)

    levels = [lv for lv in MOE_SLOT_LEVELS if lv < MOE_CAP] + [MOE_CAP]
    for n, slots in enumerate(levels):
        above = most > levels[n - 1] if n else most >= 0
        below = most <= slots if n + 1 < len(levels) else most >= 0
        pl.when(above & below)(functools.partial(combine, slots))

    @pl.when(most > MOE_CAP)
    def _():
        m_ref[...] = _rms(h_ref[...], gffn_ref[...]).astype(BF16)
        rows = lax.broadcasted_iota(jnp.int32, (MOE_ROWS, tb), 0).astype(F32)

        def expert(e, carry):
            extra = cnt_ref[blk * LANES + e] - MOE_CAP

            @pl.when(extra > 0)
            def _():
                copies = [pltpu.make_async_copy(src.at[layer, e], dst, sems.at[n])
                          for n, (src, dst) in enumerate(((wg_hbm, wg_buf), (wu_hbm, wu_buf), (wd_hbm, wd_buf)))]
                for c in copies:
                    c.start()
                for c in copies:
                    c.wait()
                krow = keyt_ref[0, pl.ds(e, 1), :]
                wrow = wt_ref[0, pl.ds(e, 1), :]

                def chunk(ci, c2):
                    hit = krow == (rows + (MOE_CAP + ci * MOE_ROWS).astype(F32))
                    sel = jnp.where(hit, 1.0, 0.0).astype(BF16)
                    xg = jnp.dot(sel, m_ref[...], preferred_element_type=F32).astype(BF16)
                    hid = _silu(_bdot(xg, wg_buf[...])) * _bdot(xg, wu_buf[...])
                    yo = _bdot(hid, wd_buf[...]).astype(BF16)
                    acc_ref[...] += _bdot_tn(jnp.where(hit, wrow, 0.0), yo)
                    return c2

                lax.fori_loop(0, (extra + MOE_ROWS - 1) // MOE_ROWS, chunk, 0)

            return carry

        lax.fori_loop(0, N_EXPERTS, expert, 0)

    o_ref[...] = _ple_rows(acc_ref[...], p_ref[0], gple_ref, wgate_ref, bgate_ref, wproj_ref)


def _moe_combine_ple(cnt, ys, keyt, wt, hf, norm_ffn, w_gate, w_up, w_down, layer, p_all, ple_norm, ple_w_gate,
                     ple_b_gate, ple_w_proj, tb):
    n, d = hf.shape
    ff = w_gate.shape[3]
    nblk = n // tb
    pd = p_all.shape[-1]
    blk = pl.BlockSpec((1, LANES, tb), lambda i, c: (i, 0, 0))
    tok = pl.BlockSpec((tb, d), lambda i, c: (i, 0))
    vec = pl.BlockSpec((1, d), lambda i, c: (0, 0))
    hbm = pl.BlockSpec(memory_space=pl.ANY)
    grid_spec = pltpu.PrefetchScalarGridSpec(
        num_scalar_prefetch=1,
        grid=(nblk,),
        in_specs=[pl.BlockSpec((N_EXPERTS, MOE_CAP, d), lambda i, c: (0, i, 0)), blk, blk, tok,
                  pl.BlockSpec((1, tb, pd), lambda i, c: (layer, i, 0)), vec, vec,
                  pl.BlockSpec((d, d), lambda i, c: (0, 0)), vec, pl.BlockSpec((pd, d), lambda i, c: (0, 0)),
                  hbm, hbm, hbm],
        out_specs=tok,
        scratch_shapes=[pltpu.VMEM((N_EXPERTS * MOE_CAP, tb), BF16), pltpu.VMEM((tb, d), F32),
                        pltpu.VMEM((tb, d), BF16), pltpu.VMEM((d, ff), F32), pltpu.VMEM((d, ff), F32),
                        pltpu.VMEM((ff, d), F32), pltpu.SemaphoreType.DMA((3,))],
    )
    return pl.pallas_call(
        functools.partial(_combine_ple_kernel, layer=layer),
        grid_spec=grid_spec,
        out_shape=jax.ShapeDtypeStruct((n, d), F32),
        compiler_params=_params(("arbitrary",)),
        name="moe_combine_ple",
    )(cnt, ys, keyt, wt, hf, p_all.reshape(p_all.shape[0], n, pd), norm_ffn[None, :], ple_norm[None, :],
      ple_w_gate.astype(BF16), ple_b_gate[None, :], ple_w_proj.astype(BF16), w_gate, w_up, w_down)


def _proj_moe_ple(mix_a, mix_b, w_out, h, norm_g, w_group, b_group, w_expert, b_expert, w_gate, w_up, w_down,
                  layer, p_all, ple_w_proj, ple_norm, ple_w_gate, ple_b_gate):
    b, s, d = h.shape
    n = b * s
    tb = _tile(n, MOE_TB)
    hf, xs, keyt, wt, cnt = _moe_router(mix_a.reshape(n, -1), mix_b.reshape(n, -1), w_out, h.reshape(n, d),
                                        norm_g, w_group, b_group, w_expert, b_expert, tb)
    ys = _expert_mlp(xs, cnt, w_gate, w_up, w_down, layer)
    out = _moe_combine_ple(cnt.reshape(-1), ys, keyt, wt, hf, norm_g, w_gate, w_up, w_down, layer, p_all,
                           ple_norm, ple_w_gate, ple_b_gate, ple_w_proj, tb)
    return out.reshape(b, s, d)


def _even_mixers(h, positions, norm_mix, w_in, b_f, fox_qn, fox_kn, q_a_norm, w_q_up, kv_a_norm, w_kv_up,
                 mla_qn, mla_kn):
    fq, fk, fv, cum, mq, mk, mv = _even_pre(h, positions, norm_mix, w_in, b_f, fox_qn, fox_kn, q_a_norm,
                                            w_q_up, kv_a_norm, w_kv_up, mla_qn, mla_kn)
    return _attention(fq, fk, fv, cum), _attention(mq, mk, mv)


def kernel(x, p, positions, norm_mix, norm_ffn, ev_w_in, fox_b_f, fox_q_norm, fox_k_norm, mla_q_a_norm, mla_w_q_up, mla_kv_a_norm, mla_w_kv_up, mla_q_norm, mla_k_norm, ev_w_out, od_w_in, s5_a_re, s5_a_im, s5_b_re, s5_b_im, s5_c_re, s5_c_im, s5_d, s5_log_step, s5_w_glu, s5_b_glu, gdn_conv_w, gdn_a_log, gdn_dt_bias, gdn_o_norm, od_w_out, moe_w_group, moe_b_group, moe_w_expert, moe_b_expert, moe_w_gate, moe_w_up, moe_w_down, ple_w_proj, ple_norm, ple_w_gate, ple_b_gate):
    h = x
    depth = p.shape[0]
    for i in range(depth):
        j = i // 2
        if i % 2 == 0:
            mix = _even_mixers(h, positions, norm_mix[i], ev_w_in[j], fox_b_f[j], fox_q_norm[j], fox_k_norm[j],
                               mla_q_a_norm[j], mla_w_q_up[j], mla_kv_a_norm[j], mla_w_kv_up[j], mla_q_norm[j],
                               mla_k_norm[j])
            w_out = ev_w_out[j]
        else:
            mix = _odd_mixers(h, norm_mix[i], od_w_in[j], s5_a_re[j], s5_a_im[j], s5_b_re[j], s5_b_im[j],
                              s5_c_re[j], s5_c_im[j], s5_d[j], s5_log_step[j], s5_w_glu[j], s5_b_glu[j],
                              gdn_conv_w[j], gdn_a_log[j], gdn_dt_bias[j], gdn_o_norm[j])
            w_out = od_w_out[j]
        h = _proj_moe_ple(mix[0], mix[1], w_out, h, norm_ffn[i], moe_w_group[i], moe_b_group[i], moe_w_expert[i],
                          moe_b_expert[i], moe_w_gate, moe_w_up, moe_w_down, i, p, ple_w_proj[i], ple_norm[i],
                          ple_w_gate[i], ple_b_gate[i])
    return h
```

```python
import functools
import math

import numpy as np
import jax
import jax.numpy as jnp
from jax import lax
from jax.experimental import pallas as pl
from jax.experimental.pallas import tpu as pltpu

F32 = jnp.float32
BF16 = jnp.bfloat16

LANES = 128
RMS_EPS = 1e-6
ROPE_THETA = 10000.0
LOG2E = math.log2(math.e)

FOX_HEADS = 8
FOX_HEAD_DIM = 64
MLA_HEADS = 8
MLA_Q_LORA = 384
MLA_KV_LORA = 256
MLA_NOPE = 64
MLA_ROPE = 32
MLA_V = 64
MLA_QK = MLA_NOPE + MLA_ROPE

S5_CH = 512
S5_GROUP_CH = 16
S5_GROUPS = S5_CH // S5_GROUP_CH
S5_STATE = 64
S5_N = S5_GROUPS * S5_STATE

GDN_HEADS = 4
GDN_HEAD_DIM = 128
GDN_W = GDN_HEADS * GDN_HEAD_DIM
GDN_CONV = 4
GDN_CHUNK = 64

MOE_GROUPS = 4
MOE_PER_GROUP = 8
N_EXPERTS = MOE_GROUPS * MOE_PER_GROUP
MOE_TB = 512
MOE_CAP = 64
MOE_ROWS = 128
MOE_SLOT_LEVELS = (32, 48)
ATTN_TQ = 2048
ATTN_TK = 512
MIX_TM = 256
EXPERT_TR = 1024

VMEM_LIMIT = 56 * 1024 * 1024


def _tile(n, pref):
    t = min(n, pref)
    assert n % t == 0, (n, t)
    return t


def _params(sem):
    return pltpu.CompilerParams(dimension_semantics=sem, vmem_limit_bytes=VMEM_LIMIT)


def _full(shape):
    nd = len(shape)
    return pl.BlockSpec(shape, lambda *_: (0,) * nd)


def _rms(x, g):
    return x * lax.rsqrt(jnp.mean(x * x, axis=-1, keepdims=True) + RMS_EPS) * g


def _bdot(a, b):
    return jnp.dot(a.astype(BF16), b.astype(BF16), preferred_element_type=F32)


def _bdot_nt(a, b):
    return lax.dot_general(a.astype(BF16), b.astype(BF16), (((1,), (1,)), ((), ())),
                           preferred_element_type=F32)


def _bdot_tn(a, b):
    return lax.dot_general(a.astype(BF16), b.astype(BF16), (((0,), (0,)), ((), ())),
                           preferred_element_type=F32)


def _split3(x):
    x1 = x.astype(BF16)
    r = x - x1.astype(F32)
    x2 = r.astype(BF16)
    return x1, x2, (r - x2.astype(F32)).astype(BF16)


def _sel_dot(sel, x):
    return sum(jnp.dot(sel, part, preferred_element_type=F32) for part in _split3(x))


def _dot_sel(x, sel):
    return sum(jnp.dot(part, sel, preferred_element_type=F32) for part in _split3(x))


def _split_dot(x, ind):
    hi = x.astype(BF16)
    lo = (x - hi.astype(F32)).astype(BF16)
    return (jnp.dot(hi, ind, preferred_element_type=F32)
            + jnp.dot(lo, ind, preferred_element_type=F32))


def _log_sigmoid(x):
    return jnp.minimum(x, 0.0) - jnp.log(1.0 + jnp.exp(-jnp.abs(x)))


def _softplus(x):
    return jnp.maximum(x, 0.0) + jnp.log(1.0 + jnp.exp(-jnp.abs(x)))


def _silu(x):
    return x * jax.nn.sigmoid(x)


def _head_norm128(x, nheads, denom, gain):
    outs = []
    for hh in range(nheads):
        xh = x[:, LANES * hh:LANES * (hh + 1)]
        ss = jnp.sum(xh * xh, axis=-1, keepdims=True)
        outs.append(xh * lax.rsqrt(ss / denom + RMS_EPS))
    return jnp.concatenate(outs, axis=1) * gain


def _even_pre_kernel(h_ref, pos_ref, nmix_ref, win_ref, ind_ref, fqn_ref, fkn_ref, bf_ref,
                     qan_ref, wq_ref, kvan_ref, wkv_ref, mqn_ref, mkn_ref, freq_ref, s1_ref, s2_ref,
                     tri_ref, vone_ref, fq_o, fk_o, fv_o, cum_o, mq_o, mk_o, mv_o, carry_ref):
    t = pl.program_id(1)

    @pl.when(t == 0)
    def _():
        carry_ref[...] = jnp.zeros_like(carry_ref)

    tm = h_ref.shape[1]
    a = _rms(h_ref[0], nmix_ref[...])
    proj = _bdot(a, win_ref[...])
    nf = FOX_HEADS * FOX_HEAD_DIM
    fq = proj[:, 0:nf]
    fk = proj[:, nf:2 * nf]
    nv = FOX_HEADS * LANES
    fv = proj[:, 2 * nf:2 * nf + nv]
    o_cq = 2 * nf + nv
    cq = proj[:, o_cq:o_cq + MLA_Q_LORA]
    o_ckv = o_cq + MLA_Q_LORA
    ckv = proj[:, o_ckv:o_ckv + MLA_KV_LORA]
    misc = proj[:, o_ckv + MLA_KV_LORA:]

    ind = ind_ref[...]
    fq_n = fq * lax.rsqrt(_split_dot(fq * fq, ind) / FOX_HEAD_DIM + RMS_EPS) * fqn_ref[...]
    fk_n = fk * lax.rsqrt(_split_dot(fk * fk, ind) / FOX_HEAD_DIM + RMS_EPS) * fkn_ref[...]
    fq_o[0] = (fq_n * (FOX_HEAD_DIM ** -0.5 * LOG2E)).astype(BF16)
    fk_o[0] = fk_n.astype(BF16)
    fv_o[0] = (fv + vone_ref[...]).astype(BF16)

    lane = lax.broadcasted_iota(jnp.int32, (tm, LANES), 1)
    logf = jnp.where(lane < FOX_HEADS, _log_sigmoid(misc + bf_ref[...]), 0.0)
    cum = _sel_dot(tri_ref[...], logf) + carry_ref[...]
    carry_ref[...] = cum[tm - 1:tm, :]
    cum_o[0] = (cum * LOG2E).T[:FOX_HEADS, :]

    ang = pos_ref[0].astype(F32) * freq_ref[...]
    cos1 = jnp.cos(ang)
    sin1 = jnp.sin(ang)
    cos = jnp.concatenate([cos1] * MLA_HEADS, axis=1)
    sin_a = jnp.concatenate([sin1 * s1_ref[...]] * MLA_HEADS, axis=1)
    sin_b = jnp.concatenate([sin1 * s2_ref[...]] * MLA_HEADS, axis=1)
    width = MLA_HEADS * LANES
    half = MLA_ROPE // 2

    def rope(x):
        return (x * cos + pltpu.roll(x, width - half, 1) * sin_a + pltpu.roll(x, half, 1) * sin_b)

    q = _bdot(_rms(cq, qan_ref[...]), wq_ref[...])
    q = rope(_head_norm128(q, MLA_HEADS, MLA_QK, mqn_ref[...]))
    mq_o[0] = (q * (MLA_QK ** -0.5 * LOG2E)).astype(BF16)

    kv = _bdot(_rms(ckv, kvan_ref[...]), wkv_ref[...])
    kr = pltpu.roll(misc, MLA_NOPE - FOX_HEADS, 1)
    kr = jnp.where((lane >= MLA_NOPE) & (lane < MLA_QK), kr, 0.0)
    k = kv[:, :width] + jnp.concatenate([kr] * MLA_HEADS, axis=1)
    k = rope(_head_norm128(k, MLA_HEADS, MLA_QK, mkn_ref[...]))
    mk_o[0] = k.astype(BF16)
    mv_o[0] = (kv[:, width:] + vone_ref[...]).astype(BF16)


def _even_pre(h, positions, norm_mix, w_in, b_f, fox_qn, fox_kn, q_a_norm, w_q_up, kv_a_norm, w_kv_up,
              mla_qn, mla_kn):
    b, s, d = h.shape
    tm = _tile(s, MIX_TM)
    nf = FOX_HEADS * FOX_HEAD_DIM
    sizes = (nf, nf, nf, FOX_HEADS, MLA_Q_LORA, MLA_KV_LORA, MLA_ROPE)
    offs = np.concatenate([[0], np.cumsum(sizes)])
    parts = [w_in[:, offs[i]:offs[i + 1]] for i in range(len(sizes))]
    pad = jnp.zeros((d, LANES - FOX_HEADS - MLA_ROPE), w_in.dtype)
    slot_pad = ((0, 0), (0, 0), (0, LANES - FOX_HEAD_DIM))
    wfv = jnp.pad(parts[2].reshape(d, FOX_HEADS, FOX_HEAD_DIM), slot_pad).reshape(d, FOX_HEADS * LANES)
    win = jnp.concatenate([parts[0], parts[1], wfv, parts[4], parts[5], parts[3], parts[6], pad],
                          axis=1).astype(BF16)
    gidx = np.arange(nf) // FOX_HEAD_DIM
    ind = jnp.asarray(gidx[:, None] == gidx[None, :], BF16)
    fqn = jnp.tile(fox_qn, FOX_HEADS)[None, :]
    fkn = jnp.tile(fox_kn, FOX_HEADS)[None, :]
    bf = jnp.zeros((1, LANES), F32).at[0, :FOX_HEADS].set(b_f)
    padq = LANES - MLA_QK
    wq = jnp.pad(w_q_up.reshape(MLA_Q_LORA, MLA_HEADS, MLA_QK), ((0, 0), (0, 0), (0, padq)))
    wq = wq.reshape(MLA_Q_LORA, MLA_HEADS * LANES).astype(BF16)
    wkv3 = w_kv_up.reshape(MLA_KV_LORA, MLA_HEADS, MLA_NOPE + MLA_V)
    wk = jnp.pad(wkv3[:, :, :MLA_NOPE], ((0, 0), (0, 0), (0, LANES - MLA_NOPE)))
    wv = jnp.pad(wkv3[:, :, MLA_NOPE:], ((0, 0), (0, 0), (0, LANES - MLA_V)))
    wkv = jnp.concatenate([wk.reshape(MLA_KV_LORA, MLA_HEADS * LANES),
                           wv.reshape(MLA_KV_LORA, MLA_HEADS * LANES)], axis=1).astype(BF16)
    vone = jnp.tile(jnp.zeros((LANES,), F32).at[MLA_V].set(1.0), MLA_HEADS)[None, :]
    mqn = jnp.tile(jnp.pad(mla_qn, (0, padq)), MLA_HEADS)[None, :]
    mkn = jnp.tile(jnp.pad(mla_kn, (0, padq)), MLA_HEADS)[None, :]
    half = MLA_ROPE // 2
    inv = ROPE_THETA ** (-jnp.arange(half, dtype=F32) * 2.0 / MLA_ROPE)
    freq = jnp.zeros((1, LANES), F32).at[0, MLA_NOPE:MLA_NOPE + half].set(inv)
    freq = freq.at[0, MLA_NOPE + half:MLA_QK].set(inv)
    s1 = jnp.zeros((1, LANES), F32).at[0, MLA_NOPE:MLA_NOPE + half].set(-1.0)
    s2 = jnp.zeros((1, LANES), F32).at[0, MLA_NOPE + half:MLA_QK].set(1.0)
    tri = jnp.asarray(np.tril(np.ones((tm, tm), np.float32)), BF16)
    pos3 = positions.reshape(b, s, 1)

    row = lambda n: pl.BlockSpec((1, tm, n), lambda bi, ti: (bi, ti, 0))
    consts = [norm_mix[None, :], win, ind, fqn, fkn, bf, q_a_norm[None, :], wq, kv_a_norm[None, :], wkv,
              mqn, mkn, freq, s1, s2, tri, vone]
    nv = FOX_HEADS * LANES
    out_shape = [jax.ShapeDtypeStruct((b, s, nf), BF16)] * 2 + [jax.ShapeDtypeStruct((b, s, nv), BF16)] + [
        jax.ShapeDtypeStruct((b, FOX_HEADS, s), F32),
        jax.ShapeDtypeStruct((b, s, MLA_HEADS * LANES), BF16),
        jax.ShapeDtypeStruct((b, s, MLA_HEADS * LANES), BF16),
        jax.ShapeDtypeStruct((b, s, MLA_HEADS * LANES), BF16)]
    return pl.pallas_call(
        _even_pre_kernel,
        grid=(b, s // tm),
        in_specs=[row(d), row(1)] + [_full(c.shape) for c in consts],
        out_specs=[row(nf), row(nf), row(nv), pl.BlockSpec((1, FOX_HEADS, tm), lambda bi, ti: (bi, 0, ti)),
                   row(MLA_HEADS * LANES), row(MLA_HEADS * LANES), row(MLA_HEADS * LANES)],
        out_shape=out_shape,
        scratch_shapes=[pltpu.VMEM((1, LANES), F32)],
        compiler_params=_params(("arbitrary", "arbitrary")),
        name="even_pre",
    )(h, pos3, *consts)


def _attn_kernel(*refs, tq, tk, fox):
    if fox:
        q_ref, k_ref, v_ref, cr_ref, o_ref = refs
    else:
        q_ref, k_ref, v_ref, o_ref = refs
    hp = pl.program_id(1)
    i = pl.program_id(2)
    lane = lax.broadcasted_iota(jnp.int32, (tq, LANES), 1)
    qs = []
    for hh in range(2):
        if fox:
            in_head = (lane >= FOX_HEAD_DIM * hh) & (lane < FOX_HEAD_DIM * (hh + 1))
            qs.append(jnp.where(in_head, q_ref[0], jnp.zeros((), BF16)))
        else:
            qs.append(q_ref[0, :, LANES * hh:LANES * (hh + 1)])

    def step(j, carry, lo=None):
        koff = pl.multiple_of(j * tk, tk)
        top = 0 if lo is None else lo
        new = []
        for hh in range(2):
            m, acc = carry[hh]
            if fox:
                kj = k_ref[0, pl.ds(koff, tk), :]
            else:
                kj = k_ref[0, pl.ds(koff, tk), LANES * hh:LANES * (hh + 1)]
            sc = lax.dot_general(qs[hh][top:], kj, (((1,), (1,)), ((), ())), preferred_element_type=F32)
            if fox:
                sc = sc - cr_ref[0, pl.ds(2 * hp + hh, 1), pl.ds(koff, tk)]
            if lo is not None:
                rowi = lax.broadcasted_iota(jnp.int32, sc.shape, 0)
                coli = lax.broadcasted_iota(jnp.int32, sc.shape, 1)
                sc = jnp.where(coli <= rowi, sc, -jnp.inf)
            m_new = jnp.maximum(m[top:], jnp.max(sc, axis=-1, keepdims=True))
            alpha = jnp.exp2(m[top:] - m_new)
            p = jnp.exp2((sc - jnp.concatenate([m_new] * (tk // LANES), axis=1)).astype(BF16))
            vj = v_ref[0, pl.ds(koff, tk), LANES * hh:LANES * (hh + 1)]
            acc_new = alpha * acc[top:] + jnp.dot(p, vj, preferred_element_type=F32)
            if top:
                m_new = jnp.concatenate([m[:top], m_new], axis=0)
                acc_new = jnp.concatenate([acc[:top], acc_new], axis=0)
            new.append((m_new, acc_new))
        return tuple(new)

    def body(jj, carry):
        for r in range(ratio):
            carry = step(jj * ratio + r, carry)
        return carry

    one = (jnp.full((tq, LANES), -jnp.inf, F32), jnp.zeros((tq, LANES), F32))
    ratio = tq // tk
    carry = lax.fori_loop(0, i, body, (one, one))
    for r in range(ratio):
        carry = step(i * ratio + r, carry, lo=r * tk)
    outs = [acc / acc[:, MLA_V:MLA_V + 1] for _, acc in carry]
    o_ref[0] = jnp.where(lane < MLA_V, outs[0], pltpu.roll(outs[1], MLA_V, 1)).astype(o_ref.dtype)


def _attention(q, k, v, cum_row=None):
    b, s, _ = v.shape
    fox = cum_row is not None
    qw = LANES if fox else 2 * LANES
    tq = _tile(s, ATTN_TQ)
    tk = _tile(tq, ATTN_TK)
    npairs = v.shape[2] // (2 * LANES)
    in_specs = [pl.BlockSpec((1, tq, qw), lambda bi, hp, i: (bi, i, hp)),
                pl.BlockSpec((1, s, qw), lambda bi, hp, i: (bi, 0, hp)),
                pl.BlockSpec((1, s, 2 * LANES), lambda bi, hp, i: (bi, 0, hp))]
    args = [q, k, v]
    if fox:
        in_specs += [pl.BlockSpec((1, FOX_HEADS, s), lambda bi, hp, i: (bi, 0, 0))]
        args += [cum_row]
    return pl.pallas_call(
        functools.partial(_attn_kernel, tq=tq, tk=tk, fox=fox),
        grid=(b, npairs, s // tq),
        in_specs=in_specs,
        out_specs=pl.BlockSpec((1, tq, LANES), lambda bi, hp, i: (bi, i, hp)),
        out_shape=jax.ShapeDtypeStruct((b, s, npairs * LANES), BF16),
        compiler_params=_params(("arbitrary", "arbitrary", "arbitrary")),
        name="fox_attention" if fox else "mla_attention",
    )(*args)


N_S5_CONSTS = 13
N_GDN_CONSTS = 6


def _odd_kernel(h_ref, nmix_ref, win_ref, *refs):
    s5_consts = refs[:N_S5_CONSTS]
    gdn_consts = refs[N_S5_CONSTS:N_S5_CONSTS + N_GDN_CONSTS]
    y_o, o_o, x_ref, sr_ref, si_ref, xpad_ref, state_ref = refs[N_S5_CONSTS + N_GDN_CONSTS:]
    a = _rms(h_ref[0], nmix_ref[...])
    proj = _bdot(a, win_ref[...])
    o1 = S5_CH
    o2 = o1 + 3 * GDN_W
    o3 = o2 + GDN_W
    _s5_body(proj[:, :o1], *s5_consts, y_o, x_ref, sr_ref, si_ref)
    _gdn_body(proj[:, o1:o2], proj[:, o2:o3], proj[:, o3:], *gdn_consts, o_o, xpad_ref, state_ref)


def _odd_mixers(h, norm_mix, w_in, a_re, a_im, b_re, b_im, c_re, c_im, d_skip, log_step, w_glu, b_glu,
                conv_w, a_log, dt_bias, o_norm):
    b, s, d = h.shape
    tm = _tile(s, MIX_TM)
    sizes = (S5_CH, 3 * GDN_W, GDN_HEADS, GDN_HEADS, GDN_W)
    offs = np.concatenate([[0], np.cumsum(sizes)])
    parts = [w_in[:, offs[i]:offs[i + 1]] for i in range(len(sizes))]
    pad = jnp.zeros((d, LANES - 2 * GDN_HEADS), w_in.dtype)
    win = jnp.concatenate([parts[0], parts[1], parts[4], parts[2], parts[3], pad], axis=1).astype(BF16)
    s5_consts = _s5_consts(tm, a_re, a_im, b_re, b_im, c_re, c_im, d_skip, log_step, w_glu, b_glu)
    gdn_consts = _gdn_consts(tm, conv_w, a_log, dt_bias, o_norm)
    assert len(s5_consts) == N_S5_CONSTS and len(gdn_consts) == N_GDN_CONSTS
    consts = [norm_mix[None, :], win] + s5_consts + gdn_consts
    row = lambda n: pl.BlockSpec((1, tm, n), lambda bi, ti: (bi, ti, 0))
    return pl.pallas_call(
        _odd_kernel,
        grid=(b, s // tm),
        in_specs=[row(d)] + [_full(c.shape) for c in consts],
        out_specs=[row(S5_CH), row(GDN_W)],
        out_shape=[jax.ShapeDtypeStruct((b, s, S5_CH), BF16), jax.ShapeDtypeStruct((b, s, GDN_W), BF16)],
        scratch_shapes=[pltpu.VMEM((tm, 2 * S5_N), F32), pltpu.VMEM((1, S5_N), F32), pltpu.VMEM((1, S5_N), F32),
                        pltpu.VMEM((tm + 8, 3 * GDN_W), F32),
                        pltpu.VMEM((GDN_HEADS, GDN_HEAD_DIM, GDN_HEAD_DIM), F32)],
        compiler_params=_params(("arbitrary", "arbitrary")),
        name="odd_mixers",
    )(h, *consts)


def _s5_body(u, perm_ref, unperm_ref, bbd_ref, cbd_ref, ar_ref, ai_ref, asr_ref, asi_ref, pwr_ref, pwi_ref,
             d_ref, wglu_ref, bglu_ref, o_ref, x_ref, sr_ref, si_ref):
    t = pl.program_id(1)

    @pl.when(t == 0)
    def _():
        sr_ref[...] = jnp.zeros_like(sr_ref)
        si_ref[...] = jnp.zeros_like(si_ref)

    tm = u.shape[0]
    nseg = 8
    seg = tm // nseg
    u = _sel_dot(perm_ref[...], u)
    hc = S5_CH // 2
    hn = S5_N // 2
    ub = u.astype(BF16)
    for part in range(2):
        for base in (0, S5_N):
            cols = slice(base + part * hn, base + (part + 1) * hn)
            x_ref[:, cols] = jnp.dot(ub[:, part * hc:(part + 1) * hc], bbd_ref[part * hc:(part + 1) * hc, cols],
                                     preferred_element_type=F32)
    ar = ar_ref[...]
    ai = ai_ref[...]
    re = slice(0, S5_N)
    im = slice(S5_N, 2 * S5_N)

    def local(i, carry):
        xr, xi = carry
        rows = pl.ds(pl.multiple_of(i * nseg, nseg), nseg)
        nr = ar * xr - ai * xi + x_ref[rows, re]
        ni = ar * xi + ai * xr + x_ref[rows, im]
        x_ref[rows, re] = nr
        x_ref[rows, im] = ni
        return nr, ni

    zero = jnp.zeros((nseg, S5_N), F32)
    er, ei = lax.fori_loop(0, seg, local, (zero, zero), unroll=4)

    asr = asr_ref[...]
    asi = asi_ref[...]
    cr = [sr_ref[...]]
    ci = [si_ref[...]]
    for s in range(nseg):
        cr.append(asr * cr[s] - asi * ci[s] + er[s:s + 1, :])
        ci.append(asr * ci[s] + asi * cr[s] + ei[s:s + 1, :])
    sr_ref[...] = cr[nseg]
    si_ref[...] = ci[nseg]
    ent_r = jnp.concatenate(cr[:nseg], axis=0)
    ent_i = jnp.concatenate(ci[:nseg], axis=0)

    def fix(i, c):
        rows = pl.ds(pl.multiple_of(i * nseg, nseg), nseg)
        pr = pwr_ref[pl.ds(i, 1), :]
        pi = pwi_ref[pl.ds(i, 1), :]
        x_ref[rows, re] += pr * ent_r - pi * ent_i
        x_ref[rows, im] += pr * ent_i + pi * ent_r
        return c

    lax.fori_loop(0, seg, fix, 0, unroll=4)
    ys = []
    for part in range(2):
        oc = slice(part * hc, (part + 1) * hc)
        acc = None
        for base in (0, S5_N):
            rows = slice(base + part * hn, base + (part + 1) * hn)
            term = _bdot(x_ref[:, rows], cbd_ref[rows, oc])
            acc = term if acc is None else acc + term
        ys.append(acc)
    y = jnp.concatenate(ys, axis=1) + d_ref[...] * u
    hg = jax.nn.gelu(y)
    out = (hg * jax.nn.sigmoid(_bdot(hg, wglu_ref[...]) + bglu_ref[...])).astype(BF16)
    o_ref[0] = jnp.dot(unperm_ref[...], out, preferred_element_type=F32).astype(o_ref.dtype)


def _s5_consts(tm, a_re, a_im, b_re, b_im, c_re, c_im, d_skip, log_step, w_glu, b_glu):
    lam_re = jnp.minimum(a_re, -1e-4)
    lam_im = a_im
    dt = jnp.exp(log_step)[:, None]
    mag = jnp.exp(lam_re * dt)
    ab_re = mag * jnp.cos(lam_im * dt)
    ab_im = mag * jnp.sin(lam_im * dt)
    den = lam_re * lam_re + lam_im * lam_im
    nr, ni = ab_re - 1.0, ab_im
    gam_re = (nr * lam_re + ni * lam_im) / den
    gam_im = (ni * lam_re - nr * lam_im) / den
    bb_re = gam_re[..., None] * b_re - gam_im[..., None] * b_im
    bb_im = gam_re[..., None] * b_im + gam_im[..., None] * b_re
    eye = jnp.eye(S5_GROUPS, dtype=F32)
    bd_in = lambda m: jnp.einsum('gpc,gh->gchp', m, eye).reshape(S5_CH, S5_N)
    bd_out = lambda m: jnp.einsum('gcp,gh->gphc', m, eye).reshape(S5_N, S5_CH)
    bbd = jnp.concatenate([bd_in(bb_re), bd_in(bb_im)], axis=1).astype(BF16)
    cbd = jnp.concatenate([bd_out(c_re), -bd_out(c_im)], axis=0).astype(BF16)
    seg = tm // 8
    steps = jnp.arange(1, seg + 1, dtype=F32)[:, None, None] * dt[None]
    pmag = jnp.exp(lam_re[None] * steps)
    pw_re = (pmag * jnp.cos(lam_im[None] * steps)).reshape(seg, S5_N)
    pw_im = (pmag * jnp.sin(lam_im[None] * steps)).reshape(seg, S5_N)
    src = (np.arange(tm) % 8) * seg + np.arange(tm) // 8
    perm = np.zeros((tm, tm), np.float32)
    perm[np.arange(tm), src] = 1.0
    return [jnp.asarray(perm, BF16), jnp.asarray(perm.T, BF16),
            bbd, cbd, ab_re.reshape(1, S5_N), ab_im.reshape(1, S5_N), pw_re[seg - 1:seg], pw_im[seg - 1:seg],
            pw_re, pw_im, d_skip[None, :], w_glu.astype(BF16), b_glu[None, :]]


def _gdn_body(x, z, gb, cw_ref, nega_ref, dtb_ref, onorm_ref, tril_ref, triu_ref, o_ref, xpad_ref, state_ref):
    t = pl.program_id(1)
    tm = x.shape[0]
    c = GDN_CHUNK
    hd = GDN_HEAD_DIM

    @pl.when(t == 0)
    def _():
        xpad_ref[0:8, :] = jnp.zeros((8, xpad_ref.shape[1]), F32)
        state_ref[...] = jnp.zeros_like(state_ref)

    @pl.when(t > 0)
    def _():
        xpad_ref[0:8, :] = xpad_ref[tm:tm + 8, :]

    xpad_ref[8:tm + 8, :] = x
    conv = cw_ref[0:1, :] * xpad_ref[pl.ds(8 - (GDN_CONV - 1), tm), :]
    for i in range(1, GDN_CONV):
        conv = conv + cw_ref[i:i + 1, :] * xpad_ref[pl.ds(8 - (GDN_CONV - 1) + i, tm), :]
    act = _silu(conv)

    def l2n(x):
        return x * lax.rsqrt(jnp.sum(x * x, axis=-1, keepdims=True) + RMS_EPS)

    g = nega_ref[...] * _softplus(gb + dtb_ref[...])
    beta = jax.nn.sigmoid(gb)
    gc = _sel_dot(tril_ref[...], g)
    gct = _dot_sel(g.T, triu_ref[...])

    ri = lax.broadcasted_iota(jnp.int32, (tm, tm), 0)
    ci = lax.broadcasted_iota(jnp.int32, (tm, tm), 1)
    same = (ri // c) == (ci // c)
    incl = same & (ri >= ci)
    strict = same & (ri > ci)
    eye = (ri == ci).astype(F32)
    offs = []
    bs = 1
    while bs < c:
        offs.append(((ri // (2 * bs)) == (ci // (2 * bs))) & ((ri % (2 * bs)) >= bs) & ((ci % (2 * bs)) < bs))
        bs *= 2
    nchunks = tm // c

    heads = range(GDN_HEADS)
    q = [l2n(act[:, hh * hd:(hh + 1) * hd]) * (hd ** -0.5) for hh in heads]
    k = [l2n(act[:, GDN_W + hh * hd:GDN_W + (hh + 1) * hd]) for hh in heads]
    v = [act[:, 2 * GDN_W + hh * hd:2 * GDN_W + (hh + 1) * hd] for hh in heads]
    bcol = [beta[:, GDN_HEADS + hh:GDN_HEADS + hh + 1] for hh in heads]
    gcol = [gc[:, hh:hh + 1] for hh in heads]
    decay = [jnp.where(incl, jnp.exp(jnp.where(incl, gcol[hh] - gct[hh:hh + 1, :], 0.0)), 0.0) for hh in heads]
    kb = [k[hh] * bcol[hh] for hh in heads]
    a_mat = [jnp.where(strict, _bdot_nt(kb[hh], k[hh]) * decay[hh], 0.0) for hh in heads]
    t_mat = [eye - jnp.where(offs[0], a_mat[hh], 0.0) for hh in heads]
    for off in offs[1:]:
        pa = [_bdot(t_mat[hh], jnp.where(off, a_mat[hh], 0.0)) for hh in heads]
        t_mat = [t_mat[hh] - _bdot(pa[hh], t_mat[hh]) for hh in heads]
    th = [t_mat[hh].astype(BF16) for hh in heads]
    tl = [(t_mat[hh] - th[hh].astype(F32)).astype(BF16) for hh in heads]
    ah = [a_mat[hh].astype(BF16) for hh in heads]
    al = [(a_mat[hh] - ah[hh].astype(F32)).astype(BF16) for hh in heads]
    a_t = [jnp.dot(ah[hh], th[hh], preferred_element_type=F32) + jnp.dot(ah[hh], tl[hh], preferred_element_type=F32)
           + jnp.dot(al[hh], th[hh], preferred_element_type=F32) for hh in heads]
    t_mat = [t_mat[hh] + jnp.dot(th[hh], (eye - t_mat[hh] - a_t[hh]).astype(BF16), preferred_element_type=F32)
             for hh in heads]
    eg = [jnp.exp(gcol[hh]) for hh in heads]
    u = [_bdot(t_mat[hh], v[hh] * bcol[hh]) for hh in heads]
    w = [_bdot(t_mat[hh], kb[hh] * eg[hh]) for hh in heads]
    intra = [jnp.where(incl, _bdot_nt(q[hh], k[hh]) * decay[hh], 0.0).astype(BF16) for hh in heads]
    qd = [q[hh] * eg[hh] for hh in heads]
    state = [state_ref[hh] for hh in heads]
    for n in range(nchunks):
        r0 = n * c
        for hh in heads:
            lo = hh * hd
            gcn = gcol[hh][r0:r0 + c, :]
            glast = gcol[hh][r0 + c - 1:r0 + c, :]
            v_new = u[hh][r0:r0 + c, :] - _bdot(w[hh][r0:r0 + c, :], state[hh])
            v_rep = jnp.concatenate([v_new.astype(BF16)] * nchunks, axis=0)
            o = _bdot(qd[hh][r0:r0 + c, :], state[hh]) + jnp.dot(intra[hh][r0:r0 + c, :], v_rep,
                                                                  preferred_element_type=F32)
            state[hh] = state[hh] * jnp.exp(glast) + _bdot_tn(k[hh][r0:r0 + c, :] * jnp.exp(glast - gcn), v_new)
            on = o * lax.rsqrt(jnp.mean(o * o, axis=-1, keepdims=True) + RMS_EPS) * onorm_ref[...]
            o_ref[0, r0:r0 + c, lo:lo + hd] = (on * _silu(z[r0:r0 + c, lo:lo + hd])).astype(o_ref.dtype)
    for hh in heads:
        state_ref[hh] = state[hh]


def _gdn_consts(tm, conv_w, a_log, dt_bias, o_norm):
    nega = jnp.zeros((1, LANES), F32).at[0, :GDN_HEADS].set(-jnp.exp(a_log))
    dtb = jnp.zeros((1, LANES), F32).at[0, :GDN_HEADS].set(dt_bias)
    cwp = jnp.pad(conv_w, ((0, 8 - GDN_CONV), (0, 0)))
    ridx = np.arange(tm)
    same = (ridx[:, None] // GDN_CHUNK) == (ridx[None, :] // GDN_CHUNK)
    tril = jnp.asarray((same & (ridx[:, None] >= ridx[None, :])).astype(np.float32), BF16)
    triu = jnp.asarray((same & (ridx[:, None] <= ridx[None, :])).astype(np.float32), BF16)
    return [cwp, nega, dtb, o_norm[None, :], tril, triu]


def _router_kernel(a_ref, b_ref, wa_ref, wb_ref, h_ref, g_ref, wrh_ref, wrl_ref, br_ref, tri_ref,
                   h_o, xs_o, keyt_o, wt_o, cnt_o, sel_ref):
    tm = h_ref.shape[0]
    h = (h_ref[...] + jnp.dot(a_ref[...], wa_ref[...], preferred_element_type=F32)
         + jnp.dot(b_ref[...], wb_ref[...], preferred_element_type=F32))
    h_o[...] = h
    m = _rms(h, g_ref[...])
    mh = m.astype(BF16)
    ml = (m - mh.astype(F32)).astype(BF16)
    logits = (jnp.dot(mh, wrh_ref[...], preferred_element_type=F32) + jnp.dot(mh, wrl_ref[...], preferred_element_type=F32)
              + jnp.dot(ml, wrh_ref[...], preferred_element_type=F32)) + br_ref[...]
    lane = lax.broadcasted_iota(jnp.int32, (tm, LANES), 1)
    neg = -jnp.inf

    def first_argmax(x):
        mx = jnp.max(x, axis=-1, keepdims=True)
        idx = jnp.min(jnp.where(x == mx, lane, LANES), axis=-1, keepdims=True)
        return mx, idx

    is_g = (lane >= N_EXPERTS) & (lane < N_EXPERTS + MOE_GROUPS)
    gl = jnp.where(is_g, logits, neg)
    gmax, gidx = first_argmax(gl)
    g_w = 1.0 / jnp.sum(jnp.where(is_g, jnp.exp(gl - gmax), 0.0), axis=-1, keepdims=True)
    in_group = (lane // MOE_PER_GROUP) == (gidx - N_EXPERTS)
    el = jnp.where(in_group & (lane < N_EXPERTS), logits, neg)
    m1, i1 = first_argmax(el)
    el2 = jnp.where(lane == i1, neg, el)
    m2, i2 = first_argmax(el2)
    r = jnp.exp(m2 - m1)
    w1 = g_w / (1.0 + r)
    w2 = g_w * r / (1.0 + r)
    chose = (lane == i1) | (lane == i2)
    wmat = jnp.where(lane == i1, w1, jnp.where(lane == i2, w2, 0.0))
    ch = chose.astype(F32)
    rank = jnp.dot(tri_ref[...], ch.astype(BF16), preferred_element_type=F32)
    keyt = jnp.where(chose, rank, -1.0).T
    keyt_o[0] = keyt
    wt_o[0] = wmat.T
    cnt_o[0] = jnp.sum(ch, axis=0, keepdims=True).astype(jnp.int32)
    riota = lax.broadcasted_iota(jnp.int32, (MOE_CAP, tm), 0).astype(F32)
    for e in range(N_EXPERTS):
        sel_ref[e * MOE_CAP:(e + 1) * MOE_CAP, :] = jnp.where(keyt[e:e + 1, :] == riota, 1.0, 0.0).astype(BF16)
    xg = jnp.dot(sel_ref[...], mh, preferred_element_type=F32)
    xs_o[...] = xg.astype(BF16).reshape(xs_o.shape)


def _moe_router(a, bb, w_out, hf, norm_g, w_group, b_group, w_expert, b_expert, tb):
    n, d = hf.shape
    nblk = n // tb
    na, nb = a.shape[1], bb.shape[1]
    wa = w_out[:na].astype(BF16)
    wb = w_out[na:].astype(BF16)
    tok = lambda w: pl.BlockSpec((tb, w), lambda i: (i, 0))
    wr = jnp.zeros((d, LANES), F32).at[:, :N_EXPERTS].set(w_expert)
    wr = wr.at[:, N_EXPERTS:N_EXPERTS + MOE_GROUPS].set(w_group)
    wrh = wr.astype(BF16)
    wrl = (wr - wrh.astype(F32)).astype(BF16)
    br = jnp.zeros((1, LANES), F32).at[0, :N_EXPERTS].set(b_expert)
    br = br.at[0, N_EXPERTS:N_EXPERTS + MOE_GROUPS].set(b_group)
    tri = jnp.asarray(np.tril(np.ones((tb, tb), np.float32), -1), BF16)
    blk = pl.BlockSpec((1, LANES, tb), lambda i: (i, 0, 0))
    return pl.pallas_call(
        _router_kernel,
        grid=(nblk,),
        in_specs=[tok(na), tok(nb), _full(wa.shape), _full(wb.shape), tok(d), _full((1, d)), _full(wr.shape),
                  _full(wr.shape), _full(br.shape), _full(tri.shape)],
        out_specs=[tok(d), pl.BlockSpec((N_EXPERTS, MOE_CAP, d), lambda i: (0, i, 0)), blk, blk,
                   pl.BlockSpec((1, 1, LANES), lambda i: (i, 0, 0))],
        out_shape=[jax.ShapeDtypeStruct((n, d), F32),
                   jax.ShapeDtypeStruct((N_EXPERTS, nblk * MOE_CAP, d), BF16),
                   jax.ShapeDtypeStruct((nblk, LANES, tb), F32),
                   jax.ShapeDtypeStruct((nblk, LANES, tb), F32),
                   jax.ShapeDtypeStruct((nblk, 1, LANES), jnp.int32)],
        scratch_shapes=[pltpu.VMEM((N_EXPERTS * MOE_CAP, tb), BF16)],
        compiler_params=_params(("arbitrary",)),
        name="moe_router",
    )(a, bb, wa, wb, hf, norm_g[None, :], wrh, wrl, br, tri)


def _expert_mlp_kernel(top_ref, x_ref, wg_ref, wu_ref, wd_ref, y_ref, wg_sc, wu_sc, wd_sc):
    e = pl.program_id(0)
    i = pl.program_id(1)

    @pl.when(i == 0)
    def _():
        wg_sc[...] = wg_ref[0, 0].astype(BF16)
        wu_sc[...] = wu_ref[0, 0].astype(BF16)
        wd_sc[...] = wd_ref[0, 0].astype(BF16)

    tr, d = x_ref.shape[1], x_ref.shape[2]
    nb = tr // MOE_CAP
    top = top_ref[e * pl.num_programs(1) + i]

    def run(slots):
        x = x_ref[0]
        if slots < MOE_CAP:
            x = x.reshape(nb, MOE_CAP, d)[:, :slots].reshape(nb * slots, d)
        hid = _silu(jnp.dot(x, wg_sc[...], preferred_element_type=F32)) * jnp.dot(
            x, wu_sc[...], preferred_element_type=F32)
        y = jnp.dot(hid.astype(BF16), wd_sc[...], preferred_element_type=F32).astype(BF16)
        if slots < MOE_CAP:
            pad = jnp.zeros((nb, MOE_CAP - slots, d), BF16)
            y = jnp.concatenate([y.reshape(nb, slots, d), pad], axis=1).reshape(tr, d)
        y_ref[0] = y

    levels = [lv for lv in MOE_SLOT_LEVELS if lv < MOE_CAP] + [MOE_CAP]
    for n, slots in enumerate(levels):
        above = top > levels[n - 1] if n else top >= 0
        below = top <= slots if n + 1 < len(levels) else top >= 0
        pl.when(above & below)(functools.partial(run, slots))


def _expert_mlp(xs, cnt, w_gate, w_up, w_down, layer):
    ne, rows, d = xs.shape
    ff = w_gate.shape[3]
    tr = _tile(rows, EXPERT_TR)
    ntiles = rows // tr
    top = jnp.max(cnt[:, 0, :ne].reshape(ntiles, -1, ne), axis=1).T.reshape(-1)
    grid_spec = pltpu.PrefetchScalarGridSpec(
        num_scalar_prefetch=1,
        grid=(ne, ntiles),
        in_specs=[pl.BlockSpec((1, tr, d), lambda e, i, t: (e, i, 0)),
                  pl.BlockSpec((1, 1, d, ff), lambda e, i, t: (layer, e, 0, 0)),
                  pl.BlockSpec((1, 1, d, ff), lambda e, i, t: (layer, e, 0, 0)),
                  pl.BlockSpec((1, 1, ff, d), lambda e, i, t: (layer, e, 0, 0))],
        out_specs=pl.BlockSpec((1, tr, d), lambda e, i, t: (e, i, 0)),
        scratch_shapes=[pltpu.VMEM((d, ff), BF16), pltpu.VMEM((d, ff), BF16), pltpu.VMEM((ff, d), BF16)],
    )
    return pl.pallas_call(
        _expert_mlp_kernel,
        grid_spec=grid_spec,
        out_shape=jax.ShapeDtypeStruct((ne, rows, d), BF16),
        compiler_params=_params(("arbitrary", "arbitrary")),
        name="moe_expert_mlp",
    )(top, xs, w_gate, w_up, w_down)


def _ple_rows(h, p, g_ref, wg_ref, bg_ref, wp_ref):
    gate = jax.nn.sigmoid(_bdot(_rms(h, g_ref[...]), wg_ref[...]) + bg_ref[...])
    return h + gate * _bdot(p, wp_ref[...])


def _combine_ple_kernel(cnt_ref, y_ref, keyt_ref, wt_ref, h_ref, p_ref, gffn_ref, gple_ref, wgate_ref, bgate_ref,
                        wproj_ref, wg_hbm, wu_hbm, wd_hbm, o_ref, sel_ref, acc_ref, m_ref, wg_buf, wu_buf, wd_buf,
                        sems, *, layer):
    blk = pl.program_id(0)
    tb = h_ref.shape[0]
    most = lax.fori_loop(0, N_EXPERTS, lambda e, mx: jnp.maximum(mx, cnt_ref[blk * LANES + e]), 0)

    def combine(slots):
        riota = lax.broadcasted_iota(jnp.int32, (slots, tb), 0).astype(F32)
        for e in range(N_EXPERTS):
            hit = keyt_ref[0, e:e + 1, :] == riota
            sel_ref[e * slots:(e + 1) * slots, :] = jnp.where(hit, wt_ref[0, e:e + 1, :], 0.0).astype(BF16)
        y = y_ref[:, 0:slots, :].reshape(N_EXPERTS * slots, y_ref.shape[2])
        acc_ref[...] = h_ref[...] + lax.dot_general(sel_ref[0:N_EXPERTS * slots, :], y, (((0,), (0,)), ((), ())),
                                                    preferred_element_type=F32)

    levels = [lv for lv in MOE_SLOT_LEVELS if lv < MOE_CAP] + [MOE_CAP]
    for n, slots in enumerate(levels):
        above = most > levels[n - 1] if n else most >= 0
        below = most <= slots if n + 1 < len(levels) else most >= 0
        pl.when(above & below)(functools.partial(combine, slots))

    @pl.when(most > MOE_CAP)
    def _():
        m_ref[...] = _rms(h_ref[...], gffn_ref[...]).astype(BF16)
        rows = lax.broadcasted_iota(jnp.int32, (MOE_ROWS, tb), 0).astype(F32)

        def expert(e, carry):
            extra = cnt_ref[blk * LANES + e] - MOE_CAP

            @pl.when(extra > 0)
            def _():
                copies = [pltpu.make_async_copy(src.at[layer, e], dst, sems.at[n])
                          for n, (src, dst) in enumerate(((wg_hbm, wg_buf), (wu_hbm, wu_buf), (wd_hbm, wd_buf)))]
                for c in copies:
                    c.start()
                for c in copies:
                    c.wait()
                krow = keyt_ref[0, pl.ds(e, 1), :]
                wrow = wt_ref[0, pl.ds(e, 1), :]

                def chunk(ci, c2):
                    hit = krow == (rows + (MOE_CAP + ci * MOE_ROWS).astype(F32))
                    sel = jnp.where(hit, 1.0, 0.0).astype(BF16)
                    xg = jnp.dot(sel, m_ref[...], preferred_element_type=F32).astype(BF16)
                    hid = _silu(_bdot(xg, wg_buf[...])) * _bdot(xg, wu_buf[...])
                    yo = _bdot(hid, wd_buf[...]).astype(BF16)
                    acc_ref[...] += _bdot_tn(jnp.where(hit, wrow, 0.0), yo)
                    return c2

                lax.fori_loop(0, (extra + MOE_ROWS - 1) // MOE_ROWS, chunk, 0)

            return carry

        lax.fori_loop(0, N_EXPERTS, expert, 0)

    o_ref[...] = _ple_rows(acc_ref[...], p_ref[0], gple_ref, wgate_ref, bgate_ref, wproj_ref)


def _moe_combine_ple(cnt, ys, keyt, wt, hf, norm_ffn, w_gate, w_up, w_down, layer, p_all, ple_norm, ple_w_gate,
                     ple_b_gate, ple_w_proj, tb):
    n, d = hf.shape
    ff = w_gate.shape[3]
    nblk = n // tb
    pd = p_all.shape[-1]
    blk = pl.BlockSpec((1, LANES, tb), lambda i, c: (i, 0, 0))
    tok = pl.BlockSpec((tb, d), lambda i, c: (i, 0))
    vec = pl.BlockSpec((1, d), lambda i, c: (0, 0))
    hbm = pl.BlockSpec(memory_space=pl.ANY)
    grid_spec = pltpu.PrefetchScalarGridSpec(
        num_scalar_prefetch=1,
        grid=(nblk,),
        in_specs=[pl.BlockSpec((N_EXPERTS, MOE_CAP, d), lambda i, c: (0, i, 0)), blk, blk, tok,
                  pl.BlockSpec((1, tb, pd), lambda i, c: (layer, i, 0)), vec, vec,
                  pl.BlockSpec((d, d), lambda i, c: (0, 0)), vec, pl.BlockSpec((pd, d), lambda i, c: (0, 0)),
                  hbm, hbm, hbm],
        out_specs=tok,
        scratch_shapes=[pltpu.VMEM((N_EXPERTS * MOE_CAP, tb), BF16), pltpu.VMEM((tb, d), F32),
                        pltpu.VMEM((tb, d), BF16), pltpu.VMEM((d, ff), F32), pltpu.VMEM((d, ff), F32),
                        pltpu.VMEM((ff, d), F32), pltpu.SemaphoreType.DMA((3,))],
    )
    return pl.pallas_call(
        functools.partial(_combine_ple_kernel, layer=layer),
        grid_spec=grid_spec,
        out_shape=jax.ShapeDtypeStruct((n, d), F32),
        compiler_params=_params(("arbitrary",)),
        name="moe_combine_ple",
    )(cnt, ys, keyt, wt, hf, p_all.reshape(p_all.shape[0], n, pd), norm_ffn[None, :], ple_norm[None, :],
      ple_w_gate.astype(BF16), ple_b_gate[None, :], ple_w_proj.astype(BF16), w_gate, w_up, w_down)


def _proj_moe_ple(mix_a, mix_b, w_out, h, norm_g, w_group, b_group, w_expert, b_expert, w_gate, w_up, w_down,
                  layer, p_all, ple_w_proj, ple_norm, ple_w_gate, ple_b_gate):
    b, s, d = h.shape
    n = b * s
    tb = _tile(n, MOE_TB)
    hf, xs, keyt, wt, cnt = _moe_router(mix_a.reshape(n, -1), mix_b.reshape(n, -1), w_out, h.reshape(n, d),
                                        norm_g, w_group, b_group, w_expert, b_expert, tb)
    ys = _expert_mlp(xs, cnt, w_gate, w_up, w_down, layer)
    out = _moe_combine_ple(cnt.reshape(-1), ys, keyt, wt, hf, norm_g, w_gate, w_up, w_down, layer, p_all,
                           ple_norm, ple_w_gate, ple_b_gate, ple_w_proj, tb)
    return out.reshape(b, s, d)


def _even_mixers(h, positions, norm_mix, w_in, b_f, fox_qn, fox_kn, q_a_norm, w_q_up, kv_a_norm, w_kv_up,
                 mla_qn, mla_kn):
    fq, fk, fv, cum, mq, mk, mv = _even_pre(h, positions, norm_mix, w_in, b_f, fox_qn, fox_kn, q_a_norm,
                                            w_q_up, kv_a_norm, w_kv_up, mla_qn, mla_kn)
    return _attention(fq, fk, fv, cum), _attention(mq, mk, mv)


def kernel(x, p, positions, norm_mix, norm_ffn, ev_w_in, fox_b_f, fox_q_norm, fox_k_norm, mla_q_a_norm, mla_w_q_up, mla_kv_a_norm, mla_w_kv_up, mla_q_norm, mla_k_norm, ev_w_out, od_w_in, s5_a_re, s5_a_im, s5_b_re, s5_b_im, s5_c_re, s5_c_im, s5_d, s5_log_step, s5_w_glu, s5_b_glu, gdn_conv_w, gdn_a_log, gdn_dt_bias, gdn_o_norm, od_w_out, moe_w_group, moe_b_group, moe_w_expert, moe_b_expert, moe_w_gate, moe_w_up, moe_w_down, ple_w_proj, ple_norm, ple_w_gate, ple_b_gate):
    h = x
    depth = p.shape[0]
    for i in range(depth):
        j = i // 2
        if i % 2 == 0:
            mix = _even_mixers(h, positions, norm_mix[i], ev_w_in[j], fox_b_f[j], fox_q_norm[j], fox_k_norm[j],
                               mla_q_a_norm[j], mla_w_q_up[j], mla_kv_a_norm[j], mla_w_kv_up[j], mla_q_norm[j],
                               mla_k_norm[j])
            w_out = ev_w_out[j]
        else:
            mix = _odd_mixers(h, norm_mix[i], od_w_in[j], s5_a_re[j], s5_a_im[j], s5_b_re[j], s5_b_im[j],
                              s5_c_re[j], s5_c_im[j], s5_d[j], s5_log_step[j], s5_w_glu[j], s5_b_glu[j],
                              gdn_conv_w[j], gdn_a_log[j], gdn_dt_bias[j], gdn_o_norm[j])
            w_out = od_w_out[j]
        h = _proj_moe_ple(mix[0], mix[1], w_out, h, norm_ffn[i], moe_w_group[i], moe_b_group[i], moe_w_expert[i],
                          moe_b_expert[i], moe_w_gate, moe_w_up, moe_w_down, i, p, ple_w_proj[i], ple_norm[i],
                          ple_w_gate[i], ple_b_gate[i])
    return h
```

```python
import functools
import math

import numpy as np
import jax
import jax.numpy as jnp
from jax import lax
from jax.experimental import pallas as pl
from jax.experimental.pallas import tpu as pltpu

F32 = jnp.float32
BF16 = jnp.bfloat16

LANES = 128
RMS_EPS = 1e-6
ROPE_THETA = 10000.0
LOG2E = math.log2(math.e)

FOX_HEADS = 8
FOX_HEAD_DIM = 64
MLA_HEADS = 8
MLA_Q_LORA = 384
MLA_KV_LORA = 256
MLA_NOPE = 64
MLA_ROPE = 32
MLA_V = 64
MLA_QK = MLA_NOPE + MLA_ROPE

S5_CH = 512
S5_GROUP_CH = 16
S5_GROUPS = S5_CH // S5_GROUP_CH
S5_STATE = 64
S5_N = S5_GROUPS * S5_STATE

GDN_HEADS = 4
GDN_HEAD_DIM = 128
GDN_W = GDN_HEADS * GDN_HEAD_DIM
GDN_CONV = 4
GDN_CHUNK = 64

MOE_GROUPS = 4
MOE_PER_GROUP = 8
N_EXPERTS = MOE_GROUPS * MOE_PER_GROUP
MOE_TB = 512
MOE_CAP = 64
MOE_ROWS = 128
MOE_SLOT_LEVELS = (32, 48)
ATTN_TQ = 2048
ATTN_TK = 512
MIX_TM = 256
EXPERT_TR = 1024

VMEM_LIMIT = 56 * 1024 * 1024


def _tile(n, pref):
    t = min(n, pref)
    assert n % t == 0, (n, t)
    return t


def _params(sem):
    return pltpu.CompilerParams(dimension_semantics=sem, vmem_limit_bytes=VMEM_LIMIT)


def _full(shape):
    nd = len(shape)
    return pl.BlockSpec(shape, lambda *_: (0,) * nd)


def _rms(x, g):
    return x * lax.rsqrt(jnp.mean(x * x, axis=-1, keepdims=True) + RMS_EPS) * g


def _bdot(a, b):
    return jnp.dot(a.astype(BF16), b.astype(BF16), preferred_element_type=F32)


def _bdot_nt(a, b):
    return lax.dot_general(a.astype(BF16), b.astype(BF16), (((1,), (1,)), ((), ())),
                           preferred_element_type=F32)


def _bdot_tn(a, b):
    return lax.dot_general(a.astype(BF16), b.astype(BF16), (((0,), (0,)), ((), ())),
                           preferred_element_type=F32)


def _split3(x):
    x1 = x.astype(BF16)
    r = x - x1.astype(F32)
    x2 = r.astype(BF16)
    return x1, x2, (r - x2.astype(F32)).astype(BF16)


def _sel_dot(sel, x):
    return sum(jnp.dot(sel, part, preferred_element_type=F32) for part in _split3(x))


def _dot_sel(x, sel):
    return sum(jnp.dot(part, sel, preferred_element_type=F32) for part in _split3(x))


def _split_dot(x, ind):
    hi = x.astype(BF16)
    lo = (x - hi.astype(F32)).astype(BF16)
    return (jnp.dot(hi, ind, preferred_element_type=F32)
            + jnp.dot(lo, ind, preferred_element_type=F32))


def _log_sigmoid(x):
    return jnp.minimum(x, 0.0) - jnp.log(1.0 + jnp.exp(-jnp.abs(x)))


def _softplus(x):
    return jnp.maximum(x, 0.0) + jnp.log(1.0 + jnp.exp(-jnp.abs(x)))


def _silu(x):
    return x * jax.nn.sigmoid(x)


def _head_norm128(x, nheads, denom, gain):
    outs = []
    for hh in range(nheads):
        xh = x[:, LANES * hh:LANES * (hh + 1)]
        ss = jnp.sum(xh * xh, axis=-1, keepdims=True)
        outs.append(xh * lax.rsqrt(ss / denom + RMS_EPS))
    return jnp.concatenate(outs, axis=1) * gain


def _even_pre_kernel(h_ref, pos_ref, nmix_ref, win_ref, ind_ref, fqn_ref, fkn_ref, bf_ref,
                     qan_ref, wq_ref, kvan_ref, wkv_ref, mqn_ref, mkn_ref, freq_ref, s1_ref, s2_ref,
                     tri_ref, vone_ref, fq_o, fk_o, fv_o, cum_o, mq_o, mk_o, mv_o, carry_ref):
    t = pl.program_id(1)

    @pl.when(t == 0)
    def _():
        carry_ref[...] = jnp.zeros_like(carry_ref)

    tm = h_ref.shape[1]
    a = _rms(h_ref[0], nmix_ref[...])
    proj = _bdot(a, win_ref[...])
    nf = FOX_HEADS * FOX_HEAD_DIM
    fq = proj[:, 0:nf]
    fk = proj[:, nf:2 * nf]
    nv = FOX_HEADS * LANES
    fv = proj[:, 2 * nf:2 * nf + nv]
    o_cq = 2 * nf + nv
    cq = proj[:, o_cq:o_cq + MLA_Q_LORA]
    o_ckv = o_cq + MLA_Q_LORA
    ckv = proj[:, o_ckv:o_ckv + MLA_KV_LORA]
    misc = proj[:, o_ckv + MLA_KV_LORA:]

    ind = ind_ref[...]
    fq_n = fq * lax.rsqrt(_split_dot(fq * fq, ind) / FOX_HEAD_DIM + RMS_EPS) * fqn_ref[...]
    fk_n = fk * lax.rsqrt(_split_dot(fk * fk, ind) / FOX_HEAD_DIM + RMS_EPS) * fkn_ref[...]
    fq_o[0] = (fq_n * (FOX_HEAD_DIM ** -0.5 * LOG2E)).astype(BF16)
    fk_o[0] = fk_n.astype(BF16)
    fv_o[0] = (fv + vone_ref[...]).astype(BF16)

    lane = lax.broadcasted_iota(jnp.int32, (tm, LANES), 1)
    logf = jnp.where(lane < FOX_HEADS, _log_sigmoid(misc + bf_ref[...]), 0.0)
    cum = _sel_dot(tri_ref[...], logf) + carry_ref[...]
    carry_ref[...] = cum[tm - 1:tm, :]
    cum_o[0] = (cum * LOG2E).T[:FOX_HEADS, :]

    ang = pos_ref[0].astype(F32) * freq_ref[...]
    cos1 = jnp.cos(ang)
    sin1 = jnp.sin(ang)
    cos = jnp.concatenate([cos1] * MLA_HEADS, axis=1)
    sin_a = jnp.concatenate([sin1 * s1_ref[...]] * MLA_HEADS, axis=1)
    sin_b = jnp.concatenate([sin1 * s2_ref[...]] * MLA_HEADS, axis=1)
    width = MLA_HEADS * LANES
    half = MLA_ROPE // 2

    def rope(x):
        return (x * cos + pltpu.roll(x, width - half, 1) * sin_a + pltpu.roll(x, half, 1) * sin_b)

    q = _bdot(_rms(cq, qan_ref[...]), wq_ref[...])
    q = rope(_head_norm128(q, MLA_HEADS, MLA_QK, mqn_ref[...]))
    mq_o[0] = (q * (MLA_QK ** -0.5 * LOG2E)).astype(BF16)

    kv = _bdot(_rms(ckv, kvan_ref[...]), wkv_ref[...])
    kr = pltpu.roll(misc, MLA_NOPE - FOX_HEADS, 1)
    kr = jnp.where((lane >= MLA_NOPE) & (lane < MLA_QK), kr, 0.0)
    k = kv[:, :width] + jnp.concatenate([kr] * MLA_HEADS, axis=1)
    k = rope(_head_norm128(k, MLA_HEADS, MLA_QK, mkn_ref[...]))
    mk_o[0] = k.astype(BF16)
    mv_o[0] = (kv[:, width:] + vone_ref[...]).astype(BF16)


def _even_pre(h, positions, norm_mix, w_in, b_f, fox_qn, fox_kn, q_a_norm, w_q_up, kv_a_norm, w_kv_up,
              mla_qn, mla_kn):
    b, s, d = h.shape
    tm = _tile(s, MIX_TM)
    nf = FOX_HEADS * FOX_HEAD_DIM
    sizes = (nf, nf, nf, FOX_HEADS, MLA_Q_LORA, MLA_KV_LORA, MLA_ROPE)
    offs = np.concatenate([[0], np.cumsum(sizes)])
    parts = [w_in[:, offs[i]:offs[i + 1]] for i in range(len(sizes))]
    pad = jnp.zeros((d, LANES - FOX_HEADS - MLA_ROPE), w_in.dtype)
    slot_pad = ((0, 0), (0, 0), (0, LANES - FOX_HEAD_DIM))
    wfv = jnp.pad(parts[2].reshape(d, FOX_HEADS, FOX_HEAD_DIM), slot_pad).reshape(d, FOX_HEADS * LANES)
    win = jnp.concatenate([parts[0], parts[1], wfv, parts[4], parts[5], parts[3], parts[6], pad],
                          axis=1).astype(BF16)
    gidx = np.arange(nf) // FOX_HEAD_DIM
    ind = jnp.asarray(gidx[:, None] == gidx[None, :], BF16)
    fqn = jnp.tile(fox_qn, FOX_HEADS)[None, :]
    fkn = jnp.tile(fox_kn, FOX_HEADS)[None, :]
    bf = jnp.zeros((1, LANES), F32).at[0, :FOX_HEADS].set(b_f)
    padq = LANES - MLA_QK
    wq = jnp.pad(w_q_up.reshape(MLA_Q_LORA, MLA_HEADS, MLA_QK), ((0, 0), (0, 0), (0, padq)))
    wq = wq.reshape(MLA_Q_LORA, MLA_HEADS * LANES).astype(BF16)
    wkv3 = w_kv_up.reshape(MLA_KV_LORA, MLA_HEADS, MLA_NOPE + MLA_V)
    wk = jnp.pad(wkv3[:, :, :MLA_NOPE], ((0, 0), (0, 0), (0, LANES - MLA_NOPE)))
    wv = jnp.pad(wkv3[:, :, MLA_NOPE:], ((0, 0), (0, 0), (0, LANES - MLA_V)))
    wkv = jnp.concatenate([wk.reshape(MLA_KV_LORA, MLA_HEADS * LANES),
                           wv.reshape(MLA_KV_LORA, MLA_HEADS * LANES)], axis=1).astype(BF16)
    vone = jnp.tile(jnp.zeros((LANES,), F32).at[MLA_V].set(1.0), MLA_HEADS)[None, :]
    mqn = jnp.tile(jnp.pad(mla_qn, (0, padq)), MLA_HEADS)[None, :]
    mkn = jnp.tile(jnp.pad(mla_kn, (0, padq)), MLA_HEADS)[None, :]
    half = MLA_ROPE // 2
    inv = ROPE_THETA ** (-jnp.arange(half, dtype=F32) * 2.0 / MLA_ROPE)
    freq = jnp.zeros((1, LANES), F32).at[0, MLA_NOPE:MLA_NOPE + half].set(inv)
    freq = freq.at[0, MLA_NOPE + half:MLA_QK].set(inv)
    s1 = jnp.zeros((1, LANES), F32).at[0, MLA_NOPE:MLA_NOPE + half].set(-1.0)
    s2 = jnp.zeros((1, LANES), F32).at[0, MLA_NOPE + half:MLA_QK].set(1.0)
    tri = jnp.asarray(np.tril(np.ones((tm, tm), np.float32)), BF16)
    pos3 = positions.reshape(b, s, 1)

    row = lambda n: pl.BlockSpec((1, tm, n), lambda bi, ti: (bi, ti, 0))
    consts = [norm_mix[None, :], win, ind, fqn, fkn, bf, q_a_norm[None, :], wq, kv_a_norm[None, :], wkv,
              mqn, mkn, freq, s1, s2, tri, vone]
    nv = FOX_HEADS * LANES
    out_shape = [jax.ShapeDtypeStruct((b, s, nf), BF16)] * 2 + [jax.ShapeDtypeStruct((b, s, nv), BF16)] + [
        jax.ShapeDtypeStruct((b, FOX_HEADS, s), F32),
        jax.ShapeDtypeStruct((b, s, MLA_HEADS * LANES), BF16),
        jax.ShapeDtypeStruct((b, s, MLA_HEADS * LANES), BF16),
        jax.ShapeDtypeStruct((b, s, MLA_HEADS * LANES), BF16)]
    return pl.pallas_call(
        _even_pre_kernel,
        grid=(b, s // tm),
        in_specs=[row(d), row(1)] + [_full(c.shape) for c in consts],
        out_specs=[row(nf), row(nf), row(nv), pl.BlockSpec((1, FOX_HEADS, tm), lambda bi, ti: (bi, 0, ti)),
                   row(MLA_HEADS * LANES), row(MLA_HEADS * LANES), row(MLA_HEADS * LANES)],
        out_shape=out_shape,
        scratch_shapes=[pltpu.VMEM((1, LANES), F32)],
        compiler_params=_params(("arbitrary", "arbitrary")),
        name="even_pre",
    )(h, pos3, *consts)


def _attn_kernel(*refs, tq, tk, fox):
    if fox:
        q_ref, k_ref, v_ref, cr_ref, o_ref = refs
    else:
        q_ref, k_ref, v_ref, o_ref = refs
    hp = pl.program_id(1)
    i = pl.program_id(2)
    lane = lax.broadcasted_iota(jnp.int32, (tq, LANES), 1)
    qs = []
    for hh in range(2):
        if fox:
            in_head = (lane >= FOX_HEAD_DIM * hh) & (lane < FOX_HEAD_DIM * (hh + 1))
            qs.append(jnp.where(in_head, q_ref[0], jnp.zeros((), BF16)))
        else:
            qs.append(q_ref[0, :, LANES * hh:LANES * (hh + 1)])

    def step(j, carry, lo=None):
        koff = pl.multiple_of(j * tk, tk)
        top = 0 if lo is None else lo
        new = []
        for hh in range(2):
            m, acc = carry[hh]
            if fox:
                kj = k_ref[0, pl.ds(koff, tk), :]
            else:
                kj = k_ref[0, pl.ds(koff, tk), LANES * hh:LANES * (hh + 1)]
            sc = lax.dot_general(qs[hh][top:], kj, (((1,), (1,)), ((), ())), preferred_element_type=F32)
            if fox:
                sc = sc - cr_ref[0, pl.ds(2 * hp + hh, 1), pl.ds(koff, tk)]
            if lo is not None:
                rowi = lax.broadcasted_iota(jnp.int32, sc.shape, 0)
                coli = lax.broadcasted_iota(jnp.int32, sc.shape, 1)
                sc = jnp.where(coli <= rowi, sc, -jnp.inf)
            m_new = jnp.maximum(m[top:], jnp.max(sc, axis=-1, keepdims=True))
            alpha = jnp.exp2(m[top:] - m_new)
            p = jnp.exp2((sc - jnp.concatenate([m_new] * (tk // LANES), axis=1)).astype(BF16))
            vj = v_ref[0, pl.ds(koff, tk), LANES * hh:LANES * (hh + 1)]
            acc_new = alpha * acc[top:] + jnp.dot(p, vj, preferred_element_type=F32)
            if top:
                m_new = jnp.concatenate([m[:top], m_new], axis=0)
                acc_new = jnp.concatenate([acc[:top], acc_new], axis=0)
            new.append((m_new, acc_new))
        return tuple(new)

    def body(jj, carry):
        for r in range(ratio):
            carry = step(jj * ratio + r, carry)
        return carry

    one = (jnp.full((tq, LANES), -jnp.inf, F32), jnp.zeros((tq, LANES), F32))
    ratio = tq // tk
    carry = lax.fori_loop(0, i, body, (one, one))
    for r in range(ratio):
        carry = step(i * ratio + r, carry, lo=r * tk)
    outs = [acc / acc[:, MLA_V:MLA_V + 1] for _, acc in carry]
    o_ref[0] = jnp.where(lane < MLA_V, outs[0], pltpu.roll(outs[1], MLA_V, 1)).astype(o_ref.dtype)


def _attention(q, k, v, cum_row=None):
    b, s, _ = v.shape
    fox = cum_row is not None
    qw = LANES if fox else 2 * LANES
    tq = _tile(s, ATTN_TQ)
    tk = _tile(tq, ATTN_TK)
    npairs = v.shape[2] // (2 * LANES)
    in_specs = [pl.BlockSpec((1, tq, qw), lambda bi, hp, i: (bi, i, hp)),
                pl.BlockSpec((1, s, qw), lambda bi, hp, i: (bi, 0, hp)),
                pl.BlockSpec((1, s, 2 * LANES), lambda bi, hp, i: (bi, 0, hp))]
    args = [q, k, v]
    if fox:
        in_specs += [pl.BlockSpec((1, FOX_HEADS, s), lambda bi, hp, i: (bi, 0, 0))]
        args += [cum_row]
    return pl.pallas_call(
        functools.partial(_attn_kernel, tq=tq, tk=tk, fox=fox),
        grid=(b, npairs, s // tq),
        in_specs=in_specs,
        out_specs=pl.BlockSpec((1, tq, LANES), lambda bi, hp, i: (bi, i, hp)),
        out_shape=jax.ShapeDtypeStruct((b, s, npairs * LANES), BF16),
        compiler_params=_params(("arbitrary", "arbitrary", "arbitrary")),
        name="fox_attention" if fox else "mla_attention",
    )(*args)


N_S5_CONSTS = 13
N_GDN_CONSTS = 6


def _odd_kernel(h_ref, nmix_ref, win_ref, *refs):
    s5_consts = refs[:N_S5_CONSTS]
    gdn_consts = refs[N_S5_CONSTS:N_S5_CONSTS + N_GDN_CONSTS]
    y_o, o_o, x_ref, sr_ref, si_ref, xpad_ref, state_ref = refs[N_S5_CONSTS + N_GDN_CONSTS:]
    a = _rms(h_ref[0], nmix_ref[...])
    proj = _bdot(a, win_ref[...])
    o1 = S5_CH
    o2 = o1 + 3 * GDN_W
    o3 = o2 + GDN_W
    _s5_body(proj[:, :o1], *s5_consts, y_o, x_ref, sr_ref, si_ref)
    _gdn_body(proj[:, o1:o2], proj[:, o2:o3], proj[:, o3:], *gdn_consts, o_o, xpad_ref, state_ref)


def _odd_mixers(h, norm_mix, w_in, a_re, a_im, b_re, b_im, c_re, c_im, d_skip, log_step, w_glu, b_glu,
                conv_w, a_log, dt_bias, o_norm):
    b, s, d = h.shape
    tm = _tile(s, MIX_TM)
    sizes = (S5_CH, 3 * GDN_W, GDN_HEADS, GDN_HEADS, GDN_W)
    offs = np.concatenate([[0], np.cumsum(sizes)])
    parts = [w_in[:, offs[i]:offs[i + 1]] for i in range(len(sizes))]
    pad = jnp.zeros((d, LANES - 2 * GDN_HEADS), w_in.dtype)
    win = jnp.concatenate([parts[0], parts[1], parts[4], parts[2], parts[3], pad], axis=1).astype(BF16)
    s5_consts = _s5_consts(tm, a_re, a_im, b_re, b_im, c_re, c_im, d_skip, log_step, w_glu, b_glu)
    gdn_consts = _gdn_consts(tm, conv_w, a_log, dt_bias, o_norm)
    assert len(s5_consts) == N_S5_CONSTS and len(gdn_consts) == N_GDN_CONSTS
    consts = [norm_mix[None, :], win] + s5_consts + gdn_consts
    row = lambda n: pl.BlockSpec((1, tm, n), lambda bi, ti: (bi, ti, 0))
    return pl.pallas_call(
        _odd_kernel,
        grid=(b, s // tm),
        in_specs=[row(d)] + [_full(c.shape) for c in consts],
        out_specs=[row(S5_CH), row(GDN_W)],
        out_shape=[jax.ShapeDtypeStruct((b, s, S5_CH), BF16), jax.ShapeDtypeStruct((b, s, GDN_W), BF16)],
        scratch_shapes=[pltpu.VMEM((tm, 2 * S5_N), F32), pltpu.VMEM((1, S5_N), F32), pltpu.VMEM((1, S5_N), F32),
                        pltpu.VMEM((tm + 8, 3 * GDN_W), F32),
                        pltpu.VMEM((GDN_HEADS, GDN_HEAD_DIM, GDN_HEAD_DIM), F32)],
        compiler_params=_params(("arbitrary", "arbitrary")),
        name="odd_mixers",
    )(h, *consts)


def _s5_body(u, perm_ref, unperm_ref, bbd_ref, cbd_ref, ar_ref, ai_ref, asr_ref, asi_ref, pwr_ref, pwi_ref,
             d_ref, wglu_ref, bglu_ref, o_ref, x_ref, sr_ref, si_ref):
    t = pl.program_id(1)

    @pl.when(t == 0)
    def _():
        sr_ref[...] = jnp.zeros_like(sr_ref)
        si_ref[...] = jnp.zeros_like(si_ref)

    tm = u.shape[0]
    nseg = 8
    seg = tm // nseg
    u = _sel_dot(perm_ref[...], u)
    hc = S5_CH // 2
    hn = S5_N // 2
    ub = u.astype(BF16)
    for part in range(2):
        for base in (0, S5_N):
            cols = slice(base + part * hn, base + (part + 1) * hn)
            x_ref[:, cols] = jnp.dot(ub[:, part * hc:(part + 1) * hc], bbd_ref[part * hc:(part + 1) * hc, cols],
                                     preferred_element_type=F32)
    ar = ar_ref[...]
    ai = ai_ref[...]
    re = slice(0, S5_N)
    im = slice(S5_N, 2 * S5_N)

    def local(i, carry):
        xr, xi = carry
        rows = pl.ds(pl.multiple_of(i * nseg, nseg), nseg)
        nr = ar * xr - ai * xi + x_ref[rows, re]
        ni = ar * xi + ai * xr + x_ref[rows, im]
        x_ref[rows, re] = nr
        x_ref[rows, im] = ni
        return nr, ni

    zero = jnp.zeros((nseg, S5_N), F32)
    er, ei = lax.fori_loop(0, seg, local, (zero, zero), unroll=True)

    asr = asr_ref[...]
    asi = asi_ref[...]
    cr = [sr_ref[...]]
    ci = [si_ref[...]]
    for s in range(nseg):
        cr.append(asr * cr[s] - asi * ci[s] + er[s:s + 1, :])
        ci.append(asr * ci[s] + asi * cr[s] + ei[s:s + 1, :])
    sr_ref[...] = cr[nseg]
    si_ref[...] = ci[nseg]
    ent_r = jnp.concatenate(cr[:nseg], axis=0)
    ent_i = jnp.concatenate(ci[:nseg], axis=0)

    def fix(i, c):
        rows = pl.ds(pl.multiple_of(i * nseg, nseg), nseg)
        pr = pwr_ref[pl.ds(i, 1), :]
        pi = pwi_ref[pl.ds(i, 1), :]
        x_ref[rows, re] += pr * ent_r - pi * ent_i
        x_ref[rows, im] += pr * ent_i + pi * ent_r
        return c

    lax.fori_loop(0, seg, fix, 0, unroll=True)
    ys = []
    for part in range(2):
        oc = slice(part * hc, (part + 1) * hc)
        acc = None
        for base in (0, S5_N):
            rows = slice(base + part * hn, base + (part + 1) * hn)
            term = _bdot(x_ref[:, rows], cbd_ref[rows, oc])
            acc = term if acc is None else acc + term
        ys.append(acc)
    y = jnp.concatenate(ys, axis=1) + d_ref[...] * u
    hg = jax.nn.gelu(y)
    out = (hg * jax.nn.sigmoid(_bdot(hg, wglu_ref[...]) + bglu_ref[...])).astype(BF16)
    o_ref[0] = jnp.dot(unperm_ref[...], out, preferred_element_type=F32).astype(o_ref.dtype)


def _s5_consts(tm, a_re, a_im, b_re, b_im, c_re, c_im, d_skip, log_step, w_glu, b_glu):
    lam_re = jnp.minimum(a_re, -1e-4)
    lam_im = a_im
    dt = jnp.exp(log_step)[:, None]
    mag = jnp.exp(lam_re * dt)
    ab_re = mag * jnp.cos(lam_im * dt)
    ab_im = mag * jnp.sin(lam_im * dt)
    den = lam_re * lam_re + lam_im * lam_im
    nr, ni = ab_re - 1.0, ab_im
    gam_re = (nr * lam_re + ni * lam_im) / den
    gam_im = (ni * lam_re - nr * lam_im) / den
    bb_re = gam_re[..., None] * b_re - gam_im[..., None] * b_im
    bb_im = gam_re[..., None] * b_im + gam_im[..., None] * b_re
    eye = jnp.eye(S5_GROUPS, dtype=F32)
    bd_in = lambda m: jnp.einsum('gpc,gh->gchp', m, eye).reshape(S5_CH, S5_N)
    bd_out = lambda m: jnp.einsum('gcp,gh->gphc', m, eye).reshape(S5_N, S5_CH)
    bbd = jnp.concatenate([bd_in(bb_re), bd_in(bb_im)], axis=1).astype(BF16)
    cbd = jnp.concatenate([bd_out(c_re), -bd_out(c_im)], axis=0).astype(BF16)
    seg = tm // 8
    steps = jnp.arange(1, seg + 1, dtype=F32)[:, None, None] * dt[None]
    pmag = jnp.exp(lam_re[None] * steps)
    pw_re = (pmag * jnp.cos(lam_im[None] * steps)).reshape(seg, S5_N)
    pw_im = (pmag * jnp.sin(lam_im[None] * steps)).reshape(seg, S5_N)
    src = (np.arange(tm) % 8) * seg + np.arange(tm) // 8
    perm = np.zeros((tm, tm), np.float32)
    perm[np.arange(tm), src] = 1.0
    return [jnp.asarray(perm, BF16), jnp.asarray(perm.T, BF16),
            bbd, cbd, ab_re.reshape(1, S5_N), ab_im.reshape(1, S5_N), pw_re[seg - 1:seg], pw_im[seg - 1:seg],
            pw_re, pw_im, d_skip[None, :], w_glu.astype(BF16), b_glu[None, :]]


def _gdn_body(x, z, gb, cw_ref, nega_ref, dtb_ref, onorm_ref, tril_ref, triu_ref, o_ref, xpad_ref, state_ref):
    t = pl.program_id(1)
    tm = x.shape[0]
    c = GDN_CHUNK
    hd = GDN_HEAD_DIM

    @pl.when(t == 0)
    def _():
        xpad_ref[0:8, :] = jnp.zeros((8, xpad_ref.shape[1]), F32)
        state_ref[...] = jnp.zeros_like(state_ref)

    @pl.when(t > 0)
    def _():
        xpad_ref[0:8, :] = xpad_ref[tm:tm + 8, :]

    xpad_ref[8:tm + 8, :] = x
    conv = cw_ref[0:1, :] * xpad_ref[pl.ds(8 - (GDN_CONV - 1), tm), :]
    for i in range(1, GDN_CONV):
        conv = conv + cw_ref[i:i + 1, :] * xpad_ref[pl.ds(8 - (GDN_CONV - 1) + i, tm), :]
    act = _silu(conv)

    def l2n(x):
        return x * lax.rsqrt(jnp.sum(x * x, axis=-1, keepdims=True) + RMS_EPS)

    g = nega_ref[...] * _softplus(gb + dtb_ref[...])
    beta = jax.nn.sigmoid(gb)
    gc = _sel_dot(tril_ref[...], g)
    gct = _dot_sel(g.T, triu_ref[...])

    ri = lax.broadcasted_iota(jnp.int32, (tm, tm), 0)
    ci = lax.broadcasted_iota(jnp.int32, (tm, tm), 1)
    same = (ri // c) == (ci // c)
    incl = same & (ri >= ci)
    strict = same & (ri > ci)
    eye = (ri == ci).astype(F32)
    offs = []
    bs = 1
    while bs < c:
        offs.append(((ri // (2 * bs)) == (ci // (2 * bs))) & ((ri % (2 * bs)) >= bs) & ((ci % (2 * bs)) < bs))
        bs *= 2
    nchunks = tm // c

    heads = range(GDN_HEADS)
    q = [l2n(act[:, hh * hd:(hh + 1) * hd]) * (hd ** -0.5) for hh in heads]
    k = [l2n(act[:, GDN_W + hh * hd:GDN_W + (hh + 1) * hd]) for hh in heads]
    v = [act[:, 2 * GDN_W + hh * hd:2 * GDN_W + (hh + 1) * hd] for hh in heads]
    bcol = [beta[:, GDN_HEADS + hh:GDN_HEADS + hh + 1] for hh in heads]
    gcol = [gc[:, hh:hh + 1] for hh in heads]
    decay = [jnp.where(incl, jnp.exp(jnp.where(incl, gcol[hh] - gct[hh:hh + 1, :], 0.0)), 0.0) for hh in heads]
    kb = [k[hh] * bcol[hh] for hh in heads]
    a_mat = [jnp.where(strict, _bdot_nt(kb[hh], k[hh]) * decay[hh], 0.0) for hh in heads]
    t_mat = [eye - jnp.where(offs[0], a_mat[hh], 0.0) for hh in heads]
    for off in offs[1:]:
        pa = [_bdot(t_mat[hh], jnp.where(off, a_mat[hh], 0.0)) for hh in heads]
        t_mat = [t_mat[hh] - _bdot(pa[hh], t_mat[hh]) for hh in heads]
    th = [t_mat[hh].astype(BF16) for hh in heads]
    tl = [(t_mat[hh] - th[hh].astype(F32)).astype(BF16) for hh in heads]
    ah = [a_mat[hh].astype(BF16) for hh in heads]
    al = [(a_mat[hh] - ah[hh].astype(F32)).astype(BF16) for hh in heads]
    a_t = [jnp.dot(ah[hh], th[hh], preferred_element_type=F32) + jnp.dot(ah[hh], tl[hh], preferred_element_type=F32)
           + jnp.dot(al[hh], th[hh], preferred_element_type=F32) for hh in heads]
    t_mat = [t_mat[hh] + jnp.dot(th[hh], (eye - t_mat[hh] - a_t[hh]).astype(BF16), preferred_element_type=F32)
             for hh in heads]
    eg = [jnp.exp(gcol[hh]) for hh in heads]
    u = [_bdot(t_mat[hh], v[hh] * bcol[hh]) for hh in heads]
    w = [_bdot(t_mat[hh], kb[hh] * eg[hh]) for hh in heads]
    intra = [jnp.where(incl, _bdot_nt(q[hh], k[hh]) * decay[hh], 0.0).astype(BF16) for hh in heads]
    qd = [q[hh] * eg[hh] for hh in heads]
    state = [state_ref[hh] for hh in heads]
    for n in range(nchunks):
        r0 = n * c
        for hh in heads:
            lo = hh * hd
            gcn = gcol[hh][r0:r0 + c, :]
            glast = gcol[hh][r0 + c - 1:r0 + c, :]
            v_new = u[hh][r0:r0 + c, :] - _bdot(w[hh][r0:r0 + c, :], state[hh])
            v_rep = jnp.concatenate([v_new.astype(BF16)] * nchunks, axis=0)
            o = _bdot(qd[hh][r0:r0 + c, :], state[hh]) + jnp.dot(intra[hh][r0:r0 + c, :], v_rep,
                                                                  preferred_element_type=F32)
            state[hh] = state[hh] * jnp.exp(glast) + _bdot_tn(k[hh][r0:r0 + c, :] * jnp.exp(glast - gcn), v_new)
            on = o * lax.rsqrt(jnp.mean(o * o, axis=-1, keepdims=True) + RMS_EPS) * onorm_ref[...]
            o_ref[0, r0:r0 + c, lo:lo + hd] = (on * _silu(z[r0:r0 + c, lo:lo + hd])).astype(o_ref.dtype)
    for hh in heads:
        state_ref[hh] = state[hh]


def _gdn_consts(tm, conv_w, a_log, dt_bias, o_norm):
    nega = jnp.zeros((1, LANES), F32).at[0, :GDN_HEADS].set(-jnp.exp(a_log))
    dtb = jnp.zeros((1, LANES), F32).at[0, :GDN_HEADS].set(dt_bias)
    cwp = jnp.pad(conv_w, ((0, 8 - GDN_CONV), (0, 0)))
    ridx = np.arange(tm)
    same = (ridx[:, None] // GDN_CHUNK) == (ridx[None, :] // GDN_CHUNK)
    tril = jnp.asarray((same & (ridx[:, None] >= ridx[None, :])).astype(np.float32), BF16)
    triu = jnp.asarray((same & (ridx[:, None] <= ridx[None, :])).astype(np.float32), BF16)
    return [cwp, nega, dtb, o_norm[None, :], tril, triu]


def _router_kernel(a_ref, b_ref, wa_ref, wb_ref, h_ref, g_ref, wrh_ref, wrl_ref, br_ref, tri_ref,
                   h_o, xs_o, keyt_o, wt_o, cnt_o, sel_ref):
    tm = h_ref.shape[0]
    h = (h_ref[...] + jnp.dot(a_ref[...], wa_ref[...], preferred_element_type=F32)
         + jnp.dot(b_ref[...], wb_ref[...], preferred_element_type=F32))
    h_o[...] = h
    m = _rms(h, g_ref[...])
    mh = m.astype(BF16)
    ml = (m - mh.astype(F32)).astype(BF16)
    logits = (jnp.dot(mh, wrh_ref[...], preferred_element_type=F32) + jnp.dot(mh, wrl_ref[...], preferred_element_type=F32)
              + jnp.dot(ml, wrh_ref[...], preferred_element_type=F32)) + br_ref[...]
    lane = lax.broadcasted_iota(jnp.int32, (tm, LANES), 1)
    neg = -jnp.inf

    def first_argmax(x):
        mx = jnp.max(x, axis=-1, keepdims=True)
        idx = jnp.min(jnp.where(x == mx, lane, LANES), axis=-1, keepdims=True)
        return mx, idx

    is_g = (lane >= N_EXPERTS) & (lane < N_EXPERTS + MOE_GROUPS)
    gl = jnp.where(is_g, logits, neg)
    gmax, gidx = first_argmax(gl)
    g_w = 1.0 / jnp.sum(jnp.where(is_g, jnp.exp(gl - gmax), 0.0), axis=-1, keepdims=True)
    in_group = (lane // MOE_PER_GROUP) == (gidx - N_EXPERTS)
    el = jnp.where(in_group & (lane < N_EXPERTS), logits, neg)
    m1, i1 = first_argmax(el)
    el2 = jnp.where(lane == i1, neg, el)
    m2, i2 = first_argmax(el2)
    r = jnp.exp(m2 - m1)
    w1 = g_w / (1.0 + r)
    w2 = g_w * r / (1.0 + r)
    chose = (lane == i1) | (lane == i2)
    wmat = jnp.where(lane == i1, w1, jnp.where(lane == i2, w2, 0.0))
    ch = chose.astype(F32)
    rank = jnp.dot(tri_ref[...], ch.astype(BF16), preferred_element_type=F32)
    keyt = jnp.where(chose, rank, -1.0).T
    keyt_o[0] = keyt
    wt_o[0] = wmat.T
    cnt_o[0] = jnp.sum(ch, axis=0, keepdims=True).astype(jnp.int32)
    riota = lax.broadcasted_iota(jnp.int32, (MOE_CAP, tm), 0).astype(F32)
    for e in range(N_EXPERTS):
        sel_ref[e * MOE_CAP:(e + 1) * MOE_CAP, :] = jnp.where(keyt[e:e + 1, :] == riota, 1.0, 0.0).astype(BF16)
    xg = jnp.dot(sel_ref[...], mh, preferred_element_type=F32)
    xs_o[...] = xg.astype(BF16).reshape(xs_o.shape)


def _moe_router(a, bb, w_out, hf, norm_g, w_group, b_group, w_expert, b_expert, tb):
    n, d = hf.shape
    nblk = n // tb
    na, nb = a.shape[1], bb.shape[1]
    wa = w_out[:na].astype(BF16)
    wb = w_out[na:].astype(BF16)
    tok = lambda w: pl.BlockSpec((tb, w), lambda i: (i, 0))
    wr = jnp.zeros((d, LANES), F32).at[:, :N_EXPERTS].set(w_expert)
    wr = wr.at[:, N_EXPERTS:N_EXPERTS + MOE_GROUPS].set(w_group)
    wrh = wr.astype(BF16)
    wrl = (wr - wrh.astype(F32)).astype(BF16)
    br = jnp.zeros((1, LANES), F32).at[0, :N_EXPERTS].set(b_expert)
    br = br.at[0, N_EXPERTS:N_EXPERTS + MOE_GROUPS].set(b_group)
    tri = jnp.asarray(np.tril(np.ones((tb, tb), np.float32), -1), BF16)
    blk = pl.BlockSpec((1, LANES, tb), lambda i: (i, 0, 0))
    return pl.pallas_call(
        _router_kernel,
        grid=(nblk,),
        in_specs=[tok(na), tok(nb), _full(wa.shape), _full(wb.shape), tok(d), _full((1, d)), _full(wr.shape),
                  _full(wr.shape), _full(br.shape), _full(tri.shape)],
        out_specs=[tok(d), pl.BlockSpec((N_EXPERTS, MOE_CAP, d), lambda i: (0, i, 0)), blk, blk,
                   pl.BlockSpec((1, 1, LANES), lambda i: (i, 0, 0))],
        out_shape=[jax.ShapeDtypeStruct((n, d), F32),
                   jax.ShapeDtypeStruct((N_EXPERTS, nblk * MOE_CAP, d), BF16),
                   jax.ShapeDtypeStruct((nblk, LANES, tb), F32),
                   jax.ShapeDtypeStruct((nblk, LANES, tb), F32),
                   jax.ShapeDtypeStruct((nblk, 1, LANES), jnp.int32)],
        scratch_shapes=[pltpu.VMEM((N_EXPERTS * MOE_CAP, tb), BF16)],
        compiler_params=_params(("arbitrary",)),
        name="moe_router",
    )(a, bb, wa, wb, hf, norm_g[None, :], wrh, wrl, br, tri)


def _expert_mlp_kernel(top_ref, x_ref, wg_ref, wu_ref, wd_ref, y_ref, wg_sc, wu_sc, wd_sc):
    e = pl.program_id(0)
    i = pl.program_id(1)

    @pl.when(i == 0)
    def _():
        wg_sc[...] = wg_ref[0, 0].astype(BF16)
        wu_sc[...] = wu_ref[0, 0].astype(BF16)
        wd_sc[...] = wd_ref[0, 0].astype(BF16)

    tr, d = x_ref.shape[1], x_ref.shape[2]
    nb = tr // MOE_CAP
    top = top_ref[e * pl.num_programs(1) + i]

    def run(slots):
        x = x_ref[0]
        if slots < MOE_CAP:
            x = x.reshape(nb, MOE_CAP, d)[:, :slots].reshape(nb * slots, d)
        hid = _silu(jnp.dot(x, wg_sc[...], preferred_element_type=F32)) * jnp.dot(
            x, wu_sc[...], preferred_element_type=F32)
        y = jnp.dot(hid.astype(BF16), wd_sc[...], preferred_element_type=F32).astype(BF16)
        if slots < MOE_CAP:
            pad = jnp.zeros((nb, MOE_CAP - slots, d), BF16)
            y = jnp.concatenate([y.reshape(nb, slots, d), pad], axis=1).reshape(tr, d)
        y_ref[0] = y

    levels = [lv for lv in MOE_SLOT_LEVELS if lv < MOE_CAP] + [MOE_CAP]
    for n, slots in enumerate(levels):
        above = top > levels[n - 1] if n else top >= 0
        below = top <= slots if n + 1 < len(levels) else top >= 0
        pl.when(above & below)(functools.partial(run, slots))


def _expert_mlp(xs, cnt, w_gate, w_up, w_down, layer):
    ne, rows, d = xs.shape
    ff = w_gate.shape[3]
    tr = _tile(rows, EXPERT_TR)
    ntiles = rows // tr
    top = jnp.max(cnt[:, 0, :ne].reshape(ntiles, -1, ne), axis=1).T.reshape(-1)
    grid_spec = pltpu.PrefetchScalarGridSpec(
        num_scalar_prefetch=1,
        grid=(ne, ntiles),
        in_specs=[pl.BlockSpec((1, tr, d), lambda e, i, t: (e, i, 0)),
                  pl.BlockSpec((1, 1, d, ff), lambda e, i, t: (layer, e, 0, 0)),
                  pl.BlockSpec((1, 1, d, ff), lambda e, i, t: (layer, e, 0, 0)),
                  pl.BlockSpec((1, 1, ff, d), lambda e, i, t: (layer, e, 0, 0))],
        out_specs=pl.BlockSpec((1, tr, d), lambda e, i, t: (e, i, 0)),
        scratch_shapes=[pltpu.VMEM((d, ff), BF16), pltpu.VMEM((d, ff), BF16), pltpu.VMEM((ff, d), BF16)],
    )
    return pl.pallas_call(
        _expert_mlp_kernel,
        grid_spec=grid_spec,
        out_shape=jax.ShapeDtypeStruct((ne, rows, d), BF16),
        compiler_params=_params(("arbitrary", "arbitrary")),
        name="moe_expert_mlp",
    )(top, xs, w_gate, w_up, w_down)


def _ple_rows(h, p, g_ref, wg_ref, bg_ref, wp_ref):
    gate = jax.nn.sigmoid(_bdot(_rms(h, g_ref[...]), wg_ref[...]) + bg_ref[...])
    return h + gate * _bdot(p, wp_ref[...])


def _combine_ple_kernel(cnt_ref, y_ref, keyt_ref, wt_ref, h_ref, p_ref, gffn_ref, gple_ref, wgate_ref, bgate_ref,
                        wproj_ref, wg_hbm, wu_hbm, wd_hbm, o_ref, sel_ref, acc_ref, m_ref, wg_buf, wu_buf, wd_buf,
                        sems, *, layer):
    blk = pl.program_id(0)
    tb = h_ref.shape[0]
    most = lax.fori_loop(0, N_EXPERTS, lambda e, mx: jnp.maximum(mx, cnt_ref[blk * LANES + e]), 0)

    def combine(slots):
        riota = lax.broadcasted_iota(jnp.int32, (slots, tb), 0).astype(F32)
        for e in range(N_EXPERTS):
            hit = keyt_ref[0, e:e + 1, :] == riota
            sel_ref[e * slots:(e + 1) * slots, :] = jnp.where(hit, wt_ref[0, e:e + 1, :], 0.0).astype(BF16)
        y = y_ref[:, 0:slots, :].reshape(N_EXPERTS * slots, y_ref.shape[2])
        acc_ref[...] = h_ref[...] + lax.dot_general(sel_ref[0:N_EXPERTS * slots, :], y, (((0,), (0,)), ((), ())),
                                                    preferred_element_type=F32)

    levels = [lv for lv in MOE_SLOT_LEVELS if lv < MOE_CAP] + [MOE_CAP]
    for n, slots in enumerate(levels):
        above = most > levels[n - 1] if n else most >= 0
        below = most <= slots if n + 1 < len(levels) else most >= 0
        pl.when(above & below)(functools.partial(combine, slots))

    @pl.when(most > MOE_CAP)
    def _():
        m_ref[...] = _rms(h_ref[...], gffn_ref[...]).astype(BF16)
        rows = lax.broadcasted_iota(jnp.int32, (MOE_ROWS, tb), 0).astype(F32)

        def expert(e, carry):
            extra = cnt_ref[blk * LANES + e] - MOE_CAP

            @pl.when(extra > 0)
            def _():
                copies = [pltpu.make_async_copy(src.at[layer, e], dst, sems.at[n])
                          for n, (src, dst) in enumerate(((wg_hbm, wg_buf), (wu_hbm, wu_buf), (wd_hbm, wd_buf)))]
                for c in copies:
                    c.start()
                for c in copies:
                    c.wait()
                krow = keyt_ref[0, pl.ds(e, 1), :]
                wrow = wt_ref[0, pl.ds(e, 1), :]

                def chunk(ci, c2):
                    hit = krow == (rows + (MOE_CAP + ci * MOE_ROWS).astype(F32))
                    sel = jnp.where(hit, 1.0, 0.0).astype(BF16)
                    xg = jnp.dot(sel, m_ref[...], preferred_element_type=F32).astype(BF16)
                    hid = _silu(_bdot(xg, wg_buf[...])) * _bdot(xg, wu_buf[...])
                    yo = _bdot(hid, wd_buf[...]).astype(BF16)
                    acc_ref[...] += _bdot_tn(jnp.where(hit, wrow, 0.0), yo)
                    return c2

                lax.fori_loop(0, (extra + MOE_ROWS - 1) // MOE_ROWS, chunk, 0)

            return carry

        lax.fori_loop(0, N_EXPERTS, expert, 0)

    o_ref[...] = _ple_rows(acc_ref[...], p_ref[0], gple_ref, wgate_ref, bgate_ref, wproj_ref)


def _moe_combine_ple(cnt, ys, keyt, wt, hf, norm_ffn, w_gate, w_up, w_down, layer, p_all, ple_norm, ple_w_gate,
                     ple_b_gate, ple_w_proj, tb):
    n, d = hf.shape
    ff = w_gate.shape[3]
    nblk = n // tb
    pd = p_all.shape[-1]
    blk = pl.BlockSpec((1, LANES, tb), lambda i, c: (i, 0, 0))
    tok = pl.BlockSpec((tb, d), lambda i, c: (i, 0))
    vec = pl.BlockSpec((1, d), lambda i, c: (0, 0))
    hbm = pl.BlockSpec(memory_space=pl.ANY)
    grid_spec = pltpu.PrefetchScalarGridSpec(
        num_scalar_prefetch=1,
        grid=(nblk,),
        in_specs=[pl.BlockSpec((N_EXPERTS, MOE_CAP, d), lambda i, c: (0, i, 0)), blk, blk, tok,
                  pl.BlockSpec((1, tb, pd), lambda i, c: (layer, i, 0)), vec, vec,
                  pl.BlockSpec((d, d), lambda i, c: (0, 0)), vec, pl.BlockSpec((pd, d), lambda i, c: (0, 0)),
                  hbm, hbm, hbm],
        out_specs=tok,
        scratch_shapes=[pltpu.VMEM((N_EXPERTS * MOE_CAP, tb), BF16), pltpu.VMEM((tb, d), F32),
                        pltpu.VMEM((tb, d), BF16), pltpu.VMEM((d, ff), F32), pltpu.VMEM((d, ff), F32),
                        pltpu.VMEM((ff, d), F32), pltpu.SemaphoreType.DMA((3,))],
    )
    return pl.pallas_call(
        functools.partial(_combine_ple_kernel, layer=layer),
        grid_spec=grid_spec,
        out_shape=jax.ShapeDtypeStruct((n, d), F32),
        compiler_params=_params(("arbitrary",)),
        name="moe_combine_ple",
    )(cnt, ys, keyt, wt, hf, p_all.reshape(p_all.shape[0], n, pd), norm_ffn[None, :], ple_norm[None, :],
      ple_w_gate.astype(BF16), ple_b_gate[None, :], ple_w_proj.astype(BF16), w_gate, w_up, w_down)


def _proj_moe_ple(mix_a, mix_b, w_out, h, norm_g, w_group, b_group, w_expert, b_expert, w_gate, w_up, w_down,
                  layer, p_all, ple_w_proj, ple_norm, ple_w_gate, ple_b_gate):
    b, s, d = h.shape
    n = b * s
    tb = _tile(n, MOE_TB)
    hf, xs, keyt, wt, cnt = _moe_router(mix_a.reshape(n, -1), mix_b.reshape(n, -1), w_out, h.reshape(n, d),
                                        norm_g, w_group, b_group, w_expert, b_expert, tb)
    ys = _expert_mlp(xs, cnt, w_gate, w_up, w_down, layer)
    out = _moe_combine_ple(cnt.reshape(-1), ys, keyt, wt, hf, norm_g, w_gate, w_up, w_down, layer, p_all,
                           ple_norm, ple_w_gate, ple_b_gate, ple_w_proj, tb)
    return out.reshape(b, s, d)


def _even_mixers(h, positions, norm_mix, w_in, b_f, fox_qn, fox_kn, q_a_norm, w_q_up, kv_a_norm, w_kv_up,
                 mla_qn, mla_kn):
    fq, fk, fv, cum, mq, mk, mv = _even_pre(h, positions, norm_mix, w_in, b_f, fox_qn, fox_kn, q_a_norm,
                                            w_q_up, kv_a_norm, w_kv_up, mla_qn, mla_kn)
    return _attention(fq, fk, fv, cum), _attention(mq, mk, mv)


def kernel(x, p, positions, norm_mix, norm_ffn, ev_w_in, fox_b_f, fox_q_norm, fox_k_norm, mla_q_a_norm, mla_w_q_up, mla_kv_a_norm, mla_w_kv_up, mla_q_norm, mla_k_norm, ev_w_out, od_w_in, s5_a_re, s5_a_im, s5_b_re, s5_b_im, s5_c_re, s5_c_im, s5_d, s5_log_step, s5_w_glu, s5_b_glu, gdn_conv_w, gdn_a_log, gdn_dt_bias, gdn_o_norm, od_w_out, moe_w_group, moe_b_group, moe_w_expert, moe_b_expert, moe_w_gate, moe_w_up, moe_w_down, ple_w_proj, ple_norm, ple_w_gate, ple_b_gate):
    h = x
    depth = p.shape[0]
    for i in range(depth):
        j = i // 2
        if i % 2 == 0:
            mix = _even_mixers(h, positions, norm_mix[i], ev_w_in[j], fox_b_f[j], fox_q_norm[j], fox_k_norm[j],
                               mla_q_a_norm[j], mla_w_q_up[j], mla_kv_a_norm[j], mla_w_kv_up[j], mla_q_norm[j],
                               mla_k_norm[j])
            w_out = ev_w_out[j]
        else:
            mix = _odd_mixers(h, norm_mix[i], od_w_in[j], s5_a_re[j], s5_a_im[j], s5_b_re[j], s5_b_im[j],
                              s5_c_re[j], s5_c_im[j], s5_d[j], s5_log_step[j], s5_w_glu[j], s5_b_glu[j],
                              gdn_conv_w[j], gdn_a_log[j], gdn_dt_bias[j], gdn_o_norm[j])
            w_out = od_w_out[j]
        h = _proj_moe_ple(mix[0], mix[1], w_out, h, norm_ffn[i], moe_w_group[i], moe_b_group[i], moe_w_expert[i],
                          moe_b_expert[i], moe_w_gate, moe_w_up, moe_w_down, i, p, ple_w_proj[i], ple_norm[i],
                          ple_w_gate[i], ple_b_gate[i])
    return h
```
